```python
import jax
import jax.numpy as jnp
from jax import lax
import numpy as np

D_MODEL = 2048
BATCH = 16
SEQ = 256
DEPTH = 4
DEC_BATCH = 2
DEC_SEQ = 4096
PAST_LEN = 256

GRID_W = 64
ROPE_THETA = 10000.0
EPS = 1e-6
QBLK = 128
NEG_INF = -1e30

MLA_H = 4
MLA_Q_LORA = 512
MLA_KV_LORA = 256
MLA_NOPE = 128
MLA_ROPE = 64
MLA_V = 128
MLA_SCALE = (MLA_NOPE + MLA_ROPE) ** -0.5

HEAD_DIM = 128
ATTN_SCALE = HEAD_DIM ** -0.5
GQA_H = 4
GQA_KVH = 2
SWA_H = 4
SWA_KVH = 2
WINDOW = 128

RET_H = 4
RET_DK = 64
RET_DV = 128
RET_CHUNK = 128
RET_K_SCALE = RET_DK ** -0.5

N_BRANCH = 4
BRANCH_W = 512

N_GROUPS = 4
EXPERTS_PER_GROUP = 8
N_EXPERTS = N_GROUPS * EXPERTS_PER_GROUP
TOP_K = 2
D_EXPERT = 1024
MOE_BLK = 128

IN_SPLITS = (MLA_Q_LORA, MLA_KV_LORA, MLA_ROPE,
             GQA_H * HEAD_DIM, GQA_KVH * HEAD_DIM, GQA_KVH * HEAD_DIM,
             RET_H * RET_DK, RET_H * RET_DK, RET_H * RET_DV, RET_H * RET_DV, RET_H * RET_DV,
             SWA_H * HEAD_DIM, SWA_KVH * HEAD_DIM, SWA_KVH * HEAD_DIM,
             N_BRANCH * D_MODEL)
IN_WIDTH = sum(IN_SPLITS)

kernel_name = 'hybrid_prefix_diffusion_step'


def rmsnorm(x, g):
    xf = x.astype(jnp.float32)
    y = xf * lax.rsqrt(jnp.mean(xf * xf, axis=-1, keepdims=True) + EPS)
    return y.astype(x.dtype) * g


def group_norm(o, g):
    of = o.astype(jnp.float32)
    mu = jnp.mean(of, axis=-1, keepdims=True)
    var = jnp.mean(jnp.square(of - mu), axis=-1, keepdims=True)
    y = ((of - mu) * lax.rsqrt(var + EPS)).reshape(o.shape[0], o.shape[1], -1)
    return y * g


def split_in(z):
    points, acc = [], 0
    for w in IN_SPLITS[:-1]:
        acc += w
        points.append(acc)
    return jnp.split(z, points, axis=-1)


def grid_rope(n_tokens, rot_dim):
    rows = n_tokens // GRID_W
    row = jnp.repeat(jnp.arange(rows), GRID_W).astype(jnp.float32)
    col = (jnp.arange(rows * GRID_W) % GRID_W).astype(jnp.float32)
    axis_dim = rot_dim // 2
    inv = ROPE_THETA ** (-jnp.arange(0, axis_dim, 2, dtype=jnp.float32) / axis_dim)
    ang = jnp.concatenate([row[:, None] * inv, col[:, None] * inv], axis=-1)
    return jnp.cos(ang), jnp.sin(ang)


def apply_rope(x, cos, sin):
    half = x.shape[-1] // 2
    x1, x2 = x[..., :half], x[..., half:]
    c, s = cos[None, :, None, :], sin[None, :, None, :]
    return jnp.concatenate([x1 * c - x2 * s, x1 * s + x2 * c], axis=-1).astype(x.dtype)


def attend(q, k, v, sink=None):
    b, tq, h, dk = q.shape
    kvh, dv = k.shape[2], v.shape[-1]
    g = h // kvh
    nb = tq // QBLK
    qb = q.reshape(b, nb, QBLK, kvh, g, dk).swapaxes(0, 1)

    def block(qi):
        s = jnp.einsum('bqkgd,bskd->bkgqs', qi, k, preferred_element_type=jnp.float32)
        if sink is not None:
            s_snk = jnp.broadcast_to(sink.astype(jnp.float32).reshape(1, kvh, g, 1, 1), s.shape[:-1] + (1,))
            p = jax.nn.softmax(jnp.concatenate([s, s_snk], axis=-1), axis=-1)[..., :-1]
        else:
            p = jax.nn.softmax(s, axis=-1)
        return jnp.einsum('bkgqs,bskd->bqkgd', p.astype(v.dtype), v)

    o = lax.map(block, qb)
    return o.swapaxes(0, 1).reshape(b, tq, h, dv)


def window_attend(q, k, v, ck, cv, sink):
    b, t, h, dk = q.shape
    kvh = k.shape[2]
    g = h // kvh
    nb = t // QBLK
    span = QBLK + 2 * WINDOW
    n_ctx = ck.shape[1]
    pad = ((0, 0), (WINDOW, WINDOW), (0, 0), (0, 0))
    kp, vp = jnp.pad(k, pad), jnp.pad(v, pad)
    qb = q.reshape(b, nb, QBLK, kvh, g, dk).swapaxes(0, 1)
    offs = jnp.arange(span)
    band = jnp.abs(offs[None, :] - WINDOW - jnp.arange(QBLK)[:, None]) <= WINDOW
    sink_col = sink.astype(jnp.float32).reshape(1, kvh, g, 1, 1)

    def block(args):
        n, qi = args
        start = n * QBLK
        kw = lax.dynamic_slice_in_dim(kp, start, span, axis=1)
        vw = lax.dynamic_slice_in_dim(vp, start, span, axis=1)
        kpos = start - WINDOW + offs
        valid = band & ((kpos >= 0) & (kpos < t))[None, :]
        s_loc = jnp.where(valid, jnp.einsum('bqkgd,bskd->bkgqs', qi, kw, preferred_element_type=jnp.float32), NEG_INF)
        s_ctx = jnp.einsum('bqkgd,bskd->bkgqs', qi, ck, preferred_element_type=jnp.float32)
        s_snk = jnp.broadcast_to(sink_col, s_ctx.shape[:-1] + (1,))
        p = jax.nn.softmax(jnp.concatenate([s_loc, s_ctx, s_snk], axis=-1), axis=-1).astype(v.dtype)
        return (jnp.einsum('bkgqs,bskd->bqkgd', p[..., :span], vw)
                + jnp.einsum('bkgqs,bskd->bqkgd', p[..., span:span + n_ctx], cv))

    o = lax.map(block, (jnp.arange(nb), qb))
    return o.swapaxes(0, 1).reshape(b, t, h, v.shape[-1])


def retention_scan(q, k, v, log_gamma, s0):
    b, t, h, dk = q.shape
    dv = v.shape[-1]
    nc = t // RET_CHUNK
    pos = jnp.arange(RET_CHUNK, dtype=jnp.float32)
    diff = pos[:, None] - pos[None, :]
    dmask = jnp.where(diff >= 0, jnp.exp(jnp.maximum(diff, 0.0) * log_gamma[:, None, None]), 0.0)
    q_dec = jnp.exp((pos[:, None] + 1.0) * log_gamma[None, :])
    k_dec = jnp.exp((RET_CHUNK - 1.0 - pos[:, None]) * log_gamma[None, :])
    c_dec = jnp.exp(RET_CHUNK * log_gamma)

    def chunks(x):
        return x.astype(jnp.float32).reshape(b, nc, RET_CHUNK, h, x.shape[-1]).swapaxes(0, 1)

    def step(s, inp):
        qc, kc, vc = inp
        a = jnp.einsum('bihd,bjhd->bhij', qc, kc) * dmask
        o = (jnp.einsum('bhij,bjhe->bihe', a, vc)
             + jnp.einsum('bihd,bhde->bihe', qc * q_dec[None, :, :, None], s))
        s = s * c_dec[None, :, None, None] + jnp.einsum('bjhd,bjhe->bhde', kc * k_dec[None, :, :, None], vc)
        return s, o

    s_final, o = lax.scan(step, s0.astype(jnp.float32), (chunks(q), chunks(k), chunks(v)))
    return o.swapaxes(0, 1).reshape(b, t, h, dv), s_final


def mix_layer(h, mix_w, ctx):
    (w_in, q_norm, kv_norm, w_uq, w_ukv, gq_norm, gk_norm,
     decay_logit, ret_gn, sink, w_branch, w_out) = mix_w
    b, t, _ = h.shape
    latent = ctx is not None
    (mq, mkv, mkr, gq, gk, gv, rq, rk, rv, rgf, rgb, sq, sk, sv, gate_in) = split_in(h @ w_in)
    if latent:
        (c_ckv, c_kr, c_gk, c_gv, c_sk, c_sv, c_ret) = ctx
        cos_r, sin_r = grid_rope(t, MLA_ROPE)
        cos_h, sin_h = grid_rope(t, HEAD_DIM)

    qa = (rmsnorm(mq, q_norm) @ w_uq).reshape(b, t, MLA_H, MLA_NOPE + MLA_ROPE)
    q_nope, q_rope = qa[..., :MLA_NOPE], qa[..., MLA_NOPE:]
    ckv = rmsnorm(mkv, kv_norm)
    if latent:
        q_rope = apply_rope(q_rope, cos_r, sin_r)
        kr_all = jnp.concatenate([c_kr, apply_rope(mkr[:, :, None, :], cos_r, sin_r)[:, :, 0]], axis=1)
        ckv_all = jnp.concatenate([c_ckv, ckv], axis=1)
    else:
        kr_all, ckv_all = mkr, ckv
    n_keys = ckv_all.shape[1]
    kv = (ckv_all @ w_ukv).reshape(b, n_keys, MLA_H, MLA_NOPE + MLA_V)
    k_a = jnp.concatenate([kv[..., :MLA_NOPE],
                           jnp.broadcast_to(kr_all[:, :, None, :], (b, n_keys, MLA_H, MLA_ROPE))], axis=-1)
    q_a = jnp.concatenate([q_nope, q_rope], axis=-1) * MLA_SCALE
    o_a = attend(q_a, k_a, kv[..., MLA_NOPE:]).reshape(b, t, BRANCH_W)

    qg = rmsnorm(gq.reshape(b, t, GQA_H, HEAD_DIM), gq_norm)
    kg = rmsnorm(gk.reshape(b, t, GQA_KVH, HEAD_DIM), gk_norm)
    vg = gv.reshape(b, t, GQA_KVH, HEAD_DIM)
    if latent:
        o_b = attend(apply_rope(qg, cos_h, sin_h) * ATTN_SCALE,
                     jnp.concatenate([c_gk, apply_rope(kg, cos_h, sin_h)], axis=1),
                     jnp.concatenate([c_gv, vg], axis=1))
    else:
        o_b = attend(qg * ATTN_SCALE, kg, vg)
    o_b = o_b.reshape(b, t, BRANCH_W)

    rq_ = rq.reshape(b, t, RET_H, RET_DK)
    rk_ = rk.reshape(b, t, RET_H, RET_DK) * RET_K_SCALE
    rv_ = rv.reshape(b, t, RET_H, RET_DV)
    if latent:
        s_f0, s_b0 = c_ret[:, 0], c_ret[:, 1]
    else:
        s_f0 = s_b0 = jnp.zeros((b, RET_H, RET_DK, RET_DV), jnp.float32)
    log_gamma = jax.nn.log_sigmoid(decay_logit.astype(jnp.float32))
    o_f, s_f = retention_scan(rq_, rk_, rv_, log_gamma[0], s_f0)
    o_bw, s_b = retention_scan(rq_[:, ::-1], rk_[:, ::-1], rv_[:, ::-1], log_gamma[1], s_b0)
    o_c = (jax.nn.silu(rgf) * group_norm(o_f, ret_gn[0])
           + jax.nn.silu(rgb) * group_norm(o_bw[:, ::-1], ret_gn[1])).astype(h.dtype)

    qs = sq.reshape(b, t, SWA_H, HEAD_DIM)
    ks = sk.reshape(b, t, SWA_KVH, HEAD_DIM)
    vs = sv.reshape(b, t, SWA_KVH, HEAD_DIM)
    if latent:
        o_d = window_attend(apply_rope(qs, cos_h, sin_h) * ATTN_SCALE, apply_rope(ks, cos_h, sin_h), vs,
                            c_sk, c_sv, sink)
    else:
        o_d = attend(qs * ATTN_SCALE, ks, vs, sink)
    o_d = o_d.reshape(b, t, BRANCH_W)

    branches = jnp.stack([o_a, o_b, o_c, o_d], axis=2)
    gates = jax.nn.sigmoid(gate_in.reshape(b, t, N_BRANCH, D_MODEL))
    merged = jnp.einsum('btnd,btnd->btd', gates, jnp.einsum('btnw,nwd->btnd', branches, w_branch))
    out = merged @ w_out
    ctx_out = None if latent else (ckv, mkr, kg, vg, ks, vs, jnp.stack([s_f, s_b], axis=1))
    return out, ctx_out


def moe_ffn(h, rg_w, rg_b, re_w, re_b, w_gate, w_up, w_down):
    n, d = h.shape
    g_logits = jnp.einsum('nd,dg->ng', h, rg_w, preferred_element_type=jnp.float32) + rg_b
    grp = jnp.argmax(g_logits, axis=-1)
    rows = jnp.arange(n)
    p_grp = jax.nn.softmax(g_logits, axis=-1)[rows, grp][:, None]
    e_logits = (jnp.einsum('nd,de->ne', h, re_w, preferred_element_type=jnp.float32) + re_b)
    e_in = e_logits.reshape(n, N_GROUPS, EXPERTS_PER_GROUP)[rows, grp]
    top_v, top_i = lax.top_k(e_in, TOP_K)
    wts = p_grp * jax.nn.softmax(top_v, axis=-1)
    expert = (grp[:, None] * EXPERTS_PER_GROUP + top_i).reshape(-1).astype(jnp.int32)
    m = n * TOP_K
    order = jnp.argsort(expert)
    sorted_e = expert[order]
    counts = jnp.zeros((N_EXPERTS,), jnp.int32).at[expert].add(1)
    start = jnp.cumsum(counts) - counts
    padded = (counts + MOE_BLK - 1) // MOE_BLK * MOE_BLK
    pad_end = jnp.cumsum(padded)
    pad_start = pad_end - padded
    dest_sorted = pad_start[sorted_e] + jnp.arange(m, dtype=jnp.int32) - start[sorted_e]
    dest = jnp.zeros((m,), jnp.int32).at[order].set(dest_sorted)
    n_blocks = (m + N_EXPERTS * (MOE_BLK - 1) + MOE_BLK - 1) // MOE_BLK
    cap = n_blocks * MOE_BLK
    tok = jnp.arange(m) // TOP_K
    xp = jnp.zeros((cap, d), h.dtype).at[dest].set(h[tok])
    blk_e = jnp.minimum(jnp.searchsorted(pad_end, jnp.arange(n_blocks, dtype=jnp.int32) * MOE_BLK, side='right'),
                        N_EXPERTS - 1)

    def expert_block(args):
        xb, e = args
        return (jax.nn.silu(xb @ w_gate[e]) * (xb @ w_up[e])) @ w_down[e]

    yp = lax.map(expert_block, (xp.reshape(n_blocks, MOE_BLK, d), blk_e)).reshape(cap, d)
    y = yp[dest].reshape(n, TOP_K, d)
    return jnp.einsum('nkd,nk->nd', y, wts.astype(y.dtype))


def trunk_layer(x, cond, ada_w, ada_b, n_mix, n_ffn, mix_w, moe_w, ctx):
    mod = (jax.nn.silu(cond) @ ada_w + ada_b)[:, None, :]
    sh_a, sc_a, g_a, sh_m, sc_m, g_m = jnp.split(mod, 6, axis=-1)
    h = rmsnorm(x, n_mix) * (1.0 + sc_a) + sh_a
    m, ctx_out = mix_layer(h, mix_w, ctx)
    x = x + g_a * m
    h = rmsnorm(x, n_ffn) * (1.0 + sc_m) + sh_m
    b, t, d = h.shape
    x = x + g_m * moe_ffn(h.reshape(b * t, d), *moe_w).reshape(b, t, d)
    return x, ctx_out


def setup_inputs(seed: int = 0) -> dict:
    key = jax.random.key(seed)
    keys = iter(jax.random.split(key, 48))

    def nrm(shape, scale=1.0):
        return jax.random.normal(next(keys), shape, jnp.float32) * scale

    def gain(shape):
        return 1.0 + nrm(shape, 0.05)

    gamma = 1.0 - 2.0 ** (-5.0 - np.arange(RET_H, dtype=np.float32))
    decay_logit0 = jnp.asarray(np.log(gamma / (1.0 - gamma)), dtype=jnp.float32)
    d_inv = D_MODEL ** -0.5
    return {
        'x_prompt': nrm((BATCH, SEQ, D_MODEL)),
        'x_sample': nrm((DEC_BATCH, DEC_SEQ, D_MODEL)),
        'c': nrm((DEC_BATCH, D_MODEL)),
        'cache_mla_ckv': nrm((DEC_BATCH, DEPTH, PAST_LEN, MLA_KV_LORA)),
        'cache_mla_krope': nrm((DEC_BATCH, DEPTH, PAST_LEN, MLA_ROPE)),
        'cache_gqa_k': nrm((DEC_BATCH, DEPTH, PAST_LEN, GQA_KVH, HEAD_DIM)),
        'cache_gqa_v': nrm((DEC_BATCH, DEPTH, PAST_LEN, GQA_KVH, HEAD_DIM)),
        'cache_swa_k': nrm((DEC_BATCH, DEPTH, PAST_LEN, SWA_KVH, HEAD_DIM)),
        'cache_swa_v': nrm((DEC_BATCH, DEPTH, PAST_LEN, SWA_KVH, HEAD_DIM)),
        'state_ret': nrm((DEC_BATCH, DEPTH, 2, RET_H, RET_DK, RET_DV), 0.5),
        'c_ctx': nrm((D_MODEL,)),
        'ada_w': nrm((DEPTH, D_MODEL, 6 * D_MODEL), 0.5 * d_inv),
        'ada_b': nrm((DEPTH, 6 * D_MODEL), 0.02),
        'norm_mix': gain((DEPTH, D_MODEL)),
        'norm_ffn': gain((DEPTH, D_MODEL)),
        'w_in': nrm((DEPTH, D_MODEL, IN_WIDTH), d_inv),
        'mla_q_norm': gain((DEPTH, MLA_Q_LORA)),
        'mla_kv_norm': gain((DEPTH, MLA_KV_LORA)),
        'mla_w_uq': nrm((DEPTH, MLA_Q_LORA, MLA_H * (MLA_NOPE + MLA_ROPE)), MLA_Q_LORA ** -0.5),
        'mla_w_ukv': nrm((DEPTH, MLA_KV_LORA, MLA_H * (MLA_NOPE + MLA_V)), MLA_KV_LORA ** -0.5),
        'gqa_q_norm': gain((DEPTH, HEAD_DIM)),
        'gqa_k_norm': gain((DEPTH, HEAD_DIM)),
        'ret_decay_logit': decay_logit0 + nrm((DEPTH, 2, RET_H), 0.1),
        'ret_gn': gain((DEPTH, 2, RET_H * RET_DV)),
        'swa_sink': nrm((DEPTH, SWA_H), 0.5),
        'w_branch': nrm((DEPTH, N_BRANCH, BRANCH_W, D_MODEL), BRANCH_W ** -0.5),
        'w_out': nrm((DEPTH, D_MODEL, D_MODEL), d_inv),
        'router_group_w': nrm((DEPTH, D_MODEL, N_GROUPS), d_inv),
        'router_group_b': nrm((DEPTH, N_GROUPS), 0.01),
        'router_expert_w': nrm((DEPTH, D_MODEL, N_EXPERTS), d_inv),
        'router_expert_b': nrm((DEPTH, N_EXPERTS), 0.01),
        'moe_w_gate': nrm((DEPTH, N_EXPERTS, D_MODEL, D_EXPERT), d_inv),
        'moe_w_up': nrm((DEPTH, N_EXPERTS, D_MODEL, D_EXPERT), d_inv),
        'moe_w_down': nrm((DEPTH, N_EXPERTS, D_EXPERT, D_MODEL), D_EXPERT ** -0.5),
        'final_norm': gain((D_MODEL,)),
    }


def reference(x_prompt, x_sample, c, cache_mla_ckv, cache_mla_krope, cache_gqa_k, cache_gqa_v,
              cache_swa_k, cache_swa_v, state_ret, c_ctx, ada_w, ada_b, norm_mix, norm_ffn, w_in,
              mla_q_norm, mla_kv_norm, mla_w_uq, mla_w_ukv, gqa_q_norm, gqa_k_norm, ret_decay_logit, ret_gn,
              swa_sink, w_branch, w_out, router_group_w, router_group_b, router_expert_w, router_expert_b,
              moe_w_gate, moe_w_up, moe_w_down, final_norm):
    def layer_params(l):
        mix_w = (w_in[l], mla_q_norm[l], mla_kv_norm[l], mla_w_uq[l], mla_w_ukv[l], gqa_q_norm[l],
                 gqa_k_norm[l], ret_decay_logit[l], ret_gn[l], swa_sink[l], w_branch[l], w_out[l])
        moe_w = (router_group_w[l], router_group_b[l], router_expert_w[l], router_expert_b[l],
                 moe_w_gate[l], moe_w_up[l], moe_w_down[l])
        return ada_w[l], ada_b[l], norm_mix[l], norm_ffn[l], mix_w, moe_w

    xp = x_prompt
    cond_ctx = c_ctx[None, :]
    ckv_l, kr_l, gk_l, gv_l, sk_l, sv_l, ret_l = [], [], [], [], [], [], []
    for l in range(DEPTH):
        xp, st = trunk_layer(xp, cond_ctx, *layer_params(l), None)
        ckv_l.append(st[0])
        kr_l.append(st[1])
        gk_l.append(st[2])
        gv_l.append(st[3])
        sk_l.append(st[4])
        sv_l.append(st[5])
        ret_l.append(st[6])
    y_prompt = rmsnorm(xp, final_norm)
    new_mla_ckv = jnp.stack(ckv_l, axis=1)
    new_mla_krope = jnp.stack(kr_l, axis=1)
    new_gqa_k = jnp.stack(gk_l, axis=1)
    new_gqa_v = jnp.stack(gv_l, axis=1)
    new_swa_k = jnp.stack(sk_l, axis=1)
    new_swa_v = jnp.stack(sv_l, axis=1)
    new_state_ret = jnp.stack(ret_l, axis=1)

    xs = x_sample
    for l in range(DEPTH):
        ctx = (cache_mla_ckv[:, l], cache_mla_krope[:, l], cache_gqa_k[:, l], cache_gqa_v[:, l],
               cache_swa_k[:, l], cache_swa_v[:, l], state_ret[:, l])
        xs, _ = trunk_layer(xs, c, *layer_params(l), ctx)
    y_sample = rmsnorm(xs, final_norm)

    return (y_prompt, y_sample, new_mla_ckv, new_mla_krope, new_gqa_k, new_gqa_v, new_swa_k, new_swa_v, new_state_ret)
```

```python
import functools

import numpy as np
import jax
import jax.numpy as jnp
from jax import lax
from jax.experimental import pallas as pl
from jax.experimental.pallas import tpu as pltpu

F32 = jnp.float32
BF16 = jnp.bfloat16

D_MODEL = 2048
BATCH = 16
SEQ = 256
DEPTH = 4
DEC_BATCH = 2
DEC_SEQ = 4096
PAST_LEN = 256
GRID_W = 64
ROPE_THETA = 10000.0
EPS = 1e-6
QBLK = 128
NEG_INF = -1e30

MLA_H = 4
MLA_Q_LORA = 512
MLA_KV_LORA = 256
MLA_NOPE = 128
MLA_ROPE = 64
MLA_V = 128
MLA_SCALE = (MLA_NOPE + MLA_ROPE) ** -0.5
MLA_DK = MLA_NOPE + MLA_ROPE

HEAD_DIM = 128
ATTN_SCALE = HEAD_DIM ** -0.5
GQA_H = 4
GQA_KVH = 2
SWA_H = 4
SWA_KVH = 2
WINDOW = 128

RET_H = 4
RET_DK = 64
RET_DV = 128
RET_CHUNK = 128
RET_K_SCALE = RET_DK ** -0.5

N_BRANCH = 4
BRANCH_W = 512

N_GROUPS = 4
EXPERTS_PER_GROUP = 8
N_EXPERTS = N_GROUPS * EXPERTS_PER_GROUP
TOP_K = 2
D_EXPERT = 1024
MOE_BLK = 128

N_CTX = BATCH * SEQ
N_LAT = DEC_BATCH * DEC_SEQ
N_TOK = N_CTX + N_LAT
N_COND = 8

Z_SMALL = N_BRANCH * D_MODEL
Z_OFF = dict(mq=0, gq=512, sq=1024, rv=1536, rgf=2048, rgb=2560, mkv=3072, gk=3328, gv=3584,
             rq=3840, rk=4096, sk=4352, sv=4608, mkr=4864)
Z_WIDTH = Z_SMALL + 5120
IN_SPLITS = dict(mq=512, mkv=256, mkr=64, gq=512, gk=256, gv=256, rq=256, rk=256, rv=512, rgf=512, rgb=512,
                 sq=512, sk=256, sv=256, gate=N_BRANCH * D_MODEL)

MOE_ASSIGN = N_TOK * TOP_K
MOE_NBLK = (MOE_ASSIGN + N_EXPERTS * (MOE_BLK - 1) + MOE_BLK - 1) // MOE_BLK
MOE_CAP = MOE_NBLK * MOE_BLK
ROUTER_W = 128

VMEM_LIMIT = 56 * 1024 * 1024


def _params(*sem):
    return pltpu.CompilerParams(dimension_semantics=sem, vmem_limit_bytes=VMEM_LIMIT)


def _cond_row(i, tm):
    start = i * tm
    return jnp.where(start < N_CTX, 0, 1 + (start - N_CTX) // DEC_SEQ)


def _rms(x):
    return x * lax.rsqrt(jnp.mean(x * x, axis=-1, keepdims=True) + EPS)


def _ada_kernel(c_ref, w_ref, b_ref, o_ref):
    s = jax.nn.silu(c_ref[...]).astype(BF16)
    o_ref[0] = jnp.dot(s, w_ref[0].astype(BF16), preferred_element_type=F32) + b_ref[0]


def _ada(cond, ada_w, ada_b):
    tn = 1024
    n_out = 6 * D_MODEL
    return pl.pallas_call(
        _ada_kernel,
        grid=(DEPTH, n_out // tn),
        in_specs=[pl.BlockSpec((N_COND, D_MODEL), lambda l, j: (0, 0)),
                  pl.BlockSpec((1, D_MODEL, tn), lambda l, j: (l, 0, j)),
                  pl.BlockSpec((1, 1, tn), lambda l, j: (l, 0, j))],
        out_specs=pl.BlockSpec((1, N_COND, tn), lambda l, j: (l, 0, j)),
        out_shape=jax.ShapeDtypeStruct((DEPTH, N_COND, n_out), F32),
        compiler_params=_params("parallel", "parallel"),
        name="ada_mod",
    )(cond, ada_w, ada_b.reshape(DEPTH, 1, n_out))


def _in_kernel(x_ref, mod_ref, g_ref, w_ref, z_ref, h_ref):
    @pl.when(pl.program_id(1) == 0)
    def _():
        m = mod_ref[0]
        y = _rms(x_ref[...]) * g_ref[...]
        h_ref[...] = (y * (1.0 + m[:, D_MODEL:2 * D_MODEL]) + m[:, 0:D_MODEL]).astype(BF16)

    z_ref[...] = jnp.dot(h_ref[...], w_ref[...], preferred_element_type=F32)


def _in_proj(x, mod, gain, w):
    tm, tn = 1024, 1024
    return pl.pallas_call(
        _in_kernel,
        grid=(N_TOK // tm, Z_WIDTH // tn),
        in_specs=[pl.BlockSpec((tm, D_MODEL), lambda i, j: (i, 0)),
                  pl.BlockSpec((1, 1, 6 * D_MODEL), lambda i, j: (_cond_row(i, tm), 0, 0)),
                  pl.BlockSpec((1, D_MODEL), lambda i, j: (0, 0)),
                  pl.BlockSpec((D_MODEL, tn), lambda i, j: (0, j))],
        out_specs=pl.BlockSpec((tm, tn), lambda i, j: (i, j)),
        out_shape=jax.ShapeDtypeStruct((N_TOK, Z_WIDTH), F32),
        scratch_shapes=[pltpu.VMEM((tm, D_MODEL), BF16)],
        compiler_params=_params("parallel", "arbitrary"),
        name="in_proj",
    )(x, mod, gain, w)


def _rope128(x, cos, sin):
    return x * cos + pltpu.roll(x, HEAD_DIM // 2, 1) * sin


def _mla_kv(ckv, kr, wuk_ref, wuv_ref, ka_ref, va_ref):
    cb = ckv.astype(BF16)
    kn = jnp.dot(cb, wuk_ref[...], preferred_element_type=F32)
    vv = jnp.dot(cb, wuv_ref[...], preferred_element_type=F32)
    for h in range(MLA_H):
        ka_ref[h, :, 0:MLA_NOPE] = kn[:, h * MLA_NOPE:(h + 1) * MLA_NOPE].astype(BF16)
        ka_ref[h, :, MLA_NOPE:MLA_DK] = kr.astype(BF16)
        va_ref[h] = vv[:, h * MLA_V:(h + 1) * MLA_V].astype(BF16)


def _prep_kernel(*refs, latent):
    (mq_ref, gq_ref, sq_ref, mkv_ref, gk_ref, gv_ref, sk_ref, sv_ref, mkr_ref,
     qn_ref, kvn_ref, gqn_ref, gkn_ref, wqn_ref, wqr_ref, wqs_ref, wuk_ref, wuv_ref) = refs[:18]
    refs = refs[18:]
    if latent:
        cos_ref, sin_ref, c64_ref, s64_ref = refs[:4]
        refs = refs[4:]
        cos, sin = cos_ref[...], sin_ref[...]
        c64, s64 = c64_ref[...], s64_ref[...]
    qa_ref, ka_ref, va_ref, qb_ref, kb_ref, vb_ref, qd_ref, kd_ref, vd_ref = refs[:9]
    refs = refs[9:]

    hn = (_rms(mq_ref[...]) * qn_ref[...]).astype(BF16)
    q_nope = jnp.dot(hn, wqn_ref[...], preferred_element_type=F32)
    q_rope = jnp.dot(hn, wqr_ref[...], preferred_element_type=F32)
    if latent:
        q_swap = jnp.dot(hn, wqs_ref[...], preferred_element_type=F32)
    for h in range(MLA_H):
        qa_ref[h, :, 0:MLA_NOPE] = (q_nope[:, h * MLA_NOPE:(h + 1) * MLA_NOPE] * MLA_SCALE).astype(BF16)
        qr = q_rope[:, h * MLA_ROPE:(h + 1) * MLA_ROPE]
        if latent:
            qr = qr * c64 + q_swap[:, h * MLA_ROPE:(h + 1) * MLA_ROPE] * s64
        qa_ref[h, :, MLA_NOPE:MLA_DK] = (qr * MLA_SCALE).astype(BF16)
    ckv = _rms(mkv_ref[...]) * kvn_ref[...]
    kr = mkr_ref[:, 0:MLA_ROPE]
    if latent:
        kr = kr * c64 + mkr_ref[:, MLA_ROPE:2 * MLA_ROPE] * s64
    _mla_kv(ckv, kr, wuk_ref, wuv_ref, ka_ref, va_ref)

    gq = gq_ref[...]
    for h in range(GQA_H):
        q = _rms(gq[:, h * HEAD_DIM:(h + 1) * HEAD_DIM]) * gqn_ref[...]
        if latent:
            q = _rope128(q, cos, sin)
        qb_ref[h] = (q * ATTN_SCALE).astype(BF16)
    gk = gk_ref[...]
    gv = gv_ref[...]
    kg = []
    for h in range(GQA_KVH):
        k = _rms(gk[:, h * HEAD_DIM:(h + 1) * HEAD_DIM]) * gkn_ref[...]
        kg.append(k)
        if latent:
            k = _rope128(k, cos, sin)
        kb_ref[h] = k.astype(BF16)
        vb_ref[h] = gv[:, h * HEAD_DIM:(h + 1) * HEAD_DIM].astype(BF16)

    sq = sq_ref[...]
    for h in range(SWA_H):
        q = sq[:, h * HEAD_DIM:(h + 1) * HEAD_DIM]
        if latent:
            q = _rope128(q, cos, sin)
        qd_ref[h] = (q * ATTN_SCALE).astype(BF16)
    sk = sk_ref[...]
    sv = sv_ref[...]
    for h in range(SWA_KVH):
        k = sk[:, h * HEAD_DIM:(h + 1) * HEAD_DIM]
        if latent:
            k = _rope128(k, cos, sin)
        kd_ref[h] = k.astype(BF16)
        vd_ref[h] = sv[:, h * HEAD_DIM:(h + 1) * HEAD_DIM].astype(BF16)

    if not latent:
        ckv_ref, kr_ref, kg_ref = refs
        ckv_ref[...] = ckv
        kr_ref[...] = kr
        for h in range(GQA_KVH):
            kg_ref[:, h * HEAD_DIM:(h + 1) * HEAD_DIM] = kg[h]


def _prep(z, lw, tabs, latent):
    tm = 512
    rows = N_LAT if latent else N_CTX
    roff = (N_CTX if latent else 0) // tm
    n_t = DEC_SEQ // tm

    def zc(name, width):
        cb = (Z_SMALL + Z_OFF[name]) // width
        return pl.BlockSpec((tm, width), lambda i: (i + roff, cb))

    def full(a):
        nd = a.ndim
        return pl.BlockSpec(a.shape, lambda i: (0,) * nd)

    weights = [lw["q_norm"], lw["kv_norm"], lw["gq_norm"], lw["gk_norm"],
               lw["w_uq_nope"], lw["w_uq_rope"], lw["w_uq_swap"], lw["w_uk"], lw["w_uv"]]
    in_specs = [zc("mq", 512), zc("gq", 512), zc("sq", 512), zc("mkv", 256), zc("gk", 256), zc("gv", 256),
                zc("sk", 256), zc("sv", 256), zc("mkr", 128)] + [full(a) for a in weights]
    args = [z] * 9 + weights
    if latent:
        in_specs += [pl.BlockSpec((tm, HEAD_DIM), lambda i: (i % n_t, 0))] * 2
        in_specs += [pl.BlockSpec((tm, MLA_ROPE), lambda i: (i % n_t, 0))] * 2
        args += list(tabs)

    def heads(n, d):
        return (pl.BlockSpec((n, tm, d), lambda i: (0, i, 0)), jax.ShapeDtypeStruct((n, rows, d), BF16))

    outs = [heads(MLA_H, MLA_DK), heads(MLA_H, MLA_DK), heads(MLA_H, MLA_V),
            heads(GQA_H, HEAD_DIM), heads(GQA_KVH, HEAD_DIM), heads(GQA_KVH, HEAD_DIM),
            heads(SWA_H, HEAD_DIM), heads(SWA_KVH, HEAD_DIM), heads(SWA_KVH, HEAD_DIM)]
    if not latent:
        for d in (MLA_KV_LORA, MLA_ROPE, GQA_KVH * HEAD_DIM):
            outs.append((pl.BlockSpec((tm, d), lambda i: (i, 0)), jax.ShapeDtypeStruct((rows, d), F32)))
    return pl.pallas_call(
        functools.partial(_prep_kernel, latent=latent),
        grid=(rows // tm,),
        in_specs=in_specs,
        out_specs=[o[0] for o in outs],
        out_shape=[o[1] for o in outs],
        compiler_params=_params("parallel"),
        name="prep_latent" if latent else "prep_context",
    )(*args)


def _mla_ctx_kernel(ckv_ref, kr_ref, wuk_ref, wuv_ref, ka_ref, va_ref):
    _mla_kv(ckv_ref[...], kr_ref[...], wuk_ref, wuv_ref, ka_ref, va_ref)


def _mla_ctx(ckv, kr, lw):
    rows = ckv.shape[0]
    return pl.pallas_call(
        _mla_ctx_kernel,
        out_shape=[jax.ShapeDtypeStruct((MLA_H, rows, MLA_DK), BF16),
                   jax.ShapeDtypeStruct((MLA_H, rows, MLA_V), BF16)],
        name="mla_ctx_kv",
    )(ckv, kr, lw["w_uk"], lw["w_uv"])


def _attn_update(g, q, k, v, m_sc, l_sc, acc_sc):
    s = lax.dot_general(q, k, (((1,), (1,)), ((), ())), preferred_element_type=F32)
    m_prev = m_sc[g]
    m_new = jnp.maximum(m_prev, jnp.max(s, axis=-1, keepdims=True))
    p = jnp.exp(s - m_new)
    alpha = jnp.exp(m_prev - m_new)
    l_sc[g] = alpha * l_sc[g] + jnp.sum(p, axis=-1, keepdims=True)
    acc_sc[g] = alpha * acc_sc[g] + jnp.dot(p.astype(BF16), v, preferred_element_type=F32)
    m_sc[g] = m_new


def _flash_kernel(*refs, groups, has_ctx, has_sink):
    if has_sink:
        sink_ref, refs = refs[0], refs[1:]
    q_ref, k_ref, v_ref = refs[:3]
    refs = refs[3:]
    if has_ctx:
        k0_ref, v0_ref = refs[:2]
        refs = refs[2:]
    o_ref, m_sc, l_sc, acc_sc = refs
    ki = pl.program_id(3)
    last = pl.num_programs(3) - 1
    head0 = pl.program_id(1) * groups
    dv = v_ref.shape[-1]

    @pl.when(ki == 0)
    def _():
        m_sc[...] = jnp.full(m_sc.shape, NEG_INF, F32)
        l_sc[...] = jnp.zeros(l_sc.shape, F32)
        acc_sc[...] = jnp.zeros(acc_sc.shape, F32)
        if has_ctx:
            for g in range(groups):
                _attn_update(g, q_ref[g], k0_ref[0], v0_ref[0], m_sc, l_sc, acc_sc)

    for g in range(groups):
        _attn_update(g, q_ref[g], k_ref[0], v_ref[0], m_sc, l_sc, acc_sc)

    @pl.when(ki == last)
    def _():
        for g in range(groups):
            l = l_sc[g]
            if has_sink:
                l = l + jnp.exp(sink_ref[head0 + g] - m_sc[g])
            o_ref[:, g * dv:(g + 1) * dv] = (acc_sc[g] / l).astype(o_ref.dtype)


def _flash(q, k, v, *, batch, t_q, t_k, tq, tk, ctx=None, sink=None):
    n_h, _, dk = q.shape
    kvh, _, dv = v.shape
    groups = n_h // kvh
    nq, nk = t_q // tq, t_k // tk
    in_specs, args = [], []
    if sink is not None:
        in_specs.append(pl.BlockSpec(memory_space=pltpu.SMEM))
        args.append(sink)
    in_specs += [pl.BlockSpec((groups, tq, dk), lambda b, h, i, j: (h, b * nq + i, 0)),
                 pl.BlockSpec((1, tk, dk), lambda b, h, i, j: (h, b * nk + j, 0)),
                 pl.BlockSpec((1, tk, dv), lambda b, h, i, j: (h, b * nk + j, 0))]
    args += [q, k, v]
    if ctx is not None:
        in_specs += [pl.BlockSpec((1, PAST_LEN, dk), lambda b, h, i, j: (h, b, 0)),
                     pl.BlockSpec((1, PAST_LEN, dv), lambda b, h, i, j: (h, b, 0))]
        args += list(ctx)
    return pl.pallas_call(
        functools.partial(_flash_kernel, groups=groups, has_ctx=ctx is not None, has_sink=sink is not None),
        grid=(batch, kvh, nq, nk),
        in_specs=in_specs,
        out_specs=pl.BlockSpec((tq, groups * dv), lambda b, h, i, j: (b * nq + i, h)),
        out_shape=jax.ShapeDtypeStruct((batch * t_q, n_h * dv), BF16),
        scratch_shapes=[pltpu.VMEM((groups, tq, 1), F32), pltpu.VMEM((groups, tq, 1), F32),
                        pltpu.VMEM((groups, tq, dv), F32)],
        compiler_params=_params("parallel", "parallel", "parallel", "arbitrary"),
        name="flash_attention",
    )(*args)


def _window_kernel(sink_ref, q_ref, kp_ref, kc_ref, kn_ref, vp_ref, vc_ref, vn_ref, k0_ref, v0_ref, o_ref, *,
                   groups):
    n = pl.program_id(2)
    nb = pl.num_programs(2)
    head0 = pl.program_id(1) * groups
    row = lax.broadcasted_iota(jnp.int32, (QBLK, QBLK), 0)
    col = lax.broadcasted_iota(jnp.int32, (QBLK, QBLK), 1)
    keep_prev = (col >= row) & (n > 0)
    keep_next = (col <= row) & (n < nb - 1)
    dn = (((1,), (1,)), ((), ()))
    for g in range(groups):
        q = q_ref[g]
        s_p = jnp.where(keep_prev, lax.dot_general(q, kp_ref[0], dn, preferred_element_type=F32), NEG_INF)
        s_c = lax.dot_general(q, kc_ref[0], dn, preferred_element_type=F32)
        s_n = jnp.where(keep_next, lax.dot_general(q, kn_ref[0], dn, preferred_element_type=F32), NEG_INF)
        s_0 = lax.dot_general(q, k0_ref[0], dn, preferred_element_type=F32)
        snk = sink_ref[head0 + g]
        m = jnp.maximum(jnp.maximum(jnp.max(s_p, axis=-1, keepdims=True), jnp.max(s_c, axis=-1, keepdims=True)),
                        jnp.maximum(jnp.max(s_n, axis=-1, keepdims=True), jnp.max(s_0, axis=-1, keepdims=True)))
        m = jnp.maximum(m, snk)
        p_p, p_c, p_n, p_0 = jnp.exp(s_p - m), jnp.exp(s_c - m), jnp.exp(s_n - m), jnp.exp(s_0 - m)
        den = (jnp.sum(p_p, axis=-1, keepdims=True) + jnp.sum(p_c, axis=-1, keepdims=True)
               + jnp.sum(p_n, axis=-1, keepdims=True) + jnp.sum(p_0, axis=-1, keepdims=True) + jnp.exp(snk - m))
        acc = (jnp.dot(p_p.astype(BF16), vp_ref[0], preferred_element_type=F32)
               + jnp.dot(p_c.astype(BF16), vc_ref[0], preferred_element_type=F32)
               + jnp.dot(p_n.astype(BF16), vn_ref[0], preferred_element_type=F32)
               + jnp.dot(p_0.astype(BF16), v0_ref[0], preferred_element_type=F32))
        o_ref[:, g * HEAD_DIM:(g + 1) * HEAD_DIM] = (acc / den).astype(o_ref.dtype)


def _window(q, k, v, k0, v0, sink):
    n_h, kvh = q.shape[0], k.shape[0]
    groups = n_h // kvh
    nb = DEC_SEQ // QBLK

    def kv_spec(shift):
        return pl.BlockSpec((1, QBLK, HEAD_DIM),
                            lambda b, h, n: (h, b * nb + jnp.clip(n + shift, 0, nb - 1), 0))

    ctx_spec = pl.BlockSpec((1, PAST_LEN, HEAD_DIM), lambda b, h, n: (h, b, 0))
    return pl.pallas_call(
        functools.partial(_window_kernel, groups=groups),
        grid=(DEC_BATCH, kvh, nb),
        in_specs=[pl.BlockSpec(memory_space=pltpu.SMEM),
                  pl.BlockSpec((groups, QBLK, HEAD_DIM), lambda b, h, n: (h, b * nb + n, 0)),
                  kv_spec(-1), kv_spec(0), kv_spec(1), kv_spec(-1), kv_spec(0), kv_spec(1), ctx_spec, ctx_spec],
        out_specs=pl.BlockSpec((QBLK, groups * HEAD_DIM), lambda b, h, n: (b * nb + n, h)),
        out_shape=jax.ShapeDtypeStruct((N_LAT, n_h * HEAD_DIM), BF16),
        compiler_params=_params("parallel", "parallel", "parallel"),
        name="window_attention",
    )(sink, q, k, k, k, v, v, v, k0, v0)


def _ret_kernel(*refs, reverse, has_prev):
    lg_ref, q_ref, k_ref, v_ref, gate_ref, gn_ref, s0_ref = refs[:7]
    refs = refs[7:]
    if has_prev:
        prev_ref, refs = refs[0], refs[1:]
    o_ref, sfin_ref, s_sc = refs
    c = pl.program_id(1)

    @pl.when(c == 0)
    def _():
        s_sc[...] = s0_ref[0]

    cs = RET_CHUNK
    row = lax.broadcasted_iota(jnp.int32, (cs, cs), 0).astype(F32)
    col = lax.broadcasted_iota(jnp.int32, (cs, cs), 1).astype(F32)
    pos = lax.broadcasted_iota(jnp.int32, (cs, 1), 0).astype(F32)
    diff = (col - row) if reverse else (row - col)
    scan_pos = (cs - 1.0 - pos) if reverse else pos
    q_all, k_all, v_all, gate_all = q_ref[...], k_ref[...], v_ref[...], gate_ref[...]
    for h in range(RET_H):
        lg = lg_ref[h]
        dmask = jnp.where(diff >= 0, jnp.exp(jnp.maximum(diff, 0.0) * lg), 0.0)
        q_dec = jnp.exp((scan_pos + 1.0) * lg)
        k_dec = jnp.exp((cs - 1.0 - scan_pos) * lg)
        c_dec = jnp.exp(cs * lg)
        q = q_all[:, h * RET_DK:(h + 1) * RET_DK]
        k = k_all[:, h * RET_DK:(h + 1) * RET_DK] * RET_K_SCALE
        v = v_all[:, h * RET_DV:(h + 1) * RET_DV].astype(BF16)
        s = s_sc[h]
        a = lax.dot_general(q.astype(BF16), k.astype(BF16), (((1,), (1,)), ((), ())),
                            preferred_element_type=F32) * dmask
        o = (jnp.dot(a.astype(BF16), v, preferred_element_type=F32)
             + jnp.dot((q * q_dec).astype(BF16), s.astype(BF16), preferred_element_type=F32))
        kd_t = jnp.transpose(k * k_dec).astype(BF16)
        s_sc[h] = s * c_dec + jnp.dot(kd_t, v, preferred_element_type=F32)
        mu = jnp.mean(o, axis=-1, keepdims=True)
        var = jnp.mean(jnp.square(o - mu), axis=-1, keepdims=True)
        y = (o - mu) * lax.rsqrt(var + EPS) * gn_ref[:, h * RET_DV:(h + 1) * RET_DV]
        y = jax.nn.silu(gate_all[:, h * RET_DV:(h + 1) * RET_DV]) * y
        if has_prev:
            y = prev_ref[:, h * RET_DV:(h + 1) * RET_DV] + y
        o_ref[:, h * RET_DV:(h + 1) * RET_DV] = y.astype(o_ref.dtype)

    @pl.when(c == pl.num_programs(1) - 1)
    def _():
        sfin_ref[0] = s_sc[...]


def _retention(z, log_gamma, gn, s0, prev, *, latent, reverse):
    batch, t = (DEC_BATCH, DEC_SEQ) if latent else (BATCH, SEQ)
    nc = t // RET_CHUNK
    roff = (N_CTX if latent else 0) // RET_CHUNK
    gate_name = "rgb" if reverse else "rgf"

    def blk(b, c):
        return b * nc + (nc - 1 - c if reverse else c)

    def zc(name, width):
        cb = (Z_SMALL + Z_OFF[name]) // width
        return pl.BlockSpec((RET_CHUNK, width), lambda b, c: (roff + blk(b, c), cb))

    w = RET_H * RET_DV
    in_specs = [pl.BlockSpec(memory_space=pltpu.SMEM), zc("rq", 256), zc("rk", 256), zc("rv", 512),
                zc(gate_name, 512), pl.BlockSpec((1, w), lambda b, c: (0, 0)),
                pl.BlockSpec((1, RET_H, RET_DK, RET_DV), lambda b, c: (b, 0, 0, 0))]
    args = [log_gamma, z, z, z, z, gn, s0]
    if prev is not None:
        in_specs.append(pl.BlockSpec((RET_CHUNK, w), lambda b, c: (blk(b, c), 0)))
        args.append(prev)
    return pl.pallas_call(
        functools.partial(_ret_kernel, reverse=reverse, has_prev=prev is not None),
        grid=(batch, nc),
        in_specs=in_specs,
        out_specs=[pl.BlockSpec((RET_CHUNK, w), lambda b, c: (blk(b, c), 0)),
                   pl.BlockSpec((1, RET_H, RET_DK, RET_DV), lambda b, c: (b, 0, 0, 0))],
        out_shape=[jax.ShapeDtypeStruct((batch * t, w), BF16 if reverse else F32),
                   jax.ShapeDtypeStruct((batch, RET_H, RET_DK, RET_DV), F32)],
        scratch_shapes=[pltpu.VMEM((RET_H, RET_DK, RET_DV), F32)],
        compiler_params=_params("parallel", "arbitrary"),
        name="retention_bwd" if reverse else "retention_fwd",
    )(*args)


def _merge_kernel(x_ref, oa_ref, ob_ref, oc_ref, od_ref, gate_ref, wb_ref, wo_ref, mod_ref, g_ref, wr_ref,
                  br_ref, xo_ref, h_ref, lg_ref, acc_sc):
    n = pl.program_id(1)

    @pl.when(n == 0)
    def _():
        acc_sc[...] = jnp.zeros(acc_sc.shape, F32)

    for idx, b_ref in enumerate((oa_ref, ob_ref, oc_ref, od_ref)):
        @pl.when(n == idx)
        def _(b_ref=b_ref):
            t = jnp.dot(b_ref[...], wb_ref[0], preferred_element_type=F32)
            acc_sc[...] += jax.nn.sigmoid(gate_ref[...]) * t

    @pl.when(n == N_BRANCH - 1)
    def _():
        m = mod_ref[0]
        out = jnp.dot(acc_sc[...].astype(BF16), wo_ref[...], preferred_element_type=F32)
        x = x_ref[...] + m[:, 2 * D_MODEL:3 * D_MODEL] * out
        xo_ref[...] = x
        h = (_rms(x) * g_ref[...]) * (1.0 + m[:, 4 * D_MODEL:5 * D_MODEL]) + m[:, 3 * D_MODEL:4 * D_MODEL]
        h_ref[...] = h
        lg_ref[...] = jnp.dot(h.astype(BF16), wr_ref[...], preferred_element_type=F32) + br_ref[...]


def _merge(x, branches, z, w_branch, w_out, mod, gain, w_router, b_router):
    tm = 256
    row = lambda i, n: (i, 0)
    const2 = lambda i, n: (0, 0)
    return pl.pallas_call(
        _merge_kernel,
        grid=(N_TOK // tm, N_BRANCH),
        in_specs=[pl.BlockSpec((tm, D_MODEL), row)]
        + [pl.BlockSpec((tm, BRANCH_W), row)] * N_BRANCH
        + [pl.BlockSpec((tm, D_MODEL), lambda i, n: (i, n)),
           pl.BlockSpec((1, BRANCH_W, D_MODEL), lambda i, n: (n, 0, 0)),
           pl.BlockSpec((D_MODEL, D_MODEL), const2),
           pl.BlockSpec((1, 1, 6 * D_MODEL), lambda i, n: (_cond_row(i, tm), 0, 0)),
           pl.BlockSpec((1, D_MODEL), const2),
           pl.BlockSpec((D_MODEL, ROUTER_W), const2),
           pl.BlockSpec((1, ROUTER_W), const2)],
        out_specs=[pl.BlockSpec((tm, D_MODEL), row), pl.BlockSpec((tm, D_MODEL), row),
                   pl.BlockSpec((tm, ROUTER_W), row)],
        out_shape=[jax.ShapeDtypeStruct((N_TOK, D_MODEL), F32), jax.ShapeDtypeStruct((N_TOK, D_MODEL), F32),
                   jax.ShapeDtypeStruct((N_TOK, ROUTER_W), F32)],
        scratch_shapes=[pltpu.VMEM((tm, D_MODEL), F32)],
        compiler_params=_params("parallel", "arbitrary"),
        name="merge_out_proj",
    )(x, *branches, z, w_branch, w_out, mod, gain, w_router, b_router)


def _route(logits):
    n = logits.shape[0]
    g_logits = logits[:, :N_GROUPS]
    e_logits = logits[:, N_GROUPS:N_GROUPS + N_EXPERTS]
    grp = jnp.argmax(g_logits, axis=-1)
    rows = jnp.arange(n)
    p_grp = jax.nn.softmax(g_logits, axis=-1)[rows, grp][:, None]
    e_in = e_logits.reshape(n, N_GROUPS, EXPERTS_PER_GROUP)[rows, grp]
    top_v, top_i = lax.top_k(e_in, TOP_K)
    wts = p_grp * jax.nn.softmax(top_v, axis=-1)
    expert = (grp[:, None] * EXPERTS_PER_GROUP + top_i).reshape(-1).astype(jnp.int32)
    onehot = (expert[:, None] == jnp.arange(N_EXPERTS, dtype=jnp.int32)[None, :]).astype(jnp.int32)
    csum = jnp.cumsum(onehot, axis=0)
    rank = jnp.take_along_axis(csum, expert[:, None], axis=1)[:, 0] - 1
    counts = csum[-1]
    padded = (counts + MOE_BLK - 1) // MOE_BLK * MOE_BLK
    pad_end = jnp.cumsum(padded)
    dest = (pad_end - padded)[expert] + rank
    src = jnp.zeros((MOE_CAP,), jnp.int32).at[dest].set(jnp.arange(MOE_ASSIGN, dtype=jnp.int32) // TOP_K)
    blk_e = jnp.minimum(jnp.searchsorted(pad_end, jnp.arange(MOE_NBLK, dtype=jnp.int32) * MOE_BLK, side="right"),
                        N_EXPERTS - 1).astype(jnp.int32)
    n_used = (pad_end[-1:] // MOE_BLK).astype(jnp.int32)
    return wts, dest.astype(jnp.int32), src, blk_e, n_used


def _gather_rows(src_hbm, dst, sem, index_of, n_rows):
    def issue(r, carry):
        pltpu.make_async_copy(src_hbm.at[pl.ds(index_of(r), 1)], dst.at[pl.ds(r, 1)], sem).start()
        return carry

    lax.fori_loop(0, n_rows, issue, 0)
    pltpu.make_async_copy(src_hbm.at[pl.ds(0, n_rows)], dst, sem).wait()


def _expert_kernel(be_ref, nu_ref, src_ref, h_hbm, wg_ref, wu_ref, wd_ref, y_ref, xbuf, sem):
    i = pl.program_id(0)

    @pl.when(i < nu_ref[0])
    def _():
        _gather_rows(h_hbm, xbuf, sem, lambda r: src_ref[i * MOE_BLK + r], MOE_BLK)
        x = xbuf[...].astype(BF16)
        g = jnp.dot(x, wg_ref[0], preferred_element_type=F32)
        u = jnp.dot(x, wu_ref[0], preferred_element_type=F32)
        a = (jax.nn.silu(g) * u).astype(BF16)
        y_ref[...] = jnp.dot(a, wd_ref[0], preferred_element_type=F32)

    @pl.when(i >= nu_ref[0])
    def _():
        y_ref[...] = jnp.zeros(y_ref.shape, F32)


def _experts(h, blk_e, n_used, src, w_gate, w_up, w_down):
    return pl.pallas_call(
        _expert_kernel,
        grid_spec=pltpu.PrefetchScalarGridSpec(
            num_scalar_prefetch=3,
            grid=(MOE_NBLK,),
            in_specs=[pl.BlockSpec(memory_space=pl.ANY),
                      pl.BlockSpec((1, D_MODEL, D_EXPERT), lambda i, be, nu, src: (be[i], 0, 0)),
                      pl.BlockSpec((1, D_MODEL, D_EXPERT), lambda i, be, nu, src: (be[i], 0, 0)),
                      pl.BlockSpec((1, D_EXPERT, D_MODEL), lambda i, be, nu, src: (be[i], 0, 0))],
            out_specs=pl.BlockSpec((MOE_BLK, D_MODEL), lambda i, be, nu, src: (i, 0)),
            scratch_shapes=[pltpu.VMEM((MOE_BLK, D_MODEL), F32), pltpu.SemaphoreType.DMA(())]),
        out_shape=jax.ShapeDtypeStruct((MOE_CAP, D_MODEL), F32),
        compiler_params=_params("arbitrary"),
        name="moe_experts",
    )(blk_e, n_used, src, h, w_gate, w_up, w_down)


def _combine_kernel(dest_ref, yp_hbm, x_ref, wts_ref, mod_ref, fn_ref, xo_ref, yo_ref, ybuf, sem, *, tm):
    base = pl.program_id(0) * tm * TOP_K
    for k in range(TOP_K):
        _gather_rows(yp_hbm, ybuf.at[k], sem.at[k], lambda r, k=k: dest_ref[base + TOP_K * r + k], tm)
    w = wts_ref[...]
    y = ybuf[0] * w[:, 0:1] + ybuf[1] * w[:, 1:2]
    x = x_ref[...] + mod_ref[0][:, 5 * D_MODEL:6 * D_MODEL] * y
    xo_ref[...] = x
    yo_ref[...] = _rms(x) * fn_ref[...]


def _combine(yp, dest, x, wts, mod, final_gain):
    tm = 256
    row = lambda i, d: (i, 0)
    return pl.pallas_call(
        functools.partial(_combine_kernel, tm=tm),
        grid_spec=pltpu.PrefetchScalarGridSpec(
            num_scalar_prefetch=1,
            grid=(N_TOK // tm,),
            in_specs=[pl.BlockSpec(memory_space=pl.ANY),
                      pl.BlockSpec((tm, D_MODEL), row),
                      pl.BlockSpec((tm, TOP_K), row),
                      pl.BlockSpec((1, 1, 6 * D_MODEL), lambda i, d: (_cond_row(i, tm), 0, 0)),
                      pl.BlockSpec((1, D_MODEL), lambda i, d: (0, 0))],
            out_specs=[pl.BlockSpec((tm, D_MODEL), row), pl.BlockSpec((tm, D_MODEL), row)],
            scratch_shapes=[pltpu.VMEM((TOP_K, tm, D_MODEL), F32), pltpu.SemaphoreType.DMA((TOP_K,))]),
        out_shape=[jax.ShapeDtypeStruct((N_TOK, D_MODEL), F32), jax.ShapeDtypeStruct((N_TOK, D_MODEL), F32)],
        compiler_params=_params("arbitrary"),
        name="moe_combine",
    )(dest, yp, x, wts, mod, final_gain)


def _swap_halves(w, chunk):
    k, n = w.shape
    w = w.reshape(k, n // chunk, 2, chunk // 2)
    return w[:, :, ::-1, :].reshape(k, n)


def _reorder_w_in(w_in):
    offs, acc = {}, 0
    for name, width in IN_SPLITS.items():
        offs[name] = (acc, acc + width)
        acc += width
    col = lambda name: w_in[:, :, offs[name][0]:offs[name][1]]
    order = sorted(Z_OFF, key=Z_OFF.get)
    mkr = col("mkr")
    mkr_swap = jnp.concatenate([mkr[..., MLA_ROPE // 2:], mkr[..., :MLA_ROPE // 2]], axis=-1)
    pad = jnp.zeros(w_in.shape[:2] + (Z_WIDTH - Z_SMALL - Z_OFF["mkr"] - 2 * MLA_ROPE,), w_in.dtype)
    parts = [col("gate")] + [col(nm) for nm in order] + [mkr_swap, pad]
    return jnp.concatenate(parts, axis=-1).astype(BF16)


def _rope_tables(rot_dim):
    rows = DEC_SEQ // GRID_W
    row = jnp.repeat(jnp.arange(rows), GRID_W).astype(F32)
    col = (jnp.arange(rows * GRID_W) % GRID_W).astype(F32)
    axis_dim = rot_dim // 2
    inv = ROPE_THETA ** (-jnp.arange(0, axis_dim, 2, dtype=F32) / axis_dim)
    ang = jnp.concatenate([row[:, None] * inv, col[:, None] * inv], axis=-1)
    c, s = jnp.cos(ang), jnp.sin(ang)
    return jnp.concatenate([c, c], axis=-1), jnp.concatenate([-s, s], axis=-1)


def _head_major(cache):
    b, l, kvh, d = cache.shape
    return jnp.transpose(cache, (2, 0, 1, 3)).reshape(kvh, b * l, d).astype(BF16)


def kernel(x_prompt, x_sample, c, cache_mla_ckv, cache_mla_krope, cache_gqa_k, cache_gqa_v, cache_swa_k,
           cache_swa_v, state_ret, c_ctx, ada_w, ada_b, norm_mix, norm_ffn, w_in, mla_q_norm, mla_kv_norm,
           mla_w_uq, mla_w_ukv, gqa_q_norm, gqa_k_norm, ret_decay_logit, ret_gn, swa_sink, w_branch, w_out,
           router_group_w, router_group_b, router_expert_w, router_expert_b, moe_w_gate, moe_w_up, moe_w_down,
           final_norm):
    x = jnp.concatenate([x_prompt.reshape(N_CTX, D_MODEL), x_sample.reshape(N_LAT, D_MODEL)], axis=0)
    cond = jnp.concatenate([c_ctx[None, :], c, jnp.zeros((N_COND - 1 - DEC_BATCH, D_MODEL), F32)], axis=0)
    mod_all = _ada(cond, ada_w, ada_b).reshape(DEPTH, N_COND, 1, 6 * D_MODEL)

    w_in_r = _reorder_w_in(w_in)
    tabs = _rope_tables(HEAD_DIM) + _rope_tables(MLA_ROPE)
    log_gamma = jax.nn.log_sigmoid(ret_decay_logit.astype(F32))
    uq = mla_w_uq.reshape(DEPTH, MLA_Q_LORA, MLA_H, MLA_DK)
    ukv = mla_w_ukv.reshape(DEPTH, MLA_KV_LORA, MLA_H, MLA_NOPE + MLA_V)
    w_router = jnp.concatenate(
        [router_group_w, router_expert_w,
         jnp.zeros((DEPTH, D_MODEL, ROUTER_W - N_GROUPS - N_EXPERTS), F32)], axis=-1).astype(BF16)
    b_router = jnp.concatenate(
        [router_group_b, router_expert_b, jnp.zeros((DEPTH, ROUTER_W - N_GROUPS - N_EXPERTS), F32)], axis=-1)
    zero_state = jnp.zeros((BATCH, RET_H, RET_DK, RET_DV), F32)
    final_gain = final_norm.reshape(1, D_MODEL)

    caches = [[] for _ in range(7)]
    y = None
    for l in range(DEPTH):
        mod = mod_all[l]
        uq_rope = uq[l, :, :, MLA_NOPE:].reshape(MLA_Q_LORA, MLA_H * MLA_ROPE)
        lw = dict(
            q_norm=mla_q_norm[l].reshape(1, -1), kv_norm=mla_kv_norm[l].reshape(1, -1),
            gq_norm=gqa_q_norm[l].reshape(1, -1), gk_norm=gqa_k_norm[l].reshape(1, -1),
            w_uq_nope=uq[l, :, :, :MLA_NOPE].reshape(MLA_Q_LORA, MLA_H * MLA_NOPE).astype(BF16),
            w_uq_rope=uq_rope.astype(BF16), w_uq_swap=_swap_halves(uq_rope, MLA_ROPE).astype(BF16),
            w_uk=ukv[l, :, :, :MLA_NOPE].reshape(MLA_KV_LORA, MLA_H * MLA_NOPE).astype(BF16),
            w_uv=ukv[l, :, :, MLA_NOPE:].reshape(MLA_KV_LORA, MLA_H * MLA_V).astype(BF16))

        z = _in_proj(x, mod, norm_mix[l].reshape(1, -1), w_in_r[l])

        (qa, ka, va, qb, kb, vb, qd, kd, vd, ckv, kr, kg) = _prep(z, lw, None, latent=False)
        ctx_kw = dict(batch=BATCH, t_q=SEQ, t_k=SEQ, tq=SEQ, tk=SEQ)
        oa_c = _flash(qa, ka, va, **ctx_kw)
        ob_c = _flash(qb, kb, vb, **ctx_kw)
        od_c = _flash(qd, kd, vd, sink=swa_sink[l], **ctx_kw)
        gn = ret_gn[l]
        of_c, s_f = _retention(z, log_gamma[l, 0], gn[0:1], zero_state, None, latent=False, reverse=False)
        oc_c, s_b = _retention(z, log_gamma[l, 1], gn[1:2], zero_state, of_c, latent=False, reverse=True)

        (qa, ka, va, qb, kb, vb, qd, kd, vd) = _prep(z, lw, tabs, latent=True)
        ka0, va0 = _mla_ctx(cache_mla_ckv[:, l].reshape(DEC_BATCH * PAST_LEN, MLA_KV_LORA),
                            cache_mla_krope[:, l].reshape(DEC_BATCH * PAST_LEN, MLA_ROPE), lw)
        lat_kw = dict(batch=DEC_BATCH, t_q=DEC_SEQ, t_k=DEC_SEQ, tq=1024, tk=1024)
        oa_l = _flash(qa, ka, va, ctx=(ka0, va0), **lat_kw)
        ob_l = _flash(qb, kb, vb, ctx=(_head_major(cache_gqa_k[:, l]), _head_major(cache_gqa_v[:, l])), **lat_kw)
        od_l = _window(qd, kd, vd, _head_major(cache_swa_k[:, l]), _head_major(cache_swa_v[:, l]), swa_sink[l])
        of_l, _ = _retention(z, log_gamma[l, 0], gn[0:1], state_ret[:, l, 0], None, latent=True, reverse=False)
        oc_l, _ = _retention(z, log_gamma[l, 1], gn[1:2], state_ret[:, l, 1], of_l, latent=True, reverse=True)

        branches = [jnp.concatenate(pair, axis=0)
                    for pair in ((oa_c, oa_l), (ob_c, ob_l), (oc_c, oc_l), (od_c, od_l))]
        x_mid, h_ffn, logits = _merge(x, branches, z, w_branch[l].astype(BF16), w_out[l].astype(BF16), mod,
                                      norm_ffn[l].reshape(1, -1), w_router[l], b_router[l].reshape(1, -1))

        wts, dest, src, blk_e, n_used = _route(logits)
        yp = _experts(h_ffn, blk_e, n_used, src, moe_w_gate[l].astype(BF16), moe_w_up[l].astype(BF16),
                      moe_w_down[l].astype(BF16))
        x, y = _combine(yp, dest, x_mid, wts, mod, final_gain)

        zc = lambda name, width: z[:N_CTX, Z_SMALL + Z_OFF[name]:Z_SMALL + Z_OFF[name] + width]
        kvh_shape = (BATCH, SEQ, GQA_KVH, HEAD_DIM)
        for lst, val in zip(caches, (ckv.reshape(BATCH, SEQ, MLA_KV_LORA), kr.reshape(BATCH, SEQ, MLA_ROPE),
                                     kg.reshape(kvh_shape), zc("gv", 256).reshape(kvh_shape),
                                     zc("sk", 256).reshape(kvh_shape), zc("sv", 256).reshape(kvh_shape),
                                     jnp.stack([s_f, s_b], axis=1))):
            lst.append(val)

    y_prompt = y[:N_CTX].reshape(BATCH, SEQ, D_MODEL)
    y_sample = y[N_CTX:].reshape(DEC_BATCH, DEC_SEQ, D_MODEL)
    return (y_prompt, y_sample) + tuple(jnp.stack(lst, axis=1) for lst in caches)
```

```python
import functools

import jax
import jax.numpy as jnp
from jax import lax
from jax.experimental import pallas as pl
from jax.experimental.pallas import tpu as pltpu

F32 = jnp.float32
BF16 = jnp.bfloat16

D_MODEL = 2048
BATCH = 16
SEQ = 256
DEPTH = 4
DEC_BATCH = 2
DEC_SEQ = 4096
PAST_LEN = 256
GRID_W = 64
ROPE_THETA = 10000.0
EPS = 1e-6
QBLK = 128
NEG_INF = -1e30

MLA_H = 4
MLA_Q_LORA = 512
MLA_KV_LORA = 256
MLA_NOPE = 128
MLA_ROPE = 64
MLA_V = 128
MLA_SCALE = (MLA_NOPE + MLA_ROPE) ** -0.5
MLA_DK = MLA_NOPE + MLA_ROPE

HEAD_DIM = 128
ATTN_SCALE = HEAD_DIM ** -0.5
GQA_H = 4
GQA_KVH = 2
SWA_H = 4
SWA_KVH = 2
WINDOW = 128

RET_H = 4
RET_DK = 64
RET_DV = 128
RET_CHUNK = 128
RET_K_SCALE = RET_DK ** -0.5

N_BRANCH = 4
BRANCH_W = 512

N_GROUPS = 4
EXPERTS_PER_GROUP = 8
N_EXPERTS = N_GROUPS * EXPERTS_PER_GROUP
TOP_K = 2
D_EXPERT = 1024

N_CTX = BATCH * SEQ
N_LAT = DEC_BATCH * DEC_SEQ
N_TOK = N_CTX + N_LAT
N_COND = 8

Z_SMALL = N_BRANCH * D_MODEL
Z_OFF = dict(mq=0, gq=512, sq=1024, rv=1536, rgf=2048, rgb=2560, mkv=3072, gk=3328, gv=3584,
             rq=3840, rk=4096, sk=4352, sv=4608, mkr=4864)
Z_WIDTH = Z_SMALL + 5120
IN_SPLITS = dict(mq=512, mkv=256, mkr=64, gq=512, gk=256, gv=256, rq=256, rk=256, rv=512, rgf=512, rgb=512,
                 sq=512, sk=256, sv=256, gate=N_BRANCH * D_MODEL)

MOE_BLK = 256
MOE_ASSIGN = N_TOK * TOP_K
MOE_NBLK = (MOE_ASSIGN + N_EXPERTS * (MOE_BLK - 1) + MOE_BLK - 1) // MOE_BLK
MOE_CAP = MOE_NBLK * MOE_BLK
ROUTER_W = 128
R_W0, R_W1, R_E0, R_E1, R_RANK0, R_RANK1 = range(6)

VMEM_LIMIT = 56 * 1024 * 1024


def _params(*sem):
    return pltpu.CompilerParams(dimension_semantics=sem, vmem_limit_bytes=VMEM_LIMIT)


def _cond_row(i, tm):
    start = i * tm
    return jnp.where(start < N_CTX, 0, 1 + (start - N_CTX) // DEC_SEQ)


def _rms(x):
    return x * lax.rsqrt(jnp.mean(x * x, axis=-1, keepdims=True) + EPS)


def _token_specs(xs, tm, index_of):
    if len(xs) == 1:
        return [pl.BlockSpec((tm, D_MODEL), lambda *g: (index_of(*g), 0))]
    n_ct = N_CTX // tm
    return [pl.BlockSpec((tm, D_MODEL), lambda *g: (jnp.minimum(index_of(*g), n_ct - 1), 0),
                         pipeline_mode=pl.Buffered(1)),
            pl.BlockSpec((tm, D_MODEL), lambda *g: (jnp.maximum(index_of(*g) - n_ct, 0), 0),
                         pipeline_mode=pl.Buffered(1))]


def _token_tile(x_refs, i, tm):
    if len(x_refs) == 1:
        return x_refs[0][...]
    return jnp.where(i < N_CTX // tm, x_refs[0][...], x_refs[1][...])


def _ada_kernel(c_ref, w_ref, b_ref, o_ref):
    s = jax.nn.silu(c_ref[...]).astype(BF16)
    o_ref[0] = jnp.dot(s, w_ref[0].astype(BF16), preferred_element_type=F32) + b_ref[0]


def _ada(cond, ada_w, ada_b):
    tn = 1024
    n_out = 6 * D_MODEL
    return pl.pallas_call(
        _ada_kernel,
        grid=(DEPTH, n_out // tn),
        in_specs=[pl.BlockSpec((N_COND, D_MODEL), lambda l, j: (0, 0)),
                  pl.BlockSpec((1, D_MODEL, tn), lambda l, j: (l, 0, j)),
                  pl.BlockSpec((1, 1, tn), lambda l, j: (l, 0, j))],
        out_specs=pl.BlockSpec((1, N_COND, tn), lambda l, j: (l, 0, j)),
        out_shape=jax.ShapeDtypeStruct((DEPTH, N_COND, n_out), F32),
        compiler_params=_params("parallel", "parallel"),
        name="ada_mod",
    )(cond, ada_w, ada_b.reshape(DEPTH, 1, n_out))


def _in_kernel(*refs, n_x, tm):
    x_refs, (mod_ref, g_ref, w_ref, z_ref, h_ref) = refs[:n_x], refs[n_x:]

    @pl.when(pl.program_id(1) == 0)
    def _():
        m = mod_ref[0]
        y = _rms(_token_tile(x_refs, pl.program_id(0), tm)) * g_ref[...]
        h_ref[...] = (y * (1.0 + m[:, D_MODEL:2 * D_MODEL]) + m[:, 0:D_MODEL]).astype(BF16)

    z_ref[...] = jnp.dot(h_ref[...], w_ref[...], preferred_element_type=F32)


def _in_proj(xs, mod, gain, w_all, layer):
    tm, tn = 1024, 1024
    return pl.pallas_call(
        functools.partial(_in_kernel, n_x=len(xs), tm=tm),
        grid=(N_TOK // tm, Z_WIDTH // tn),
        in_specs=_token_specs(xs, tm, lambda i, j: i)
        + [pl.BlockSpec((1, 1, 6 * D_MODEL), lambda i, j: (_cond_row(i, tm), 0, 0)),
           pl.BlockSpec((1, D_MODEL), lambda i, j: (0, 0)),
           pl.BlockSpec((None, D_MODEL, tn), lambda i, j: (layer, 0, j))],
        out_specs=pl.BlockSpec((tm, tn), lambda i, j: (i, j)),
        out_shape=jax.ShapeDtypeStruct((N_TOK, Z_WIDTH), F32),
        scratch_shapes=[pltpu.VMEM((tm, D_MODEL), BF16)],
        compiler_params=_params("parallel", "arbitrary"),
        name="in_proj",
    )(*xs, mod, gain, w_all)


def _rope128(x, cos, sin):
    return x * cos + pltpu.roll(x, HEAD_DIM // 2, 1) * sin


def _mla_kv(ckv, kr, wuk_ref, wuv_ref, ka_ref, va_ref):
    cb = ckv.astype(BF16)
    kn = jnp.dot(cb, wuk_ref[...], preferred_element_type=F32)
    vv = jnp.dot(cb, wuv_ref[...], preferred_element_type=F32)
    for h in range(MLA_H):
        ka_ref[h, :, 0:MLA_NOPE] = kn[:, h * MLA_NOPE:(h + 1) * MLA_NOPE].astype(BF16)
        ka_ref[h, :, MLA_NOPE:MLA_DK] = kr.astype(BF16)
        va_ref[h] = vv[:, h * MLA_V:(h + 1) * MLA_V].astype(BF16)


def _prep_kernel(*refs, latent):
    (mq_ref, gq_ref, sq_ref, mkv_ref, gk_ref, gv_ref, sk_ref, sv_ref, mkr_ref,
     qn_ref, kvn_ref, gqn_ref, gkn_ref, wqn_ref, wqr_ref, wqs_ref, wuk_ref, wuv_ref) = refs[:18]
    refs = refs[18:]
    if latent:
        cos_ref, sin_ref, c64_ref, s64_ref = refs[:4]
        refs = refs[4:]
        cos, sin = cos_ref[...], sin_ref[...]
        c64, s64 = c64_ref[...], s64_ref[...]
    qa_ref, ka_ref, va_ref, qb_ref, kb_ref, vb_ref, qd_ref, kd_ref, vd_ref = refs[:9]
    refs = refs[9:]

    hn = (_rms(mq_ref[...]) * qn_ref[...]).astype(BF16)
    q_nope = jnp.dot(hn, wqn_ref[...], preferred_element_type=F32)
    q_rope = jnp.dot(hn, wqr_ref[...], preferred_element_type=F32)
    if latent:
        q_swap = jnp.dot(hn, wqs_ref[...], preferred_element_type=F32)
    for h in range(MLA_H):
        qa_ref[h, :, 0:MLA_NOPE] = (q_nope[:, h * MLA_NOPE:(h + 1) * MLA_NOPE] * MLA_SCALE).astype(BF16)
        qr = q_rope[:, h * MLA_ROPE:(h + 1) * MLA_ROPE]
        if latent:
            qr = qr * c64 + q_swap[:, h * MLA_ROPE:(h + 1) * MLA_ROPE] * s64
        qa_ref[h, :, MLA_NOPE:MLA_DK] = (qr * MLA_SCALE).astype(BF16)
    ckv = _rms(mkv_ref[...]) * kvn_ref[...]
    kr = mkr_ref[:, 0:MLA_ROPE]
    if latent:
        kr = kr * c64 + mkr_ref[:, MLA_ROPE:2 * MLA_ROPE] * s64
    _mla_kv(ckv, kr, wuk_ref, wuv_ref, ka_ref, va_ref)

    gq = gq_ref[...]
    for h in range(GQA_H):
        q = _rms(gq[:, h * HEAD_DIM:(h + 1) * HEAD_DIM]) * gqn_ref[...]
        if latent:
            q = _rope128(q, cos, sin)
        qb_ref[h] = (q * ATTN_SCALE).astype(BF16)
    gk = gk_ref[...]
    gv = gv_ref[...]
    kg = []
    for h in range(GQA_KVH):
        k = _rms(gk[:, h * HEAD_DIM:(h + 1) * HEAD_DIM]) * gkn_ref[...]
        kg.append(k)
        if latent:
            k = _rope128(k, cos, sin)
        kb_ref[h] = k.astype(BF16)
        vb_ref[h] = gv[:, h * HEAD_DIM:(h + 1) * HEAD_DIM].astype(BF16)

    sq = sq_ref[...]
    for h in range(SWA_H):
        q = sq[:, h * HEAD_DIM:(h + 1) * HEAD_DIM]
        if latent:
            q = _rope128(q, cos, sin)
        qd_ref[h] = (q * ATTN_SCALE).astype(BF16)
    sk = sk_ref[...]
    sv = sv_ref[...]
    for h in range(SWA_KVH):
        k = sk[:, h * HEAD_DIM:(h + 1) * HEAD_DIM]
        if latent:
            k = _rope128(k, cos, sin)
        kd_ref[h] = k.astype(BF16)
        vd_ref[h] = sv[:, h * HEAD_DIM:(h + 1) * HEAD_DIM].astype(BF16)

    if not latent:
        ckv_ref, kr_ref, kg_ref, vg_ref, ks_ref, vs_ref = refs
        ckv_ref[...] = ckv
        kr_ref[...] = kr
        for h in range(GQA_KVH):
            kg_ref[:, h * HEAD_DIM:(h + 1) * HEAD_DIM] = kg[h]
        vg_ref[...] = gv
        ks_ref[...] = sk
        vs_ref[...] = sv


def _prep(z, lw, tabs, latent):
    tm = 512
    rows = N_LAT if latent else N_CTX
    roff = (N_CTX if latent else 0) // tm
    n_t = DEC_SEQ // tm

    def zc(name, width):
        cb = (Z_SMALL + Z_OFF[name]) // width
        return pl.BlockSpec((tm, width), lambda i: (i + roff, cb))

    def full(a):
        nd = a.ndim
        return pl.BlockSpec(a.shape, lambda i: (0,) * nd)

    weights = [lw["q_norm"], lw["kv_norm"], lw["gq_norm"], lw["gk_norm"],
               lw["w_uq_nope"], lw["w_uq_rope"], lw["w_uq_swap"], lw["w_uk"], lw["w_uv"]]
    in_specs = [zc("mq", 512), zc("gq", 512), zc("sq", 512), zc("mkv", 256), zc("gk", 256), zc("gv", 256),
                zc("sk", 256), zc("sv", 256), zc("mkr", 128)] + [full(a) for a in weights]
    args = [z] * 9 + weights
    if latent:
        in_specs += [pl.BlockSpec((tm, HEAD_DIM), lambda i: (i % n_t, 0))] * 2
        in_specs += [pl.BlockSpec((tm, MLA_ROPE), lambda i: (i % n_t, 0))] * 2
        args += list(tabs)

    def heads(n, d):
        return (pl.BlockSpec((n, tm, d), lambda i: (0, i, 0)), jax.ShapeDtypeStruct((n, rows, d), BF16))

    outs = [heads(MLA_H, MLA_DK), heads(MLA_H, MLA_DK), heads(MLA_H, MLA_V),
            heads(GQA_H, HEAD_DIM), heads(GQA_KVH, HEAD_DIM), heads(GQA_KVH, HEAD_DIM),
            heads(SWA_H, HEAD_DIM), heads(SWA_KVH, HEAD_DIM), heads(SWA_KVH, HEAD_DIM)]
    if not latent:
        kv_w = GQA_KVH * HEAD_DIM
        for d in (MLA_KV_LORA, MLA_ROPE, kv_w, kv_w, kv_w, kv_w):
            outs.append((pl.BlockSpec((tm, d), lambda i: (i, 0)), jax.ShapeDtypeStruct((rows, d), F32)))
    return pl.pallas_call(
        functools.partial(_prep_kernel, latent=latent),
        grid=(rows // tm,),
        in_specs=in_specs,
        out_specs=[o[0] for o in outs],
        out_shape=[o[1] for o in outs],
        compiler_params=_params("parallel"),
        name="prep_latent" if latent else "prep_context",
    )(*args)


def _mla_ctx_kernel(ckv_ref, kr_ref, wuk_ref, wuv_ref, ka_ref, va_ref):
    _mla_kv(ckv_ref[...], kr_ref[...], wuk_ref, wuv_ref, ka_ref, va_ref)


def _mla_ctx(ckv, kr, lw):
    rows = ckv.shape[0]
    return pl.pallas_call(
        _mla_ctx_kernel,
        out_shape=[jax.ShapeDtypeStruct((MLA_H, rows, MLA_DK), BF16),
                   jax.ShapeDtypeStruct((MLA_H, rows, MLA_V), BF16)],
        name="mla_ctx_kv",
    )(ckv, kr, lw["w_uk"], lw["w_uv"])


def _attn_update(g, q, k, v, m_sc, l_sc, acc_sc):
    s = lax.dot_general(q, k, (((1,), (1,)), ((), ())), preferred_element_type=F32)
    m_prev = m_sc[g]
    m_new = jnp.maximum(m_prev, jnp.max(s, axis=-1, keepdims=True))
    p = jnp.exp(s - m_new)
    alpha = jnp.exp(m_prev - m_new)
    l_sc[g] = alpha * l_sc[g] + jnp.sum(p, axis=-1, keepdims=True)
    acc_sc[g] = alpha * acc_sc[g] + jnp.dot(p.astype(BF16), v, preferred_element_type=F32)
    m_sc[g] = m_new


def _flash_kernel(*refs, groups, has_ctx, has_sink):
    if has_sink:
        sink_ref, refs = refs[0], refs[1:]
    q_ref, k_ref, v_ref = refs[:3]
    refs = refs[3:]
    if has_ctx:
        k0_ref, v0_ref = refs[:2]
        refs = refs[2:]
    o_ref, m_sc, l_sc, acc_sc = refs
    ki = pl.program_id(3)
    last = pl.num_programs(3) - 1
    head0 = pl.program_id(1) * groups
    dv = v_ref.shape[-1]

    @pl.when(ki == 0)
    def _():
        m_sc[...] = jnp.full(m_sc.shape, NEG_INF, F32)
        l_sc[...] = jnp.zeros(l_sc.shape, F32)
        acc_sc[...] = jnp.zeros(acc_sc.shape, F32)
        if has_ctx:
            for g in range(groups):
                _attn_update(g, q_ref[g], k0_ref[0], v0_ref[0], m_sc, l_sc, acc_sc)

    for g in range(groups):
        _attn_update(g, q_ref[g], k_ref[0], v_ref[0], m_sc, l_sc, acc_sc)

    @pl.when(ki == last)
    def _():
        for g in range(groups):
            l = l_sc[g]
            if has_sink:
                l = l + jnp.exp(sink_ref[head0 + g] - m_sc[g])
            o_ref[:, g * dv:(g + 1) * dv] = (acc_sc[g] / l).astype(o_ref.dtype)


def _flash(q, k, v, *, batch, t_q, t_k, tq, tk, ctx=None, sink=None):
    n_h, _, dk = q.shape
    kvh, _, dv = v.shape
    groups = n_h // kvh
    nq, nk = t_q // tq, t_k // tk
    in_specs, args = [], []
    if sink is not None:
        in_specs.append(pl.BlockSpec(memory_space=pltpu.SMEM))
        args.append(sink)
    in_specs += [pl.BlockSpec((groups, tq, dk), lambda b, h, i, j: (h, b * nq + i, 0)),
                 pl.BlockSpec((1, tk, dk), lambda b, h, i, j: (h, b * nk + j, 0)),
                 pl.BlockSpec((1, tk, dv), lambda b, h, i, j: (h, b * nk + j, 0))]
    args += [q, k, v]
    if ctx is not None:
        in_specs += [pl.BlockSpec((1, PAST_LEN, dk), lambda b, h, i, j: (h, b, 0)),
                     pl.BlockSpec((1, PAST_LEN, dv), lambda b, h, i, j: (h, b, 0))]
        args += list(ctx)
    return pl.pallas_call(
        functools.partial(_flash_kernel, groups=groups, has_ctx=ctx is not None, has_sink=sink is not None),
        grid=(batch, kvh, nq, nk),
        in_specs=in_specs,
        out_specs=pl.BlockSpec((tq, groups * dv), lambda b, h, i, j: (b * nq + i, h)),
        out_shape=jax.ShapeDtypeStruct((batch * t_q, n_h * dv), BF16),
        scratch_shapes=[pltpu.VMEM((groups, tq, 1), F32), pltpu.VMEM((groups, tq, 1), F32),
                        pltpu.VMEM((groups, tq, dv), F32)],
        compiler_params=_params("parallel", "parallel", "parallel", "arbitrary"),
        name="flash_attention",
    )(*args)


def _window_kernel(sink_ref, q_ref, kp_ref, kc_ref, kn_ref, vp_ref, vc_ref, vn_ref, k0_ref, v0_ref, o_ref, *,
                   groups):
    n = pl.program_id(2)
    nb = pl.num_programs(2)
    head0 = pl.program_id(1) * groups
    row = lax.broadcasted_iota(jnp.int32, (QBLK, QBLK), 0)
    col = lax.broadcasted_iota(jnp.int32, (QBLK, QBLK), 1)
    keep_prev = (col >= row) & (n > 0)
    keep_next = (col <= row) & (n < nb - 1)
    dn = (((1,), (1,)), ((), ()))
    for g in range(groups):
        q = q_ref[g]
        s_p = jnp.where(keep_prev, lax.dot_general(q, kp_ref[0], dn, preferred_element_type=F32), NEG_INF)
        s_c = lax.dot_general(q, kc_ref[0], dn, preferred_element_type=F32)
        s_n = jnp.where(keep_next, lax.dot_general(q, kn_ref[0], dn, preferred_element_type=F32), NEG_INF)
        s_0 = lax.dot_general(q, k0_ref[0], dn, preferred_element_type=F32)
        snk = sink_ref[head0 + g]
        m = jnp.maximum(jnp.maximum(jnp.max(s_p, axis=-1, keepdims=True), jnp.max(s_c, axis=-1, keepdims=True)),
                        jnp.maximum(jnp.max(s_n, axis=-1, keepdims=True), jnp.max(s_0, axis=-1, keepdims=True)))
        m = jnp.maximum(m, snk)
        p_p, p_c, p_n, p_0 = jnp.exp(s_p - m), jnp.exp(s_c - m), jnp.exp(s_n - m), jnp.exp(s_0 - m)
        den = (jnp.sum(p_p, axis=-1, keepdims=True) + jnp.sum(p_c, axis=-1, keepdims=True)
               + jnp.sum(p_n, axis=-1, keepdims=True) + jnp.sum(p_0, axis=-1, keepdims=True) + jnp.exp(snk - m))
        acc = (jnp.dot(p_p.astype(BF16), vp_ref[0], preferred_element_type=F32)
               + jnp.dot(p_c.astype(BF16), vc_ref[0], preferred_element_type=F32)
               + jnp.dot(p_n.astype(BF16), vn_ref[0], preferred_element_type=F32)
               + jnp.dot(p_0.astype(BF16), v0_ref[0], preferred_element_type=F32))
        o_ref[:, g * HEAD_DIM:(g + 1) * HEAD_DIM] = (acc / den).astype(o_ref.dtype)


def _window(q, k, v, k0, v0, sink):
    n_h, kvh = q.shape[0], k.shape[0]
    groups = n_h // kvh
    nb = DEC_SEQ // QBLK

    def kv_spec(shift):
        return pl.BlockSpec((1, QBLK, HEAD_DIM),
                            lambda b, h, n: (h, b * nb + jnp.clip(n + shift, 0, nb - 1), 0))

    ctx_spec = pl.BlockSpec((1, PAST_LEN, HEAD_DIM), lambda b, h, n: (h, b, 0))
    return pl.pallas_call(
        functools.partial(_window_kernel, groups=groups),
        grid=(DEC_BATCH, kvh, nb),
        in_specs=[pl.BlockSpec(memory_space=pltpu.SMEM),
                  pl.BlockSpec((groups, QBLK, HEAD_DIM), lambda b, h, n: (h, b * nb + n, 0)),
                  kv_spec(-1), kv_spec(0), kv_spec(1), kv_spec(-1), kv_spec(0), kv_spec(1), ctx_spec, ctx_spec],
        out_specs=pl.BlockSpec((QBLK, groups * HEAD_DIM), lambda b, h, n: (b * nb + n, h)),
        out_shape=jax.ShapeDtypeStruct((N_LAT, n_h * HEAD_DIM), BF16),
        compiler_params=_params("parallel", "parallel", "parallel"),
        name="window_attention",
    )(sink, q, k, k, k, v, v, v, k0, v0)


def _ret_kernel(*refs, reverse, has_prev):
    lg_ref, q_ref, k_ref, v_ref, gate_ref, gn_ref, s0_ref = refs[:7]
    refs = refs[7:]
    if has_prev:
        prev_ref, refs = refs[0], refs[1:]
    o_ref, sfin_ref, s_sc = refs
    c = pl.program_id(1)

    @pl.when(c == 0)
    def _():
        s_sc[...] = s0_ref[0]

    cs = RET_CHUNK
    row = lax.broadcasted_iota(jnp.int32, (cs, cs), 0).astype(F32)
    col = lax.broadcasted_iota(jnp.int32, (cs, cs), 1).astype(F32)
    pos = lax.broadcasted_iota(jnp.int32, (cs, 1), 0).astype(F32)
    diff = (col - row) if reverse else (row - col)
    scan_pos = (cs - 1.0 - pos) if reverse else pos
    q_all, k_all, v_all, gate_all = q_ref[...], k_ref[...], v_ref[...], gate_ref[...]
    for h in range(RET_H):
        lg = lg_ref[h]
        dmask = jnp.where(diff >= 0, jnp.exp(jnp.maximum(diff, 0.0) * lg), 0.0)
        q_dec = jnp.exp((scan_pos + 1.0) * lg)
        k_dec = jnp.exp((cs - 1.0 - scan_pos) * lg)
        c_dec = jnp.exp(cs * lg)
        q = q_all[:, h * RET_DK:(h + 1) * RET_DK]
        k = k_all[:, h * RET_DK:(h + 1) * RET_DK] * RET_K_SCALE
        v = v_all[:, h * RET_DV:(h + 1) * RET_DV].astype(BF16)
        s = s_sc[h]
        a = lax.dot_general(q.astype(BF16), k.astype(BF16), (((1,), (1,)), ((), ())),
                            preferred_element_type=F32) * dmask
        o = (jnp.dot(a.astype(BF16), v, preferred_element_type=F32)
             + jnp.dot((q * q_dec).astype(BF16), s.astype(BF16), preferred_element_type=F32))
        kd_t = jnp.transpose(k * k_dec).astype(BF16)
        s_sc[h] = s * c_dec + jnp.dot(kd_t, v, preferred_element_type=F32)
        mu = jnp.mean(o, axis=-1, keepdims=True)
        var = jnp.mean(jnp.square(o - mu), axis=-1, keepdims=True)
        y = (o - mu) * lax.rsqrt(var + EPS) * gn_ref[:, h * RET_DV:(h + 1) * RET_DV]
        y = jax.nn.silu(gate_all[:, h * RET_DV:(h + 1) * RET_DV]) * y
        if has_prev:
            y = prev_ref[:, h * RET_DV:(h + 1) * RET_DV] + y
        o_ref[:, h * RET_DV:(h + 1) * RET_DV] = y.astype(o_ref.dtype)

    @pl.when(c == pl.num_programs(1) - 1)
    def _():
        sfin_ref[0] = s_sc[...]


def _retention(z, log_gamma, gn, s0, prev, *, latent, reverse):
    batch, t = (DEC_BATCH, DEC_SEQ) if latent else (BATCH, SEQ)
    nc = t // RET_CHUNK
    roff = (N_CTX if latent else 0) // RET_CHUNK
    gate_name = "rgb" if reverse else "rgf"

    def blk(b, c):
        return b * nc + (nc - 1 - c if reverse else c)

    def zc(name, width):
        cb = (Z_SMALL + Z_OFF[name]) // width
        return pl.BlockSpec((RET_CHUNK, width), lambda b, c: (roff + blk(b, c), cb))

    w = RET_H * RET_DV
    in_specs = [pl.BlockSpec(memory_space=pltpu.SMEM), zc("rq", 256), zc("rk", 256), zc("rv", 512),
                zc(gate_name, 512), pl.BlockSpec((1, w), lambda b, c: (0, 0)),
                pl.BlockSpec((1, RET_H, RET_DK, RET_DV), lambda b, c: (b, 0, 0, 0))]
    args = [log_gamma, z, z, z, z, gn, s0]
    if prev is not None:
        in_specs.append(pl.BlockSpec((RET_CHUNK, w), lambda b, c: (blk(b, c), 0)))
        args.append(prev)
    return pl.pallas_call(
        functools.partial(_ret_kernel, reverse=reverse, has_prev=prev is not None),
        grid=(batch, nc),
        in_specs=in_specs,
        out_specs=[pl.BlockSpec((RET_CHUNK, w), lambda b, c: (blk(b, c), 0)),
                   pl.BlockSpec((1, RET_H, RET_DK, RET_DV), lambda b, c: (b, 0, 0, 0))],
        out_shape=[jax.ShapeDtypeStruct((batch * t, w), BF16 if reverse else F32),
                   jax.ShapeDtypeStruct((batch, RET_H, RET_DK, RET_DV), F32)],
        scratch_shapes=[pltpu.VMEM((RET_H, RET_DK, RET_DV), F32)],
        compiler_params=_params("parallel", "arbitrary"),
        name="retention_bwd" if reverse else "retention_fwd",
    )(*args)


def _merge_kernel(*refs, n_x, tm):
    x_refs, refs = refs[:n_x], refs[n_x:]
    ctx_refs, lat_refs = refs[:N_BRANCH], refs[N_BRANCH:2 * N_BRANCH]
    (gate_ref, wb_ref, wo_ref, mod_ref, g_ref, wr_ref, br_ref, xo_ref, h_ref, lg_ref, acc_sc) = refs[2 * N_BRANCH:]
    i = pl.program_id(0)
    n = pl.program_id(1)
    is_ctx = i < N_CTX // tm

    @pl.when(n == 0)
    def _():
        acc_sc[...] = jnp.zeros(acc_sc.shape, F32)

    for idx in range(N_BRANCH):
        for in_group, b_ref in ((is_ctx, ctx_refs[idx]), (jnp.logical_not(is_ctx), lat_refs[idx])):
            @pl.when((n == idx) & in_group)
            def _(b_ref=b_ref):
                t = jnp.dot(b_ref[...], wb_ref[0], preferred_element_type=F32)
                acc_sc[...] += jax.nn.sigmoid(gate_ref[...]) * t

    @pl.when(n == N_BRANCH - 1)
    def _():
        m = mod_ref[0]
        out = jnp.dot(acc_sc[...].astype(BF16), wo_ref[...], preferred_element_type=F32)
        x = _token_tile(x_refs, i, tm) + m[:, 2 * D_MODEL:3 * D_MODEL] * out
        xo_ref[...] = x
        h = (_rms(x) * g_ref[...]) * (1.0 + m[:, 4 * D_MODEL:5 * D_MODEL]) + m[:, 3 * D_MODEL:4 * D_MODEL]
        h_ref[...] = h
        lg_ref[...] = jnp.dot(h.astype(BF16), wr_ref[...], preferred_element_type=F32) + br_ref[...]


def _merge(xs, ctx_branches, lat_branches, z, w_branch, w_out, mod, gain, w_router, b_router):
    tm = 256
    n_ct = N_CTX // tm
    row = lambda i, n: (i, 0)
    const2 = lambda i, n: (0, 0)
    return pl.pallas_call(
        functools.partial(_merge_kernel, n_x=len(xs), tm=tm),
        grid=(N_TOK // tm, N_BRANCH),
        in_specs=_token_specs(xs, tm, lambda i, n: i)
        + [pl.BlockSpec((tm, BRANCH_W), lambda i, n: (jnp.minimum(i, n_ct - 1), 0))] * N_BRANCH
        + [pl.BlockSpec((tm, BRANCH_W), lambda i, n: (jnp.maximum(i - n_ct, 0), 0))] * N_BRANCH
        + [pl.BlockSpec((tm, D_MODEL), lambda i, n: (i, n)),
           pl.BlockSpec((1, BRANCH_W, D_MODEL), lambda i, n: (n, 0, 0)),
           pl.BlockSpec((D_MODEL, D_MODEL), const2),
           pl.BlockSpec((1, 1, 6 * D_MODEL), lambda i, n: (_cond_row(i, tm), 0, 0)),
           pl.BlockSpec((1, D_MODEL), const2),
           pl.BlockSpec((D_MODEL, ROUTER_W), const2),
           pl.BlockSpec((1, ROUTER_W), const2)],
        out_specs=[pl.BlockSpec((tm, D_MODEL), row), pl.BlockSpec((tm, D_MODEL), row),
                   pl.BlockSpec((tm, ROUTER_W), row)],
        out_shape=[jax.ShapeDtypeStruct((N_TOK, D_MODEL), F32), jax.ShapeDtypeStruct((N_TOK, D_MODEL), F32),
                   jax.ShapeDtypeStruct((N_TOK, ROUTER_W), F32)],
        scratch_shapes=[pltpu.VMEM((tm, D_MODEL), F32)],
        compiler_params=_params("parallel", "arbitrary"),
        name="merge_out_proj",
    )(*xs, *ctx_branches, *lat_branches, z, w_branch, w_out, mod, gain, w_router, b_router)


def _route_kernel(lg_ref, rec_ref, cnt_ref, tri_sc, carry_sc, *, tm):
    @pl.when(pl.program_id(0) == 0)
    def _():
        r = lax.broadcasted_iota(jnp.int32, (tm, tm), 0)
        c = lax.broadcasted_iota(jnp.int32, (tm, tm), 1)
        tri_sc[...] = (c < r).astype(BF16)
        carry_sc[...] = jnp.zeros(carry_sc.shape, F32)

    lg = lg_ref[...]
    lane = lax.broadcasted_iota(jnp.int32, lg.shape, 1)
    big = jnp.int32(ROUTER_W)

    def first_max(vals):
        top = jnp.max(vals, axis=-1, keepdims=True)
        return top, jnp.min(jnp.where(vals == top, lane, big), axis=-1, keepdims=True)

    g_logits = jnp.where(lane < N_GROUPS, lg, -jnp.inf)
    g_top, grp = first_max(g_logits)
    p_grp = 1.0 / jnp.sum(jnp.exp(g_logits - g_top), axis=-1, keepdims=True)
    lo = N_GROUPS + grp * EXPERTS_PER_GROUP
    e_logits = jnp.where((lane >= lo) & (lane < lo + EXPERTS_PER_GROUP), lg, -jnp.inf)
    v0, i0 = first_max(e_logits)
    v1, i1 = first_max(jnp.where(lane == i0, -jnp.inf, e_logits))
    e1 = jnp.exp(v1 - v0)
    den = 1.0 + e1
    w0 = p_grp * (1.0 / den)
    w1 = p_grp * (e1 / den)

    hot0 = (lane == i0).astype(F32)
    hot1 = (lane == i1).astype(F32)
    before = jnp.dot(tri_sc[...], (hot0 + hot1).astype(BF16), preferred_element_type=F32) + carry_sc[...]
    rank0 = jnp.sum(before * hot0, axis=-1, keepdims=True)
    rank1 = jnp.sum(before * hot1, axis=-1, keepdims=True)
    carry_sc[...] += jnp.sum(hot0 + hot1, axis=0, keepdims=True)

    rec = jnp.zeros(lg.shape, F32)
    for slot, val in ((R_W0, w0), (R_W1, w1), (R_E0, (i0 - N_GROUPS).astype(F32)),
                      (R_E1, (i1 - N_GROUPS).astype(F32)), (R_RANK0, rank0), (R_RANK1, rank1)):
        rec = jnp.where(lane == slot, val, rec)
    rec_ref[...] = rec
    cnt_ref[...] = jnp.broadcast_to(carry_sc[...], cnt_ref.shape)


def _route(logits):
    tm = 512
    rec, cnt = pl.pallas_call(
        functools.partial(_route_kernel, tm=tm),
        grid=(N_TOK // tm,),
        in_specs=[pl.BlockSpec((tm, ROUTER_W), lambda i: (i, 0))],
        out_specs=[pl.BlockSpec((tm, ROUTER_W), lambda i: (i, 0)), pl.BlockSpec((8, ROUTER_W), lambda i: (0, 0))],
        out_shape=[jax.ShapeDtypeStruct((N_TOK, ROUTER_W), F32), jax.ShapeDtypeStruct((8, ROUTER_W), F32)],
        scratch_shapes=[pltpu.VMEM((tm, tm), BF16), pltpu.VMEM((1, ROUTER_W), F32)],
        compiler_params=_params("arbitrary"),
        name="moe_route",
    )(logits)
    expert = rec[:, R_E0:R_E1 + 1].astype(jnp.int32)
    rank = rec[:, R_RANK0:R_RANK1 + 1].astype(jnp.int32)
    counts = cnt[0, N_GROUPS:N_GROUPS + N_EXPERTS].astype(jnp.int32)
    padded = (counts + MOE_BLK - 1) // MOE_BLK * MOE_BLK
    pad_end = jnp.cumsum(padded)
    dest = ((pad_end - padded)[expert] + rank).reshape(-1)
    token = jnp.arange(MOE_ASSIGN, dtype=jnp.int32) // TOP_K
    src = jnp.zeros((MOE_CAP,), jnp.int32).at[dest].set(token)
    blk_start = jnp.arange(MOE_NBLK, dtype=jnp.int32) * MOE_BLK
    blk_e = jnp.minimum(jnp.sum(pad_end[None, :] <= blk_start[:, None], axis=1), N_EXPERTS - 1).astype(jnp.int32)
    n_used = (pad_end[-1:] // MOE_BLK).astype(jnp.int32)
    ids = jnp.arange(N_EXPERTS, dtype=jnp.int32)
    later = jnp.where((ids[None, :] > ids[:, None]) & (counts[None, :] > 0), ids[None, :], N_EXPERTS)
    nxt = jnp.min(later, axis=1)
    nxt_e = jnp.where(nxt < N_EXPERTS, nxt, -1).astype(jnp.int32)[blk_e]
    return rec, dest, src, blk_e, nxt_e, n_used


def _start_row_gather(src_hbm, dst, sem, index_of, n_rows):
    def issue(r, carry):
        pltpu.make_async_copy(src_hbm.at[pl.ds(index_of(r), 1)], dst.at[pl.ds(r, 1)], sem).start()
        return carry

    lax.fori_loop(0, n_rows, issue, 0, unroll=8)


def _wait_row_gather(src_hbm, dst, sem, n_rows):
    pltpu.make_async_copy(src_hbm.at[pl.ds(0, n_rows)], dst, sem).wait()


def _expert_kernel(be_ref, nx_ref, nu_ref, src_ref, h_hbm, wg_hbm, wu_hbm, wd_hbm, y_ref,
                   stage_g, stage_u, stage_d, res_g, res_u, res_d, xbuf, wsem, gsem, *, layer):
    i = pl.program_id(0)
    n_used = nu_ref[0]

    def weight_copies(e):
        return [pltpu.make_async_copy(w.at[layer, e], st, wsem.at[k])
                for k, (w, st) in enumerate(((wg_hbm, stage_g), (wu_hbm, stage_u), (wd_hbm, stage_d)))]

    def gather_start(blk, slot):
        _start_row_gather(h_hbm, xbuf.at[slot], gsem.at[slot], lambda r: src_ref[blk * MOE_BLK + r], MOE_BLK)

    def to_bf16(stage, res):
        rows = 256

        def body(c, carry):
            r = pl.multiple_of(c * rows, rows)
            res[pl.ds(r, rows), :] = stage[pl.ds(r, rows), :].astype(BF16)
            return carry

        lax.fori_loop(0, stage.shape[0] // rows, body, 0)

    @pl.when(i < n_used)
    def _():
        e = be_ref[i]
        first = (i == 0) | (e != be_ref[jnp.maximum(i - 1, 0)])

        @pl.when(i == 0)
        def _():
            for cp in weight_copies(e):
                cp.start()
            gather_start(0, 0)

        @pl.when(first)
        def _():
            for cp in weight_copies(e):
                cp.wait()
            to_bf16(stage_g, res_g)
            to_bf16(stage_u, res_u)
            to_bf16(stage_d, res_d)
            nxt = nx_ref[i]

            @pl.when(nxt >= 0)
            def _():
                for cp in weight_copies(nxt):
                    cp.start()

        slot = i % 2
        _wait_row_gather(h_hbm, xbuf.at[slot], gsem.at[slot], MOE_BLK)

        @pl.when(i + 1 < n_used)
        def _():
            gather_start(i + 1, 1 - slot)

        x = xbuf[slot].astype(BF16)
        g = jnp.dot(x, res_g[...], preferred_element_type=F32)
        u = jnp.dot(x, res_u[...], preferred_element_type=F32)
        a = (jax.nn.silu(g) * u).astype(BF16)
        y_ref[...] = jnp.dot(a, res_d[...], preferred_element_type=F32)

    @pl.when(i >= n_used)
    def _():
        y_ref[...] = jnp.zeros(y_ref.shape, F32)


def _experts(h, blk_e, nxt_e, n_used, src, w_gate, w_up, w_down, layer):
    any_spec = pl.BlockSpec(memory_space=pl.ANY)
    return pl.pallas_call(
        functools.partial(_expert_kernel, layer=layer),
        grid_spec=pltpu.PrefetchScalarGridSpec(
            num_scalar_prefetch=4,
            grid=(MOE_NBLK,),
            in_specs=[any_spec, any_spec, any_spec, any_spec],
            out_specs=pl.BlockSpec((MOE_BLK, D_MODEL), lambda i, *_: (i, 0)),
            scratch_shapes=[pltpu.VMEM((D_MODEL, D_EXPERT), F32), pltpu.VMEM((D_MODEL, D_EXPERT), F32),
                            pltpu.VMEM((D_EXPERT, D_MODEL), F32),
                            pltpu.VMEM((D_MODEL, D_EXPERT), BF16), pltpu.VMEM((D_MODEL, D_EXPERT), BF16),
                            pltpu.VMEM((D_EXPERT, D_MODEL), BF16),
                            pltpu.VMEM((2, MOE_BLK, D_MODEL), F32),
                            pltpu.SemaphoreType.DMA((3,)), pltpu.SemaphoreType.DMA((2,))]),
        out_shape=jax.ShapeDtypeStruct((MOE_CAP, D_MODEL), F32),
        compiler_params=_params("arbitrary"),
        name="moe_experts",
    )(blk_e, nxt_e, n_used, src, h, w_gate, w_up, w_down)


def _combine_kernel(*refs, tm, final):
    dest_ref, yp_hbm, x_ref, rec_ref, mod_ref = refs[:5]
    refs = refs[5:]
    if final:
        fn_ref, refs = refs[0], refs[1:]
    xo_ref = refs[0]
    ybuf, sem = refs[-2:]
    i = pl.program_id(0)

    def gather_start(tile, slot):
        for k in range(TOP_K):
            _start_row_gather(yp_hbm, ybuf.at[slot, k], sem.at[slot, k],
                              lambda r, k=k: dest_ref[(tile * tm + r) * TOP_K + k], tm)

    @pl.when(i == 0)
    def _():
        gather_start(0, 0)

    slot = i % 2
    for k in range(TOP_K):
        _wait_row_gather(yp_hbm, ybuf.at[slot, k], sem.at[slot, k], tm)

    @pl.when(i + 1 < pl.num_programs(0))
    def _():
        gather_start(i + 1, 1 - slot)

    rec = rec_ref[...]
    y = ybuf[slot, 0] * rec[:, R_W0:R_W0 + 1] + ybuf[slot, 1] * rec[:, R_W1:R_W1 + 1]
    x = x_ref[...] + mod_ref[0][:, 5 * D_MODEL:6 * D_MODEL] * y
    if final:
        xo_ref[...] = _rms(x) * fn_ref[...]
    else:
        xo_ref[...] = x


def _combine(yp, dest, x, rec, mod, final_gain):
    tm = 256
    final = final_gain is not None
    row = lambda i, d: (i, 0)
    in_specs = [pl.BlockSpec(memory_space=pl.ANY),
                pl.BlockSpec((tm, D_MODEL), row),
                pl.BlockSpec((tm, ROUTER_W), row),
                pl.BlockSpec((1, 1, 6 * D_MODEL), lambda i, d: (_cond_row(i, tm), 0, 0))]
    args = [dest, yp, x, rec, mod]
    if final:
        in_specs.append(pl.BlockSpec((1, D_MODEL), lambda i, d: (0, 0)))
        args.append(final_gain)
    return pl.pallas_call(
        functools.partial(_combine_kernel, tm=tm, final=final),
        grid_spec=pltpu.PrefetchScalarGridSpec(
            num_scalar_prefetch=1,
            grid=(N_TOK // tm,),
            in_specs=in_specs,
            out_specs=pl.BlockSpec((tm, D_MODEL), row),
            scratch_shapes=[pltpu.VMEM((2, TOP_K, tm, D_MODEL), F32), pltpu.SemaphoreType.DMA((2, TOP_K))]),
        out_shape=jax.ShapeDtypeStruct((N_TOK, D_MODEL), F32),
        compiler_params=_params("arbitrary"),
        name="moe_combine",
    )(*args)


def _swap_halves(w, chunk):
    k, n = w.shape
    w = w.reshape(k, n // chunk, 2, chunk // 2)
    return w[:, :, ::-1, :].reshape(k, n)


def _reorder_w_in(w_in):
    offs, acc = {}, 0
    for name, width in IN_SPLITS.items():
        offs[name] = (acc, acc + width)
        acc += width
    col = lambda name: w_in[:, :, offs[name][0]:offs[name][1]]
    order = sorted(Z_OFF, key=Z_OFF.get)
    mkr = col("mkr")
    mkr_swap = jnp.concatenate([mkr[..., MLA_ROPE // 2:], mkr[..., :MLA_ROPE // 2]], axis=-1)
    pad = jnp.zeros(w_in.shape[:2] + (Z_WIDTH - Z_SMALL - Z_OFF["mkr"] - 2 * MLA_ROPE,), w_in.dtype)
    parts = [col("gate")] + [col(nm) for nm in order] + [mkr_swap, pad]
    return jnp.concatenate(parts, axis=-1).astype(BF16)


def _rope_tables(rot_dim):
    rows = DEC_SEQ // GRID_W
    row = jnp.repeat(jnp.arange(rows), GRID_W).astype(F32)
    col = (jnp.arange(rows * GRID_W) % GRID_W).astype(F32)
    axis_dim = rot_dim // 2
    inv = ROPE_THETA ** (-jnp.arange(0, axis_dim, 2, dtype=F32) / axis_dim)
    ang = jnp.concatenate([row[:, None] * inv, col[:, None] * inv], axis=-1)
    c, s = jnp.cos(ang), jnp.sin(ang)
    return jnp.concatenate([c, c], axis=-1), jnp.concatenate([-s, s], axis=-1)


def _head_major(cache):
    b, l, kvh, d = cache.shape
    return jnp.transpose(cache, (2, 0, 1, 3)).reshape(kvh, b * l, d).astype(BF16)


def kernel(x_prompt, x_sample, c, cache_mla_ckv, cache_mla_krope, cache_gqa_k, cache_gqa_v, cache_swa_k,
           cache_swa_v, state_ret, c_ctx, ada_w, ada_b, norm_mix, norm_ffn, w_in, mla_q_norm, mla_kv_norm,
           mla_w_uq, mla_w_ukv, gqa_q_norm, gqa_k_norm, ret_decay_logit, ret_gn, swa_sink, w_branch, w_out,
           router_group_w, router_group_b, router_expert_w, router_expert_b, moe_w_gate, moe_w_up, moe_w_down,
           final_norm):
    xs = (x_prompt.reshape(N_CTX, D_MODEL), x_sample.reshape(N_LAT, D_MODEL))
    cond = jnp.concatenate([c_ctx[None, :], c, jnp.zeros((N_COND - 1 - DEC_BATCH, D_MODEL), F32)], axis=0)
    mod_all = _ada(cond, ada_w, ada_b).reshape(DEPTH, N_COND, 1, 6 * D_MODEL)

    w_in_r = _reorder_w_in(w_in)
    tabs = _rope_tables(HEAD_DIM) + _rope_tables(MLA_ROPE)
    log_gamma = jax.nn.log_sigmoid(ret_decay_logit.astype(F32))
    uq = mla_w_uq.reshape(DEPTH, MLA_Q_LORA, MLA_H, MLA_DK)
    ukv = mla_w_ukv.reshape(DEPTH, MLA_KV_LORA, MLA_H, MLA_NOPE + MLA_V)
    w_router = jnp.concatenate(
        [router_group_w, router_expert_w,
         jnp.zeros((DEPTH, D_MODEL, ROUTER_W - N_GROUPS - N_EXPERTS), F32)], axis=-1).astype(BF16)
    b_router = jnp.concatenate(
        [router_group_b, router_expert_b, jnp.zeros((DEPTH, ROUTER_W - N_GROUPS - N_EXPERTS), F32)], axis=-1)
    zero_state = jnp.zeros((BATCH, RET_H, RET_DK, RET_DV), F32)

    caches = [[] for _ in range(7)]
    for l in range(DEPTH):
        mod = mod_all[l]
        uq_rope = uq[l, :, :, MLA_NOPE:].reshape(MLA_Q_LORA, MLA_H * MLA_ROPE)
        lw = dict(
            q_norm=mla_q_norm[l].reshape(1, -1), kv_norm=mla_kv_norm[l].reshape(1, -1),
            gq_norm=gqa_q_norm[l].reshape(1, -1), gk_norm=gqa_k_norm[l].reshape(1, -1),
            w_uq_nope=uq[l, :, :, :MLA_NOPE].reshape(MLA_Q_LORA, MLA_H * MLA_NOPE).astype(BF16),
            w_uq_rope=uq_rope.astype(BF16), w_uq_swap=_swap_halves(uq_rope, MLA_ROPE).astype(BF16),
            w_uk=ukv[l, :, :, :MLA_NOPE].reshape(MLA_KV_LORA, MLA_H * MLA_NOPE).astype(BF16),
            w_uv=ukv[l, :, :, MLA_NOPE:].reshape(MLA_KV_LORA, MLA_H * MLA_V).astype(BF16))

        z = _in_proj(xs, mod, norm_mix[l].reshape(1, -1), w_in_r, l)

        (qa, ka, va, qb, kb, vb, qd, kd, vd, ckv, kr, kg, vg, ks, vs) = _prep(z, lw, None, latent=False)
        ctx_kw = dict(batch=BATCH, t_q=SEQ, t_k=SEQ, tq=SEQ, tk=SEQ)
        oa_c = _flash(qa, ka, va, **ctx_kw)
        ob_c = _flash(qb, kb, vb, **ctx_kw)
        od_c = _flash(qd, kd, vd, sink=swa_sink[l], **ctx_kw)
        gn = ret_gn[l]
        of_c, s_f = _retention(z, log_gamma[l, 0], gn[0:1], zero_state, None, latent=False, reverse=False)
        oc_c, s_b = _retention(z, log_gamma[l, 1], gn[1:2], zero_state, of_c, latent=False, reverse=True)

        (qa, ka, va, qb, kb, vb, qd, kd, vd) = _prep(z, lw, tabs, latent=True)
        ka0, va0 = _mla_ctx(cache_mla_ckv[:, l].reshape(DEC_BATCH * PAST_LEN, MLA_KV_LORA),
                            cache_mla_krope[:, l].reshape(DEC_BATCH * PAST_LEN, MLA_ROPE), lw)
        lat_kw = dict(batch=DEC_BATCH, t_q=DEC_SEQ, t_k=DEC_SEQ, tq=1024, tk=1024)
        oa_l = _flash(qa, ka, va, ctx=(ka0, va0), **lat_kw)
        ob_l = _flash(qb, kb, vb, ctx=(_head_major(cache_gqa_k[:, l]), _head_major(cache_gqa_v[:, l])), **lat_kw)
        od_l = _window(qd, kd, vd, _head_major(cache_swa_k[:, l]), _head_major(cache_swa_v[:, l]), swa_sink[l])
        of_l, _ = _retention(z, log_gamma[l, 0], gn[0:1], state_ret[:, l, 0], None, latent=True, reverse=False)
        oc_l, _ = _retention(z, log_gamma[l, 1], gn[1:2], state_ret[:, l, 1], of_l, latent=True, reverse=True)

        x_mid, h_ffn, logits = _merge(xs, (oa_c, ob_c, oc_c, od_c), (oa_l, ob_l, oc_l, od_l), z,
                                      w_branch[l].astype(BF16), w_out[l].astype(BF16), mod,
                                      norm_ffn[l].reshape(1, -1), w_router[l], b_router[l].reshape(1, -1))

        rec, dest, src, blk_e, nxt_e, n_used = _route(logits)
        yp = _experts(h_ffn, blk_e, nxt_e, n_used, src, moe_w_gate, moe_w_up, moe_w_down, l)
        final_gain = final_norm.reshape(1, D_MODEL) if l == DEPTH - 1 else None
        xs = (_combine(yp, dest, x_mid, rec, mod, final_gain),)

        kvh_shape = (BATCH, SEQ, GQA_KVH, HEAD_DIM)
        for lst, val in zip(caches, (ckv.reshape(BATCH, SEQ, MLA_KV_LORA), kr.reshape(BATCH, SEQ, MLA_ROPE),
                                     kg.reshape(kvh_shape), vg.reshape(kvh_shape), ks.reshape(kvh_shape),
                                     vs.reshape(kvh_shape), jnp.stack([s_f, s_b], axis=1))):
            lst.append(val)

    y = xs[0]
    y_prompt = y[:N_CTX].reshape(BATCH, SEQ, D_MODEL)
    y_sample = y[N_CTX:].reshape(DEC_BATCH, DEC_SEQ, D_MODEL)
    return (y_prompt, y_sample) + tuple(jnp.stack(lst, axis=1) for lst in caches)
```

```python
import functools

import jax
import jax.numpy as jnp
from jax import lax
from jax.experimental import pallas as pl
from jax.experimental.pallas import tpu as pltpu

F32 = jnp.float32
BF16 = jnp.bfloat16

D_MODEL = 2048
BATCH = 16
SEQ = 256
DEPTH = 4
DEC_BATCH = 2
DEC_SEQ = 4096
PAST_LEN = 256
GRID_W = 64
ROPE_THETA = 10000.0
EPS = 1e-6
QBLK = 128
NEG_INF = -1e30

MLA_H = 4
MLA_Q_LORA = 512
MLA_KV_LORA = 256
MLA_NOPE = 128
MLA_ROPE = 64
MLA_V = 128
MLA_SCALE = (MLA_NOPE + MLA_ROPE) ** -0.5
MLA_DK = MLA_NOPE + MLA_ROPE

HEAD_DIM = 128
ATTN_SCALE = HEAD_DIM ** -0.5
GQA_H = 4
GQA_KVH = 2
SWA_H = 4
SWA_KVH = 2
WINDOW = 128

RET_H = 4
RET_DK = 64
RET_DV = 128
RET_CHUNK = 128
RET_K_SCALE = RET_DK ** -0.5

N_BRANCH = 4
BRANCH_W = 512

N_GROUPS = 4
EXPERTS_PER_GROUP = 8
N_EXPERTS = N_GROUPS * EXPERTS_PER_GROUP
TOP_K = 2
D_EXPERT = 1024

N_CTX = BATCH * SEQ
N_LAT = DEC_BATCH * DEC_SEQ
N_TOK = N_CTX + N_LAT
N_COND = 8

Z_SMALL = N_BRANCH * D_MODEL
Z_OFF = dict(mq=0, gq=512, sq=1024, rv=1536, rgf=2048, rgb=2560, mkv=3072, gk=3328, gv=3584,
             rq=3840, rk=4096, sk=4352, sv=4608, mkr=4864)
Z_WIDTH = Z_SMALL + 5120
IN_SPLITS = dict(mq=512, mkv=256, mkr=64, gq=512, gk=256, gv=256, rq=256, rk=256, rv=512, rgf=512, rgb=512,
                 sq=512, sk=256, sv=256, gate=N_BRANCH * D_MODEL)

MOE_BLK = 256
MOE_ASSIGN = N_TOK * TOP_K
MOE_NBLK = (MOE_ASSIGN + N_EXPERTS * (MOE_BLK - 1) + MOE_BLK - 1) // MOE_BLK
MOE_CAP = MOE_NBLK * MOE_BLK
ROUTER_W = 128
R_W0, R_W1, R_E0, R_E1, R_RANK0, R_RANK1 = range(6)

VMEM_LIMIT = 56 * 1024 * 1024


def _params(*sem):
    return pltpu.CompilerParams(dimension_semantics=sem, vmem_limit_bytes=VMEM_LIMIT)


def _cond_row(i, tm):
    start = i * tm
    return jnp.where(start < N_CTX, 0, 1 + (start - N_CTX) // DEC_SEQ)


def _rms(x):
    return x * lax.rsqrt(jnp.mean(x * x, axis=-1, keepdims=True) + EPS)


def _token_specs(xs, tm, index_of):
    if len(xs) == 1:
        return [pl.BlockSpec((tm, D_MODEL), lambda *g: (index_of(*g), 0))]
    n_ct = N_CTX // tm
    return [pl.BlockSpec((tm, D_MODEL), lambda *g: (jnp.minimum(index_of(*g), n_ct - 1), 0),
                         pipeline_mode=pl.Buffered(1)),
            pl.BlockSpec((tm, D_MODEL), lambda *g: (jnp.maximum(index_of(*g) - n_ct, 0), 0),
                         pipeline_mode=pl.Buffered(1))]


def _token_tile(x_refs, i, tm):
    if len(x_refs) == 1:
        return x_refs[0][...]
    return jnp.where(i < N_CTX // tm, x_refs[0][...], x_refs[1][...])


def _ada_kernel(c_ref, w_ref, b_ref, o_ref):
    s = jax.nn.silu(c_ref[...]).astype(BF16)
    o_ref[0] = jnp.dot(s, w_ref[0].astype(BF16), preferred_element_type=F32) + b_ref[0]


def _ada(cond, ada_w, ada_b):
    tn = 1024
    n_out = 6 * D_MODEL
    return pl.pallas_call(
        _ada_kernel,
        grid=(DEPTH, n_out // tn),
        in_specs=[pl.BlockSpec((N_COND, D_MODEL), lambda l, j: (0, 0)),
                  pl.BlockSpec((1, D_MODEL, tn), lambda l, j: (l, 0, j)),
                  pl.BlockSpec((1, 1, tn), lambda l, j: (l, 0, j))],
        out_specs=pl.BlockSpec((1, N_COND, tn), lambda l, j: (l, 0, j)),
        out_shape=jax.ShapeDtypeStruct((DEPTH, N_COND, n_out), F32),
        compiler_params=_params("parallel", "parallel"),
        name="ada_mod",
    )(cond, ada_w, ada_b.reshape(DEPTH, 1, n_out))


def _in_kernel(*refs, n_x, tm, gates):
    x_refs, (mod_ref, g_ref, w_ref, o_ref, h_ref) = refs[:n_x], refs[n_x:]

    @pl.when(pl.program_id(1) == 0)
    def _():
        m = mod_ref[0]
        y = _rms(_token_tile(x_refs, pl.program_id(0), tm)) * g_ref[...]
        h_ref[...] = (y * (1.0 + m[:, D_MODEL:2 * D_MODEL]) + m[:, 0:D_MODEL]).astype(BF16)

    acc = jnp.dot(h_ref[...], w_ref[...], preferred_element_type=F32)
    o_ref[...] = jax.nn.sigmoid(acc).astype(BF16) if gates else acc


def _in_proj(xs, mod, gain, w_all, layer, gates):
    tm, tn = 1024, 1024
    col0, width = (0, Z_SMALL) if gates else (Z_SMALL // tn, Z_WIDTH - Z_SMALL)
    return pl.pallas_call(
        functools.partial(_in_kernel, n_x=len(xs), tm=tm, gates=gates),
        grid=(N_TOK // tm, width // tn),
        in_specs=_token_specs(xs, tm, lambda i, j: i)
        + [pl.BlockSpec((1, 1, 6 * D_MODEL), lambda i, j: (_cond_row(i, tm), 0, 0)),
           pl.BlockSpec((1, D_MODEL), lambda i, j: (0, 0)),
           pl.BlockSpec((None, D_MODEL, tn), lambda i, j: (layer, 0, col0 + j))],
        out_specs=pl.BlockSpec((tm, tn), lambda i, j: (i, j)),
        out_shape=jax.ShapeDtypeStruct((N_TOK, width), BF16 if gates else F32),
        scratch_shapes=[pltpu.VMEM((tm, D_MODEL), BF16)],
        compiler_params=_params("parallel", "arbitrary"),
        name="in_proj_gates" if gates else "in_proj",
    )(*xs, mod, gain, w_all)


def _rope128(x, cos, sin):
    return x * cos + pltpu.roll(x, HEAD_DIM // 2, 1) * sin


def _mla_kv(ckv, kr, wuk_ref, wuv_ref, ka_ref, va_ref):
    cb = ckv.astype(BF16)
    kn = jnp.dot(cb, wuk_ref[...], preferred_element_type=F32)
    vv = jnp.dot(cb, wuv_ref[...], preferred_element_type=F32)
    for h in range(MLA_H):
        ka_ref[h, :, 0:MLA_NOPE] = kn[:, h * MLA_NOPE:(h + 1) * MLA_NOPE].astype(BF16)
        ka_ref[h, :, MLA_NOPE:MLA_DK] = kr.astype(BF16)
        va_ref[h] = vv[:, h * MLA_V:(h + 1) * MLA_V].astype(BF16)


def _prep_kernel(*refs, latent):
    (mq_ref, gq_ref, sq_ref, mkv_ref, gk_ref, gv_ref, sk_ref, sv_ref, mkr_ref,
     qn_ref, kvn_ref, gqn_ref, gkn_ref, wqn_ref, wqr_ref, wqs_ref, wuk_ref, wuv_ref) = refs[:18]
    refs = refs[18:]
    if latent:
        cos_ref, sin_ref, c64_ref, s64_ref = refs[:4]
        refs = refs[4:]
        cos, sin = cos_ref[...], sin_ref[...]
        c64, s64 = c64_ref[...], s64_ref[...]
    qa_ref, ka_ref, va_ref, qb_ref, kb_ref, vb_ref, qd_ref, kd_ref, vd_ref = refs[:9]
    refs = refs[9:]

    hn = (_rms(mq_ref[...]) * qn_ref[...]).astype(BF16)
    q_nope = jnp.dot(hn, wqn_ref[...], preferred_element_type=F32)
    q_rope = jnp.dot(hn, wqr_ref[...], preferred_element_type=F32)
    if latent:
        q_swap = jnp.dot(hn, wqs_ref[...], preferred_element_type=F32)
    for h in range(MLA_H):
        qa_ref[h, :, 0:MLA_NOPE] = (q_nope[:, h * MLA_NOPE:(h + 1) * MLA_NOPE] * MLA_SCALE).astype(BF16)
        qr = q_rope[:, h * MLA_ROPE:(h + 1) * MLA_ROPE]
        if latent:
            qr = qr * c64 + q_swap[:, h * MLA_ROPE:(h + 1) * MLA_ROPE] * s64
        qa_ref[h, :, MLA_NOPE:MLA_DK] = (qr * MLA_SCALE).astype(BF16)
    ckv = _rms(mkv_ref[...]) * kvn_ref[...]
    kr = mkr_ref[:, 0:MLA_ROPE]
    if latent:
        kr = kr * c64 + mkr_ref[:, MLA_ROPE:2 * MLA_ROPE] * s64
    _mla_kv(ckv, kr, wuk_ref, wuv_ref, ka_ref, va_ref)

    gq = gq_ref[...]
    for h in range(GQA_H):
        q = _rms(gq[:, h * HEAD_DIM:(h + 1) * HEAD_DIM]) * gqn_ref[...]
        if latent:
            q = _rope128(q, cos, sin)
        qb_ref[h] = (q * ATTN_SCALE).astype(BF16)
    gk = gk_ref[...]
    gv = gv_ref[...]
    kg = []
    for h in range(GQA_KVH):
        k = _rms(gk[:, h * HEAD_DIM:(h + 1) * HEAD_DIM]) * gkn_ref[...]
        kg.append(k)
        if latent:
            k = _rope128(k, cos, sin)
        kb_ref[h] = k.astype(BF16)
        vb_ref[h] = gv[:, h * HEAD_DIM:(h + 1) * HEAD_DIM].astype(BF16)

    sq = sq_ref[...]
    for h in range(SWA_H):
        q = sq[:, h * HEAD_DIM:(h + 1) * HEAD_DIM]
        if latent:
            q = _rope128(q, cos, sin)
        qd_ref[h] = (q * ATTN_SCALE).astype(BF16)
    sk = sk_ref[...]
    sv = sv_ref[...]
    for h in range(SWA_KVH):
        k = sk[:, h * HEAD_DIM:(h + 1) * HEAD_DIM]
        if latent:
            k = _rope128(k, cos, sin)
        kd_ref[h] = k.astype(BF16)
        vd_ref[h] = sv[:, h * HEAD_DIM:(h + 1) * HEAD_DIM].astype(BF16)

    if not latent:
        ckv_ref, kr_ref, kg_ref, vg_ref, ks_ref, vs_ref = refs
        ckv_ref[...] = ckv
        kr_ref[...] = kr
        for h in range(GQA_KVH):
            kg_ref[:, h * HEAD_DIM:(h + 1) * HEAD_DIM] = kg[h]
        vg_ref[...] = gv
        ks_ref[...] = sk
        vs_ref[...] = sv


def _prep(z, lw, tabs, latent):
    tm = 512
    rows = N_LAT if latent else N_CTX
    roff = (N_CTX if latent else 0) // tm
    n_t = DEC_SEQ // tm

    def zc(name, width):
        cb = Z_OFF[name] // width
        return pl.BlockSpec((tm, width), lambda i: (i + roff, cb))

    def full(a):
        nd = a.ndim
        return pl.BlockSpec(a.shape, lambda i: (0,) * nd)

    weights = [lw["q_norm"], lw["kv_norm"], lw["gq_norm"], lw["gk_norm"],
               lw["w_uq_nope"], lw["w_uq_rope"], lw["w_uq_swap"], lw["w_uk"], lw["w_uv"]]
    in_specs = [zc("mq", 512), zc("gq", 512), zc("sq", 512), zc("mkv", 256), zc("gk", 256), zc("gv", 256),
                zc("sk", 256), zc("sv", 256), zc("mkr", 128)] + [full(a) for a in weights]
    args = [z] * 9 + weights
    if latent:
        in_specs += [pl.BlockSpec((tm, HEAD_DIM), lambda i: (i % n_t, 0))] * 2
        in_specs += [pl.BlockSpec((tm, MLA_ROPE), lambda i: (i % n_t, 0))] * 2
        args += list(tabs)

    def heads(n, d):
        return (pl.BlockSpec((n, tm, d), lambda i: (0, i, 0)), jax.ShapeDtypeStruct((n, rows, d), BF16))

    outs = [heads(MLA_H, MLA_DK), heads(MLA_H, MLA_DK), heads(MLA_H, MLA_V),
            heads(GQA_H, HEAD_DIM), heads(GQA_KVH, HEAD_DIM), heads(GQA_KVH, HEAD_DIM),
            heads(SWA_H, HEAD_DIM), heads(SWA_KVH, HEAD_DIM), heads(SWA_KVH, HEAD_DIM)]
    if not latent:
        kv_w = GQA_KVH * HEAD_DIM
        for d in (MLA_KV_LORA, MLA_ROPE, kv_w, kv_w, kv_w, kv_w):
            outs.append((pl.BlockSpec((tm, d), lambda i: (i, 0)), jax.ShapeDtypeStruct((rows, d), F32)))
    return pl.pallas_call(
        functools.partial(_prep_kernel, latent=latent),
        grid=(rows // tm,),
        in_specs=in_specs,
        out_specs=[o[0] for o in outs],
        out_shape=[o[1] for o in outs],
        compiler_params=_params("parallel"),
        name="prep_latent" if latent else "prep_context",
    )(*args)


def _mla_ctx_kernel(ckv_ref, kr_ref, wuk_ref, wuv_ref, ka_ref, va_ref):
    _mla_kv(ckv_ref[...], kr_ref[...], wuk_ref, wuv_ref, ka_ref, va_ref)


def _mla_ctx(ckv, kr, lw):
    rows = ckv.shape[0]
    return pl.pallas_call(
        _mla_ctx_kernel,
        out_shape=[jax.ShapeDtypeStruct((MLA_H, rows, MLA_DK), BF16),
                   jax.ShapeDtypeStruct((MLA_H, rows, MLA_V), BF16)],
        name="mla_ctx_kv",
    )(ckv, kr, lw["w_uk"], lw["w_uv"])


def _attn_update(g, q, k, v, m_sc, l_sc, acc_sc):
    s = lax.dot_general(q, k, (((1,), (1,)), ((), ())), preferred_element_type=F32)
    m_prev = m_sc[g]
    m_new = jnp.maximum(m_prev, jnp.max(s, axis=-1, keepdims=True))
    p = jnp.exp(s - m_new)
    alpha = jnp.exp(m_prev - m_new)
    l_sc[g] = alpha * l_sc[g] + jnp.sum(p, axis=-1, keepdims=True)
    acc_sc[g] = alpha * acc_sc[g] + jnp.dot(p.astype(BF16), v, preferred_element_type=F32)
    m_sc[g] = m_new


def _flash_kernel(*refs, groups, has_ctx, has_sink):
    if has_sink:
        sink_ref, refs = refs[0], refs[1:]
    q_ref, k_ref, v_ref = refs[:3]
    refs = refs[3:]
    if has_ctx:
        k0_ref, v0_ref = refs[:2]
        refs = refs[2:]
    o_ref, m_sc, l_sc, acc_sc = refs
    ki = pl.program_id(3)
    last = pl.num_programs(3) - 1
    head0 = pl.program_id(1) * groups
    dv = v_ref.shape[-1]

    @pl.when(ki == 0)
    def _():
        m_sc[...] = jnp.full(m_sc.shape, NEG_INF, F32)
        l_sc[...] = jnp.zeros(l_sc.shape, F32)
        acc_sc[...] = jnp.zeros(acc_sc.shape, F32)
        if has_ctx:
            for g in range(groups):
                _attn_update(g, q_ref[g], k0_ref[0], v0_ref[0], m_sc, l_sc, acc_sc)

    for g in range(groups):
        _attn_update(g, q_ref[g], k_ref[0], v_ref[0], m_sc, l_sc, acc_sc)

    @pl.when(ki == last)
    def _():
        for g in range(groups):
            l = l_sc[g]
            if has_sink:
                l = l + jnp.exp(sink_ref[head0 + g] - m_sc[g])
            o_ref[:, g * dv:(g + 1) * dv] = (acc_sc[g] / l).astype(o_ref.dtype)


def _flash(q, k, v, *, batch, t_q, t_k, tq, tk, ctx=None, sink=None):
    n_h, _, dk = q.shape
    kvh, _, dv = v.shape
    groups = n_h // kvh
    nq, nk = t_q // tq, t_k // tk
    in_specs, args = [], []
    if sink is not None:
        in_specs.append(pl.BlockSpec(memory_space=pltpu.SMEM))
        args.append(sink)
    in_specs += [pl.BlockSpec((groups, tq, dk), lambda b, h, i, j: (h, b * nq + i, 0)),
                 pl.BlockSpec((1, tk, dk), lambda b, h, i, j: (h, b * nk + j, 0)),
                 pl.BlockSpec((1, tk, dv), lambda b, h, i, j: (h, b * nk + j, 0))]
    args += [q, k, v]
    if ctx is not None:
        in_specs += [pl.BlockSpec((1, PAST_LEN, dk), lambda b, h, i, j: (h, b, 0)),
                     pl.BlockSpec((1, PAST_LEN, dv), lambda b, h, i, j: (h, b, 0))]
        args += list(ctx)
    return pl.pallas_call(
        functools.partial(_flash_kernel, groups=groups, has_ctx=ctx is not None, has_sink=sink is not None),
        grid=(batch, kvh, nq, nk),
        in_specs=in_specs,
        out_specs=pl.BlockSpec((tq, groups * dv), lambda b, h, i, j: (b * nq + i, h)),
        out_shape=jax.ShapeDtypeStruct((batch * t_q, n_h * dv), BF16),
        scratch_shapes=[pltpu.VMEM((groups, tq, 1), F32), pltpu.VMEM((groups, tq, 1), F32),
                        pltpu.VMEM((groups, tq, dv), F32)],
        compiler_params=_params("parallel", "parallel", "parallel", "arbitrary"),
        name="flash_attention",
    )(*args)


LANES = 128
KEY_CHUNK = 256


def _flash_latent_kernel(q_ref, k_ref, v_ref, k0_ref, v0_ref, o_ref, m_sc, l_sc, acc_sc, *, groups):
    dv = v_ref.shape[-1]
    n_chunks = k_ref.shape[1] // KEY_CHUNK
    dn = (((1,), (1,)), ((), ()))

    def scores(g, k):
        return lax.dot_general(q_ref[g], k, dn, preferred_element_type=F32)

    def lane_max(s):
        m = s[:, 0:LANES]
        for c in range(1, s.shape[1] // LANES):
            m = jnp.maximum(m, s[:, c * LANES:(c + 1) * LANES])
        return m

    for g in range(groups):
        m_sc[g] = lane_max(scores(g, k0_ref[0]))

    def max_body(j, carry):
        off = pl.multiple_of(j * KEY_CHUNK, KEY_CHUNK)
        for g in range(groups):
            m_sc[g] = jnp.maximum(m_sc[g], lane_max(scores(g, k_ref[0, pl.ds(off, KEY_CHUNK), :])))
        return carry

    lax.fori_loop(0, n_chunks, max_body, 0, unroll=True)
    for g in range(groups):
        m_sc[g] = jnp.broadcast_to(jnp.max(m_sc[g], axis=-1, keepdims=True), m_sc.shape[1:])

    def accumulate(g, k, v, first):
        s = scores(g, k)
        m = m_sc[g]
        ps = [jnp.exp(s[:, c * LANES:(c + 1) * LANES] - m) for c in range(s.shape[1] // LANES)]
        lsum = ps[0]
        for p in ps[1:]:
            lsum = lsum + p
        pv = jnp.dot(jnp.concatenate(ps, axis=1).astype(BF16), v, preferred_element_type=F32)
        if first:
            l_sc[g] = lsum
            acc_sc[g] = pv
        else:
            l_sc[g] += lsum
            acc_sc[g] += pv

    for g in range(groups):
        accumulate(g, k0_ref[0], v0_ref[0], True)

    def sum_body(j, carry):
        off = pl.multiple_of(j * KEY_CHUNK, KEY_CHUNK)
        for g in range(groups):
            accumulate(g, k_ref[0, pl.ds(off, KEY_CHUNK), :], v_ref[0, pl.ds(off, KEY_CHUNK), :], False)
        return carry

    lax.fori_loop(0, n_chunks, sum_body, 0, unroll=True)
    for g in range(groups):
        l = jnp.sum(l_sc[g], axis=-1, keepdims=True)
        o_ref[:, g * dv:(g + 1) * dv] = (acc_sc[g] / l).astype(o_ref.dtype)


def _flash_latent(q, k, v, ctx):
    tq = 1024
    n_h, _, dk = q.shape
    kvh, _, dv = v.shape
    groups = n_h // kvh
    nq = DEC_SEQ // tq
    return pl.pallas_call(
        functools.partial(_flash_latent_kernel, groups=groups),
        grid=(DEC_BATCH, kvh, nq),
        in_specs=[pl.BlockSpec((groups, tq, dk), lambda b, h, i: (h, b * nq + i, 0)),
                  pl.BlockSpec((1, DEC_SEQ, dk), lambda b, h, i: (h, b, 0)),
                  pl.BlockSpec((1, DEC_SEQ, dv), lambda b, h, i: (h, b, 0)),
                  pl.BlockSpec((1, PAST_LEN, dk), lambda b, h, i: (h, b, 0)),
                  pl.BlockSpec((1, PAST_LEN, dv), lambda b, h, i: (h, b, 0))],
        out_specs=pl.BlockSpec((tq, groups * dv), lambda b, h, i: (b * nq + i, h)),
        out_shape=jax.ShapeDtypeStruct((N_LAT, n_h * dv), BF16),
        scratch_shapes=[pltpu.VMEM((groups, tq, LANES), F32), pltpu.VMEM((groups, tq, LANES), F32),
                        pltpu.VMEM((groups, tq, dv), F32)],
        compiler_params=_params("parallel", "parallel", "parallel"),
        name="flash_latent",
    )(q, k, v, *ctx)


def _window_kernel(sink_ref, q_ref, kp_ref, kc_ref, kn_ref, vp_ref, vc_ref, vn_ref, k0_ref, v0_ref, o_ref, *,
                   groups):
    n = pl.program_id(2)
    nb = pl.num_programs(2)
    head0 = pl.program_id(1) * groups
    row = lax.broadcasted_iota(jnp.int32, (QBLK, QBLK), 0)
    col = lax.broadcasted_iota(jnp.int32, (QBLK, QBLK), 1)
    keep_prev = (col >= row) & (n > 0)
    keep_next = (col <= row) & (n < nb - 1)
    dn = (((1,), (1,)), ((), ()))
    for g in range(groups):
        q = q_ref[g]
        s_p = jnp.where(keep_prev, lax.dot_general(q, kp_ref[0], dn, preferred_element_type=F32), NEG_INF)
        s_c = lax.dot_general(q, kc_ref[0], dn, preferred_element_type=F32)
        s_n = jnp.where(keep_next, lax.dot_general(q, kn_ref[0], dn, preferred_element_type=F32), NEG_INF)
        s_0 = lax.dot_general(q, k0_ref[0], dn, preferred_element_type=F32)
        snk = sink_ref[head0 + g]
        m = jnp.maximum(jnp.maximum(jnp.max(s_p, axis=-1, keepdims=True), jnp.max(s_c, axis=-1, keepdims=True)),
                        jnp.maximum(jnp.max(s_n, axis=-1, keepdims=True), jnp.max(s_0, axis=-1, keepdims=True)))
        m = jnp.maximum(m, snk)
        p_p, p_c, p_n, p_0 = jnp.exp(s_p - m), jnp.exp(s_c - m), jnp.exp(s_n - m), jnp.exp(s_0 - m)
        den = (jnp.sum(p_p, axis=-1, keepdims=True) + jnp.sum(p_c, axis=-1, keepdims=True)
               + jnp.sum(p_n, axis=-1, keepdims=True) + jnp.sum(p_0, axis=-1, keepdims=True) + jnp.exp(snk - m))
        acc = (jnp.dot(p_p.astype(BF16), vp_ref[0], preferred_element_type=F32)
               + jnp.dot(p_c.astype(BF16), vc_ref[0], preferred_element_type=F32)
               + jnp.dot(p_n.astype(BF16), vn_ref[0], preferred_element_type=F32)
               + jnp.dot(p_0.astype(BF16), v0_ref[0], preferred_element_type=F32))
        o_ref[:, g * HEAD_DIM:(g + 1) * HEAD_DIM] = (acc / den).astype(o_ref.dtype)


def _window(q, k, v, k0, v0, sink):
    n_h, kvh = q.shape[0], k.shape[0]
    groups = n_h // kvh
    nb = DEC_SEQ // QBLK

    def kv_spec(shift):
        return pl.BlockSpec((1, QBLK, HEAD_DIM),
                            lambda b, h, n: (h, b * nb + jnp.clip(n + shift, 0, nb - 1), 0))

    ctx_spec = pl.BlockSpec((1, PAST_LEN, HEAD_DIM), lambda b, h, n: (h, b, 0))
    return pl.pallas_call(
        functools.partial(_window_kernel, groups=groups),
        grid=(DEC_BATCH, kvh, nb),
        in_specs=[pl.BlockSpec(memory_space=pltpu.SMEM),
                  pl.BlockSpec((groups, QBLK, HEAD_DIM), lambda b, h, n: (h, b * nb + n, 0)),
                  kv_spec(-1), kv_spec(0), kv_spec(1), kv_spec(-1), kv_spec(0), kv_spec(1), ctx_spec, ctx_spec],
        out_specs=pl.BlockSpec((QBLK, groups * HEAD_DIM), lambda b, h, n: (b * nb + n, h)),
        out_shape=jax.ShapeDtypeStruct((N_LAT, n_h * HEAD_DIM), BF16),
        compiler_params=_params("parallel", "parallel", "parallel"),
        name="window_attention",
    )(sink, q, k, k, k, v, v, v, k0, v0)


def _ret_kernel(*refs, reverse, has_prev):
    lg_ref, q_ref, k_ref, v_ref, gate_ref, gn_ref, s0_ref = refs[:7]
    refs = refs[7:]
    if has_prev:
        prev_ref, refs = refs[0], refs[1:]
    o_ref, sfin_ref, s_sc = refs
    c = pl.program_id(1)

    @pl.when(c == 0)
    def _():
        s_sc[...] = s0_ref[0]

    cs = RET_CHUNK
    row = lax.broadcasted_iota(jnp.int32, (cs, cs), 0).astype(F32)
    col = lax.broadcasted_iota(jnp.int32, (cs, cs), 1).astype(F32)
    pos = lax.broadcasted_iota(jnp.int32, (cs, 1), 0).astype(F32)
    diff = (col - row) if reverse else (row - col)
    scan_pos = (cs - 1.0 - pos) if reverse else pos
    q_all, k_all, v_all, gate_all = q_ref[...], k_ref[...], v_ref[...], gate_ref[...]
    for h in range(RET_H):
        lg = lg_ref[h]
        dmask = jnp.where(diff >= 0, jnp.exp(jnp.maximum(diff, 0.0) * lg), 0.0)
        q_dec = jnp.exp((scan_pos + 1.0) * lg)
        k_dec = jnp.exp((cs - 1.0 - scan_pos) * lg)
        c_dec = jnp.exp(cs * lg)
        q = q_all[:, h * RET_DK:(h + 1) * RET_DK]
        k = k_all[:, h * RET_DK:(h + 1) * RET_DK] * RET_K_SCALE
        v = v_all[:, h * RET_DV:(h + 1) * RET_DV].astype(BF16)
        s = s_sc[h]
        a = lax.dot_general(q.astype(BF16), k.astype(BF16), (((1,), (1,)), ((), ())),
                            preferred_element_type=F32) * dmask
        o = (jnp.dot(a.astype(BF16), v, preferred_element_type=F32)
             + jnp.dot((q * q_dec).astype(BF16), s.astype(BF16), preferred_element_type=F32))
        kd_t = jnp.transpose(k * k_dec).astype(BF16)
        s_sc[h] = s * c_dec + jnp.dot(kd_t, v, preferred_element_type=F32)
        mu = jnp.mean(o, axis=-1, keepdims=True)
        var = jnp.mean(jnp.square(o - mu), axis=-1, keepdims=True)
        y = (o - mu) * lax.rsqrt(var + EPS) * gn_ref[:, h * RET_DV:(h + 1) * RET_DV]
        y = jax.nn.silu(gate_all[:, h * RET_DV:(h + 1) * RET_DV]) * y
        if has_prev:
            y = prev_ref[:, h * RET_DV:(h + 1) * RET_DV] + y
        o_ref[:, h * RET_DV:(h + 1) * RET_DV] = y.astype(o_ref.dtype)

    @pl.when(c == pl.num_programs(1) - 1)
    def _():
        sfin_ref[0] = s_sc[...]


def _retention(z, log_gamma, gn, s0, prev, *, latent, reverse):
    batch, t = (DEC_BATCH, DEC_SEQ) if latent else (BATCH, SEQ)
    nc = t // RET_CHUNK
    roff = (N_CTX if latent else 0) // RET_CHUNK
    gate_name = "rgb" if reverse else "rgf"

    def blk(b, c):
        return b * nc + (nc - 1 - c if reverse else c)

    def zc(name, width):
        cb = Z_OFF[name] // width
        return pl.BlockSpec((RET_CHUNK, width), lambda b, c: (roff + blk(b, c), cb))

    w = RET_H * RET_DV
    in_specs = [pl.BlockSpec(memory_space=pltpu.SMEM), zc("rq", 256), zc("rk", 256), zc("rv", 512),
                zc(gate_name, 512), pl.BlockSpec((1, w), lambda b, c: (0, 0)),
                pl.BlockSpec((1, RET_H, RET_DK, RET_DV), lambda b, c: (b, 0, 0, 0))]
    args = [log_gamma, z, z, z, z, gn, s0]
    if prev is not None:
        in_specs.append(pl.BlockSpec((RET_CHUNK, w), lambda b, c: (blk(b, c), 0)))
        args.append(prev)
    return pl.pallas_call(
        functools.partial(_ret_kernel, reverse=reverse, has_prev=prev is not None),
        grid=(batch, nc),
        in_specs=in_specs,
        out_specs=[pl.BlockSpec((RET_CHUNK, w), lambda b, c: (blk(b, c), 0)),
                   pl.BlockSpec((1, RET_H, RET_DK, RET_DV), lambda b, c: (b, 0, 0, 0))],
        out_shape=[jax.ShapeDtypeStruct((batch * t, w), BF16 if reverse else F32),
                   jax.ShapeDtypeStruct((batch, RET_H, RET_DK, RET_DV), F32)],
        scratch_shapes=[pltpu.VMEM((RET_H, RET_DK, RET_DV), F32)],
        compiler_params=_params("parallel", "arbitrary"),
        name="retention_bwd" if reverse else "retention_fwd",
    )(*args)


def _merge_kernel(*refs, n_x, tm):
    x_refs, refs = refs[:n_x], refs[n_x:]
    ctx_refs, lat_refs = refs[:N_BRANCH], refs[N_BRANCH:2 * N_BRANCH]
    (gate_ref, wb_ref, wo_ref, mod_ref, g_ref, wr_ref, br_ref, xo_ref, h_ref, lg_ref) = refs[2 * N_BRANCH:]
    i = pl.program_id(0)
    is_ctx = i < N_CTX // tm

    merged = None
    for n in range(N_BRANCH):
        branch = jnp.where(is_ctx, ctx_refs[n][...], lat_refs[n][...])
        t = jnp.dot(branch, wb_ref[n], preferred_element_type=F32)
        term = gate_ref[:, n * D_MODEL:(n + 1) * D_MODEL].astype(F32) * t
        merged = term if merged is None else merged + term

    m = mod_ref[0]
    out = jnp.dot(merged.astype(BF16), wo_ref[...], preferred_element_type=F32)
    x = _token_tile(x_refs, i, tm) + m[:, 2 * D_MODEL:3 * D_MODEL] * out
    xo_ref[...] = x
    h = (_rms(x) * g_ref[...]) * (1.0 + m[:, 4 * D_MODEL:5 * D_MODEL]) + m[:, 3 * D_MODEL:4 * D_MODEL]
    h_ref[...] = h
    lg_ref[...] = jnp.dot(h.astype(BF16), wr_ref[...], preferred_element_type=F32) + br_ref[...]


def _merge(xs, ctx_branches, lat_branches, gates, w_branch, w_out, mod, gain, w_router, b_router):
    tm = 256
    n_ct = N_CTX // tm
    row = lambda i: (i, 0)
    const2 = lambda i: (0, 0)
    resident = pl.Buffered(1)
    return pl.pallas_call(
        functools.partial(_merge_kernel, n_x=len(xs), tm=tm),
        grid=(N_TOK // tm,),
        in_specs=_token_specs(xs, tm, lambda i: i)
        + [pl.BlockSpec((tm, BRANCH_W), lambda i: (jnp.minimum(i, n_ct - 1), 0))] * N_BRANCH
        + [pl.BlockSpec((tm, BRANCH_W), lambda i: (jnp.maximum(i - n_ct, 0), 0))] * N_BRANCH
        + [pl.BlockSpec((tm, N_BRANCH * D_MODEL), row),
           pl.BlockSpec((N_BRANCH, BRANCH_W, D_MODEL), lambda i: (0, 0, 0), pipeline_mode=resident),
           pl.BlockSpec((D_MODEL, D_MODEL), const2, pipeline_mode=resident),
           pl.BlockSpec((1, 1, 6 * D_MODEL), lambda i: (_cond_row(i, tm), 0, 0)),
           pl.BlockSpec((1, D_MODEL), const2),
           pl.BlockSpec((D_MODEL, ROUTER_W), const2),
           pl.BlockSpec((1, ROUTER_W), const2)],
        out_specs=[pl.BlockSpec((tm, D_MODEL), row), pl.BlockSpec((tm, D_MODEL), row),
                   pl.BlockSpec((tm, ROUTER_W), row)],
        out_shape=[jax.ShapeDtypeStruct((N_TOK, D_MODEL), F32), jax.ShapeDtypeStruct((N_TOK, D_MODEL), F32),
                   jax.ShapeDtypeStruct((N_TOK, ROUTER_W), F32)],
        compiler_params=_params("parallel"),
        name="merge_out_proj",
    )(*xs, *ctx_branches, *lat_branches, gates, w_branch, w_out, mod, gain, w_router, b_router)


def _route_kernel(lg_ref, rec_ref, cnt_ref, tri_sc, carry_sc, *, tm):
    @pl.when(pl.program_id(0) == 0)
    def _():
        r = lax.broadcasted_iota(jnp.int32, (tm, tm), 0)
        c = lax.broadcasted_iota(jnp.int32, (tm, tm), 1)
        tri_sc[...] = (c < r).astype(BF16)
        carry_sc[...] = jnp.zeros(carry_sc.shape, F32)

    lg = lg_ref[...]
    lane = lax.broadcasted_iota(jnp.int32, lg.shape, 1)
    big = jnp.int32(ROUTER_W)

    def first_max(vals):
        top = jnp.max(vals, axis=-1, keepdims=True)
        return top, jnp.min(jnp.where(vals == top, lane, big), axis=-1, keepdims=True)

    g_logits = jnp.where(lane < N_GROUPS, lg, -jnp.inf)
    g_top, grp = first_max(g_logits)
    p_grp = 1.0 / jnp.sum(jnp.exp(g_logits - g_top), axis=-1, keepdims=True)
    lo = N_GROUPS + grp * EXPERTS_PER_GROUP
    e_logits = jnp.where((lane >= lo) & (lane < lo + EXPERTS_PER_GROUP), lg, -jnp.inf)
    v0, i0 = first_max(e_logits)
    v1, i1 = first_max(jnp.where(lane == i0, -jnp.inf, e_logits))
    e1 = jnp.exp(v1 - v0)
    den = 1.0 + e1
    w0 = p_grp * (1.0 / den)
    w1 = p_grp * (e1 / den)

    hot0 = (lane == i0).astype(F32)
    hot1 = (lane == i1).astype(F32)
    before = jnp.dot(tri_sc[...], (hot0 + hot1).astype(BF16), preferred_element_type=F32) + carry_sc[...]
    rank0 = jnp.sum(before * hot0, axis=-1, keepdims=True)
    rank1 = jnp.sum(before * hot1, axis=-1, keepdims=True)
    carry_sc[...] += jnp.sum(hot0 + hot1, axis=0, keepdims=True)

    rec = jnp.zeros(lg.shape, F32)
    for slot, val in ((R_W0, w0), (R_W1, w1), (R_E0, (i0 - N_GROUPS).astype(F32)),
                      (R_E1, (i1 - N_GROUPS).astype(F32)), (R_RANK0, rank0), (R_RANK1, rank1)):
        rec = jnp.where(lane == slot, val, rec)
    rec_ref[...] = rec
    cnt_ref[...] = jnp.broadcast_to(carry_sc[...], cnt_ref.shape)


def _route(logits):
    tm = 512
    rec, cnt = pl.pallas_call(
        functools.partial(_route_kernel, tm=tm),
        grid=(N_TOK // tm,),
        in_specs=[pl.BlockSpec((tm, ROUTER_W), lambda i: (i, 0))],
        out_specs=[pl.BlockSpec((tm, ROUTER_W), lambda i: (i, 0)), pl.BlockSpec((8, ROUTER_W), lambda i: (0, 0))],
        out_shape=[jax.ShapeDtypeStruct((N_TOK, ROUTER_W), F32), jax.ShapeDtypeStruct((8, ROUTER_W), F32)],
        scratch_shapes=[pltpu.VMEM((tm, tm), BF16), pltpu.VMEM((1, ROUTER_W), F32)],
        compiler_params=_params("arbitrary"),
        name="moe_route",
    )(logits)
    expert = rec[:, R_E0:R_E1 + 1].astype(jnp.int32)
    rank = rec[:, R_RANK0:R_RANK1 + 1].astype(jnp.int32)
    counts = cnt[0, N_GROUPS:N_GROUPS + N_EXPERTS].astype(jnp.int32)
    padded = (counts + MOE_BLK - 1) // MOE_BLK * MOE_BLK
    pad_end = jnp.cumsum(padded)
    dest = ((pad_end - padded)[expert] + rank).reshape(-1)
    token = jnp.arange(MOE_ASSIGN, dtype=jnp.int32) // TOP_K
    src = jnp.zeros((MOE_CAP,), jnp.int32).at[dest].set(token)
    blk_start = jnp.arange(MOE_NBLK, dtype=jnp.int32) * MOE_BLK
    blk_e = jnp.minimum(jnp.sum(pad_end[None, :] <= blk_start[:, None], axis=1), N_EXPERTS - 1).astype(jnp.int32)
    n_used = (pad_end[-1:] // MOE_BLK).astype(jnp.int32)
    ids = jnp.arange(N_EXPERTS, dtype=jnp.int32)
    later = jnp.where((ids[None, :] > ids[:, None]) & (counts[None, :] > 0), ids[None, :], N_EXPERTS)
    nxt = jnp.min(later, axis=1)
    nxt_e = jnp.where(nxt < N_EXPERTS, nxt, -1).astype(jnp.int32)[blk_e]
    return rec, dest, src, blk_e, nxt_e, n_used


def _start_row_gather(src_hbm, dst, sem, index_of, n_rows, inline=False):
    def issue(r, carry):
        pltpu.make_async_copy(src_hbm.at[pl.ds(index_of(r), 1)], dst.at[pl.ds(r, 1)], sem).start()
        return carry

    if inline:
        for r in range(n_rows):
            issue(r, 0)
    else:
        lax.fori_loop(0, n_rows, issue, 0, unroll=8)


def _wait_row_gather(src_hbm, dst, sem, n_rows):
    pltpu.make_async_copy(src_hbm.at[pl.ds(0, n_rows)], dst, sem).wait()


def _expert_kernel(be_ref, nx_ref, nu_ref, src_ref, h_hbm, wg_hbm, wu_hbm, wd_hbm, y_ref,
                   stage_g, stage_u, stage_d, res_g, res_u, res_d, xbuf, wsem, gsem, *, layer):
    i = pl.program_id(0)
    n_used = nu_ref[0]

    def weight_copies(e):
        return [pltpu.make_async_copy(w.at[layer, e], st, wsem.at[k])
                for k, (w, st) in enumerate(((wg_hbm, stage_g), (wu_hbm, stage_u), (wd_hbm, stage_d)))]

    def gather_start(blk, slot, inline=False):
        _start_row_gather(h_hbm, xbuf.at[slot], gsem.at[slot], lambda r: src_ref[blk * MOE_BLK + r], MOE_BLK,
                          inline)

    def to_bf16(stage, res):
        rows = 256

        def body(c, carry):
            r = pl.multiple_of(c * rows, rows)
            res[pl.ds(r, rows), :] = stage[pl.ds(r, rows), :].astype(BF16)
            return carry

        lax.fori_loop(0, stage.shape[0] // rows, body, 0)

    @pl.when(i < n_used)
    def _():
        e = be_ref[i]
        first = (i == 0) | (e != be_ref[jnp.maximum(i - 1, 0)])

        @pl.when(i == 0)
        def _():
            for cp in weight_copies(e):
                cp.start()
            gather_start(0, 0)

        @pl.when(first)
        def _():
            for cp in weight_copies(e):
                cp.wait()
            to_bf16(stage_g, res_g)
            to_bf16(stage_u, res_u)
            to_bf16(stage_d, res_d)
            nxt = nx_ref[i]

            @pl.when(nxt >= 0)
            def _():
                for cp in weight_copies(nxt):
                    cp.start()

        slot = i % 2
        _wait_row_gather(h_hbm, xbuf.at[slot], gsem.at[slot], MOE_BLK)
        gather_start(jnp.minimum(i + 1, n_used - 1), 1 - slot, inline=True)

        x = xbuf[slot].astype(BF16)
        g = jnp.dot(x, res_g[...], preferred_element_type=F32)
        u = jnp.dot(x, res_u[...], preferred_element_type=F32)
        a = (jax.nn.silu(g) * u).astype(BF16)
        y_ref[...] = jnp.dot(a, res_d[...], preferred_element_type=F32)

        @pl.when(i == n_used - 1)
        def _():
            _wait_row_gather(h_hbm, xbuf.at[1 - slot], gsem.at[1 - slot], MOE_BLK)

    @pl.when(i >= n_used)
    def _():
        y_ref[...] = jnp.zeros(y_ref.shape, F32)


def _experts(h, blk_e, nxt_e, n_used, src, w_gate, w_up, w_down, layer):
    any_spec = pl.BlockSpec(memory_space=pl.ANY)
    return pl.pallas_call(
        functools.partial(_expert_kernel, layer=layer),
        grid_spec=pltpu.PrefetchScalarGridSpec(
            num_scalar_prefetch=4,
            grid=(MOE_NBLK,),
            in_specs=[any_spec, any_spec, any_spec, any_spec],
            out_specs=pl.BlockSpec((MOE_BLK, D_MODEL), lambda i, *_: (i, 0)),
            scratch_shapes=[pltpu.VMEM((D_MODEL, D_EXPERT), F32), pltpu.VMEM((D_MODEL, D_EXPERT), F32),
                            pltpu.VMEM((D_EXPERT, D_MODEL), F32),
                            pltpu.VMEM((D_MODEL, D_EXPERT), BF16), pltpu.VMEM((D_MODEL, D_EXPERT), BF16),
                            pltpu.VMEM((D_EXPERT, D_MODEL), BF16),
                            pltpu.VMEM((2, MOE_BLK, D_MODEL), F32),
                            pltpu.SemaphoreType.DMA((3,)), pltpu.SemaphoreType.DMA((2,))]),
        out_shape=jax.ShapeDtypeStruct((MOE_CAP, D_MODEL), F32),
        compiler_params=_params("arbitrary"),
        name="moe_experts",
    )(blk_e, nxt_e, n_used, src, h, w_gate, w_up, w_down)


def _combine_kernel(*refs, tm, final):
    dest_ref, yp_hbm, x_ref, rec_ref, mod_ref = refs[:5]
    refs = refs[5:]
    if final:
        fn_ref, yc_ref, yl_ref = refs[:3]
    else:
        xo_ref = refs[0]
    ybuf, sem = refs[-2:]
    i = pl.program_id(0)

    def gather_start(tile, slot):
        for k in range(TOP_K):
            _start_row_gather(yp_hbm, ybuf.at[slot, k], sem.at[slot, k],
                              lambda r, k=k: dest_ref[(tile * tm + r) * TOP_K + k], tm)

    @pl.when(i == 0)
    def _():
        gather_start(0, 0)

    slot = i % 2
    for k in range(TOP_K):
        _wait_row_gather(yp_hbm, ybuf.at[slot, k], sem.at[slot, k], tm)

    @pl.when(i + 1 < pl.num_programs(0))
    def _():
        gather_start(i + 1, 1 - slot)

    rec = rec_ref[...]
    y = ybuf[slot, 0] * rec[:, R_W0:R_W0 + 1] + ybuf[slot, 1] * rec[:, R_W1:R_W1 + 1]
    x = x_ref[...] + mod_ref[0][:, 5 * D_MODEL:6 * D_MODEL] * y
    if final:
        y = _rms(x) * fn_ref[...]

        @pl.when(i < N_CTX // tm)
        def _():
            yc_ref[...] = y

        @pl.when(i >= N_CTX // tm)
        def _():
            yl_ref[...] = y
    else:
        xo_ref[...] = x


def _combine(yp, dest, x, rec, mod, final_gain):
    tm = 256
    n_ct = N_CTX // tm
    final = final_gain is not None
    row = lambda i, d: (i, 0)
    in_specs = [pl.BlockSpec(memory_space=pl.ANY),
                pl.BlockSpec((tm, D_MODEL), row),
                pl.BlockSpec((tm, ROUTER_W), row),
                pl.BlockSpec((1, 1, 6 * D_MODEL), lambda i, d: (_cond_row(i, tm), 0, 0))]
    args = [dest, yp, x, rec, mod]
    if final:
        in_specs.append(pl.BlockSpec((1, D_MODEL), lambda i, d: (0, 0)))
        args.append(final_gain)
        out_specs = [pl.BlockSpec((tm, D_MODEL), lambda i, d: (jnp.minimum(i, n_ct - 1), 0)),
                     pl.BlockSpec((tm, D_MODEL), lambda i, d: (jnp.maximum(i - n_ct, 0), 0))]
        out_shape = [jax.ShapeDtypeStruct((N_CTX, D_MODEL), F32), jax.ShapeDtypeStruct((N_LAT, D_MODEL), F32)]
    else:
        out_specs = pl.BlockSpec((tm, D_MODEL), row)
        out_shape = jax.ShapeDtypeStruct((N_TOK, D_MODEL), F32)
    return pl.pallas_call(
        functools.partial(_combine_kernel, tm=tm, final=final),
        grid_spec=pltpu.PrefetchScalarGridSpec(
            num_scalar_prefetch=1,
            grid=(N_TOK // tm,),
            in_specs=in_specs,
            out_specs=out_specs,
            scratch_shapes=[pltpu.VMEM((2, TOP_K, tm, D_MODEL), F32), pltpu.SemaphoreType.DMA((2, TOP_K))]),
        out_shape=out_shape,
        compiler_params=_params("arbitrary"),
        name="moe_combine",
    )(*args)


def _swap_halves(w, chunk):
    k, n = w.shape
    w = w.reshape(k, n // chunk, 2, chunk // 2)
    return w[:, :, ::-1, :].reshape(k, n)


def _reorder_w_in(w_in):
    offs, acc = {}, 0
    for name, width in IN_SPLITS.items():
        offs[name] = (acc, acc + width)
        acc += width
    col = lambda name: w_in[:, :, offs[name][0]:offs[name][1]]
    order = sorted(Z_OFF, key=Z_OFF.get)
    mkr = col("mkr")
    mkr_swap = jnp.concatenate([mkr[..., MLA_ROPE // 2:], mkr[..., :MLA_ROPE // 2]], axis=-1)
    pad = jnp.zeros(w_in.shape[:2] + (Z_WIDTH - Z_SMALL - Z_OFF["mkr"] - 2 * MLA_ROPE,), w_in.dtype)
    parts = [col("gate")] + [col(nm) for nm in order] + [mkr_swap, pad]
    return jnp.concatenate(parts, axis=-1).astype(BF16)


def _rope_tables(rot_dim):
    rows = DEC_SEQ // GRID_W
    row = jnp.repeat(jnp.arange(rows), GRID_W).astype(F32)
    col = (jnp.arange(rows * GRID_W) % GRID_W).astype(F32)
    axis_dim = rot_dim // 2
    inv = ROPE_THETA ** (-jnp.arange(0, axis_dim, 2, dtype=F32) / axis_dim)
    ang = jnp.concatenate([row[:, None] * inv, col[:, None] * inv], axis=-1)
    c, s = jnp.cos(ang), jnp.sin(ang)
    return jnp.concatenate([c, c], axis=-1), jnp.concatenate([-s, s], axis=-1)


def _head_major(cache):
    b, l, kvh, d = cache.shape
    return jnp.transpose(cache, (2, 0, 1, 3)).reshape(kvh, b * l, d).astype(BF16)


def kernel(x_prompt, x_sample, c, cache_mla_ckv, cache_mla_krope, cache_gqa_k, cache_gqa_v, cache_swa_k,
           cache_swa_v, state_ret, c_ctx, ada_w, ada_b, norm_mix, norm_ffn, w_in, mla_q_norm, mla_kv_norm,
           mla_w_uq, mla_w_ukv, gqa_q_norm, gqa_k_norm, ret_decay_logit, ret_gn, swa_sink, w_branch, w_out,
           router_group_w, router_group_b, router_expert_w, router_expert_b, moe_w_gate, moe_w_up, moe_w_down,
           final_norm):
    xs = (x_prompt.reshape(N_CTX, D_MODEL), x_sample.reshape(N_LAT, D_MODEL))
    cond = jnp.concatenate([c_ctx[None, :], c, jnp.zeros((N_COND - 1 - DEC_BATCH, D_MODEL), F32)], axis=0)
    mod_all = _ada(cond, ada_w, ada_b).reshape(DEPTH, N_COND, 1, 6 * D_MODEL)

    w_in_r = _reorder_w_in(w_in)
    tabs = _rope_tables(HEAD_DIM) + _rope_tables(MLA_ROPE)
    log_gamma = jax.nn.log_sigmoid(ret_decay_logit.astype(F32))
    uq = mla_w_uq.reshape(DEPTH, MLA_Q_LORA, MLA_H, MLA_DK)
    ukv = mla_w_ukv.reshape(DEPTH, MLA_KV_LORA, MLA_H, MLA_NOPE + MLA_V)
    w_router = jnp.concatenate(
        [router_group_w, router_expert_w,
         jnp.zeros((DEPTH, D_MODEL, ROUTER_W - N_GROUPS - N_EXPERTS), F32)], axis=-1).astype(BF16)
    b_router = jnp.concatenate(
        [router_group_b, router_expert_b, jnp.zeros((DEPTH, ROUTER_W - N_GROUPS - N_EXPERTS), F32)], axis=-1)
    zero_state = jnp.zeros((BATCH, RET_H, RET_DK, RET_DV), F32)

    caches = [[] for _ in range(7)]
    for l in range(DEPTH):
        mod = mod_all[l]
        uq_rope = uq[l, :, :, MLA_NOPE:].reshape(MLA_Q_LORA, MLA_H * MLA_ROPE)
        lw = dict(
            q_norm=mla_q_norm[l].reshape(1, -1), kv_norm=mla_kv_norm[l].reshape(1, -1),
            gq_norm=gqa_q_norm[l].reshape(1, -1), gk_norm=gqa_k_norm[l].reshape(1, -1),
            w_uq_nope=uq[l, :, :, :MLA_NOPE].reshape(MLA_Q_LORA, MLA_H * MLA_NOPE).astype(BF16),
            w_uq_rope=uq_rope.astype(BF16), w_uq_swap=_swap_halves(uq_rope, MLA_ROPE).astype(BF16),
            w_uk=ukv[l, :, :, :MLA_NOPE].reshape(MLA_KV_LORA, MLA_H * MLA_NOPE).astype(BF16),
            w_uv=ukv[l, :, :, MLA_NOPE:].reshape(MLA_KV_LORA, MLA_H * MLA_V).astype(BF16))

        z = _in_proj(xs, mod, norm_mix[l].reshape(1, -1), w_in_r, l, gates=False)
        gates = _in_proj(xs, mod, norm_mix[l].reshape(1, -1), w_in_r, l, gates=True)

        (qa, ka, va, qb, kb, vb, qd, kd, vd, ckv, kr, kg, vg, ks, vs) = _prep(z, lw, None, latent=False)
        ctx_kw = dict(batch=BATCH, t_q=SEQ, t_k=SEQ, tq=SEQ, tk=SEQ)
        oa_c = _flash(qa, ka, va, **ctx_kw)
        ob_c = _flash(qb, kb, vb, **ctx_kw)
        od_c = _flash(qd, kd, vd, sink=swa_sink[l], **ctx_kw)
        gn = ret_gn[l]
        of_c, s_f = _retention(z, log_gamma[l, 0], gn[0:1], zero_state, None, latent=False, reverse=False)
        oc_c, s_b = _retention(z, log_gamma[l, 1], gn[1:2], zero_state, of_c, latent=False, reverse=True)

        (qa, ka, va, qb, kb, vb, qd, kd, vd) = _prep(z, lw, tabs, latent=True)
        ka0, va0 = _mla_ctx(cache_mla_ckv[:, l].reshape(DEC_BATCH * PAST_LEN, MLA_KV_LORA),
                            cache_mla_krope[:, l].reshape(DEC_BATCH * PAST_LEN, MLA_ROPE), lw)
        oa_l = _flash_latent(qa, ka, va, (ka0, va0))
        ob_l = _flash_latent(qb, kb, vb, (_head_major(cache_gqa_k[:, l]), _head_major(cache_gqa_v[:, l])))
        od_l = _window(qd, kd, vd, _head_major(cache_swa_k[:, l]), _head_major(cache_swa_v[:, l]), swa_sink[l])
        of_l, _ = _retention(z, log_gamma[l, 0], gn[0:1], state_ret[:, l, 0], None, latent=True, reverse=False)
        oc_l, _ = _retention(z, log_gamma[l, 1], gn[1:2], state_ret[:, l, 1], of_l, latent=True, reverse=True)

        x_mid, h_ffn, logits = _merge(xs, (oa_c, ob_c, oc_c, od_c), (oa_l, ob_l, oc_l, od_l), gates,
                                      w_branch[l].astype(BF16), w_out[l].astype(BF16), mod,
                                      norm_ffn[l].reshape(1, -1), w_router[l], b_router[l].reshape(1, -1))

        rec, dest, src, blk_e, nxt_e, n_used = _route(logits)
        yp = _experts(h_ffn, blk_e, nxt_e, n_used, src, moe_w_gate, moe_w_up, moe_w_down, l)
        if l < DEPTH - 1:
            xs = (_combine(yp, dest, x_mid, rec, mod, None),)
        else:
            y_ctx, y_lat = _combine(yp, dest, x_mid, rec, mod, final_norm.reshape(1, D_MODEL))

        kvh_shape = (BATCH, SEQ, GQA_KVH, HEAD_DIM)
        for lst, val in zip(caches, (ckv.reshape(BATCH, SEQ, MLA_KV_LORA), kr.reshape(BATCH, SEQ, MLA_ROPE),
                                     kg.reshape(kvh_shape), vg.reshape(kvh_shape), ks.reshape(kvh_shape),
                                     vs.reshape(kvh_shape), jnp.stack([s_f, s_b], axis=1))):
            lst.append(val)

    y_prompt = y_ctx.reshape(BATCH, SEQ, D_MODEL)
    y_sample = y_lat.reshape(DEC_BATCH, DEC_SEQ, D_MODEL)
    return (y_prompt, y_sample) + tuple(jnp.stack(lst, axis=1) for lst in caches)
```

```python
import functools

import jax
import jax.numpy as jnp
from jax import lax
from jax.experimental import pallas as pl
from jax.experimental.pallas import tpu as pltpu

F32 = jnp.float32
BF16 = jnp.bfloat16
LANES = 128

D_MODEL = 2048
BATCH = 16
SEQ = 256
DEPTH = 4
DEC_BATCH = 2
DEC_SEQ = 4096
PAST_LEN = 256
GRID_W = 64
ROPE_THETA = 10000.0
EPS = 1e-6
QBLK = 128
NEG_INF = -1e30

MLA_H = 4
MLA_Q_LORA = 512
MLA_KV_LORA = 256
MLA_NOPE = 128
MLA_ROPE = 64
MLA_V = 128
MLA_SCALE = (MLA_NOPE + MLA_ROPE) ** -0.5
MLA_DK = MLA_NOPE + MLA_ROPE

HEAD_DIM = 128
ATTN_SCALE = HEAD_DIM ** -0.5
GQA_H = 4
GQA_KVH = 2
SWA_H = 4
SWA_KVH = 2
WINDOW = 128

RET_H = 4
RET_DK = 64
RET_DV = 128
RET_CHUNK = 128
RET_K_SCALE = RET_DK ** -0.5

N_BRANCH = 4
BRANCH_W = 512

N_GROUPS = 4
EXPERTS_PER_GROUP = 8
N_EXPERTS = N_GROUPS * EXPERTS_PER_GROUP
TOP_K = 2
D_EXPERT = 1024

N_CTX = BATCH * SEQ
N_LAT = DEC_BATCH * DEC_SEQ
N_TOK = N_CTX + N_LAT
N_COND = 8

W_IN_HEAD = MLA_Q_LORA + MLA_KV_LORA + MLA_ROPE
IN_WIDTH = (W_IN_HEAD + (GQA_H + 2 * GQA_KVH) * HEAD_DIM + 2 * RET_H * RET_DK + 3 * RET_H * RET_DV
            + (SWA_H + 2 * SWA_KVH) * HEAD_DIM + N_BRANCH * D_MODEL)
Z_TAIL = 1024
Z_WIDTH = Z_TAIL + IN_WIDTH - W_IN_HEAD
Z_OFF = dict(mq=0, mkv=512, mkr=768, gq=1024, gk=1536, gv=1792, rq=2048, rk=2304, rv=2560, rgf=3072, rgb=3584,
             sq=4096, sk=4608, sv=4864, gate=5120)
Z_TILE = 1024

MOE_BLK = 256
MOE_ASSIGN = N_TOK * TOP_K
MOE_NBLK = (MOE_ASSIGN + N_EXPERTS * (MOE_BLK - 1) + MOE_BLK - 1) // MOE_BLK
MOE_CAP = MOE_NBLK * MOE_BLK
WEIGHT_DMA_PRIORITY = 1
ROUTER_W = 128
R_W0, R_W1, R_E0, R_E1, R_RANK0, R_RANK1 = range(6)

VMEM_LIMIT = 56 * 1024 * 1024


def _params(*sem):
    return pltpu.CompilerParams(dimension_semantics=sem, vmem_limit_bytes=VMEM_LIMIT)


def _cond_row(i, tm):
    start = i * tm
    return jnp.where(start < N_CTX, 0, 1 + (start - N_CTX) // DEC_SEQ)


def _rms(x):
    return x * lax.rsqrt(jnp.mean(x * x, axis=-1, keepdims=True) + EPS)


def _token_specs(xs, tm, index_of):
    if len(xs) == 1:
        return [pl.BlockSpec((tm, D_MODEL), lambda *g: (index_of(*g), 0))]
    n_ct = N_CTX // tm
    return [pl.BlockSpec((tm, D_MODEL), lambda *g: (jnp.minimum(index_of(*g), n_ct - 1), 0),
                         pipeline_mode=pl.Buffered(1)),
            pl.BlockSpec((tm, D_MODEL), lambda *g: (jnp.maximum(index_of(*g) - n_ct, 0), 0),
                         pipeline_mode=pl.Buffered(1))]


def _token_tile(x_refs, i, tm):
    if len(x_refs) == 1:
        return x_refs[0][...]
    return jnp.where(i < N_CTX // tm, x_refs[0][...], x_refs[1][...])


def _ada_kernel(c_ref, w_ref, b_ref, o_ref):
    s = jax.nn.silu(c_ref[...]).astype(BF16)
    o_ref[0] = jnp.dot(s, w_ref[0].astype(BF16), preferred_element_type=F32) + b_ref[0]


def _ada(cond, ada_w, ada_b):
    tn = 1024
    n_out = 6 * D_MODEL
    return pl.pallas_call(
        _ada_kernel,
        grid=(DEPTH, n_out // tn),
        in_specs=[pl.BlockSpec((N_COND, D_MODEL), lambda l, j: (0, 0)),
                  pl.BlockSpec((1, D_MODEL, tn), lambda l, j: (l, 0, j)),
                  pl.BlockSpec((1, 1, tn), lambda l, j: (l, 0, j))],
        out_specs=pl.BlockSpec((1, N_COND, tn), lambda l, j: (l, 0, j)),
        out_shape=jax.ShapeDtypeStruct((DEPTH, N_COND, n_out), F32),
        compiler_params=_params("parallel", "parallel"),
        name="ada_mod",
    )(cond, ada_w, ada_b.reshape(DEPTH, 1, n_out))


def _w_in_kernel(w_ref, o_ref):
    w = w_ref[0]

    @pl.when(pl.program_id(1) == 0)
    def _():
        half = MLA_ROPE // 2
        mkr0 = W_IN_HEAD - MLA_ROPE
        o_ref[0, :, 0:W_IN_HEAD] = w[:, 0:W_IN_HEAD].astype(BF16)
        o_ref[0, :, W_IN_HEAD:W_IN_HEAD + half] = w[:, mkr0 + half:W_IN_HEAD].astype(BF16)
        o_ref[0, :, W_IN_HEAD + half:W_IN_HEAD + MLA_ROPE] = w[:, mkr0:mkr0 + half].astype(BF16)
        o_ref[0, :, W_IN_HEAD + MLA_ROPE:Z_TAIL] = jnp.zeros((w.shape[0], Z_TAIL - W_IN_HEAD - MLA_ROPE), BF16)

    @pl.when(pl.program_id(1) > 0)
    def _():
        o_ref[0] = w[:, W_IN_HEAD % LANES:W_IN_HEAD % LANES + Z_TILE].astype(BF16)


def _w_in_layout(w_in):
    tr = 512
    shift = W_IN_HEAD % LANES
    win = Z_TILE + LANES
    n_tiles = Z_WIDTH // Z_TILE
    pad_hi = (n_tiles - 2) * Z_TILE + W_IN_HEAD - shift + win - IN_WIDTH

    def src_col(j):
        return pl.multiple_of(jnp.where(j > 0, (j - 1) * Z_TILE + W_IN_HEAD - shift, 0), LANES)

    return pl.pallas_call(
        _w_in_kernel,
        grid=(DEPTH, Z_WIDTH // Z_TILE, D_MODEL // tr),
        in_specs=[pl.BlockSpec((pl.Element(1), pl.Element(tr), pl.Element(win, (0, pad_hi))),
                               lambda l, j, r: (l, pl.multiple_of(r * tr, tr), src_col(j)))],
        out_specs=pl.BlockSpec((1, tr, Z_TILE), lambda l, j, r: (l, r, j)),
        out_shape=jax.ShapeDtypeStruct((DEPTH, D_MODEL, Z_WIDTH), BF16),
        compiler_params=_params("parallel", "parallel", "parallel"),
        name="w_in_layout",
    )(w_in)


def _in_kernel(*refs, n_x, tm):
    x_refs, (mod_ref, g_ref, w_ref, z_ref, h_ref) = refs[:n_x], refs[n_x:]

    @pl.when(pl.program_id(1) == 0)
    def _():
        m = mod_ref[0]
        y = _rms(_token_tile(x_refs, pl.program_id(0), tm)) * g_ref[...]
        h_ref[...] = (y * (1.0 + m[:, D_MODEL:2 * D_MODEL]) + m[:, 0:D_MODEL]).astype(BF16)

    z_ref[...] = jnp.dot(h_ref[...], w_ref[...], preferred_element_type=F32)


def _in_proj(xs, mod, gain, w_all, layer):
    tm, tn = 1024, Z_TILE
    return pl.pallas_call(
        functools.partial(_in_kernel, n_x=len(xs), tm=tm),
        grid=(N_TOK // tm, Z_WIDTH // tn),
        in_specs=_token_specs(xs, tm, lambda i, j: i)
        + [pl.BlockSpec((1, 1, 6 * D_MODEL), lambda i, j: (_cond_row(i, tm), 0, 0)),
           pl.BlockSpec((1, D_MODEL), lambda i, j: (0, 0)),
           pl.BlockSpec((None, D_MODEL, tn), lambda i, j: (layer, 0, j))],
        out_specs=pl.BlockSpec((tm, tn), lambda i, j: (i, j)),
        out_shape=jax.ShapeDtypeStruct((N_TOK, Z_WIDTH), F32),
        scratch_shapes=[pltpu.VMEM((tm, D_MODEL), BF16)],
        compiler_params=_params("parallel", "arbitrary"),
        name="in_proj",
    )(*xs, mod, gain, w_all)


def _rope128(x, cos, sin):
    return x * cos + pltpu.roll(x, HEAD_DIM // 2, 1) * sin


def _mla_kv(ckv, kr, wuk_ref, wuv_ref, ka_ref, va_ref):
    cb = ckv.astype(BF16)
    kn = jnp.dot(cb, wuk_ref[...], preferred_element_type=F32)
    vv = jnp.dot(cb, wuv_ref[...], preferred_element_type=F32)
    for h in range(MLA_H):
        ka_ref[h, :, 0:MLA_NOPE] = kn[:, h * MLA_NOPE:(h + 1) * MLA_NOPE].astype(BF16)
        ka_ref[h, :, MLA_NOPE:MLA_DK] = kr.astype(BF16)
        va_ref[h] = vv[:, h * MLA_V:(h + 1) * MLA_V].astype(BF16)


def _prep_kernel(*refs, latent):
    (mq_ref, gq_ref, sq_ref, mkv_ref, gk_ref, gv_ref, sk_ref, sv_ref, mkr_ref,
     qn_ref, kvn_ref, gqn_ref, gkn_ref, wqn_ref, wqr_ref, wqs_ref, wuk_ref, wuv_ref) = refs[:18]
    refs = refs[18:]
    if latent:
        cos_ref, sin_ref, c64_ref, s64_ref = refs[:4]
        refs = refs[4:]
        cos, sin = cos_ref[...], sin_ref[...]
        c64, s64 = c64_ref[...], s64_ref[...]
    qa_ref, ka_ref, va_ref, qb_ref, kb_ref, vb_ref, qd_ref, kd_ref, vd_ref = refs[:9]
    refs = refs[9:]

    hn = (_rms(mq_ref[...]) * qn_ref[...]).astype(BF16)
    q_nope = jnp.dot(hn, wqn_ref[...], preferred_element_type=F32)
    q_rope = jnp.dot(hn, wqr_ref[...], preferred_element_type=F32)
    if latent:
        q_swap = jnp.dot(hn, wqs_ref[...], preferred_element_type=F32)
    for h in range(MLA_H):
        qa_ref[h, :, 0:MLA_NOPE] = (q_nope[:, h * MLA_NOPE:(h + 1) * MLA_NOPE] * MLA_SCALE).astype(BF16)
        qr = q_rope[:, h * MLA_ROPE:(h + 1) * MLA_ROPE]
        if latent:
            qr = qr * c64 + q_swap[:, h * MLA_ROPE:(h + 1) * MLA_ROPE] * s64
        qa_ref[h, :, MLA_NOPE:MLA_DK] = (qr * MLA_SCALE).astype(BF16)
    ckv = _rms(mkv_ref[...]) * kvn_ref[...]
    kr = mkr_ref[:, 0:MLA_ROPE]
    if latent:
        kr = kr * c64 + mkr_ref[:, MLA_ROPE:2 * MLA_ROPE] * s64
    _mla_kv(ckv, kr, wuk_ref, wuv_ref, ka_ref, va_ref)

    gq = gq_ref[...]
    for h in range(GQA_H):
        q = _rms(gq[:, h * HEAD_DIM:(h + 1) * HEAD_DIM]) * gqn_ref[...]
        if latent:
            q = _rope128(q, cos, sin)
        qb_ref[h] = (q * ATTN_SCALE).astype(BF16)
    gk = gk_ref[...]
    gv = gv_ref[...]
    kg = []
    for h in range(GQA_KVH):
        k = _rms(gk[:, h * HEAD_DIM:(h + 1) * HEAD_DIM]) * gkn_ref[...]
        kg.append(k)
        if latent:
            k = _rope128(k, cos, sin)
        kb_ref[h] = k.astype(BF16)
        vb_ref[h] = gv[:, h * HEAD_DIM:(h + 1) * HEAD_DIM].astype(BF16)

    sq = sq_ref[...]
    for h in range(SWA_H):
        q = sq[:, h * HEAD_DIM:(h + 1) * HEAD_DIM]
        if latent:
            q = _rope128(q, cos, sin)
        qd_ref[h] = (q * ATTN_SCALE).astype(BF16)
    sk = sk_ref[...]
    sv = sv_ref[...]
    for h in range(SWA_KVH):
        k = sk[:, h * HEAD_DIM:(h + 1) * HEAD_DIM]
        if latent:
            k = _rope128(k, cos, sin)
        kd_ref[h] = k.astype(BF16)
        vd_ref[h] = sv[:, h * HEAD_DIM:(h + 1) * HEAD_DIM].astype(BF16)

    if not latent:
        ckv_ref, kr_ref, kg_ref, vg_ref, ks_ref, vs_ref = refs
        ckv_ref[...] = ckv
        kr_ref[...] = kr
        for h in range(GQA_KVH):
            kg_ref[:, h * HEAD_DIM:(h + 1) * HEAD_DIM] = kg[h]
        vg_ref[...] = gv
        ks_ref[...] = sk
        vs_ref[...] = sv


def _prep(z, lw, tabs, latent):
    tm = 512
    rows = N_LAT if latent else N_CTX
    roff = (N_CTX if latent else 0) // tm
    n_t = DEC_SEQ // tm

    def zc(name, width):
        cb = Z_OFF[name] // width
        return pl.BlockSpec((tm, width), lambda i: (i + roff, cb))

    def full(a):
        nd = a.ndim
        return pl.BlockSpec(a.shape, lambda i: (0,) * nd)

    weights = [lw["q_norm"], lw["kv_norm"], lw["gq_norm"], lw["gk_norm"],
               lw["w_uq_nope"], lw["w_uq_rope"], lw["w_uq_swap"], lw["w_uk"], lw["w_uv"]]
    in_specs = [zc("mq", 512), zc("gq", 512), zc("sq", 512), zc("mkv", 256), zc("gk", 256), zc("gv", 256),
                zc("sk", 256), zc("sv", 256), zc("mkr", 128)] + [full(a) for a in weights]
    args = [z] * 9 + weights
    if latent:
        in_specs += [pl.BlockSpec((tm, HEAD_DIM), lambda i: (i % n_t, 0))] * 2
        in_specs += [pl.BlockSpec((tm, MLA_ROPE), lambda i: (i % n_t, 0))] * 2
        args += list(tabs)

    def heads(n, d):
        return (pl.BlockSpec((n, tm, d), lambda i: (0, i, 0)), jax.ShapeDtypeStruct((n, rows, d), BF16))

    outs = [heads(MLA_H, MLA_DK), heads(MLA_H, MLA_DK), heads(MLA_H, MLA_V),
            heads(GQA_H, HEAD_DIM), heads(GQA_KVH, HEAD_DIM), heads(GQA_KVH, HEAD_DIM),
            heads(SWA_H, HEAD_DIM), heads(SWA_KVH, HEAD_DIM), heads(SWA_KVH, HEAD_DIM)]
    if not latent:
        kv_w = GQA_KVH * HEAD_DIM
        for d in (MLA_KV_LORA, MLA_ROPE, kv_w, kv_w, kv_w, kv_w):
            outs.append((pl.BlockSpec((tm, d), lambda i: (i, 0)), jax.ShapeDtypeStruct((rows, d), F32)))
    return pl.pallas_call(
        functools.partial(_prep_kernel, latent=latent),
        grid=(rows // tm,),
        in_specs=in_specs,
        out_specs=[o[0] for o in outs],
        out_shape=[o[1] for o in outs],
        compiler_params=_params("parallel"),
        name="prep_latent" if latent else "prep_context",
    )(*args)


def _mla_ctx_kernel(ckv_ref, kr_ref, wuk_ref, wuv_ref, ka_ref, va_ref):
    _mla_kv(ckv_ref[...], kr_ref[...], wuk_ref, wuv_ref, ka_ref, va_ref)


def _mla_ctx(ckv, kr, lw):
    rows = ckv.shape[0]
    return pl.pallas_call(
        _mla_ctx_kernel,
        out_shape=[jax.ShapeDtypeStruct((MLA_H, rows, MLA_DK), BF16),
                   jax.ShapeDtypeStruct((MLA_H, rows, MLA_V), BF16)],
        name="mla_ctx_kv",
    )(ckv, kr, lw["w_uk"], lw["w_uv"])


def _softmax_pv(q, k, v, sink):
    s = lax.dot_general(q, k, (((1,), (1,)), ((), ())), preferred_element_type=F32)
    m = jnp.max(s, axis=-1, keepdims=True)
    if sink is not None:
        m = jnp.maximum(m, sink)
    p = jnp.exp(s - m)
    l = jnp.sum(p, axis=-1, keepdims=True)
    if sink is not None:
        l = l + jnp.exp(sink - m)
    return jnp.dot(p.astype(BF16), v, preferred_element_type=F32) / l


def _ctx_attn_kernel(sink_ref, qa, ka, va, qb, kb, vb, qd, kd, vd, oa, ob, od):
    for h in range(MLA_H):
        oa[:, h * MLA_V:(h + 1) * MLA_V] = _softmax_pv(qa[h], ka[h], va[h], None).astype(BF16)
    for h in range(GQA_H):
        kv = h // (GQA_H // GQA_KVH)
        ob[:, h * HEAD_DIM:(h + 1) * HEAD_DIM] = _softmax_pv(qb[h], kb[kv], vb[kv], None).astype(BF16)
    for h in range(SWA_H):
        kv = h // (SWA_H // SWA_KVH)
        od[:, h * HEAD_DIM:(h + 1) * HEAD_DIM] = _softmax_pv(qd[h], kd[kv], vd[kv], sink_ref[h]).astype(BF16)


def _ctx_attention(operands, sink):
    def spec(a):
        return pl.BlockSpec((a.shape[0], SEQ, a.shape[2]), lambda b: (0, b, 0))

    out = jax.ShapeDtypeStruct((N_CTX, BRANCH_W), BF16)
    return pl.pallas_call(
        _ctx_attn_kernel,
        grid=(BATCH,),
        in_specs=[pl.BlockSpec(memory_space=pltpu.SMEM)] + [spec(a) for a in operands],
        out_specs=[pl.BlockSpec((SEQ, BRANCH_W), lambda b: (b, 0))] * 3,
        out_shape=[out, out, out],
        compiler_params=_params("parallel"),
        name="context_attention",
    )(sink, *operands)


KEY_CHUNK = 256


def _flash_latent_kernel(q_ref, k_ref, v_ref, k0_ref, v0_ref, o_ref, m_sc, l_sc, acc_sc, *, groups):
    dv = v_ref.shape[-1]
    n_chunks = k_ref.shape[1] // KEY_CHUNK
    dn = (((1,), (1,)), ((), ()))

    def scores(g, k):
        return lax.dot_general(q_ref[g], k, dn, preferred_element_type=F32)

    def lane_max(s):
        m = s[:, 0:LANES]
        for c in range(1, s.shape[1] // LANES):
            m = jnp.maximum(m, s[:, c * LANES:(c + 1) * LANES])
        return m

    for g in range(groups):
        m_sc[g] = lane_max(scores(g, k0_ref[0]))

    def max_body(j, carry):
        off = pl.multiple_of(j * KEY_CHUNK, KEY_CHUNK)
        for g in range(groups):
            m_sc[g] = jnp.maximum(m_sc[g], lane_max(scores(g, k_ref[0, pl.ds(off, KEY_CHUNK), :])))
        return carry

    lax.fori_loop(0, n_chunks, max_body, 0, unroll=True)
    for g in range(groups):
        m_sc[g] = jnp.broadcast_to(jnp.max(m_sc[g], axis=-1, keepdims=True), m_sc.shape[1:])

    def accumulate(g, k, v, first):
        s = scores(g, k)
        m = m_sc[g]
        ps = [jnp.exp(s[:, c * LANES:(c + 1) * LANES] - m) for c in range(s.shape[1] // LANES)]
        lsum = ps[0]
        for p in ps[1:]:
            lsum = lsum + p
        pv = jnp.dot(jnp.concatenate(ps, axis=1).astype(BF16), v, preferred_element_type=F32)
        if first:
            l_sc[g] = lsum
            acc_sc[g] = pv
        else:
            l_sc[g] += lsum
            acc_sc[g] += pv

    for g in range(groups):
        accumulate(g, k0_ref[0], v0_ref[0], True)

    def sum_body(j, carry):
        off = pl.multiple_of(j * KEY_CHUNK, KEY_CHUNK)
        for g in range(groups):
            accumulate(g, k_ref[0, pl.ds(off, KEY_CHUNK), :], v_ref[0, pl.ds(off, KEY_CHUNK), :], False)
        return carry

    lax.fori_loop(0, n_chunks, sum_body, 0, unroll=True)
    for g in range(groups):
        l = jnp.sum(l_sc[g], axis=-1, keepdims=True)
        o_ref[:, g * dv:(g + 1) * dv] = (acc_sc[g] / l).astype(o_ref.dtype)


def _flash_latent(q, k, v, ctx):
    tq = 1024
    n_h, _, dk = q.shape
    kvh, _, dv = v.shape
    groups = n_h // kvh
    nq = DEC_SEQ // tq
    return pl.pallas_call(
        functools.partial(_flash_latent_kernel, groups=groups),
        grid=(DEC_BATCH, kvh, nq),
        in_specs=[pl.BlockSpec((groups, tq, dk), lambda b, h, i: (h, b * nq + i, 0)),
                  pl.BlockSpec((1, DEC_SEQ, dk), lambda b, h, i: (h, b, 0)),
                  pl.BlockSpec((1, DEC_SEQ, dv), lambda b, h, i: (h, b, 0)),
                  pl.BlockSpec((1, PAST_LEN, dk), lambda b, h, i: (h, b, 0)),
                  pl.BlockSpec((1, PAST_LEN, dv), lambda b, h, i: (h, b, 0))],
        out_specs=pl.BlockSpec((tq, groups * dv), lambda b, h, i: (b * nq + i, h)),
        out_shape=jax.ShapeDtypeStruct((N_LAT, n_h * dv), BF16),
        scratch_shapes=[pltpu.VMEM((groups, tq, LANES), F32), pltpu.VMEM((groups, tq, LANES), F32),
                        pltpu.VMEM((groups, tq, dv), F32)],
        compiler_params=_params("parallel", "parallel", "parallel"),
        name="flash_latent",
    )(q, k, v, *ctx)


WIN_TQ = 512
WIN_SPAN = WIN_TQ + 2 * WINDOW


def _lanewise(op, x):
    r = x[:, 0:LANES]
    for c in range(1, x.shape[1] // LANES):
        r = op(r, x[:, c * LANES:(c + 1) * LANES])
    return r


def _exp_tiles(s, m):
    return jnp.concatenate([jnp.exp(s[:, c * LANES:(c + 1) * LANES] - m) for c in range(s.shape[1] // LANES)],
                           axis=1)


def _window_kernel(sink_ref, q_ref, k_ref, v_ref, k0_ref, v0_ref, o_ref, *, groups):
    head0 = pl.program_id(1) * groups
    t = k_ref.shape[1]
    q0 = pl.program_id(2) * WIN_TQ
    k_start = pl.multiple_of(jnp.clip(q0 - WINDOW, 0, t - WIN_SPAN), LANES)
    qpos = q0 + lax.broadcasted_iota(jnp.int32, (WIN_TQ, WIN_SPAN), 0)
    kpos = k_start + lax.broadcasted_iota(jnp.int32, (WIN_TQ, WIN_SPAN), 1)
    band = jnp.abs(kpos - qpos) <= WINDOW
    kw = k_ref[0, pl.ds(k_start, WIN_SPAN), :]
    vw = v_ref[0, pl.ds(k_start, WIN_SPAN), :]
    dn = (((1,), (1,)), ((), ()))
    for g in range(groups):
        q = q_ref[g]
        s_loc = jnp.where(band, lax.dot_general(q, kw, dn, preferred_element_type=F32), NEG_INF)
        s_ctx = lax.dot_general(q, k0_ref[0], dn, preferred_element_type=F32)
        snk = sink_ref[head0 + g]
        m = jnp.max(jnp.maximum(_lanewise(jnp.maximum, s_loc), _lanewise(jnp.maximum, s_ctx)),
                    axis=-1, keepdims=True)
        m = jnp.broadcast_to(jnp.maximum(m, snk), (WIN_TQ, LANES))
        p_loc = _exp_tiles(s_loc, m)
        p_ctx = _exp_tiles(s_ctx, m)
        den = (jnp.sum(_lanewise(jnp.add, p_loc) + _lanewise(jnp.add, p_ctx), axis=-1, keepdims=True)
               + jnp.exp(snk - m[:, 0:1]))
        acc = (jnp.dot(p_loc.astype(BF16), vw, preferred_element_type=F32)
               + jnp.dot(p_ctx.astype(BF16), v0_ref[0], preferred_element_type=F32))
        o_ref[:, g * HEAD_DIM:(g + 1) * HEAD_DIM] = (acc / den).astype(o_ref.dtype)


def _window(q, k, v, k0, v0, sink):
    n_h, kvh = q.shape[0], k.shape[0]
    groups = n_h // kvh
    nq = DEC_SEQ // WIN_TQ
    full = pl.BlockSpec((1, DEC_SEQ, HEAD_DIM), lambda b, h, n: (h, b, 0))
    ctx_spec = pl.BlockSpec((1, PAST_LEN, HEAD_DIM), lambda b, h, n: (h, b, 0))
    return pl.pallas_call(
        functools.partial(_window_kernel, groups=groups),
        grid=(DEC_BATCH, kvh, nq),
        in_specs=[pl.BlockSpec(memory_space=pltpu.SMEM),
                  pl.BlockSpec((groups, WIN_TQ, HEAD_DIM), lambda b, h, n: (h, b * nq + n, 0)),
                  full, full, ctx_spec, ctx_spec],
        out_specs=pl.BlockSpec((WIN_TQ, groups * HEAD_DIM), lambda b, h, n: (b * nq + n, h)),
        out_shape=jax.ShapeDtypeStruct((N_LAT, n_h * HEAD_DIM), BF16),
        compiler_params=_params("parallel", "parallel", "parallel"),
        name="window_attention",
    )(sink, q, k, v, k0, v0)


def _ret_kernel(*refs, reverse, has_prev):
    lg_ref, q_ref, k_ref, v_ref, gate_ref, gn_ref, s0_ref = refs[:7]
    refs = refs[7:]
    if has_prev:
        prev_ref, refs = refs[0], refs[1:]
    o_ref, sfin_ref, s_sc = refs
    c = pl.program_id(1)

    @pl.when(c == 0)
    def _():
        s_sc[...] = s0_ref[0]

    cs = RET_CHUNK
    row = lax.broadcasted_iota(jnp.int32, (cs, cs), 0).astype(F32)
    col = lax.broadcasted_iota(jnp.int32, (cs, cs), 1).astype(F32)
    pos = lax.broadcasted_iota(jnp.int32, (cs, 1), 0).astype(F32)
    diff = (col - row) if reverse else (row - col)
    scan_pos = (cs - 1.0 - pos) if reverse else pos
    q_all, k_all, v_all, gate_all = q_ref[...], k_ref[...], v_ref[...], gate_ref[...]
    for h in range(RET_H):
        lg = lg_ref[h]
        dmask = jnp.where(diff >= 0, jnp.exp(jnp.maximum(diff, 0.0) * lg), 0.0)
        q_dec = jnp.exp((scan_pos + 1.0) * lg)
        k_dec = jnp.exp((cs - 1.0 - scan_pos) * lg)
        c_dec = jnp.exp(cs * lg)
        q = q_all[:, h * RET_DK:(h + 1) * RET_DK]
        k = k_all[:, h * RET_DK:(h + 1) * RET_DK] * RET_K_SCALE
        v = v_all[:, h * RET_DV:(h + 1) * RET_DV].astype(BF16)
        s = s_sc[h]
        a = lax.dot_general(q.astype(BF16), k.astype(BF16), (((1,), (1,)), ((), ())),
                            preferred_element_type=F32) * dmask
        o = (jnp.dot(a.astype(BF16), v, preferred_element_type=F32)
             + jnp.dot((q * q_dec).astype(BF16), s.astype(BF16), preferred_element_type=F32))
        kd_t = jnp.transpose(k * k_dec).astype(BF16)
        s_sc[h] = s * c_dec + jnp.dot(kd_t, v, preferred_element_type=F32)
        mu = jnp.mean(o, axis=-1, keepdims=True)
        var = jnp.mean(jnp.square(o - mu), axis=-1, keepdims=True)
        y = (o - mu) * lax.rsqrt(var + EPS) * gn_ref[:, h * RET_DV:(h + 1) * RET_DV]
        y = jax.nn.silu(gate_all[:, h * RET_DV:(h + 1) * RET_DV]) * y
        if has_prev:
            y = prev_ref[:, h * RET_DV:(h + 1) * RET_DV] + y
        o_ref[:, h * RET_DV:(h + 1) * RET_DV] = y.astype(o_ref.dtype)

    @pl.when(c == pl.num_programs(1) - 1)
    def _():
        sfin_ref[0] = s_sc[...]


def _retention(z, log_gamma, gn, s0, prev, *, latent, reverse):
    batch, t = (DEC_BATCH, DEC_SEQ) if latent else (BATCH, SEQ)
    nc = t // RET_CHUNK
    roff = (N_CTX if latent else 0) // RET_CHUNK
    gate_name = "rgb" if reverse else "rgf"

    def blk(b, c):
        return b * nc + (nc - 1 - c if reverse else c)

    def zc(name, width):
        cb = Z_OFF[name] // width
        return pl.BlockSpec((RET_CHUNK, width), lambda b, c: (roff + blk(b, c), cb))

    w = RET_H * RET_DV
    in_specs = [pl.BlockSpec(memory_space=pltpu.SMEM), zc("rq", 256), zc("rk", 256), zc("rv", 512),
                zc(gate_name, 512), pl.BlockSpec((1, w), lambda b, c: (0, 0)),
                pl.BlockSpec((1, RET_H, RET_DK, RET_DV), lambda b, c: (b, 0, 0, 0))]
    args = [log_gamma, z, z, z, z, gn, s0]
    if prev is not None:
        in_specs.append(pl.BlockSpec((RET_CHUNK, w), lambda b, c: (blk(b, c), 0)))
        args.append(prev)
    return pl.pallas_call(
        functools.partial(_ret_kernel, reverse=reverse, has_prev=prev is not None),
        grid=(batch, nc),
        in_specs=in_specs,
        out_specs=[pl.BlockSpec((RET_CHUNK, w), lambda b, c: (blk(b, c), 0)),
                   pl.BlockSpec((1, RET_H, RET_DK, RET_DV), lambda b, c: (b, 0, 0, 0))],
        out_shape=[jax.ShapeDtypeStruct((batch * t, w), BF16 if reverse else F32),
                   jax.ShapeDtypeStruct((batch, RET_H, RET_DK, RET_DV), F32)],
        scratch_shapes=[pltpu.VMEM((RET_H, RET_DK, RET_DV), F32)],
        compiler_params=_params("parallel", "arbitrary"),
        name="retention_bwd" if reverse else "retention_fwd",
    )(*args)


def _merge_kernel(*refs, n_x, tm):
    x_refs, refs = refs[:n_x], refs[n_x:]
    ctx_refs, lat_refs = refs[:N_BRANCH], refs[N_BRANCH:2 * N_BRANCH]
    (gate_ref, wb_ref, wo_ref, mod_ref, g_ref, wr_ref, br_ref, xo_ref, h_ref, lg_ref) = refs[2 * N_BRANCH:]
    i = pl.program_id(0)
    is_ctx = i < N_CTX // tm

    merged = None
    for n in range(N_BRANCH):
        branch = jnp.where(is_ctx, ctx_refs[n][...], lat_refs[n][...])
        t = jnp.dot(branch, wb_ref[n], preferred_element_type=F32)
        term = jax.nn.sigmoid(gate_ref[:, n * D_MODEL:(n + 1) * D_MODEL]) * t
        merged = term if merged is None else merged + term

    m = mod_ref[0]
    out = jnp.dot(merged.astype(BF16), wo_ref[...], preferred_element_type=F32)
    x = _token_tile(x_refs, i, tm) + m[:, 2 * D_MODEL:3 * D_MODEL] * out
    xo_ref[...] = x
    h = (_rms(x) * g_ref[...]) * (1.0 + m[:, 4 * D_MODEL:5 * D_MODEL]) + m[:, 3 * D_MODEL:4 * D_MODEL]
    h_ref[...] = h
    lg_ref[...] = jnp.dot(h.astype(BF16), wr_ref[...], preferred_element_type=F32) + br_ref[...]


def _merge(xs, ctx_branches, lat_branches, z, w_branch, w_out, mod, gain, w_router, b_router):
    tm = 256
    n_ct = N_CTX // tm
    row = lambda i: (i, 0)
    const2 = lambda i: (0, 0)
    resident = pl.Buffered(1)
    return pl.pallas_call(
        functools.partial(_merge_kernel, n_x=len(xs), tm=tm),
        grid=(N_TOK // tm,),
        in_specs=_token_specs(xs, tm, lambda i: i)
        + [pl.BlockSpec((tm, BRANCH_W), lambda i: (jnp.minimum(i, n_ct - 1), 0))] * N_BRANCH
        + [pl.BlockSpec((tm, BRANCH_W), lambda i: (jnp.maximum(i - n_ct, 0), 0))] * N_BRANCH
        + [pl.BlockSpec((pl.Element(tm), pl.Element(N_BRANCH * D_MODEL)),
                        lambda i: (pl.multiple_of(i * tm, tm), Z_OFF["gate"])),
           pl.BlockSpec((N_BRANCH, BRANCH_W, D_MODEL), lambda i: (0, 0, 0), pipeline_mode=resident),
           pl.BlockSpec((D_MODEL, D_MODEL), const2, pipeline_mode=resident),
           pl.BlockSpec((1, 1, 6 * D_MODEL), lambda i: (_cond_row(i, tm), 0, 0)),
           pl.BlockSpec((1, D_MODEL), const2),
           pl.BlockSpec((D_MODEL, ROUTER_W), const2),
           pl.BlockSpec((1, ROUTER_W), const2)],
        out_specs=[pl.BlockSpec((tm, D_MODEL), row), pl.BlockSpec((tm, D_MODEL), row),
                   pl.BlockSpec((tm, ROUTER_W), row)],
        out_shape=[jax.ShapeDtypeStruct((N_TOK, D_MODEL), F32), jax.ShapeDtypeStruct((N_TOK, D_MODEL), F32),
                   jax.ShapeDtypeStruct((N_TOK, ROUTER_W), F32)],
        compiler_params=_params("parallel"),
        name="merge_out_proj",
    )(*xs, *ctx_branches, *lat_branches, z, w_branch, w_out, mod, gain, w_router, b_router)


def _route_kernel(lg_ref, rec_ref, cnt_ref, tri_sc, carry_sc, *, tm):
    @pl.when(pl.program_id(0) == 0)
    def _():
        r = lax.broadcasted_iota(jnp.int32, (tm, tm), 0)
        c = lax.broadcasted_iota(jnp.int32, (tm, tm), 1)
        tri_sc[...] = (c < r).astype(BF16)
        carry_sc[...] = jnp.zeros(carry_sc.shape, F32)

    lg = lg_ref[...]
    lane = lax.broadcasted_iota(jnp.int32, lg.shape, 1)
    big = jnp.int32(ROUTER_W)

    def first_max(vals):
        top = jnp.max(vals, axis=-1, keepdims=True)
        return top, jnp.min(jnp.where(vals == top, lane, big), axis=-1, keepdims=True)

    g_logits = jnp.where(lane < N_GROUPS, lg, -jnp.inf)
    g_top, grp = first_max(g_logits)
    p_grp = 1.0 / jnp.sum(jnp.exp(g_logits - g_top), axis=-1, keepdims=True)
    lo = N_GROUPS + grp * EXPERTS_PER_GROUP
    e_logits = jnp.where((lane >= lo) & (lane < lo + EXPERTS_PER_GROUP), lg, -jnp.inf)
    v0, i0 = first_max(e_logits)
    v1, i1 = first_max(jnp.where(lane == i0, -jnp.inf, e_logits))
    e1 = jnp.exp(v1 - v0)
    den = 1.0 + e1
    w0 = p_grp * (1.0 / den)
    w1 = p_grp * (e1 / den)

    hot0 = (lane == i0).astype(F32)
    hot1 = (lane == i1).astype(F32)
    before = jnp.dot(tri_sc[...], (hot0 + hot1).astype(BF16), preferred_element_type=F32) + carry_sc[...]
    rank0 = jnp.sum(before * hot0, axis=-1, keepdims=True)
    rank1 = jnp.sum(before * hot1, axis=-1, keepdims=True)
    carry_sc[...] += jnp.sum(hot0 + hot1, axis=0, keepdims=True)

    rec = jnp.zeros(lg.shape, F32)
    for slot, val in ((R_W0, w0), (R_W1, w1), (R_E0, (i0 - N_GROUPS).astype(F32)),
                      (R_E1, (i1 - N_GROUPS).astype(F32)), (R_RANK0, rank0), (R_RANK1, rank1)):
        rec = jnp.where(lane == slot, val, rec)
    rec_ref[...] = rec
    cnt_ref[...] = jnp.broadcast_to(carry_sc[...], cnt_ref.shape)


def _route(logits):
    tm = 512
    rec, cnt = pl.pallas_call(
        functools.partial(_route_kernel, tm=tm),
        grid=(N_TOK // tm,),
        in_specs=[pl.BlockSpec((tm, ROUTER_W), lambda i: (i, 0))],
        out_specs=[pl.BlockSpec((tm, ROUTER_W), lambda i: (i, 0)), pl.BlockSpec((8, ROUTER_W), lambda i: (0, 0))],
        out_shape=[jax.ShapeDtypeStruct((N_TOK, ROUTER_W), F32), jax.ShapeDtypeStruct((8, ROUTER_W), F32)],
        scratch_shapes=[pltpu.VMEM((tm, tm), BF16), pltpu.VMEM((1, ROUTER_W), F32)],
        compiler_params=_params("arbitrary"),
        name="moe_route",
    )(logits)
    expert = rec[:, R_E0:R_E1 + 1].astype(jnp.int32)
    rank = rec[:, R_RANK0:R_RANK1 + 1].astype(jnp.int32)
    counts = cnt[0, N_GROUPS:N_GROUPS + N_EXPERTS].astype(jnp.int32)
    padded = (counts + MOE_BLK - 1) // MOE_BLK * MOE_BLK
    pad_end = jnp.cumsum(padded)
    dest = ((pad_end - padded)[expert] + rank).reshape(-1)
    token = jnp.arange(MOE_ASSIGN, dtype=jnp.int32) // TOP_K
    src = jnp.zeros((MOE_CAP,), jnp.int32).at[dest].set(token)
    blk_start = jnp.arange(MOE_NBLK, dtype=jnp.int32) * MOE_BLK
    blk_e = jnp.minimum(jnp.sum(pad_end[None, :] <= blk_start[:, None], axis=1), N_EXPERTS - 1).astype(jnp.int32)
    n_used = (pad_end[-1:] // MOE_BLK).astype(jnp.int32)
    ids = jnp.arange(N_EXPERTS, dtype=jnp.int32)
    later = jnp.where((ids[None, :] > ids[:, None]) & (counts[None, :] > 0), ids[None, :], N_EXPERTS)
    nxt = jnp.min(later, axis=1)
    nxt_e = jnp.where(nxt < N_EXPERTS, nxt, -1).astype(jnp.int32)[blk_e]
    return rec, dest, src, blk_e, nxt_e, n_used


def _start_row_gather(src_hbm, dst, sem, index_of, n_rows, inline=False):
    def issue(r, carry):
        pltpu.make_async_copy(src_hbm.at[pl.ds(index_of(r), 1)], dst.at[pl.ds(r, 1)], sem).start()
        return carry

    if inline:
        for r in range(n_rows):
            issue(r, 0)
    else:
        lax.fori_loop(0, n_rows, issue, 0, unroll=8)


def _wait_row_gather(src_hbm, dst, sem, n_rows):
    pltpu.make_async_copy(src_hbm.at[pl.ds(0, n_rows)], dst, sem).wait()


def _expert_kernel(be_ref, nx_ref, nu_ref, src_ref, h_hbm, wg_hbm, wu_hbm, wd_hbm, y_ref,
                   stage_g, stage_u, stage_d, res_g, res_u, res_d, xbuf, wsem, gsem, *, layer):
    i = pl.program_id(0)
    n_used = nu_ref[0]

    def weight_copies(e):
        return [pltpu.make_async_copy(w.at[layer, e], st, wsem.at[k])
                for k, (w, st) in enumerate(((wg_hbm, stage_g), (wu_hbm, stage_u), (wd_hbm, stage_d)))]

    def gather_start(blk, slot, inline=False):
        _start_row_gather(h_hbm, xbuf.at[slot], gsem.at[slot], lambda r: src_ref[blk * MOE_BLK + r], MOE_BLK,
                          inline)

    def to_bf16(stage, res):
        rows = 256

        def body(c, carry):
            r = pl.multiple_of(c * rows, rows)
            res[pl.ds(r, rows), :] = stage[pl.ds(r, rows), :].astype(BF16)
            return carry

        lax.fori_loop(0, stage.shape[0] // rows, body, 0)

    @pl.when(i < n_used)
    def _():
        e = be_ref[i]
        first = (i == 0) | (e != be_ref[jnp.maximum(i - 1, 0)])

        @pl.when(i == 0)
        def _():
            for cp in weight_copies(e):
                cp.start(priority=WEIGHT_DMA_PRIORITY)
            gather_start(0, 0)

        @pl.when(first)
        def _():
            for cp in weight_copies(e):
                cp.wait()
            to_bf16(stage_g, res_g)
            to_bf16(stage_u, res_u)
            to_bf16(stage_d, res_d)
            nxt = nx_ref[i]

            @pl.when(nxt >= 0)
            def _():
                for cp in weight_copies(nxt):
                    cp.start(priority=WEIGHT_DMA_PRIORITY)

        slot = i % 2
        _wait_row_gather(h_hbm, xbuf.at[slot], gsem.at[slot], MOE_BLK)
        gather_start(jnp.minimum(i + 1, n_used - 1), 1 - slot, inline=True)

        x = xbuf[slot].astype(BF16)
        g = jnp.dot(x, res_g[...], preferred_element_type=F32)
        u = jnp.dot(x, res_u[...], preferred_element_type=F32)
        a = (jax.nn.silu(g) * u).astype(BF16)
        y_ref[...] = jnp.dot(a, res_d[...], preferred_element_type=F32)

        @pl.when(i == n_used - 1)
        def _():
            _wait_row_gather(h_hbm, xbuf.at[1 - slot], gsem.at[1 - slot], MOE_BLK)

    @pl.when(i >= n_used)
    def _():
        y_ref[...] = jnp.zeros(y_ref.shape, F32)


def _experts(h, blk_e, nxt_e, n_used, src, w_gate, w_up, w_down, layer):
    any_spec = pl.BlockSpec(memory_space=pl.ANY)
    return pl.pallas_call(
        functools.partial(_expert_kernel, layer=layer),
        grid_spec=pltpu.PrefetchScalarGridSpec(
            num_scalar_prefetch=4,
            grid=(MOE_NBLK,),
            in_specs=[any_spec, any_spec, any_spec, any_spec],
            out_specs=pl.BlockSpec((MOE_BLK, D_MODEL), lambda i, *_: (i, 0)),
            scratch_shapes=[pltpu.VMEM((D_MODEL, D_EXPERT), F32), pltpu.VMEM((D_MODEL, D_EXPERT), F32),
                            pltpu.VMEM((D_EXPERT, D_MODEL), F32),
                            pltpu.VMEM((D_MODEL, D_EXPERT), BF16), pltpu.VMEM((D_MODEL, D_EXPERT), BF16),
                            pltpu.VMEM((D_EXPERT, D_MODEL), BF16),
                            pltpu.VMEM((2, MOE_BLK, D_MODEL), F32),
                            pltpu.SemaphoreType.DMA((3,)), pltpu.SemaphoreType.DMA((2,))]),
        out_shape=jax.ShapeDtypeStruct((MOE_CAP, D_MODEL), F32),
        compiler_params=_params("arbitrary"),
        name="moe_experts",
    )(blk_e, nxt_e, n_used, src, h, w_gate, w_up, w_down)


def _combine_kernel(*refs, tm, final):
    dest_ref, yp_hbm, x_ref, rec_ref, mod_ref = refs[:5]
    refs = refs[5:]
    if final:
        fn_ref, yc_ref, yl_ref = refs[:3]
    else:
        xo_ref = refs[0]
    ybuf, sem = refs[-2:]
    i = pl.program_id(0)

    def gather_start(tile, slot):
        for k in range(TOP_K):
            _start_row_gather(yp_hbm, ybuf.at[slot, k], sem.at[slot, k],
                              lambda r, k=k: dest_ref[(tile * tm + r) * TOP_K + k], tm)

    @pl.when(i == 0)
    def _():
        gather_start(0, 0)

    slot = i % 2
    for k in range(TOP_K):
        _wait_row_gather(yp_hbm, ybuf.at[slot, k], sem.at[slot, k], tm)

    @pl.when(i + 1 < pl.num_programs(0))
    def _():
        gather_start(i + 1, 1 - slot)

    rec = rec_ref[...]
    y = ybuf[slot, 0] * rec[:, R_W0:R_W0 + 1] + ybuf[slot, 1] * rec[:, R_W1:R_W1 + 1]
    x = x_ref[...] + mod_ref[0][:, 5 * D_MODEL:6 * D_MODEL] * y
    if final:
        y = _rms(x) * fn_ref[...]

        @pl.when(i < N_CTX // tm)
        def _():
            yc_ref[...] = y

        @pl.when(i >= N_CTX // tm)
        def _():
            yl_ref[...] = y
    else:
        xo_ref[...] = x


def _combine(yp, dest, x, rec, mod, final_gain):
    tm = 256
    n_ct = N_CTX // tm
    final = final_gain is not None
    row = lambda i, d: (i, 0)
    in_specs = [pl.BlockSpec(memory_space=pl.ANY),
                pl.BlockSpec((tm, D_MODEL), row),
                pl.BlockSpec((tm, ROUTER_W), row),
                pl.BlockSpec((1, 1, 6 * D_MODEL), lambda i, d: (_cond_row(i, tm), 0, 0))]
    args = [dest, yp, x, rec, mod]
    if final:
        in_specs.append(pl.BlockSpec((1, D_MODEL), lambda i, d: (0, 0)))
        args.append(final_gain)
        out_specs = [pl.BlockSpec((tm, D_MODEL), lambda i, d: (jnp.minimum(i, n_ct - 1), 0)),
                     pl.BlockSpec((tm, D_MODEL), lambda i, d: (jnp.maximum(i - n_ct, 0), 0))]
        out_shape = [jax.ShapeDtypeStruct((N_CTX, D_MODEL), F32), jax.ShapeDtypeStruct((N_LAT, D_MODEL), F32)]
    else:
        out_specs = pl.BlockSpec((tm, D_MODEL), row)
        out_shape = jax.ShapeDtypeStruct((N_TOK, D_MODEL), F32)
    return pl.pallas_call(
        functools.partial(_combine_kernel, tm=tm, final=final),
        grid_spec=pltpu.PrefetchScalarGridSpec(
            num_scalar_prefetch=1,
            grid=(N_TOK // tm,),
            in_specs=in_specs,
            out_specs=out_specs,
            scratch_shapes=[pltpu.VMEM((2, TOP_K, tm, D_MODEL), F32), pltpu.SemaphoreType.DMA((2, TOP_K))]),
        out_shape=out_shape,
        compiler_params=_params("arbitrary"),
        name="moe_combine",
    )(*args)


def _swap_halves(w, chunk):
    k, n = w.shape
    w = w.reshape(k, n // chunk, 2, chunk // 2)
    return w[:, :, ::-1, :].reshape(k, n)


def _rope_tables(rot_dim):
    rows = DEC_SEQ // GRID_W
    row = jnp.repeat(jnp.arange(rows), GRID_W).astype(F32)
    col = (jnp.arange(rows * GRID_W) % GRID_W).astype(F32)
    axis_dim = rot_dim // 2
    inv = ROPE_THETA ** (-jnp.arange(0, axis_dim, 2, dtype=F32) / axis_dim)
    ang = jnp.concatenate([row[:, None] * inv, col[:, None] * inv], axis=-1)
    c, s = jnp.cos(ang), jnp.sin(ang)
    return jnp.concatenate([c, c], axis=-1), jnp.concatenate([-s, s], axis=-1)


def _head_major(cache):
    b, l, kvh, d = cache.shape
    return jnp.transpose(cache, (2, 0, 1, 3)).reshape(kvh, b * l, d).astype(BF16)


def kernel(x_prompt, x_sample, c, cache_mla_ckv, cache_mla_krope, cache_gqa_k, cache_gqa_v, cache_swa_k,
           cache_swa_v, state_ret, c_ctx, ada_w, ada_b, norm_mix, norm_ffn, w_in, mla_q_norm, mla_kv_norm,
           mla_w_uq, mla_w_ukv, gqa_q_norm, gqa_k_norm, ret_decay_logit, ret_gn, swa_sink, w_branch, w_out,
           router_group_w, router_group_b, router_expert_w, router_expert_b, moe_w_gate, moe_w_up, moe_w_down,
           final_norm):
    xs = (x_prompt.reshape(N_CTX, D_MODEL), x_sample.reshape(N_LAT, D_MODEL))
    cond = jnp.concatenate([c_ctx[None, :], c, jnp.zeros((N_COND - 1 - DEC_BATCH, D_MODEL), F32)], axis=0)
    mod_all = _ada(cond, ada_w, ada_b).reshape(DEPTH, N_COND, 1, 6 * D_MODEL)

    w_in_r = _w_in_layout(w_in)
    tabs = _rope_tables(HEAD_DIM) + _rope_tables(MLA_ROPE)
    log_gamma = jax.nn.log_sigmoid(ret_decay_logit.astype(F32))
    uq = mla_w_uq.reshape(DEPTH, MLA_Q_LORA, MLA_H, MLA_DK)
    ukv = mla_w_ukv.reshape(DEPTH, MLA_KV_LORA, MLA_H, MLA_NOPE + MLA_V)
    w_router = jnp.concatenate(
        [router_group_w, router_expert_w,
         jnp.zeros((DEPTH, D_MODEL, ROUTER_W - N_GROUPS - N_EXPERTS), F32)], axis=-1).astype(BF16)
    b_router = jnp.concatenate(
        [router_group_b, router_expert_b, jnp.zeros((DEPTH, ROUTER_W - N_GROUPS - N_EXPERTS), F32)], axis=-1)
    zero_state = jnp.zeros((BATCH, RET_H, RET_DK, RET_DV), F32)

    caches = [[] for _ in range(7)]
    for l in range(DEPTH):
        mod = mod_all[l]
        uq_rope = uq[l, :, :, MLA_NOPE:].reshape(MLA_Q_LORA, MLA_H * MLA_ROPE)
        lw = dict(
            q_norm=mla_q_norm[l].reshape(1, -1), kv_norm=mla_kv_norm[l].reshape(1, -1),
            gq_norm=gqa_q_norm[l].reshape(1, -1), gk_norm=gqa_k_norm[l].reshape(1, -1),
            w_uq_nope=uq[l, :, :, :MLA_NOPE].reshape(MLA_Q_LORA, MLA_H * MLA_NOPE).astype(BF16),
            w_uq_rope=uq_rope.astype(BF16), w_uq_swap=_swap_halves(uq_rope, MLA_ROPE).astype(BF16),
            w_uk=ukv[l, :, :, :MLA_NOPE].reshape(MLA_KV_LORA, MLA_H * MLA_NOPE).astype(BF16),
            w_uv=ukv[l, :, :, MLA_NOPE:].reshape(MLA_KV_LORA, MLA_H * MLA_V).astype(BF16))

        z = _in_proj(xs, mod, norm_mix[l].reshape(1, -1), w_in_r, l)

        ctx_ops = _prep(z, lw, None, latent=False)
        ckv, kr, kg, vg, ks, vs = ctx_ops[9:]
        oa_c, ob_c, od_c = _ctx_attention(ctx_ops[:9], swa_sink[l])
        gn = ret_gn[l]
        of_c, s_f = _retention(z, log_gamma[l, 0], gn[0:1], zero_state, None, latent=False, reverse=False)
        oc_c, s_b = _retention(z, log_gamma[l, 1], gn[1:2], zero_state, of_c, latent=False, reverse=True)

        (qa, ka, va, qb, kb, vb, qd, kd, vd) = _prep(z, lw, tabs, latent=True)
        ka0, va0 = _mla_ctx(cache_mla_ckv[:, l].reshape(DEC_BATCH * PAST_LEN, MLA_KV_LORA),
                            cache_mla_krope[:, l].reshape(DEC_BATCH * PAST_LEN, MLA_ROPE), lw)
        oa_l = _flash_latent(qa, ka, va, (ka0, va0))
        ob_l = _flash_latent(qb, kb, vb, (_head_major(cache_gqa_k[:, l]), _head_major(cache_gqa_v[:, l])))
        od_l = _window(qd, kd, vd, _head_major(cache_swa_k[:, l]), _head_major(cache_swa_v[:, l]), swa_sink[l])
        of_l, _ = _retention(z, log_gamma[l, 0], gn[0:1], state_ret[:, l, 0], None, latent=True, reverse=False)
        oc_l, _ = _retention(z, log_gamma[l, 1], gn[1:2], state_ret[:, l, 1], of_l, latent=True, reverse=True)

        x_mid, h_ffn, logits = _merge(xs, (oa_c, ob_c, oc_c, od_c), (oa_l, ob_l, oc_l, od_l), z,
                                      w_branch[l].astype(BF16), w_out[l].astype(BF16), mod,
                                      norm_ffn[l].reshape(1, -1), w_router[l], b_router[l].reshape(1, -1))

        rec, dest, src, blk_e, nxt_e, n_used = _route(logits)
        yp = _experts(h_ffn, blk_e, nxt_e, n_used, src, moe_w_gate, moe_w_up, moe_w_down, l)
        if l < DEPTH - 1:
            xs = (_combine(yp, dest, x_mid, rec, mod, None),)
        else:
            y_ctx, y_lat = _combine(yp, dest, x_mid, rec, mod, final_norm.reshape(1, D_MODEL))

        kvh_shape = (BATCH, SEQ, GQA_KVH, HEAD_DIM)
        for lst, val in zip(caches, (ckv.reshape(BATCH, SEQ, MLA_KV_LORA), kr.reshape(BATCH, SEQ, MLA_ROPE),
                                     kg.reshape(kvh_shape), vg.reshape(kvh_shape), ks.reshape(kvh_shape),
                                     vs.reshape(kvh_shape), jnp.stack([s_f, s_b], axis=1))):
            lst.append(val)

    y_prompt = y_ctx.reshape(BATCH, SEQ, D_MODEL)
    y_sample = y_lat.reshape(DEC_BATCH, DEC_SEQ, D_MODEL)
    return (y_prompt, y_sample) + tuple(jnp.stack(lst, axis=1) for lst in caches)
```

```python
import functools

import jax
import jax.numpy as jnp
from jax import lax
from jax.experimental import pallas as pl
from jax.experimental.pallas import tpu as pltpu

F32 = jnp.float32
BF16 = jnp.bfloat16
LANES = 128

D_MODEL = 2048
BATCH = 16
SEQ = 256
DEPTH = 4
DEC_BATCH = 2
DEC_SEQ = 4096
PAST_LEN = 256
GRID_W = 64
ROPE_THETA = 10000.0
EPS = 1e-6
QBLK = 128
NEG_INF = -1e30

MLA_H = 4
MLA_Q_LORA = 512
MLA_KV_LORA = 256
MLA_NOPE = 128
MLA_ROPE = 64
MLA_V = 128
MLA_SCALE = (MLA_NOPE + MLA_ROPE) ** -0.5
MLA_DK = MLA_NOPE + MLA_ROPE

HEAD_DIM = 128
ATTN_SCALE = HEAD_DIM ** -0.5
GQA_H = 4
GQA_KVH = 2
SWA_H = 4
SWA_KVH = 2
WINDOW = 128

RET_H = 4
RET_DK = 64
RET_DV = 128
RET_CHUNK = 128
RET_K_SCALE = RET_DK ** -0.5

N_BRANCH = 4
BRANCH_W = 512

N_GROUPS = 4
EXPERTS_PER_GROUP = 8
N_EXPERTS = N_GROUPS * EXPERTS_PER_GROUP
TOP_K = 2
D_EXPERT = 1024

N_CTX = BATCH * SEQ
N_LAT = DEC_BATCH * DEC_SEQ
N_TOK = N_CTX + N_LAT
N_COND = 8

W_IN_HEAD = MLA_Q_LORA + MLA_KV_LORA + MLA_ROPE
IN_WIDTH = (W_IN_HEAD + (GQA_H + 2 * GQA_KVH) * HEAD_DIM + 2 * RET_H * RET_DK + 3 * RET_H * RET_DV
            + (SWA_H + 2 * SWA_KVH) * HEAD_DIM + N_BRANCH * D_MODEL)
Z_TAIL = 1024
Z_WIDTH = Z_TAIL + IN_WIDTH - W_IN_HEAD
Z_OFF = dict(mq=0, mkv=512, mkr=768, gq=1024, gk=1536, gv=1792, rq=2048, rk=2304, rv=2560, rgf=3072, rgb=3584,
             sq=4096, sk=4608, sv=4864, gate=5120)
Z_TILE = 1024

MOE_BLK = 256
MOE_ASSIGN = N_TOK * TOP_K
MOE_NBLK = (MOE_ASSIGN + N_EXPERTS * (MOE_BLK - 1) + MOE_BLK - 1) // MOE_BLK
MOE_CAP = MOE_NBLK * MOE_BLK
ROUTER_W = 128
R_W0, R_W1, R_E0, R_E1, R_RANK0, R_RANK1 = range(6)

VMEM_LIMIT = 56 * 1024 * 1024


def _params(*sem):
    return pltpu.CompilerParams(dimension_semantics=sem, vmem_limit_bytes=VMEM_LIMIT)


def _cond_row(i, tm):
    start = i * tm
    return jnp.where(start < N_CTX, 0, 1 + (start - N_CTX) // DEC_SEQ)


def _rms(x):
    return x * lax.rsqrt(jnp.mean(x * x, axis=-1, keepdims=True) + EPS)


def _token_specs(xs, tm, index_of):
    if len(xs) == 1:
        return [pl.BlockSpec((tm, D_MODEL), lambda *g: (index_of(*g), 0))]
    n_ct = N_CTX // tm
    return [pl.BlockSpec((tm, D_MODEL), lambda *g: (jnp.minimum(index_of(*g), n_ct - 1), 0),
                         pipeline_mode=pl.Buffered(1)),
            pl.BlockSpec((tm, D_MODEL), lambda *g: (jnp.maximum(index_of(*g) - n_ct, 0), 0),
                         pipeline_mode=pl.Buffered(1))]


def _token_tile(x_refs, i, tm):
    if len(x_refs) == 1:
        return x_refs[0][...]
    return jnp.where(i < N_CTX // tm, x_refs[0][...], x_refs[1][...])


def _ada_kernel(c_ref, w_ref, b_ref, o_ref):
    s = jax.nn.silu(c_ref[...]).astype(BF16)
    o_ref[0] = jnp.dot(s, w_ref[0].astype(BF16), preferred_element_type=F32) + b_ref[0]


def _ada(cond, ada_w, ada_b):
    tn = 1024
    n_out = 6 * D_MODEL
    return pl.pallas_call(
        _ada_kernel,
        grid=(DEPTH, n_out // tn),
        in_specs=[pl.BlockSpec((N_COND, D_MODEL), lambda l, j: (0, 0)),
                  pl.BlockSpec((1, D_MODEL, tn), lambda l, j: (l, 0, j)),
                  pl.BlockSpec((1, 1, tn), lambda l, j: (l, 0, j))],
        out_specs=pl.BlockSpec((1, N_COND, tn), lambda l, j: (l, 0, j)),
        out_shape=jax.ShapeDtypeStruct((DEPTH, N_COND, n_out), F32),
        compiler_params=_params("parallel", "parallel"),
        name="ada_mod",
    )(cond, ada_w, ada_b.reshape(DEPTH, 1, n_out))


def _w_in_kernel(w_ref, o_ref):
    w = w_ref[0]

    @pl.when(pl.program_id(1) == 0)
    def _():
        half = MLA_ROPE // 2
        mkr0 = W_IN_HEAD - MLA_ROPE
        o_ref[0, 0:W_IN_HEAD] = w[0:W_IN_HEAD].astype(BF16)
        o_ref[0, W_IN_HEAD:W_IN_HEAD + half] = w[mkr0 + half:W_IN_HEAD].astype(BF16)
        o_ref[0, W_IN_HEAD + half:W_IN_HEAD + MLA_ROPE] = w[mkr0:mkr0 + half].astype(BF16)
        o_ref[0, W_IN_HEAD + MLA_ROPE:Z_TAIL] = jnp.zeros((Z_TAIL - W_IN_HEAD - MLA_ROPE, w.shape[1]), BF16)

    @pl.when(pl.program_id(1) > 0)
    def _():
        o_ref[0] = w.astype(BF16)


def _w_in_layout(w_in):
    w_t = jnp.swapaxes(w_in, 1, 2)

    def src_row(j):
        return pl.multiple_of(jnp.where(j > 0, (j - 1) * Z_TILE + W_IN_HEAD, 0), 16)

    return pl.pallas_call(
        _w_in_kernel,
        grid=(DEPTH, Z_WIDTH // Z_TILE),
        in_specs=[pl.BlockSpec((pl.Element(1), pl.Element(Z_TILE), pl.Element(D_MODEL)),
                               lambda l, j: (l, src_row(j), 0))],
        out_specs=pl.BlockSpec((1, Z_TILE, D_MODEL), lambda l, j: (l, j, 0)),
        out_shape=jax.ShapeDtypeStruct((DEPTH, Z_WIDTH, D_MODEL), BF16),
        compiler_params=_params("parallel", "parallel"),
        name="w_in_layout",
    )(w_t)


def _in_kernel(*refs, n_x, tm):
    x_refs, (mod_ref, g_ref, w_ref, z_ref, h_ref) = refs[:n_x], refs[n_x:]

    @pl.when(pl.program_id(1) == 0)
    def _():
        m = mod_ref[0]
        y = _rms(_token_tile(x_refs, pl.program_id(0), tm)) * g_ref[...]
        h_ref[...] = (y * (1.0 + m[:, D_MODEL:2 * D_MODEL]) + m[:, 0:D_MODEL]).astype(BF16)

    z_ref[...] = lax.dot_general(h_ref[...], w_ref[...], (((1,), (1,)), ((), ())), preferred_element_type=F32)


def _in_proj(xs, mod, gain, w_all, layer):
    tm, tn = 1024, Z_TILE
    return pl.pallas_call(
        functools.partial(_in_kernel, n_x=len(xs), tm=tm),
        grid=(N_TOK // tm, Z_WIDTH // tn),
        in_specs=_token_specs(xs, tm, lambda i, j: i)
        + [pl.BlockSpec((1, 1, 6 * D_MODEL), lambda i, j: (_cond_row(i, tm), 0, 0)),
           pl.BlockSpec((1, D_MODEL), lambda i, j: (0, 0)),
           pl.BlockSpec((None, tn, D_MODEL), lambda i, j: (layer, j, 0))],
        out_specs=pl.BlockSpec((tm, tn), lambda i, j: (i, j)),
        out_shape=jax.ShapeDtypeStruct((N_TOK, Z_WIDTH), F32),
        scratch_shapes=[pltpu.VMEM((tm, D_MODEL), BF16)],
        compiler_params=_params("parallel", "arbitrary"),
        name="in_proj",
    )(*xs, mod, gain, w_all)


def _rope128(x, cos, sin):
    return x * cos + pltpu.roll(x, HEAD_DIM // 2, 1) * sin


def _mla_kv(ckv, kr, wuk_ref, wuv_ref, ka_ref, va_ref):
    cb = ckv.astype(BF16)
    kn = jnp.dot(cb, wuk_ref[...], preferred_element_type=F32)
    vv = jnp.dot(cb, wuv_ref[...], preferred_element_type=F32)
    for h in range(MLA_H):
        ka_ref[h, :, 0:MLA_NOPE] = kn[:, h * MLA_NOPE:(h + 1) * MLA_NOPE].astype(BF16)
        ka_ref[h, :, MLA_NOPE:MLA_DK] = kr.astype(BF16)
        va_ref[h] = vv[:, h * MLA_V:(h + 1) * MLA_V].astype(BF16)


def _prep_kernel(*refs, latent):
    (mq_ref, gq_ref, sq_ref, mkv_ref, gk_ref, gv_ref, sk_ref, sv_ref, mkr_ref,
     qn_ref, kvn_ref, gqn_ref, gkn_ref, wqn_ref, wqr_ref, wqs_ref, wuk_ref, wuv_ref) = refs[:18]
    refs = refs[18:]
    if latent:
        cos_ref, sin_ref, c64_ref, s64_ref = refs[:4]
        refs = refs[4:]
        cos, sin = cos_ref[...], sin_ref[...]
        c64, s64 = c64_ref[...], s64_ref[...]
    qa_ref, ka_ref, va_ref, qb_ref, kb_ref, vb_ref, qd_ref, kd_ref, vd_ref = refs[:9]
    refs = refs[9:]

    hn = (_rms(mq_ref[...]) * qn_ref[...]).astype(BF16)
    q_nope = jnp.dot(hn, wqn_ref[...], preferred_element_type=F32)
    q_rope = jnp.dot(hn, wqr_ref[...], preferred_element_type=F32)
    if latent:
        q_swap = jnp.dot(hn, wqs_ref[...], preferred_element_type=F32)
    for h in range(MLA_H):
        qa_ref[h, :, 0:MLA_NOPE] = (q_nope[:, h * MLA_NOPE:(h + 1) * MLA_NOPE] * MLA_SCALE).astype(BF16)
        qr = q_rope[:, h * MLA_ROPE:(h + 1) * MLA_ROPE]
        if latent:
            qr = qr * c64 + q_swap[:, h * MLA_ROPE:(h + 1) * MLA_ROPE] * s64
        qa_ref[h, :, MLA_NOPE:MLA_DK] = (qr * MLA_SCALE).astype(BF16)
    ckv = _rms(mkv_ref[...]) * kvn_ref[...]
    kr = mkr_ref[:, 0:MLA_ROPE]
    if latent:
        kr = kr * c64 + mkr_ref[:, MLA_ROPE:2 * MLA_ROPE] * s64
    _mla_kv(ckv, kr, wuk_ref, wuv_ref, ka_ref, va_ref)

    gq = gq_ref[...]
    for h in range(GQA_H):
        q = _rms(gq[:, h * HEAD_DIM:(h + 1) * HEAD_DIM]) * gqn_ref[...]
        if latent:
            q = _rope128(q, cos, sin)
        qb_ref[h] = (q * ATTN_SCALE).astype(BF16)
    gk = gk_ref[...]
    gv = gv_ref[...]
    kg = []
    for h in range(GQA_KVH):
        k = _rms(gk[:, h * HEAD_DIM:(h + 1) * HEAD_DIM]) * gkn_ref[...]
        kg.append(k)
        if latent:
            k = _rope128(k, cos, sin)
        kb_ref[h] = k.astype(BF16)
        vb_ref[h] = gv[:, h * HEAD_DIM:(h + 1) * HEAD_DIM].astype(BF16)

    sq = sq_ref[...]
    for h in range(SWA_H):
        q = sq[:, h * HEAD_DIM:(h + 1) * HEAD_DIM]
        if latent:
            q = _rope128(q, cos, sin)
        qd_ref[h] = (q * ATTN_SCALE).astype(BF16)
    sk = sk_ref[...]
    sv = sv_ref[...]
    for h in range(SWA_KVH):
        k = sk[:, h * HEAD_DIM:(h + 1) * HEAD_DIM]
        if latent:
            k = _rope128(k, cos, sin)
        kd_ref[h] = k.astype(BF16)
        vd_ref[h] = sv[:, h * HEAD_DIM:(h + 1) * HEAD_DIM].astype(BF16)

    if not latent:
        ckv_ref, kr_ref, kg_ref, vg_ref, ks_ref, vs_ref = refs
        ckv_ref[...] = ckv
        kr_ref[...] = kr
        for h in range(GQA_KVH):
            kg_ref[:, h * HEAD_DIM:(h + 1) * HEAD_DIM] = kg[h]
        vg_ref[...] = gv
        ks_ref[...] = sk
        vs_ref[...] = sv


def _prep(z, lw, tabs, latent):
    tm = 512
    rows = N_LAT if latent else N_CTX
    roff = (N_CTX if latent else 0) // tm
    n_t = DEC_SEQ // tm

    def zc(name, width):
        cb = Z_OFF[name] // width
        return pl.BlockSpec((tm, width), lambda i: (i + roff, cb))

    def full(a):
        nd = a.ndim
        return pl.BlockSpec(a.shape, lambda i: (0,) * nd)

    weights = [lw["q_norm"], lw["kv_norm"], lw["gq_norm"], lw["gk_norm"],
               lw["w_uq_nope"], lw["w_uq_rope"], lw["w_uq_swap"], lw["w_uk"], lw["w_uv"]]
    in_specs = [zc("mq", 512), zc("gq", 512), zc("sq", 512), zc("mkv", 256), zc("gk", 256), zc("gv", 256),
                zc("sk", 256), zc("sv", 256), zc("mkr", 128)] + [full(a) for a in weights]
    args = [z] * 9 + weights
    if latent:
        in_specs += [pl.BlockSpec((tm, HEAD_DIM), lambda i: (i % n_t, 0))] * 2
        in_specs += [pl.BlockSpec((tm, MLA_ROPE), lambda i: (i % n_t, 0))] * 2
        args += list(tabs)

    def heads(n, d):
        return (pl.BlockSpec((n, tm, d), lambda i: (0, i, 0)), jax.ShapeDtypeStruct((n, rows, d), BF16))

    outs = [heads(MLA_H, MLA_DK), heads(MLA_H, MLA_DK), heads(MLA_H, MLA_V),
            heads(GQA_H, HEAD_DIM), heads(GQA_KVH, HEAD_DIM), heads(GQA_KVH, HEAD_DIM),
            heads(SWA_H, HEAD_DIM), heads(SWA_KVH, HEAD_DIM), heads(SWA_KVH, HEAD_DIM)]
    if not latent:
        kv_w = GQA_KVH * HEAD_DIM
        for d in (MLA_KV_LORA, MLA_ROPE, kv_w, kv_w, kv_w, kv_w):
            outs.append((pl.BlockSpec((tm, d), lambda i: (i, 0)), jax.ShapeDtypeStruct((rows, d), F32)))
    return pl.pallas_call(
        functools.partial(_prep_kernel, latent=latent),
        grid=(rows // tm,),
        in_specs=in_specs,
        out_specs=[o[0] for o in outs],
        out_shape=[o[1] for o in outs],
        compiler_params=_params("parallel"),
        name="prep_latent" if latent else "prep_context",
    )(*args)


def _mla_ctx_kernel(ckv_ref, kr_ref, wuk_ref, wuv_ref, ka_ref, va_ref):
    _mla_kv(ckv_ref[...], kr_ref[...], wuk_ref, wuv_ref, ka_ref, va_ref)


def _mla_ctx(ckv, kr, lw):
    rows = ckv.shape[0]
    return pl.pallas_call(
        _mla_ctx_kernel,
        out_shape=[jax.ShapeDtypeStruct((MLA_H, rows, MLA_DK), BF16),
                   jax.ShapeDtypeStruct((MLA_H, rows, MLA_V), BF16)],
        name="mla_ctx_kv",
    )(ckv, kr, lw["w_uk"], lw["w_uv"])


def _softmax_pv(q, k, v, sink):
    s = lax.dot_general(q, k, (((1,), (1,)), ((), ())), preferred_element_type=F32)
    m = jnp.max(s, axis=-1, keepdims=True)
    if sink is not None:
        m = jnp.maximum(m, sink)
    p = jnp.exp(s - m)
    l = jnp.sum(p, axis=-1, keepdims=True)
    if sink is not None:
        l = l + jnp.exp(sink - m)
    return jnp.dot(p.astype(BF16), v, preferred_element_type=F32) / l


def _ctx_attn_kernel(sink_ref, qa, ka, va, qb, kb, vb, qd, kd, vd, oa, ob, od):
    for h in range(MLA_H):
        oa[:, h * MLA_V:(h + 1) * MLA_V] = _softmax_pv(qa[h], ka[h], va[h], None).astype(BF16)
    for h in range(GQA_H):
        kv = h // (GQA_H // GQA_KVH)
        ob[:, h * HEAD_DIM:(h + 1) * HEAD_DIM] = _softmax_pv(qb[h], kb[kv], vb[kv], None).astype(BF16)
    for h in range(SWA_H):
        kv = h // (SWA_H // SWA_KVH)
        od[:, h * HEAD_DIM:(h + 1) * HEAD_DIM] = _softmax_pv(qd[h], kd[kv], vd[kv], sink_ref[h]).astype(BF16)


def _ctx_attention(operands, sink):
    def spec(a):
        return pl.BlockSpec((a.shape[0], SEQ, a.shape[2]), lambda b: (0, b, 0))

    out = jax.ShapeDtypeStruct((N_CTX, BRANCH_W), BF16)
    return pl.pallas_call(
        _ctx_attn_kernel,
        grid=(BATCH,),
        in_specs=[pl.BlockSpec(memory_space=pltpu.SMEM)] + [spec(a) for a in operands],
        out_specs=[pl.BlockSpec((SEQ, BRANCH_W), lambda b: (b, 0))] * 3,
        out_shape=[out, out, out],
        compiler_params=_params("parallel"),
        name="context_attention",
    )(sink, *operands)


KEY_CHUNK = 256


def _flash_latent_kernel(q_ref, k_ref, v_ref, k0_ref, v0_ref, o_ref, m_sc, l_sc, acc_sc, *, groups):
    dv = v_ref.shape[-1]
    n_chunks = k_ref.shape[1] // KEY_CHUNK
    dn = (((1,), (1,)), ((), ()))

    def scores(g, k):
        return lax.dot_general(q_ref[g], k, dn, preferred_element_type=F32)

    def lane_max(s):
        m = s[:, 0:LANES]
        for c in range(1, s.shape[1] // LANES):
            m = jnp.maximum(m, s[:, c * LANES:(c + 1) * LANES])
        return m

    for g in range(groups):
        m_sc[g] = lane_max(scores(g, k0_ref[0]))

    def max_body(j, carry):
        off = pl.multiple_of(j * KEY_CHUNK, KEY_CHUNK)
        for g in range(groups):
            m_sc[g] = jnp.maximum(m_sc[g], lane_max(scores(g, k_ref[0, pl.ds(off, KEY_CHUNK), :])))
        return carry

    lax.fori_loop(0, n_chunks, max_body, 0, unroll=True)
    for g in range(groups):
        m_sc[g] = jnp.broadcast_to(jnp.max(m_sc[g], axis=-1, keepdims=True), m_sc.shape[1:])

    def accumulate(g, k, v, first):
        s = scores(g, k)
        m = m_sc[g]
        ps = [jnp.exp(s[:, c * LANES:(c + 1) * LANES] - m) for c in range(s.shape[1] // LANES)]
        lsum = ps[0]
        for p in ps[1:]:
            lsum = lsum + p
        pv = jnp.dot(jnp.concatenate(ps, axis=1).astype(BF16), v, preferred_element_type=F32)
        if first:
            l_sc[g] = lsum
            acc_sc[g] = pv
        else:
            l_sc[g] += lsum
            acc_sc[g] += pv

    for g in range(groups):
        accumulate(g, k0_ref[0], v0_ref[0], True)

    def sum_body(j, carry):
        off = pl.multiple_of(j * KEY_CHUNK, KEY_CHUNK)
        for g in range(groups):
            accumulate(g, k_ref[0, pl.ds(off, KEY_CHUNK), :], v_ref[0, pl.ds(off, KEY_CHUNK), :], False)
        return carry

    lax.fori_loop(0, n_chunks, sum_body, 0, unroll=True)
    for g in range(groups):
        l = jnp.sum(l_sc[g], axis=-1, keepdims=True)
        o_ref[:, g * dv:(g + 1) * dv] = (acc_sc[g] / l).astype(o_ref.dtype)


def _flash_latent(q, k, v, ctx):
    tq = 1024
    n_h, _, dk = q.shape
    kvh, _, dv = v.shape
    groups = n_h // kvh
    nq = DEC_SEQ // tq
    return pl.pallas_call(
        functools.partial(_flash_latent_kernel, groups=groups),
        grid=(DEC_BATCH, kvh, nq),
        in_specs=[pl.BlockSpec((groups, tq, dk), lambda b, h, i: (h, b * nq + i, 0)),
                  pl.BlockSpec((1, DEC_SEQ, dk), lambda b, h, i: (h, b, 0)),
                  pl.BlockSpec((1, DEC_SEQ, dv), lambda b, h, i: (h, b, 0)),
                  pl.BlockSpec((1, PAST_LEN, dk), lambda b, h, i: (h, b, 0)),
                  pl.BlockSpec((1, PAST_LEN, dv), lambda b, h, i: (h, b, 0))],
        out_specs=pl.BlockSpec((tq, groups * dv), lambda b, h, i: (b * nq + i, h)),
        out_shape=jax.ShapeDtypeStruct((N_LAT, n_h * dv), BF16),
        scratch_shapes=[pltpu.VMEM((groups, tq, LANES), F32), pltpu.VMEM((groups, tq, LANES), F32),
                        pltpu.VMEM((groups, tq, dv), F32)],
        compiler_params=_params("parallel", "parallel", "parallel"),
        name="flash_latent",
    )(q, k, v, *ctx)


WIN_TQ = 512
WIN_SPAN = WIN_TQ + 2 * WINDOW


def _lanewise(op, x):
    r = x[:, 0:LANES]
    for c in range(1, x.shape[1] // LANES):
        r = op(r, x[:, c * LANES:(c + 1) * LANES])
    return r


def _exp_tiles(s, m):
    return jnp.concatenate([jnp.exp(s[:, c * LANES:(c + 1) * LANES] - m) for c in range(s.shape[1] // LANES)],
                           axis=1)


def _window_kernel(sink_ref, q_ref, k_ref, v_ref, k0_ref, v0_ref, o_ref, *, groups):
    head0 = pl.program_id(1) * groups
    t = k_ref.shape[1]
    q0 = pl.program_id(2) * WIN_TQ
    k_start = pl.multiple_of(jnp.clip(q0 - WINDOW, 0, t - WIN_SPAN), LANES)
    qpos = q0 + lax.broadcasted_iota(jnp.int32, (WIN_TQ, WIN_SPAN), 0)
    kpos = k_start + lax.broadcasted_iota(jnp.int32, (WIN_TQ, WIN_SPAN), 1)
    band = jnp.abs(kpos - qpos) <= WINDOW
    kw = k_ref[0, pl.ds(k_start, WIN_SPAN), :]
    vw = v_ref[0, pl.ds(k_start, WIN_SPAN), :]
    dn = (((1,), (1,)), ((), ()))
    for g in range(groups):
        q = q_ref[g]
        s_loc = jnp.where(band, lax.dot_general(q, kw, dn, preferred_element_type=F32), NEG_INF)
        s_ctx = lax.dot_general(q, k0_ref[0], dn, preferred_element_type=F32)
        snk = sink_ref[head0 + g]
        m = jnp.max(jnp.maximum(_lanewise(jnp.maximum, s_loc), _lanewise(jnp.maximum, s_ctx)),
                    axis=-1, keepdims=True)
        m = jnp.broadcast_to(jnp.maximum(m, snk), (WIN_TQ, LANES))
        p_loc = _exp_tiles(s_loc, m)
        p_ctx = _exp_tiles(s_ctx, m)
        den = (jnp.sum(_lanewise(jnp.add, p_loc) + _lanewise(jnp.add, p_ctx), axis=-1, keepdims=True)
               + jnp.exp(snk - m[:, 0:1]))
        acc = (jnp.dot(p_loc.astype(BF16), vw, preferred_element_type=F32)
               + jnp.dot(p_ctx.astype(BF16), v0_ref[0], preferred_element_type=F32))
        o_ref[:, g * HEAD_DIM:(g + 1) * HEAD_DIM] = (acc / den).astype(o_ref.dtype)


def _window(q, k, v, k0, v0, sink):
    n_h, kvh = q.shape[0], k.shape[0]
    groups = n_h // kvh
    nq = DEC_SEQ // WIN_TQ
    full = pl.BlockSpec((1, DEC_SEQ, HEAD_DIM), lambda b, h, n: (h, b, 0))
    ctx_spec = pl.BlockSpec((1, PAST_LEN, HEAD_DIM), lambda b, h, n: (h, b, 0))
    return pl.pallas_call(
        functools.partial(_window_kernel, groups=groups),
        grid=(DEC_BATCH, kvh, nq),
        in_specs=[pl.BlockSpec(memory_space=pltpu.SMEM),
                  pl.BlockSpec((groups, WIN_TQ, HEAD_DIM), lambda b, h, n: (h, b * nq + n, 0)),
                  full, full, ctx_spec, ctx_spec],
        out_specs=pl.BlockSpec((WIN_TQ, groups * HEAD_DIM), lambda b, h, n: (b * nq + n, h)),
        out_shape=jax.ShapeDtypeStruct((N_LAT, n_h * HEAD_DIM), BF16),
        compiler_params=_params("parallel", "parallel", "parallel"),
        name="window_attention",
    )(sink, q, k, v, k0, v0)


def _ret_kernel(*refs, reverse, has_prev):
    lg_ref, q_ref, k_ref, v_ref, gate_ref, gn_ref, s0_ref = refs[:7]
    refs = refs[7:]
    if has_prev:
        prev_ref, refs = refs[0], refs[1:]
    o_ref, sfin_ref, s_sc = refs
    c = pl.program_id(1)

    @pl.when(c == 0)
    def _():
        s_sc[...] = s0_ref[0]

    cs = RET_CHUNK
    row = lax.broadcasted_iota(jnp.int32, (cs, cs), 0).astype(F32)
    col = lax.broadcasted_iota(jnp.int32, (cs, cs), 1).astype(F32)
    pos = lax.broadcasted_iota(jnp.int32, (cs, 1), 0).astype(F32)
    diff = (col - row) if reverse else (row - col)
    scan_pos = (cs - 1.0 - pos) if reverse else pos
    q_all, k_all, v_all, gate_all = q_ref[...], k_ref[...], v_ref[...], gate_ref[...]
    for h in range(RET_H):
        lg = lg_ref[h]
        dmask = jnp.where(diff >= 0, jnp.exp(jnp.maximum(diff, 0.0) * lg), 0.0)
        q_dec = jnp.exp((scan_pos + 1.0) * lg)
        k_dec = jnp.exp((cs - 1.0 - scan_pos) * lg)
        c_dec = jnp.exp(cs * lg)
        q = q_all[:, h * RET_DK:(h + 1) * RET_DK]
        k = k_all[:, h * RET_DK:(h + 1) * RET_DK] * RET_K_SCALE
        v = v_all[:, h * RET_DV:(h + 1) * RET_DV].astype(BF16)
        s = s_sc[h]
        a = lax.dot_general(q.astype(BF16), k.astype(BF16), (((1,), (1,)), ((), ())),
                            preferred_element_type=F32) * dmask
        o = (jnp.dot(a.astype(BF16), v, preferred_element_type=F32)
             + jnp.dot((q * q_dec).astype(BF16), s.astype(BF16), preferred_element_type=F32))
        kd_t = jnp.transpose(k * k_dec).astype(BF16)
        s_sc[h] = s * c_dec + jnp.dot(kd_t, v, preferred_element_type=F32)
        mu = jnp.mean(o, axis=-1, keepdims=True)
        var = jnp.mean(jnp.square(o - mu), axis=-1, keepdims=True)
        y = (o - mu) * lax.rsqrt(var + EPS) * gn_ref[:, h * RET_DV:(h + 1) * RET_DV]
        y = jax.nn.silu(gate_all[:, h * RET_DV:(h + 1) * RET_DV]) * y
        if has_prev:
            y = prev_ref[:, h * RET_DV:(h + 1) * RET_DV] + y
        o_ref[:, h * RET_DV:(h + 1) * RET_DV] = y.astype(o_ref.dtype)

    @pl.when(c == pl.num_programs(1) - 1)
    def _():
        sfin_ref[0] = s_sc[...]


def _retention(z, log_gamma, gn, s0, prev, *, latent, reverse):
    batch, t = (DEC_BATCH, DEC_SEQ) if latent else (BATCH, SEQ)
    nc = t // RET_CHUNK
    roff = (N_CTX if latent else 0) // RET_CHUNK
    gate_name = "rgb" if reverse else "rgf"

    def blk(b, c):
        return b * nc + (nc - 1 - c if reverse else c)

    def zc(name, width):
        cb = Z_OFF[name] // width
        return pl.BlockSpec((RET_CHUNK, width), lambda b, c: (roff + blk(b, c), cb))

    w = RET_H * RET_DV
    in_specs = [pl.BlockSpec(memory_space=pltpu.SMEM), zc("rq", 256), zc("rk", 256), zc("rv", 512),
                zc(gate_name, 512), pl.BlockSpec((1, w), lambda b, c: (0, 0)),
                pl.BlockSpec((1, RET_H, RET_DK, RET_DV), lambda b, c: (b, 0, 0, 0))]
    args = [log_gamma, z, z, z, z, gn, s0]
    if prev is not None:
        in_specs.append(pl.BlockSpec((RET_CHUNK, w), lambda b, c: (blk(b, c), 0)))
        args.append(prev)
    return pl.pallas_call(
        functools.partial(_ret_kernel, reverse=reverse, has_prev=prev is not None),
        grid=(batch, nc),
        in_specs=in_specs,
        out_specs=[pl.BlockSpec((RET_CHUNK, w), lambda b, c: (blk(b, c), 0)),
                   pl.BlockSpec((1, RET_H, RET_DK, RET_DV), lambda b, c: (b, 0, 0, 0))],
        out_shape=[jax.ShapeDtypeStruct((batch * t, w), BF16 if reverse else F32),
                   jax.ShapeDtypeStruct((batch, RET_H, RET_DK, RET_DV), F32)],
        scratch_shapes=[pltpu.VMEM((RET_H, RET_DK, RET_DV), F32)],
        compiler_params=_params("parallel", "arbitrary"),
        name="retention_bwd" if reverse else "retention_fwd",
    )(*args)


def _merge_kernel(*refs, n_x, tm):
    x_refs, refs = refs[:n_x], refs[n_x:]
    ctx_refs, lat_refs = refs[:N_BRANCH], refs[N_BRANCH:2 * N_BRANCH]
    (gate_ref, wb_ref, wo_ref, mod_ref, g_ref, wr_ref, br_ref, xo_ref, h_ref, lg_ref) = refs[2 * N_BRANCH:]
    i = pl.program_id(0)
    is_ctx = i < N_CTX // tm

    merged = None
    for n in range(N_BRANCH):
        branch = jnp.where(is_ctx, ctx_refs[n][...], lat_refs[n][...])
        t = jnp.dot(branch, wb_ref[n], preferred_element_type=F32)
        term = jax.nn.sigmoid(gate_ref[:, n * D_MODEL:(n + 1) * D_MODEL]) * t
        merged = term if merged is None else merged + term

    m = mod_ref[0]
    out = jnp.dot(merged.astype(BF16), wo_ref[...], preferred_element_type=F32)
    x = _token_tile(x_refs, i, tm) + m[:, 2 * D_MODEL:3 * D_MODEL] * out
    xo_ref[...] = x
    h = (_rms(x) * g_ref[...]) * (1.0 + m[:, 4 * D_MODEL:5 * D_MODEL]) + m[:, 3 * D_MODEL:4 * D_MODEL]
    h_ref[...] = h
    lg_ref[...] = jnp.dot(h.astype(BF16), wr_ref[...], preferred_element_type=F32) + br_ref[...]


def _merge(xs, ctx_branches, lat_branches, z, w_branch, w_out, mod, gain, w_router, b_router):
    tm = 256
    n_ct = N_CTX // tm
    row = lambda i: (i, 0)
    const2 = lambda i: (0, 0)
    resident = pl.Buffered(1)
    return pl.pallas_call(
        functools.partial(_merge_kernel, n_x=len(xs), tm=tm),
        grid=(N_TOK // tm,),
        in_specs=_token_specs(xs, tm, lambda i: i)
        + [pl.BlockSpec((tm, BRANCH_W), lambda i: (jnp.minimum(i, n_ct - 1), 0))] * N_BRANCH
        + [pl.BlockSpec((tm, BRANCH_W), lambda i: (jnp.maximum(i - n_ct, 0), 0))] * N_BRANCH
        + [pl.BlockSpec((pl.Element(tm), pl.Element(N_BRANCH * D_MODEL)),
                        lambda i: (pl.multiple_of(i * tm, tm), Z_OFF["gate"])),
           pl.BlockSpec((N_BRANCH, BRANCH_W, D_MODEL), lambda i: (0, 0, 0), pipeline_mode=resident),
           pl.BlockSpec((D_MODEL, D_MODEL), const2, pipeline_mode=resident),
           pl.BlockSpec((1, 1, 6 * D_MODEL), lambda i: (_cond_row(i, tm), 0, 0)),
           pl.BlockSpec((1, D_MODEL), const2),
           pl.BlockSpec((D_MODEL, ROUTER_W), const2),
           pl.BlockSpec((1, ROUTER_W), const2)],
        out_specs=[pl.BlockSpec((tm, D_MODEL), row), pl.BlockSpec((tm, D_MODEL), row),
                   pl.BlockSpec((tm, ROUTER_W), row)],
        out_shape=[jax.ShapeDtypeStruct((N_TOK, D_MODEL), F32), jax.ShapeDtypeStruct((N_TOK, D_MODEL), F32),
                   jax.ShapeDtypeStruct((N_TOK, ROUTER_W), F32)],
        compiler_params=_params("parallel"),
        name="merge_out_proj",
    )(*xs, *ctx_branches, *lat_branches, z, w_branch, w_out, mod, gain, w_router, b_router)


def _route_kernel(lg_ref, rec_ref, cnt_ref, tri_sc, carry_sc, *, tm):
    @pl.when(pl.program_id(0) == 0)
    def _():
        r = lax.broadcasted_iota(jnp.int32, (tm, tm), 0)
        c = lax.broadcasted_iota(jnp.int32, (tm, tm), 1)
        tri_sc[...] = (c < r).astype(BF16)
        carry_sc[...] = jnp.zeros(carry_sc.shape, F32)

    lg = lg_ref[...]
    lane = lax.broadcasted_iota(jnp.int32, lg.shape, 1)
    big = jnp.int32(ROUTER_W)

    def first_max(vals):
        top = jnp.max(vals, axis=-1, keepdims=True)
        return top, jnp.min(jnp.where(vals == top, lane, big), axis=-1, keepdims=True)

    g_logits = jnp.where(lane < N_GROUPS, lg, -jnp.inf)
    g_top, grp = first_max(g_logits)
    p_grp = 1.0 / jnp.sum(jnp.exp(g_logits - g_top), axis=-1, keepdims=True)
    lo = N_GROUPS + grp * EXPERTS_PER_GROUP
    e_logits = jnp.where((lane >= lo) & (lane < lo + EXPERTS_PER_GROUP), lg, -jnp.inf)
    v0, i0 = first_max(e_logits)
    v1, i1 = first_max(jnp.where(lane == i0, -jnp.inf, e_logits))
    e1 = jnp.exp(v1 - v0)
    den = 1.0 + e1
    w0 = p_grp * (1.0 / den)
    w1 = p_grp * (e1 / den)

    hot0 = (lane == i0).astype(F32)
    hot1 = (lane == i1).astype(F32)
    before = jnp.dot(tri_sc[...], (hot0 + hot1).astype(BF16), preferred_element_type=F32) + carry_sc[...]
    rank0 = jnp.sum(before * hot0, axis=-1, keepdims=True)
    rank1 = jnp.sum(before * hot1, axis=-1, keepdims=True)
    carry_sc[...] += jnp.sum(hot0 + hot1, axis=0, keepdims=True)

    rec = jnp.zeros(lg.shape, F32)
    for slot, val in ((R_W0, w0), (R_W1, w1), (R_E0, (i0 - N_GROUPS).astype(F32)),
                      (R_E1, (i1 - N_GROUPS).astype(F32)), (R_RANK0, rank0), (R_RANK1, rank1)):
        rec = jnp.where(lane == slot, val, rec)
    rec_ref[...] = rec
    cnt_ref[...] = jnp.broadcast_to(carry_sc[...], cnt_ref.shape)


def _route(logits):
    tm = 512
    rec, cnt = pl.pallas_call(
        functools.partial(_route_kernel, tm=tm),
        grid=(N_TOK // tm,),
        in_specs=[pl.BlockSpec((tm, ROUTER_W), lambda i: (i, 0))],
        out_specs=[pl.BlockSpec((tm, ROUTER_W), lambda i: (i, 0)), pl.BlockSpec((8, ROUTER_W), lambda i: (0, 0))],
        out_shape=[jax.ShapeDtypeStruct((N_TOK, ROUTER_W), F32), jax.ShapeDtypeStruct((8, ROUTER_W), F32)],
        scratch_shapes=[pltpu.VMEM((tm, tm), BF16), pltpu.VMEM((1, ROUTER_W), F32)],
        compiler_params=_params("arbitrary"),
        name="moe_route",
    )(logits)
    expert = rec[:, R_E0:R_E1 + 1].astype(jnp.int32)
    rank = rec[:, R_RANK0:R_RANK1 + 1].astype(jnp.int32)
    counts = cnt[0, N_GROUPS:N_GROUPS + N_EXPERTS].astype(jnp.int32)
    padded = (counts + MOE_BLK - 1) // MOE_BLK * MOE_BLK
    pad_end = jnp.cumsum(padded)
    dest = ((pad_end - padded)[expert] + rank).reshape(-1)
    token = jnp.arange(MOE_ASSIGN, dtype=jnp.int32) // TOP_K
    src = jnp.zeros((MOE_CAP,), jnp.int32).at[dest].set(token)
    blk_start = jnp.arange(MOE_NBLK, dtype=jnp.int32) * MOE_BLK
    blk_e = jnp.minimum(jnp.sum(pad_end[None, :] <= blk_start[:, None], axis=1), N_EXPERTS - 1).astype(jnp.int32)
    n_used = (pad_end[-1:] // MOE_BLK).astype(jnp.int32)
    ids = jnp.arange(N_EXPERTS, dtype=jnp.int32)
    later = jnp.where((ids[None, :] > ids[:, None]) & (counts[None, :] > 0), ids[None, :], N_EXPERTS)
    nxt = jnp.min(later, axis=1)
    nxt_e = jnp.where(nxt < N_EXPERTS, nxt, -1).astype(jnp.int32)[blk_e]
    return rec, dest, src, blk_e, nxt_e, n_used


def _start_row_gather(src_hbm, dst, sem, index_of, n_rows):
    def issue(r, carry):
        pltpu.make_async_copy(src_hbm.at[pl.ds(index_of(r), 1)], dst.at[pl.ds(r, 1)], sem).start()
        return carry

    lax.fori_loop(0, n_rows, issue, 0, unroll=8)


def _wait_row_gather(src_hbm, dst, sem, n_rows):
    pltpu.make_async_copy(src_hbm.at[pl.ds(0, n_rows)], dst, sem).wait()


def _expert_kernel(be_ref, nx_ref, nu_ref, x_ref, wg_hbm, wu_hbm, wd_hbm, y_ref,
                   stage_g, stage_u, stage_d, res_g, res_u, res_d, wsem, *, layer):
    i = pl.program_id(0)
    n_used = nu_ref[0]

    def weight_copies(e):
        return [pltpu.make_async_copy(w.at[layer, e], st, wsem.at[k])
                for k, (w, st) in enumerate(((wg_hbm, stage_g), (wu_hbm, stage_u), (wd_hbm, stage_d)))]

    def to_bf16(stage, res):
        rows = 256

        def body(c, carry):
            r = pl.multiple_of(c * rows, rows)
            res[pl.ds(r, rows), :] = stage[pl.ds(r, rows), :].astype(BF16)
            return carry

        lax.fori_loop(0, stage.shape[0] // rows, body, 0)

    @pl.when(i < n_used)
    def _():
        e = be_ref[i]
        first = (i == 0) | (e != be_ref[jnp.maximum(i - 1, 0)])

        @pl.when(i == 0)
        def _():
            for cp in weight_copies(e):
                cp.start()

        @pl.when(first)
        def _():
            for cp in weight_copies(e):
                cp.wait()
            to_bf16(stage_g, res_g)
            to_bf16(stage_u, res_u)
            to_bf16(stage_d, res_d)
            nxt = nx_ref[i]

            @pl.when(nxt >= 0)
            def _():
                for cp in weight_copies(nxt):
                    cp.start()

        x = x_ref[...].astype(BF16)
        g = jnp.dot(x, res_g[...], preferred_element_type=F32)
        u = jnp.dot(x, res_u[...], preferred_element_type=F32)
        a = (jax.nn.silu(g) * u).astype(BF16)
        y_ref[...] = jnp.dot(a, res_d[...], preferred_element_type=F32)

    @pl.when(i >= n_used)
    def _():
        y_ref[...] = jnp.zeros(y_ref.shape, F32)


def _experts(xp, blk_e, nxt_e, n_used, w_gate, w_up, w_down, layer):
    any_spec = pl.BlockSpec(memory_space=pl.ANY)
    return pl.pallas_call(
        functools.partial(_expert_kernel, layer=layer),
        grid_spec=pltpu.PrefetchScalarGridSpec(
            num_scalar_prefetch=3,
            grid=(MOE_NBLK,),
            in_specs=[pl.BlockSpec((MOE_BLK, D_MODEL), lambda i, be, nx, nu: (jnp.minimum(i, nu[0] - 1), 0)),
                      any_spec, any_spec, any_spec],
            out_specs=pl.BlockSpec((MOE_BLK, D_MODEL), lambda i, *_: (i, 0)),
            scratch_shapes=[pltpu.VMEM((D_MODEL, D_EXPERT), F32), pltpu.VMEM((D_MODEL, D_EXPERT), F32),
                            pltpu.VMEM((D_EXPERT, D_MODEL), F32),
                            pltpu.VMEM((D_MODEL, D_EXPERT), BF16), pltpu.VMEM((D_MODEL, D_EXPERT), BF16),
                            pltpu.VMEM((D_EXPERT, D_MODEL), BF16),
                            pltpu.SemaphoreType.DMA((3,))]),
        out_shape=jax.ShapeDtypeStruct((MOE_CAP, D_MODEL), F32),
        compiler_params=_params("arbitrary"),
        name="moe_experts",
    )(blk_e, nxt_e, n_used, xp, w_gate, w_up, w_down)


DISPATCH_ROWS = 1024


def _dispatch_kernel(src_ref, h_hbm, xp_hbm, sem):
    i = pl.program_id(0)
    base = i * DISPATCH_ROWS

    def issue(r, carry):
        pltpu.make_async_copy(h_hbm.at[pl.ds(src_ref[base + r], 1)], xp_hbm.at[pl.ds(base + r, 1)],
                              sem.at[i % 2]).start()
        return carry

    lax.fori_loop(0, DISPATCH_ROWS, issue, 0, unroll=8)

    def wait_step(step):
        _wait_row_gather(h_hbm, xp_hbm.at[pl.ds(0, DISPATCH_ROWS)], sem.at[step % 2], DISPATCH_ROWS)

    @pl.when(i > 0)
    def _():
        wait_step(i - 1)

    @pl.when(i == pl.num_programs(0) - 1)
    def _():
        wait_step(i)


def _dispatch(h, src):
    return pl.pallas_call(
        _dispatch_kernel,
        grid_spec=pltpu.PrefetchScalarGridSpec(
            num_scalar_prefetch=1,
            grid=(MOE_CAP // DISPATCH_ROWS,),
            in_specs=[pl.BlockSpec(memory_space=pl.ANY)],
            out_specs=pl.BlockSpec(memory_space=pl.ANY),
            scratch_shapes=[pltpu.SemaphoreType.DMA((2,))]),
        out_shape=jax.ShapeDtypeStruct((MOE_CAP, D_MODEL), F32),
        compiler_params=_params("arbitrary"),
        name="moe_dispatch",
    )(src, h)


def _combine_kernel(*refs, tm, final):
    dest_ref, yp_hbm, x_ref, rec_ref, mod_ref = refs[:5]
    refs = refs[5:]
    if final:
        fn_ref, yc_ref, yl_ref = refs[:3]
    else:
        xo_ref = refs[0]
    ybuf, sem = refs[-2:]
    i = pl.program_id(0)

    def gather_start(tile, slot):
        for k in range(TOP_K):
            _start_row_gather(yp_hbm, ybuf.at[slot, k], sem.at[slot, k],
                              lambda r, k=k: dest_ref[(tile * tm + r) * TOP_K + k], tm)

    @pl.when(i == 0)
    def _():
        gather_start(0, 0)

    slot = i % 2
    for k in range(TOP_K):
        _wait_row_gather(yp_hbm, ybuf.at[slot, k], sem.at[slot, k], tm)

    @pl.when(i + 1 < pl.num_programs(0))
    def _():
        gather_start(i + 1, 1 - slot)

    rec = rec_ref[...]
    y = ybuf[slot, 0] * rec[:, R_W0:R_W0 + 1] + ybuf[slot, 1] * rec[:, R_W1:R_W1 + 1]
    x = x_ref[...] + mod_ref[0][:, 5 * D_MODEL:6 * D_MODEL] * y
    if final:
        y = _rms(x) * fn_ref[...]

        @pl.when(i < N_CTX // tm)
        def _():
            yc_ref[...] = y

        @pl.when(i >= N_CTX // tm)
        def _():
            yl_ref[...] = y
    else:
        xo_ref[...] = x


def _combine(yp, dest, x, rec, mod, final_gain):
    tm = 256
    n_ct = N_CTX // tm
    final = final_gain is not None
    row = lambda i, d: (i, 0)
    in_specs = [pl.BlockSpec(memory_space=pl.ANY),
                pl.BlockSpec((tm, D_MODEL), row),
                pl.BlockSpec((tm, ROUTER_W), row),
                pl.BlockSpec((1, 1, 6 * D_MODEL), lambda i, d: (_cond_row(i, tm), 0, 0))]
    args = [dest, yp, x, rec, mod]
    if final:
        in_specs.append(pl.BlockSpec((1, D_MODEL), lambda i, d: (0, 0)))
        args.append(final_gain)
        out_specs = [pl.BlockSpec((tm, D_MODEL), lambda i, d: (jnp.minimum(i, n_ct - 1), 0)),
                     pl.BlockSpec((tm, D_MODEL), lambda i, d: (jnp.maximum(i - n_ct, 0), 0))]
        out_shape = [jax.ShapeDtypeStruct((N_CTX, D_MODEL), F32), jax.ShapeDtypeStruct((N_LAT, D_MODEL), F32)]
    else:
        out_specs = pl.BlockSpec((tm, D_MODEL), row)
        out_shape = jax.ShapeDtypeStruct((N_TOK, D_MODEL), F32)
    return pl.pallas_call(
        functools.partial(_combine_kernel, tm=tm, final=final),
        grid_spec=pltpu.PrefetchScalarGridSpec(
            num_scalar_prefetch=1,
            grid=(N_TOK // tm,),
            in_specs=in_specs,
            out_specs=out_specs,
            scratch_shapes=[pltpu.VMEM((2, TOP_K, tm, D_MODEL), F32), pltpu.SemaphoreType.DMA((2, TOP_K))]),
        out_shape=out_shape,
        compiler_params=_params("arbitrary"),
        name="moe_combine",
    )(*args)


def _swap_halves(w, chunk):
    k, n = w.shape
    w = w.reshape(k, n // chunk, 2, chunk // 2)
    return w[:, :, ::-1, :].reshape(k, n)


def _rope_tables(rot_dim):
    rows = DEC_SEQ // GRID_W
    row = jnp.repeat(jnp.arange(rows), GRID_W).astype(F32)
    col = (jnp.arange(rows * GRID_W) % GRID_W).astype(F32)
    axis_dim = rot_dim // 2
    inv = ROPE_THETA ** (-jnp.arange(0, axis_dim, 2, dtype=F32) / axis_dim)
    ang = jnp.concatenate([row[:, None] * inv, col[:, None] * inv], axis=-1)
    c, s = jnp.cos(ang), jnp.sin(ang)
    return jnp.concatenate([c, c], axis=-1), jnp.concatenate([-s, s], axis=-1)


def _head_major(cache):
    b, l, kvh, d = cache.shape
    return jnp.transpose(cache, (2, 0, 1, 3)).reshape(kvh, b * l, d).astype(BF16)


def kernel(x_prompt, x_sample, c, cache_mla_ckv, cache_mla_krope, cache_gqa_k, cache_gqa_v, cache_swa_k,
           cache_swa_v, state_ret, c_ctx, ada_w, ada_b, norm_mix, norm_ffn, w_in, mla_q_norm, mla_kv_norm,
           mla_w_uq, mla_w_ukv, gqa_q_norm, gqa_k_norm, ret_decay_logit, ret_gn, swa_sink, w_branch, w_out,
           router_group_w, router_group_b, router_expert_w, router_expert_b, moe_w_gate, moe_w_up, moe_w_down,
           final_norm):
    xs = (x_prompt.reshape(N_CTX, D_MODEL), x_sample.reshape(N_LAT, D_MODEL))
    cond = jnp.concatenate([c_ctx[None, :], c, jnp.zeros((N_COND - 1 - DEC_BATCH, D_MODEL), F32)], axis=0)
    mod_all = _ada(cond, ada_w, ada_b).reshape(DEPTH, N_COND, 1, 6 * D_MODEL)

    w_in_r = _w_in_layout(w_in)
    tabs = _rope_tables(HEAD_DIM) + _rope_tables(MLA_ROPE)
    log_gamma = jax.nn.log_sigmoid(ret_decay_logit.astype(F32))
    uq = mla_w_uq.reshape(DEPTH, MLA_Q_LORA, MLA_H, MLA_DK)
    ukv = mla_w_ukv.reshape(DEPTH, MLA_KV_LORA, MLA_H, MLA_NOPE + MLA_V)
    w_router = jnp.concatenate(
        [router_group_w, router_expert_w,
         jnp.zeros((DEPTH, D_MODEL, ROUTER_W - N_GROUPS - N_EXPERTS), F32)], axis=-1).astype(BF16)
    b_router = jnp.concatenate(
        [router_group_b, router_expert_b, jnp.zeros((DEPTH, ROUTER_W - N_GROUPS - N_EXPERTS), F32)], axis=-1)
    zero_state = jnp.zeros((BATCH, RET_H, RET_DK, RET_DV), F32)

    caches = [[] for _ in range(7)]
    for l in range(DEPTH):
        mod = mod_all[l]
        uq_rope = uq[l, :, :, MLA_NOPE:].reshape(MLA_Q_LORA, MLA_H * MLA_ROPE)
        lw = dict(
            q_norm=mla_q_norm[l].reshape(1, -1), kv_norm=mla_kv_norm[l].reshape(1, -1),
            gq_norm=gqa_q_norm[l].reshape(1, -1), gk_norm=gqa_k_norm[l].reshape(1, -1),
            w_uq_nope=uq[l, :, :, :MLA_NOPE].reshape(MLA_Q_LORA, MLA_H * MLA_NOPE).astype(BF16),
            w_uq_rope=uq_rope.astype(BF16), w_uq_swap=_swap_halves(uq_rope, MLA_ROPE).astype(BF16),
            w_uk=ukv[l, :, :, :MLA_NOPE].reshape(MLA_KV_LORA, MLA_H * MLA_NOPE).astype(BF16),
            w_uv=ukv[l, :, :, MLA_NOPE:].reshape(MLA_KV_LORA, MLA_H * MLA_V).astype(BF16))

        z = _in_proj(xs, mod, norm_mix[l].reshape(1, -1), w_in_r, l)

        ctx_ops = _prep(z, lw, None, latent=False)
        ckv, kr, kg, vg, ks, vs = ctx_ops[9:]
        oa_c, ob_c, od_c = _ctx_attention(ctx_ops[:9], swa_sink[l])
        gn = ret_gn[l]
        of_c, s_f = _retention(z, log_gamma[l, 0], gn[0:1], zero_state, None, latent=False, reverse=False)
        oc_c, s_b = _retention(z, log_gamma[l, 1], gn[1:2], zero_state, of_c, latent=False, reverse=True)

        (qa, ka, va, qb, kb, vb, qd, kd, vd) = _prep(z, lw, tabs, latent=True)
        ka0, va0 = _mla_ctx(cache_mla_ckv[:, l].reshape(DEC_BATCH * PAST_LEN, MLA_KV_LORA),
                            cache_mla_krope[:, l].reshape(DEC_BATCH * PAST_LEN, MLA_ROPE), lw)
        oa_l = _flash_latent(qa, ka, va, (ka0, va0))
        ob_l = _flash_latent(qb, kb, vb, (_head_major(cache_gqa_k[:, l]), _head_major(cache_gqa_v[:, l])))
        od_l = _window(qd, kd, vd, _head_major(cache_swa_k[:, l]), _head_major(cache_swa_v[:, l]), swa_sink[l])
        of_l, _ = _retention(z, log_gamma[l, 0], gn[0:1], state_ret[:, l, 0], None, latent=True, reverse=False)
        oc_l, _ = _retention(z, log_gamma[l, 1], gn[1:2], state_ret[:, l, 1], of_l, latent=True, reverse=True)

        x_mid, h_ffn, logits = _merge(xs, (oa_c, ob_c, oc_c, od_c), (oa_l, ob_l, oc_l, od_l), z,
                                      w_branch[l].astype(BF16), w_out[l].astype(BF16), mod,
                                      norm_ffn[l].reshape(1, -1), w_router[l], b_router[l].reshape(1, -1))

        rec, dest, src, blk_e, nxt_e, n_used = _route(logits)
        yp = _experts(_dispatch(h_ffn, src), blk_e, nxt_e, n_used, moe_w_gate, moe_w_up, moe_w_down, l)
        if l < DEPTH - 1:
            xs = (_combine(yp, dest, x_mid, rec, mod, None),)
        else:
            y_ctx, y_lat = _combine(yp, dest, x_mid, rec, mod, final_norm.reshape(1, D_MODEL))

        kvh_shape = (BATCH, SEQ, GQA_KVH, HEAD_DIM)
        for lst, val in zip(caches, (ckv.reshape(BATCH, SEQ, MLA_KV_LORA), kr.reshape(BATCH, SEQ, MLA_ROPE),
                                     kg.reshape(kvh_shape), vg.reshape(kvh_shape), ks.reshape(kvh_shape),
                                     vs.reshape(kvh_shape), jnp.stack([s_f, s_b], axis=1))):
            lst.append(val)

    y_prompt = y_ctx.reshape(BATCH, SEQ, D_MODEL)
    y_sample = y_lat.reshape(DEC_BATCH, DEC_SEQ, D_MODEL)
    return (y_prompt, y_sample) + tuple(jnp.stack(lst, axis=1) for lst in caches)
```

```python
import functools

import jax
import jax.numpy as jnp
from jax import lax
from jax.experimental import pallas as pl
from jax.experimental.pallas import tpu as pltpu

F32 = jnp.float32
BF16 = jnp.bfloat16
LANES = 128

D_MODEL = 2048
BATCH = 16
SEQ = 256
DEPTH = 4
DEC_BATCH = 2
DEC_SEQ = 4096
PAST_LEN = 256
GRID_W = 64
ROPE_THETA = 10000.0
EPS = 1e-6
QBLK = 128
NEG_INF = -1e30

MLA_H = 4
MLA_Q_LORA = 512
MLA_KV_LORA = 256
MLA_NOPE = 128
MLA_ROPE = 64
MLA_V = 128
MLA_SCALE = (MLA_NOPE + MLA_ROPE) ** -0.5
MLA_DK = MLA_NOPE + MLA_ROPE

HEAD_DIM = 128
ATTN_SCALE = HEAD_DIM ** -0.5
GQA_H = 4
GQA_KVH = 2
SWA_H = 4
SWA_KVH = 2
WINDOW = 128

RET_H = 4
RET_DK = 64
RET_DV = 128
RET_CHUNK = 128
RET_K_SCALE = RET_DK ** -0.5

N_BRANCH = 4
BRANCH_W = 512

N_GROUPS = 4
EXPERTS_PER_GROUP = 8
N_EXPERTS = N_GROUPS * EXPERTS_PER_GROUP
TOP_K = 2
D_EXPERT = 1024

N_CTX = BATCH * SEQ
N_LAT = DEC_BATCH * DEC_SEQ
N_TOK = N_CTX + N_LAT
N_COND = 8

W_IN_HEAD = MLA_Q_LORA + MLA_KV_LORA + MLA_ROPE
IN_WIDTH = (W_IN_HEAD + (GQA_H + 2 * GQA_KVH) * HEAD_DIM + 2 * RET_H * RET_DK + 3 * RET_H * RET_DV
            + (SWA_H + 2 * SWA_KVH) * HEAD_DIM + N_BRANCH * D_MODEL)
Z_TAIL = 1024
Z_WIDTH = Z_TAIL + IN_WIDTH - W_IN_HEAD
Z_OFF = dict(mq=0, mkv=512, mkr=768, gq=1024, gk=1536, gv=1792, rq=2048, rk=2304, rv=2560, rgf=3072, rgb=3584,
             sq=4096, sk=4608, sv=4864, gate=5120)
Z_TILE = 1024

MOE_BLK = 256
MOE_ASSIGN = N_TOK * TOP_K
MOE_NBLK = (MOE_ASSIGN + N_EXPERTS * (MOE_BLK - 1) + MOE_BLK - 1) // MOE_BLK
MOE_CAP = MOE_NBLK * MOE_BLK
ROUTER_W = 128
R_W0, R_W1, R_E0, R_E1, R_RANK0, R_RANK1 = range(6)

VMEM_LIMIT = 56 * 1024 * 1024


def _params(*sem):
    return pltpu.CompilerParams(dimension_semantics=sem, vmem_limit_bytes=VMEM_LIMIT)


def _cond_row(i, tm):
    start = i * tm
    return jnp.where(start < N_CTX, 0, 1 + (start - N_CTX) // DEC_SEQ)


def _rms(x):
    return x * lax.rsqrt(jnp.mean(x * x, axis=-1, keepdims=True) + EPS)


def _token_specs(xs, tm, index_of):
    if len(xs) == 1:
        return [pl.BlockSpec((tm, D_MODEL), lambda *g: (index_of(*g), 0))]
    n_ct = N_CTX // tm
    return [pl.BlockSpec((tm, D_MODEL), lambda *g: (jnp.minimum(index_of(*g), n_ct - 1), 0),
                         pipeline_mode=pl.Buffered(1)),
            pl.BlockSpec((tm, D_MODEL), lambda *g: (jnp.maximum(index_of(*g) - n_ct, 0), 0),
                         pipeline_mode=pl.Buffered(1))]


def _token_tile(x_refs, i, tm):
    if len(x_refs) == 1:
        return x_refs[0][...]
    return jnp.where(i < N_CTX // tm, x_refs[0][...], x_refs[1][...])


def _ada_kernel(c_ref, w_ref, b_ref, o_ref):
    s = jax.nn.silu(c_ref[...]).astype(BF16)
    o_ref[0] = jnp.dot(s, w_ref[0].astype(BF16), preferred_element_type=F32) + b_ref[0]


def _ada(cond, ada_w, ada_b):
    tn = 1024
    n_out = 6 * D_MODEL
    return pl.pallas_call(
        _ada_kernel,
        grid=(DEPTH, n_out // tn),
        in_specs=[pl.BlockSpec((N_COND, D_MODEL), lambda l, j: (0, 0)),
                  pl.BlockSpec((1, D_MODEL, tn), lambda l, j: (l, 0, j)),
                  pl.BlockSpec((1, 1, tn), lambda l, j: (l, 0, j))],
        out_specs=pl.BlockSpec((1, N_COND, tn), lambda l, j: (l, 0, j)),
        out_shape=jax.ShapeDtypeStruct((DEPTH, N_COND, n_out), F32),
        compiler_params=_params("parallel", "parallel"),
        name="ada_mod",
    )(cond, ada_w, ada_b.reshape(DEPTH, 1, n_out))


def _w_in_kernel(w_ref, o_ref):
    w = w_ref[0]

    @pl.when(pl.program_id(1) == 0)
    def _():
        half = MLA_ROPE // 2
        mkr0 = W_IN_HEAD - MLA_ROPE
        o_ref[0, 0:W_IN_HEAD] = w[0:W_IN_HEAD].astype(BF16)
        o_ref[0, W_IN_HEAD:W_IN_HEAD + half] = w[mkr0 + half:W_IN_HEAD].astype(BF16)
        o_ref[0, W_IN_HEAD + half:W_IN_HEAD + MLA_ROPE] = w[mkr0:mkr0 + half].astype(BF16)
        o_ref[0, W_IN_HEAD + MLA_ROPE:Z_TAIL] = jnp.zeros((Z_TAIL - W_IN_HEAD - MLA_ROPE, w.shape[1]), BF16)

    @pl.when(pl.program_id(1) > 0)
    def _():
        o_ref[0] = w.astype(BF16)


def _w_in_layout(w_in):
    w_t = jnp.swapaxes(w_in, 1, 2)

    def src_row(j):
        return pl.multiple_of(jnp.where(j > 0, (j - 1) * Z_TILE + W_IN_HEAD, 0), 16)

    return pl.pallas_call(
        _w_in_kernel,
        grid=(DEPTH, Z_WIDTH // Z_TILE),
        in_specs=[pl.BlockSpec((pl.Element(1), pl.Element(Z_TILE), pl.Element(D_MODEL)),
                               lambda l, j: (l, src_row(j), 0))],
        out_specs=pl.BlockSpec((1, Z_TILE, D_MODEL), lambda l, j: (l, j, 0)),
        out_shape=jax.ShapeDtypeStruct((DEPTH, Z_WIDTH, D_MODEL), BF16),
        compiler_params=_params("parallel", "parallel"),
        name="w_in_layout",
    )(w_t)


def _in_kernel(*refs, n_x, tm):
    x_refs, (mod_ref, g_ref, w_ref, z_ref, h_ref) = refs[:n_x], refs[n_x:]

    @pl.when(pl.program_id(1) == 0)
    def _():
        m = mod_ref[0]
        y = _rms(_token_tile(x_refs, pl.program_id(0), tm)) * g_ref[...]
        h_ref[...] = (y * (1.0 + m[:, D_MODEL:2 * D_MODEL]) + m[:, 0:D_MODEL]).astype(BF16)

    z_ref[...] = lax.dot_general(h_ref[...], w_ref[...], (((1,), (1,)), ((), ())), preferred_element_type=F32)


def _in_proj(xs, mod, gain, w_all, layer):
    tm, tn = 1024, Z_TILE
    return pl.pallas_call(
        functools.partial(_in_kernel, n_x=len(xs), tm=tm),
        grid=(N_TOK // tm, Z_WIDTH // tn),
        in_specs=_token_specs(xs, tm, lambda i, j: i)
        + [pl.BlockSpec((1, 1, 6 * D_MODEL), lambda i, j: (_cond_row(i, tm), 0, 0)),
           pl.BlockSpec((1, D_MODEL), lambda i, j: (0, 0)),
           pl.BlockSpec((None, tn, D_MODEL), lambda i, j: (layer, j, 0))],
        out_specs=pl.BlockSpec((tm, tn), lambda i, j: (i, j)),
        out_shape=jax.ShapeDtypeStruct((N_TOK, Z_WIDTH), F32),
        scratch_shapes=[pltpu.VMEM((tm, D_MODEL), BF16)],
        compiler_params=_params("parallel", "arbitrary"),
        name="in_proj",
    )(*xs, mod, gain, w_all)


def _rope128(x, cos, sin):
    return x * cos + pltpu.roll(x, HEAD_DIM // 2, 1) * sin


def _mla_kv(ckv, kr, wuk_ref, wuv_ref, ka_ref, va_ref):
    cb = ckv.astype(BF16)
    kn = jnp.dot(cb, wuk_ref[...], preferred_element_type=F32)
    vv = jnp.dot(cb, wuv_ref[...], preferred_element_type=F32)
    for h in range(MLA_H):
        ka_ref[h, :, 0:MLA_NOPE] = kn[:, h * MLA_NOPE:(h + 1) * MLA_NOPE].astype(BF16)
        ka_ref[h, :, MLA_NOPE:MLA_DK] = kr.astype(BF16)
        va_ref[h] = vv[:, h * MLA_V:(h + 1) * MLA_V].astype(BF16)


def _prep_kernel(*refs, latent):
    (mq_ref, gq_ref, sq_ref, mkv_ref, gk_ref, gv_ref, sk_ref, sv_ref, mkr_ref,
     qn_ref, kvn_ref, gqn_ref, gkn_ref, wqn_ref, wqr_ref, wqs_ref, wuk_ref, wuv_ref) = refs[:18]
    refs = refs[18:]
    if latent:
        cos_ref, sin_ref, c64_ref, s64_ref = refs[:4]
        refs = refs[4:]
        cos, sin = cos_ref[...], sin_ref[...]
        c64, s64 = c64_ref[...], s64_ref[...]
    qa_ref, ka_ref, va_ref, qb_ref, kb_ref, vb_ref, qd_ref, kd_ref, vd_ref = refs[:9]
    refs = refs[9:]

    hn = (_rms(mq_ref[...]) * qn_ref[...]).astype(BF16)
    q_nope = jnp.dot(hn, wqn_ref[...], preferred_element_type=F32)
    q_rope = jnp.dot(hn, wqr_ref[...], preferred_element_type=F32)
    if latent:
        q_swap = jnp.dot(hn, wqs_ref[...], preferred_element_type=F32)
    for h in range(MLA_H):
        qa_ref[h, :, 0:MLA_NOPE] = (q_nope[:, h * MLA_NOPE:(h + 1) * MLA_NOPE] * MLA_SCALE).astype(BF16)
        qr = q_rope[:, h * MLA_ROPE:(h + 1) * MLA_ROPE]
        if latent:
            qr = qr * c64 + q_swap[:, h * MLA_ROPE:(h + 1) * MLA_ROPE] * s64
        qa_ref[h, :, MLA_NOPE:MLA_DK] = (qr * MLA_SCALE).astype(BF16)
    ckv = _rms(mkv_ref[...]) * kvn_ref[...]
    kr = mkr_ref[:, 0:MLA_ROPE]
    if latent:
        kr = kr * c64 + mkr_ref[:, MLA_ROPE:2 * MLA_ROPE] * s64
    _mla_kv(ckv, kr, wuk_ref, wuv_ref, ka_ref, va_ref)

    gq = gq_ref[...]
    for h in range(GQA_H):
        q = _rms(gq[:, h * HEAD_DIM:(h + 1) * HEAD_DIM]) * gqn_ref[...]
        if latent:
            q = _rope128(q, cos, sin)
        qb_ref[h] = (q * ATTN_SCALE).astype(BF16)
    gk = gk_ref[...]
    gv = gv_ref[...]
    kg = []
    for h in range(GQA_KVH):
        k = _rms(gk[:, h * HEAD_DIM:(h + 1) * HEAD_DIM]) * gkn_ref[...]
        kg.append(k)
        if latent:
            k = _rope128(k, cos, sin)
        kb_ref[h] = k.astype(BF16)
        vb_ref[h] = gv[:, h * HEAD_DIM:(h + 1) * HEAD_DIM].astype(BF16)

    sq = sq_ref[...]
    for h in range(SWA_H):
        q = sq[:, h * HEAD_DIM:(h + 1) * HEAD_DIM]
        if latent:
            q = _rope128(q, cos, sin)
        qd_ref[h] = (q * ATTN_SCALE).astype(BF16)
    sk = sk_ref[...]
    sv = sv_ref[...]
    for h in range(SWA_KVH):
        k = sk[:, h * HEAD_DIM:(h + 1) * HEAD_DIM]
        if latent:
            k = _rope128(k, cos, sin)
        kd_ref[h] = k.astype(BF16)
        vd_ref[h] = sv[:, h * HEAD_DIM:(h + 1) * HEAD_DIM].astype(BF16)

    if not latent:
        ckv_ref, kr_ref, kg_ref, vg_ref, ks_ref, vs_ref = refs
        ckv_ref[...] = ckv
        kr_ref[...] = kr
        for h in range(GQA_KVH):
            kg_ref[:, h * HEAD_DIM:(h + 1) * HEAD_DIM] = kg[h]
        vg_ref[...] = gv
        ks_ref[...] = sk
        vs_ref[...] = sv


def _prep(z, lw, tabs, latent):
    tm = 512
    rows = N_LAT if latent else N_CTX
    roff = (N_CTX if latent else 0) // tm
    n_t = DEC_SEQ // tm

    def zc(name, width):
        cb = Z_OFF[name] // width
        return pl.BlockSpec((tm, width), lambda i: (i + roff, cb))

    def full(a):
        nd = a.ndim
        return pl.BlockSpec(a.shape, lambda i: (0,) * nd)

    weights = [lw["q_norm"], lw["kv_norm"], lw["gq_norm"], lw["gk_norm"],
               lw["w_uq_nope"], lw["w_uq_rope"], lw["w_uq_swap"], lw["w_uk"], lw["w_uv"]]
    in_specs = [zc("mq", 512), zc("gq", 512), zc("sq", 512), zc("mkv", 256), zc("gk", 256), zc("gv", 256),
                zc("sk", 256), zc("sv", 256), zc("mkr", 128)] + [full(a) for a in weights]
    args = [z] * 9 + weights
    if latent:
        in_specs += [pl.BlockSpec((tm, HEAD_DIM), lambda i: (i % n_t, 0))] * 2
        in_specs += [pl.BlockSpec((tm, MLA_ROPE), lambda i: (i % n_t, 0))] * 2
        args += list(tabs)

    def heads(n, d):
        return (pl.BlockSpec((n, tm, d), lambda i: (0, i, 0)), jax.ShapeDtypeStruct((n, rows, d), BF16))

    outs = [heads(MLA_H, MLA_DK), heads(MLA_H, MLA_DK), heads(MLA_H, MLA_V),
            heads(GQA_H, HEAD_DIM), heads(GQA_KVH, HEAD_DIM), heads(GQA_KVH, HEAD_DIM),
            heads(SWA_H, HEAD_DIM), heads(SWA_KVH, HEAD_DIM), heads(SWA_KVH, HEAD_DIM)]
    if not latent:
        kv_w = GQA_KVH * HEAD_DIM
        for d in (MLA_KV_LORA, MLA_ROPE, kv_w, kv_w, kv_w, kv_w):
            outs.append((pl.BlockSpec((tm, d), lambda i: (i, 0)), jax.ShapeDtypeStruct((rows, d), F32)))
    return pl.pallas_call(
        functools.partial(_prep_kernel, latent=latent),
        grid=(rows // tm,),
        in_specs=in_specs,
        out_specs=[o[0] for o in outs],
        out_shape=[o[1] for o in outs],
        compiler_params=_params("parallel"),
        name="prep_latent" if latent else "prep_context",
    )(*args)


def _mla_ctx_kernel(ckv_ref, kr_ref, wuk_ref, wuv_ref, ka_ref, va_ref):
    _mla_kv(ckv_ref[...], kr_ref[...], wuk_ref, wuv_ref, ka_ref, va_ref)


def _mla_ctx(ckv, kr, lw):
    rows = ckv.shape[0]
    return pl.pallas_call(
        _mla_ctx_kernel,
        out_shape=[jax.ShapeDtypeStruct((MLA_H, rows, MLA_DK), BF16),
                   jax.ShapeDtypeStruct((MLA_H, rows, MLA_V), BF16)],
        name="mla_ctx_kv",
    )(ckv, kr, lw["w_uk"], lw["w_uv"])


def _softmax_pv(q, k, v, sink):
    s = lax.dot_general(q, k, (((1,), (1,)), ((), ())), preferred_element_type=F32)
    m = jnp.max(s, axis=-1, keepdims=True)
    if sink is not None:
        m = jnp.maximum(m, sink)
    p = jnp.exp(s - m)
    l = jnp.sum(p, axis=-1, keepdims=True)
    if sink is not None:
        l = l + jnp.exp(sink - m)
    return jnp.dot(p.astype(BF16), v, preferred_element_type=F32) / l


def _ctx_attn_kernel(sink_ref, qa, ka, va, qb, kb, vb, qd, kd, vd, oa, ob, od):
    for h in range(MLA_H):
        oa[:, h * MLA_V:(h + 1) * MLA_V] = _softmax_pv(qa[h], ka[h], va[h], None).astype(BF16)
    for h in range(GQA_H):
        kv = h // (GQA_H // GQA_KVH)
        ob[:, h * HEAD_DIM:(h + 1) * HEAD_DIM] = _softmax_pv(qb[h], kb[kv], vb[kv], None).astype(BF16)
    for h in range(SWA_H):
        kv = h // (SWA_H // SWA_KVH)
        od[:, h * HEAD_DIM:(h + 1) * HEAD_DIM] = _softmax_pv(qd[h], kd[kv], vd[kv], sink_ref[h]).astype(BF16)


def _ctx_attention(operands, sink):
    def spec(a):
        return pl.BlockSpec((a.shape[0], SEQ, a.shape[2]), lambda b: (0, b, 0))

    out = jax.ShapeDtypeStruct((N_CTX, BRANCH_W), BF16)
    return pl.pallas_call(
        _ctx_attn_kernel,
        grid=(BATCH,),
        in_specs=[pl.BlockSpec(memory_space=pltpu.SMEM)] + [spec(a) for a in operands],
        out_specs=[pl.BlockSpec((SEQ, BRANCH_W), lambda b: (b, 0))] * 3,
        out_shape=[out, out, out],
        compiler_params=_params("parallel"),
        name="context_attention",
    )(sink, *operands)


KEY_CHUNK = 256


def _flash_latent_kernel(q_ref, k_ref, v_ref, k0_ref, v0_ref, o_ref, m_sc, l_sc, acc_sc, *, groups):
    dv = v_ref.shape[-1]
    n_chunks = k_ref.shape[1] // KEY_CHUNK
    dn = (((1,), (1,)), ((), ()))

    def scores(g, k):
        return lax.dot_general(q_ref[g], k, dn, preferred_element_type=F32)

    def lane_max(s):
        m = s[:, 0:LANES]
        for c in range(1, s.shape[1] // LANES):
            m = jnp.maximum(m, s[:, c * LANES:(c + 1) * LANES])
        return m

    for g in range(groups):
        m_sc[g] = lane_max(scores(g, k0_ref[0]))

    def max_body(j, carry):
        off = pl.multiple_of(j * KEY_CHUNK, KEY_CHUNK)
        for g in range(groups):
            m_sc[g] = jnp.maximum(m_sc[g], lane_max(scores(g, k_ref[0, pl.ds(off, KEY_CHUNK), :])))
        return carry

    lax.fori_loop(0, n_chunks, max_body, 0, unroll=True)
    for g in range(groups):
        m_sc[g] = jnp.broadcast_to(jnp.max(m_sc[g], axis=-1, keepdims=True), m_sc.shape[1:])

    def accumulate(g, k, v, first):
        s = scores(g, k)
        m = m_sc[g]
        ps = [jnp.exp(s[:, c * LANES:(c + 1) * LANES] - m) for c in range(s.shape[1] // LANES)]
        lsum = ps[0]
        for p in ps[1:]:
            lsum = lsum + p
        pv = jnp.dot(jnp.concatenate(ps, axis=1).astype(BF16), v, preferred_element_type=F32)
        if first:
            l_sc[g] = lsum
            acc_sc[g] = pv
        else:
            l_sc[g] += lsum
            acc_sc[g] += pv

    for g in range(groups):
        accumulate(g, k0_ref[0], v0_ref[0], True)

    def sum_body(j, carry):
        off = pl.multiple_of(j * KEY_CHUNK, KEY_CHUNK)
        for g in range(groups):
            accumulate(g, k_ref[0, pl.ds(off, KEY_CHUNK), :], v_ref[0, pl.ds(off, KEY_CHUNK), :], False)
        return carry

    lax.fori_loop(0, n_chunks, sum_body, 0, unroll=True)
    for g in range(groups):
        l = jnp.sum(l_sc[g], axis=-1, keepdims=True)
        o_ref[:, g * dv:(g + 1) * dv] = (acc_sc[g] / l).astype(o_ref.dtype)


def _flash_latent(q, k, v, ctx):
    tq = 1024
    n_h, _, dk = q.shape
    kvh, _, dv = v.shape
    groups = n_h // kvh
    nq = DEC_SEQ // tq
    return pl.pallas_call(
        functools.partial(_flash_latent_kernel, groups=groups),
        grid=(DEC_BATCH, kvh, nq),
        in_specs=[pl.BlockSpec((groups, tq, dk), lambda b, h, i: (h, b * nq + i, 0)),
                  pl.BlockSpec((1, DEC_SEQ, dk), lambda b, h, i: (h, b, 0)),
                  pl.BlockSpec((1, DEC_SEQ, dv), lambda b, h, i: (h, b, 0)),
                  pl.BlockSpec((1, PAST_LEN, dk), lambda b, h, i: (h, b, 0)),
                  pl.BlockSpec((1, PAST_LEN, dv), lambda b, h, i: (h, b, 0))],
        out_specs=pl.BlockSpec((tq, groups * dv), lambda b, h, i: (b * nq + i, h)),
        out_shape=jax.ShapeDtypeStruct((N_LAT, n_h * dv), BF16),
        scratch_shapes=[pltpu.VMEM((groups, tq, LANES), F32), pltpu.VMEM((groups, tq, LANES), F32),
                        pltpu.VMEM((groups, tq, dv), F32)],
        compiler_params=_params("parallel", "parallel", "parallel"),
        name="flash_latent",
    )(q, k, v, *ctx)


WIN_TQ = 512
WIN_SPAN = WIN_TQ + 2 * WINDOW


def _lanewise(op, x):
    r = x[:, 0:LANES]
    for c in range(1, x.shape[1] // LANES):
        r = op(r, x[:, c * LANES:(c + 1) * LANES])
    return r


def _exp_tiles(s, m):
    return jnp.concatenate([jnp.exp(s[:, c * LANES:(c + 1) * LANES] - m) for c in range(s.shape[1] // LANES)],
                           axis=1)


def _window_kernel(sink_ref, q_ref, k_ref, v_ref, k0_ref, v0_ref, o_ref, *, groups):
    head0 = pl.program_id(1) * groups
    t = k_ref.shape[1]
    q0 = pl.program_id(2) * WIN_TQ
    k_start = pl.multiple_of(jnp.clip(q0 - WINDOW, 0, t - WIN_SPAN), LANES)
    qpos = q0 + lax.broadcasted_iota(jnp.int32, (WIN_TQ, WIN_SPAN), 0)
    kpos = k_start + lax.broadcasted_iota(jnp.int32, (WIN_TQ, WIN_SPAN), 1)
    band = jnp.abs(kpos - qpos) <= WINDOW
    kw = k_ref[0, pl.ds(k_start, WIN_SPAN), :]
    vw = v_ref[0, pl.ds(k_start, WIN_SPAN), :]
    dn = (((1,), (1,)), ((), ()))
    for g in range(groups):
        q = q_ref[g]
        s_loc = jnp.where(band, lax.dot_general(q, kw, dn, preferred_element_type=F32), NEG_INF)
        s_ctx = lax.dot_general(q, k0_ref[0], dn, preferred_element_type=F32)
        snk = sink_ref[head0 + g]
        m = jnp.max(jnp.maximum(_lanewise(jnp.maximum, s_loc), _lanewise(jnp.maximum, s_ctx)),
                    axis=-1, keepdims=True)
        m = jnp.broadcast_to(jnp.maximum(m, snk), (WIN_TQ, LANES))
        p_loc = _exp_tiles(s_loc, m)
        p_ctx = _exp_tiles(s_ctx, m)
        den = (jnp.sum(_lanewise(jnp.add, p_loc) + _lanewise(jnp.add, p_ctx), axis=-1, keepdims=True)
               + jnp.exp(snk - m[:, 0:1]))
        acc = (jnp.dot(p_loc.astype(BF16), vw, preferred_element_type=F32)
               + jnp.dot(p_ctx.astype(BF16), v0_ref[0], preferred_element_type=F32))
        o_ref[:, g * HEAD_DIM:(g + 1) * HEAD_DIM] = (acc / den).astype(o_ref.dtype)


def _window(q, k, v, k0, v0, sink):
    n_h, kvh = q.shape[0], k.shape[0]
    groups = n_h // kvh
    nq = DEC_SEQ // WIN_TQ
    full = pl.BlockSpec((1, DEC_SEQ, HEAD_DIM), lambda b, h, n: (h, b, 0))
    ctx_spec = pl.BlockSpec((1, PAST_LEN, HEAD_DIM), lambda b, h, n: (h, b, 0))
    return pl.pallas_call(
        functools.partial(_window_kernel, groups=groups),
        grid=(DEC_BATCH, kvh, nq),
        in_specs=[pl.BlockSpec(memory_space=pltpu.SMEM),
                  pl.BlockSpec((groups, WIN_TQ, HEAD_DIM), lambda b, h, n: (h, b * nq + n, 0)),
                  full, full, ctx_spec, ctx_spec],
        out_specs=pl.BlockSpec((WIN_TQ, groups * HEAD_DIM), lambda b, h, n: (b * nq + n, h)),
        out_shape=jax.ShapeDtypeStruct((N_LAT, n_h * HEAD_DIM), BF16),
        compiler_params=_params("parallel", "parallel", "parallel"),
        name="window_attention",
    )(sink, q, k, v, k0, v0)


def _ret_kernel(*refs, reverse, has_prev):
    lg_ref, q_ref, k_ref, v_ref, gate_ref, gn_ref, s0_ref = refs[:7]
    refs = refs[7:]
    if has_prev:
        prev_ref, refs = refs[0], refs[1:]
    o_ref, sfin_ref, s_sc = refs
    c = pl.program_id(1)

    @pl.when(c == 0)
    def _():
        s_sc[...] = s0_ref[0]

    cs = RET_CHUNK
    row = lax.broadcasted_iota(jnp.int32, (cs, cs), 0).astype(F32)
    col = lax.broadcasted_iota(jnp.int32, (cs, cs), 1).astype(F32)
    pos = lax.broadcasted_iota(jnp.int32, (cs, 1), 0).astype(F32)
    diff = (col - row) if reverse else (row - col)
    scan_pos = (cs - 1.0 - pos) if reverse else pos
    q_all, k_all, v_all, gate_all = q_ref[...], k_ref[...], v_ref[...], gate_ref[...]
    for h in range(RET_H):
        lg = lg_ref[h]
        dmask = jnp.where(diff >= 0, jnp.exp(jnp.maximum(diff, 0.0) * lg), 0.0)
        q_dec = jnp.exp((scan_pos + 1.0) * lg)
        k_dec = jnp.exp((cs - 1.0 - scan_pos) * lg)
        c_dec = jnp.exp(cs * lg)
        q = q_all[:, h * RET_DK:(h + 1) * RET_DK]
        k = k_all[:, h * RET_DK:(h + 1) * RET_DK] * RET_K_SCALE
        v = v_all[:, h * RET_DV:(h + 1) * RET_DV].astype(BF16)
        s = s_sc[h]
        a = lax.dot_general(q.astype(BF16), k.astype(BF16), (((1,), (1,)), ((), ())),
                            preferred_element_type=F32) * dmask
        o = (jnp.dot(a.astype(BF16), v, preferred_element_type=F32)
             + jnp.dot((q * q_dec).astype(BF16), s.astype(BF16), preferred_element_type=F32))
        kd_t = jnp.transpose(k * k_dec).astype(BF16)
        s_sc[h] = s * c_dec + jnp.dot(kd_t, v, preferred_element_type=F32)
        mu = jnp.mean(o, axis=-1, keepdims=True)
        var = jnp.mean(jnp.square(o - mu), axis=-1, keepdims=True)
        y = (o - mu) * lax.rsqrt(var + EPS) * gn_ref[:, h * RET_DV:(h + 1) * RET_DV]
        y = jax.nn.silu(gate_all[:, h * RET_DV:(h + 1) * RET_DV]) * y
        if has_prev:
            y = prev_ref[:, h * RET_DV:(h + 1) * RET_DV] + y
        o_ref[:, h * RET_DV:(h + 1) * RET_DV] = y.astype(o_ref.dtype)

    @pl.when(c == pl.num_programs(1) - 1)
    def _():
        sfin_ref[0] = s_sc[...]


def _retention(z, log_gamma, gn, s0, prev, *, latent, reverse):
    batch, t = (DEC_BATCH, DEC_SEQ) if latent else (BATCH, SEQ)
    nc = t // RET_CHUNK
    roff = (N_CTX if latent else 0) // RET_CHUNK
    gate_name = "rgb" if reverse else "rgf"

    def blk(b, c):
        return b * nc + (nc - 1 - c if reverse else c)

    def zc(name, width):
        cb = Z_OFF[name] // width
        return pl.BlockSpec((RET_CHUNK, width), lambda b, c: (roff + blk(b, c), cb))

    w = RET_H * RET_DV
    in_specs = [pl.BlockSpec(memory_space=pltpu.SMEM), zc("rq", 256), zc("rk", 256), zc("rv", 512),
                zc(gate_name, 512), pl.BlockSpec((1, w), lambda b, c: (0, 0)),
                pl.BlockSpec((1, RET_H, RET_DK, RET_DV), lambda b, c: (b, 0, 0, 0))]
    args = [log_gamma, z, z, z, z, gn, s0]
    if prev is not None:
        in_specs.append(pl.BlockSpec((RET_CHUNK, w), lambda b, c: (blk(b, c), 0)))
        args.append(prev)
    return pl.pallas_call(
        functools.partial(_ret_kernel, reverse=reverse, has_prev=prev is not None),
        grid=(batch, nc),
        in_specs=in_specs,
        out_specs=[pl.BlockSpec((RET_CHUNK, w), lambda b, c: (blk(b, c), 0)),
                   pl.BlockSpec((1, RET_H, RET_DK, RET_DV), lambda b, c: (b, 0, 0, 0))],
        out_shape=[jax.ShapeDtypeStruct((batch * t, w), BF16 if reverse else F32),
                   jax.ShapeDtypeStruct((batch, RET_H, RET_DK, RET_DV), F32)],
        scratch_shapes=[pltpu.VMEM((RET_H, RET_DK, RET_DV), F32)],
        compiler_params=_params("parallel", "arbitrary"),
        name="retention_bwd" if reverse else "retention_fwd",
    )(*args)


def _merge_kernel(*refs, n_x, tm):
    x_refs, refs = refs[:n_x], refs[n_x:]
    ctx_refs, lat_refs = refs[:N_BRANCH], refs[N_BRANCH:2 * N_BRANCH]
    (gate_ref, wb_ref, wo_ref, mod_ref, g_ref, wr_ref, br_ref, xo_ref, h_ref, lg_ref) = refs[2 * N_BRANCH:]
    i = pl.program_id(0)
    is_ctx = i < N_CTX // tm

    merged = None
    for n in range(N_BRANCH):
        branch = jnp.where(is_ctx, ctx_refs[n][...], lat_refs[n][...])
        t = jnp.dot(branch, wb_ref[n], preferred_element_type=F32)
        term = jax.nn.sigmoid(gate_ref[:, n * D_MODEL:(n + 1) * D_MODEL]) * t
        merged = term if merged is None else merged + term

    m = mod_ref[0]
    out = jnp.dot(merged.astype(BF16), wo_ref[...], preferred_element_type=F32)
    x = _token_tile(x_refs, i, tm) + m[:, 2 * D_MODEL:3 * D_MODEL] * out
    xo_ref[...] = x
    h = (_rms(x) * g_ref[...]) * (1.0 + m[:, 4 * D_MODEL:5 * D_MODEL]) + m[:, 3 * D_MODEL:4 * D_MODEL]
    h_ref[...] = h
    lg_ref[...] = jnp.dot(h.astype(BF16), wr_ref[...], preferred_element_type=F32) + br_ref[...]


def _merge(xs, ctx_branches, lat_branches, z, w_branch, w_out, mod, gain, w_router, b_router):
    tm = 256
    n_ct = N_CTX // tm
    row = lambda i: (i, 0)
    const2 = lambda i: (0, 0)
    resident = pl.Buffered(1)
    return pl.pallas_call(
        functools.partial(_merge_kernel, n_x=len(xs), tm=tm),
        grid=(N_TOK // tm,),
        in_specs=_token_specs(xs, tm, lambda i: i)
        + [pl.BlockSpec((tm, BRANCH_W), lambda i: (jnp.minimum(i, n_ct - 1), 0))] * N_BRANCH
        + [pl.BlockSpec((tm, BRANCH_W), lambda i: (jnp.maximum(i - n_ct, 0), 0))] * N_BRANCH
        + [pl.BlockSpec((pl.Element(tm), pl.Element(N_BRANCH * D_MODEL)),
                        lambda i: (pl.multiple_of(i * tm, tm), Z_OFF["gate"])),
           pl.BlockSpec((N_BRANCH, BRANCH_W, D_MODEL), lambda i: (0, 0, 0), pipeline_mode=resident),
           pl.BlockSpec((D_MODEL, D_MODEL), const2, pipeline_mode=resident),
           pl.BlockSpec((1, 1, 6 * D_MODEL), lambda i: (_cond_row(i, tm), 0, 0)),
           pl.BlockSpec((1, D_MODEL), const2),
           pl.BlockSpec((D_MODEL, ROUTER_W), const2),
           pl.BlockSpec((1, ROUTER_W), const2)],
        out_specs=[pl.BlockSpec((tm, D_MODEL), row), pl.BlockSpec((tm, D_MODEL), row),
                   pl.BlockSpec((tm, ROUTER_W), row)],
        out_shape=[jax.ShapeDtypeStruct((N_TOK, D_MODEL), F32), jax.ShapeDtypeStruct((N_TOK, D_MODEL), F32),
                   jax.ShapeDtypeStruct((N_TOK, ROUTER_W), F32)],
        compiler_params=_params("parallel"),
        name="merge_out_proj",
    )(*xs, *ctx_branches, *lat_branches, z, w_branch, w_out, mod, gain, w_router, b_router)


def _route_kernel(lg_ref, rec_ref, cnt_ref, tri_sc, carry_sc, *, tm):
    @pl.when(pl.program_id(0) == 0)
    def _():
        r = lax.broadcasted_iota(jnp.int32, (tm, tm), 0)
        c = lax.broadcasted_iota(jnp.int32, (tm, tm), 1)
        tri_sc[...] = (c < r).astype(BF16)
        carry_sc[...] = jnp.zeros(carry_sc.shape, F32)

    lg = lg_ref[...]
    lane = lax.broadcasted_iota(jnp.int32, lg.shape, 1)
    big = jnp.int32(ROUTER_W)

    def first_max(vals):
        top = jnp.max(vals, axis=-1, keepdims=True)
        return top, jnp.min(jnp.where(vals == top, lane, big), axis=-1, keepdims=True)

    g_logits = jnp.where(lane < N_GROUPS, lg, -jnp.inf)
    g_top, grp = first_max(g_logits)
    p_grp = 1.0 / jnp.sum(jnp.exp(g_logits - g_top), axis=-1, keepdims=True)
    lo = N_GROUPS + grp * EXPERTS_PER_GROUP
    e_logits = jnp.where((lane >= lo) & (lane < lo + EXPERTS_PER_GROUP), lg, -jnp.inf)
    v0, i0 = first_max(e_logits)
    v1, i1 = first_max(jnp.where(lane == i0, -jnp.inf, e_logits))
    e1 = jnp.exp(v1 - v0)
    den = 1.0 + e1
    w0 = p_grp * (1.0 / den)
    w1 = p_grp * (e1 / den)

    hot0 = (lane == i0).astype(F32)
    hot1 = (lane == i1).astype(F32)
    before = jnp.dot(tri_sc[...], (hot0 + hot1).astype(BF16), preferred_element_type=F32) + carry_sc[...]
    rank0 = jnp.sum(before * hot0, axis=-1, keepdims=True)
    rank1 = jnp.sum(before * hot1, axis=-1, keepdims=True)
    carry_sc[...] += jnp.sum(hot0 + hot1, axis=0, keepdims=True)

    rec = jnp.zeros(lg.shape, F32)
    for slot, val in ((R_W0, w0), (R_W1, w1), (R_E0, (i0 - N_GROUPS).astype(F32)),
                      (R_E1, (i1 - N_GROUPS).astype(F32)), (R_RANK0, rank0), (R_RANK1, rank1)):
        rec = jnp.where(lane == slot, val, rec)
    rec_ref[...] = rec
    cnt_ref[...] = jnp.broadcast_to(carry_sc[...], cnt_ref.shape)


def _route(logits):
    tm = 512
    rec, cnt = pl.pallas_call(
        functools.partial(_route_kernel, tm=tm),
        grid=(N_TOK // tm,),
        in_specs=[pl.BlockSpec((tm, ROUTER_W), lambda i: (i, 0))],
        out_specs=[pl.BlockSpec((tm, ROUTER_W), lambda i: (i, 0)), pl.BlockSpec((8, ROUTER_W), lambda i: (0, 0))],
        out_shape=[jax.ShapeDtypeStruct((N_TOK, ROUTER_W), F32), jax.ShapeDtypeStruct((8, ROUTER_W), F32)],
        scratch_shapes=[pltpu.VMEM((tm, tm), BF16), pltpu.VMEM((1, ROUTER_W), F32)],
        compiler_params=_params("arbitrary"),
        name="moe_route",
    )(logits)
    expert = rec[:, R_E0:R_E1 + 1].astype(jnp.int32)
    rank = rec[:, R_RANK0:R_RANK1 + 1].astype(jnp.int32)
    counts = cnt[0, N_GROUPS:N_GROUPS + N_EXPERTS].astype(jnp.int32)
    padded = (counts + MOE_BLK - 1) // MOE_BLK * MOE_BLK
    pad_end = jnp.cumsum(padded)
    dest = ((pad_end - padded)[expert] + rank).reshape(-1)
    token = jnp.arange(MOE_ASSIGN, dtype=jnp.int32) // TOP_K
    src = jnp.zeros((MOE_CAP,), jnp.int32).at[dest].set(token)
    blk_start = jnp.arange(MOE_NBLK, dtype=jnp.int32) * MOE_BLK
    blk_e = jnp.minimum(jnp.sum(pad_end[None, :] <= blk_start[:, None], axis=1), N_EXPERTS - 1).astype(jnp.int32)
    n_used = (pad_end[-1:] // MOE_BLK).astype(jnp.int32)
    ids = jnp.arange(N_EXPERTS, dtype=jnp.int32)
    later = jnp.where((ids[None, :] > ids[:, None]) & (counts[None, :] > 0), ids[None, :], N_EXPERTS)
    nxt = jnp.min(later, axis=1)
    nxt_e = jnp.where(nxt < N_EXPERTS, nxt, -1).astype(jnp.int32)[blk_e]
    return rec, dest, src, blk_e, nxt_e, n_used


def _start_row_gather(src_hbm, dst, sem, index_of, n_rows):
    def issue(r, carry):
        pltpu.make_async_copy(src_hbm.at[pl.ds(index_of(r), 1)], dst.at[pl.ds(r, 1)], sem).start()
        return carry

    lax.fori_loop(0, n_rows, issue, 0, unroll=8)


def _wait_row_gather(src_hbm, dst, sem, n_rows):
    pltpu.make_async_copy(src_hbm.at[pl.ds(0, n_rows)], dst, sem).wait()


def _expert_kernel(be_ref, nx_ref, nu_ref, x_ref, wg_hbm, wu_hbm, wd_hbm, y_ref,
                   stage_g, stage_u, stage_d, res_g, res_u, res_d, wsem, *, layer):
    i = pl.program_id(0)
    n_used = nu_ref[0]

    def weight_copies(e):
        return [pltpu.make_async_copy(w.at[layer, e], st, wsem.at[k])
                for k, (w, st) in enumerate(((wg_hbm, stage_g), (wu_hbm, stage_u), (wd_hbm, stage_d)))]

    def to_bf16(stage, res):
        rows = 256

        def body(c, carry):
            r = pl.multiple_of(c * rows, rows)
            res[pl.ds(r, rows), :] = stage[pl.ds(r, rows), :].astype(BF16)
            return carry

        lax.fori_loop(0, stage.shape[0] // rows, body, 0)

    @pl.when(i < n_used)
    def _():
        e = be_ref[i]
        first = (i == 0) | (e != be_ref[jnp.maximum(i - 1, 0)])

        @pl.when(i == 0)
        def _():
            for cp in weight_copies(e):
                cp.start()

        @pl.when(first)
        def _():
            for cp in weight_copies(e):
                cp.wait()
            to_bf16(stage_g, res_g)
            to_bf16(stage_u, res_u)
            to_bf16(stage_d, res_d)
            nxt = nx_ref[i]

            @pl.when(nxt >= 0)
            def _():
                for cp in weight_copies(nxt):
                    cp.start()

        x = x_ref[...]
        g = jnp.dot(x, res_g[...], preferred_element_type=F32)
        u = jnp.dot(x, res_u[...], preferred_element_type=F32)
        a = (jax.nn.silu(g) * u).astype(BF16)
        y_ref[...] = jnp.dot(a, res_d[...], preferred_element_type=F32)

    @pl.when(i >= n_used)
    def _():
        y_ref[...] = jnp.zeros(y_ref.shape, F32)


def _experts(xp, blk_e, nxt_e, n_used, w_gate, w_up, w_down, layer):
    any_spec = pl.BlockSpec(memory_space=pl.ANY)
    return pl.pallas_call(
        functools.partial(_expert_kernel, layer=layer),
        grid_spec=pltpu.PrefetchScalarGridSpec(
            num_scalar_prefetch=3,
            grid=(MOE_NBLK,),
            in_specs=[pl.BlockSpec((MOE_BLK, D_MODEL), lambda i, be, nx, nu: (jnp.minimum(i, nu[0] - 1), 0)),
                      any_spec, any_spec, any_spec],
            out_specs=pl.BlockSpec((MOE_BLK, D_MODEL), lambda i, *_: (i, 0)),
            scratch_shapes=[pltpu.VMEM((D_MODEL, D_EXPERT), F32), pltpu.VMEM((D_MODEL, D_EXPERT), F32),
                            pltpu.VMEM((D_EXPERT, D_MODEL), F32),
                            pltpu.VMEM((D_MODEL, D_EXPERT), BF16), pltpu.VMEM((D_MODEL, D_EXPERT), BF16),
                            pltpu.VMEM((D_EXPERT, D_MODEL), BF16),
                            pltpu.SemaphoreType.DMA((3,))]),
        out_shape=jax.ShapeDtypeStruct((MOE_CAP, D_MODEL), F32),
        compiler_params=_params("arbitrary"),
        name="moe_experts",
    )(blk_e, nxt_e, n_used, xp, w_gate, w_up, w_down)


DISPATCH_ROWS = 512


def _dispatch_kernel(src_ref, h_hbm, xp_ref, buf, sem):
    i = pl.program_id(0)

    def gather_start(tile, slot):
        _start_row_gather(h_hbm, buf.at[slot], sem.at[slot], lambda r: src_ref[tile * DISPATCH_ROWS + r],
                          DISPATCH_ROWS)

    @pl.when(i == 0)
    def _():
        gather_start(0, 0)

    slot = i % 2
    _wait_row_gather(h_hbm, buf.at[slot], sem.at[slot], DISPATCH_ROWS)

    @pl.when(i + 1 < pl.num_programs(0))
    def _():
        gather_start(i + 1, 1 - slot)

    xp_ref[...] = buf[slot].astype(BF16)


def _dispatch(h, src):
    return pl.pallas_call(
        _dispatch_kernel,
        grid_spec=pltpu.PrefetchScalarGridSpec(
            num_scalar_prefetch=1,
            grid=(MOE_CAP // DISPATCH_ROWS,),
            in_specs=[pl.BlockSpec(memory_space=pl.ANY)],
            out_specs=pl.BlockSpec((DISPATCH_ROWS, D_MODEL), lambda i, s: (i, 0)),
            scratch_shapes=[pltpu.VMEM((2, DISPATCH_ROWS, D_MODEL), F32), pltpu.SemaphoreType.DMA((2,))]),
        out_shape=jax.ShapeDtypeStruct((MOE_CAP, D_MODEL), BF16),
        compiler_params=_params("arbitrary"),
        name="moe_dispatch",
    )(src, h)


def _combine_kernel(*refs, tm, final):
    dest_ref, yp_hbm, x_ref, rec_ref, mod_ref = refs[:5]
    refs = refs[5:]
    if final:
        fn_ref, yc_ref, yl_ref = refs[:3]
    else:
        xo_ref = refs[0]
    ybuf, sem = refs[-2:]
    i = pl.program_id(0)

    def gather_start(tile, slot):
        for k in range(TOP_K):
            _start_row_gather(yp_hbm, ybuf.at[slot, k], sem.at[slot, k],
                              lambda r, k=k: dest_ref[(tile * tm + r) * TOP_K + k], tm)

    @pl.when(i == 0)
    def _():
        gather_start(0, 0)

    slot = i % 2
    for k in range(TOP_K):
        _wait_row_gather(yp_hbm, ybuf.at[slot, k], sem.at[slot, k], tm)

    @pl.when(i + 1 < pl.num_programs(0))
    def _():
        gather_start(i + 1, 1 - slot)

    rec = rec_ref[...]
    y = ybuf[slot, 0] * rec[:, R_W0:R_W0 + 1] + ybuf[slot, 1] * rec[:, R_W1:R_W1 + 1]
    x = x_ref[...] + mod_ref[0][:, 5 * D_MODEL:6 * D_MODEL] * y
    if final:
        y = _rms(x) * fn_ref[...]

        @pl.when(i < N_CTX // tm)
        def _():
            yc_ref[...] = y

        @pl.when(i >= N_CTX // tm)
        def _():
            yl_ref[...] = y
    else:
        xo_ref[...] = x


def _combine(yp, dest, x, rec, mod, final_gain):
    tm = 256
    n_ct = N_CTX // tm
    final = final_gain is not None
    row = lambda i, d: (i, 0)
    in_specs = [pl.BlockSpec(memory_space=pl.ANY),
                pl.BlockSpec((tm, D_MODEL), row),
                pl.BlockSpec((tm, ROUTER_W), row),
                pl.BlockSpec((1, 1, 6 * D_MODEL), lambda i, d: (_cond_row(i, tm), 0, 0))]
    args = [dest, yp, x, rec, mod]
    if final:
        in_specs.append(pl.BlockSpec((1, D_MODEL), lambda i, d: (0, 0)))
        args.append(final_gain)
        out_specs = [pl.BlockSpec((tm, D_MODEL), lambda i, d: (jnp.minimum(i, n_ct - 1), 0)),
                     pl.BlockSpec((tm, D_MODEL), lambda i, d: (jnp.maximum(i - n_ct, 0), 0))]
        out_shape = [jax.ShapeDtypeStruct((N_CTX, D_MODEL), F32), jax.ShapeDtypeStruct((N_LAT, D_MODEL), F32)]
    else:
        out_specs = pl.BlockSpec((tm, D_MODEL), row)
        out_shape = jax.ShapeDtypeStruct((N_TOK, D_MODEL), F32)
    return pl.pallas_call(
        functools.partial(_combine_kernel, tm=tm, final=final),
        grid_spec=pltpu.PrefetchScalarGridSpec(
            num_scalar_prefetch=1,
            grid=(N_TOK // tm,),
            in_specs=in_specs,
            out_specs=out_specs,
            scratch_shapes=[pltpu.VMEM((2, TOP_K, tm, D_MODEL), F32), pltpu.SemaphoreType.DMA((2, TOP_K))]),
        out_shape=out_shape,
        compiler_params=_params("arbitrary"),
        name="moe_combine",
    )(*args)


def _swap_halves(w, chunk):
    k, n = w.shape
    w = w.reshape(k, n // chunk, 2, chunk // 2)
    return w[:, :, ::-1, :].reshape(k, n)


def _rope_tables(rot_dim):
    rows = DEC_SEQ // GRID_W
    row = jnp.repeat(jnp.arange(rows), GRID_W).astype(F32)
    col = (jnp.arange(rows * GRID_W) % GRID_W).astype(F32)
    axis_dim = rot_dim // 2
    inv = ROPE_THETA ** (-jnp.arange(0, axis_dim, 2, dtype=F32) / axis_dim)
    ang = jnp.concatenate([row[:, None] * inv, col[:, None] * inv], axis=-1)
    c, s = jnp.cos(ang), jnp.sin(ang)
    return jnp.concatenate([c, c], axis=-1), jnp.concatenate([-s, s], axis=-1)


def _head_major(cache):
    b, l, kvh, d = cache.shape
    return jnp.transpose(cache, (2, 0, 1, 3)).reshape(kvh, b * l, d).astype(BF16)


def kernel(x_prompt, x_sample, c, cache_mla_ckv, cache_mla_krope, cache_gqa_k, cache_gqa_v, cache_swa_k,
           cache_swa_v, state_ret, c_ctx, ada_w, ada_b, norm_mix, norm_ffn, w_in, mla_q_norm, mla_kv_norm,
           mla_w_uq, mla_w_ukv, gqa_q_norm, gqa_k_norm, ret_decay_logit, ret_gn, swa_sink, w_branch, w_out,
           router_group_w, router_group_b, router_expert_w, router_expert_b, moe_w_gate, moe_w_up, moe_w_down,
           final_norm):
    xs = (x_prompt.reshape(N_CTX, D_MODEL), x_sample.reshape(N_LAT, D_MODEL))
    cond = jnp.concatenate([c_ctx[None, :], c, jnp.zeros((N_COND - 1 - DEC_BATCH, D_MODEL), F32)], axis=0)
    mod_all = _ada(cond, ada_w, ada_b).reshape(DEPTH, N_COND, 1, 6 * D_MODEL)

    w_in_r = _w_in_layout(w_in)
    tabs = _rope_tables(HEAD_DIM) + _rope_tables(MLA_ROPE)
    log_gamma = jax.nn.log_sigmoid(ret_decay_logit.astype(F32))
    uq = mla_w_uq.reshape(DEPTH, MLA_Q_LORA, MLA_H, MLA_DK)
    ukv = mla_w_ukv.reshape(DEPTH, MLA_KV_LORA, MLA_H, MLA_NOPE + MLA_V)
    w_router = jnp.concatenate(
        [router_group_w, router_expert_w,
         jnp.zeros((DEPTH, D_MODEL, ROUTER_W - N_GROUPS - N_EXPERTS), F32)], axis=-1).astype(BF16)
    b_router = jnp.concatenate(
        [router_group_b, router_expert_b, jnp.zeros((DEPTH, ROUTER_W - N_GROUPS - N_EXPERTS), F32)], axis=-1)
    zero_state = jnp.zeros((BATCH, RET_H, RET_DK, RET_DV), F32)

    caches = [[] for _ in range(7)]
    for l in range(DEPTH):
        mod = mod_all[l]
        uq_rope = uq[l, :, :, MLA_NOPE:].reshape(MLA_Q_LORA, MLA_H * MLA_ROPE)
        lw = dict(
            q_norm=mla_q_norm[l].reshape(1, -1), kv_norm=mla_kv_norm[l].reshape(1, -1),
            gq_norm=gqa_q_norm[l].reshape(1, -1), gk_norm=gqa_k_norm[l].reshape(1, -1),
            w_uq_nope=uq[l, :, :, :MLA_NOPE].reshape(MLA_Q_LORA, MLA_H * MLA_NOPE).astype(BF16),
            w_uq_rope=uq_rope.astype(BF16), w_uq_swap=_swap_halves(uq_rope, MLA_ROPE).astype(BF16),
            w_uk=ukv[l, :, :, :MLA_NOPE].reshape(MLA_KV_LORA, MLA_H * MLA_NOPE).astype(BF16),
            w_uv=ukv[l, :, :, MLA_NOPE:].reshape(MLA_KV_LORA, MLA_H * MLA_V).astype(BF16))

        z = _in_proj(xs, mod, norm_mix[l].reshape(1, -1), w_in_r, l)

        ctx_ops = _prep(z, lw, None, latent=False)
        ckv, kr, kg, vg, ks, vs = ctx_ops[9:]
        oa_c, ob_c, od_c = _ctx_attention(ctx_ops[:9], swa_sink[l])
        gn = ret_gn[l]
        of_c, s_f = _retention(z, log_gamma[l, 0], gn[0:1], zero_state, None, latent=False, reverse=False)
        oc_c, s_b = _retention(z, log_gamma[l, 1], gn[1:2], zero_state, of_c, latent=False, reverse=True)

        (qa, ka, va, qb, kb, vb, qd, kd, vd) = _prep(z, lw, tabs, latent=True)
        ka0, va0 = _mla_ctx(cache_mla_ckv[:, l].reshape(DEC_BATCH * PAST_LEN, MLA_KV_LORA),
                            cache_mla_krope[:, l].reshape(DEC_BATCH * PAST_LEN, MLA_ROPE), lw)
        oa_l = _flash_latent(qa, ka, va, (ka0, va0))
        ob_l = _flash_latent(qb, kb, vb, (_head_major(cache_gqa_k[:, l]), _head_major(cache_gqa_v[:, l])))
        od_l = _window(qd, kd, vd, _head_major(cache_swa_k[:, l]), _head_major(cache_swa_v[:, l]), swa_sink[l])
        of_l, _ = _retention(z, log_gamma[l, 0], gn[0:1], state_ret[:, l, 0], None, latent=True, reverse=False)
        oc_l, _ = _retention(z, log_gamma[l, 1], gn[1:2], state_ret[:, l, 1], of_l, latent=True, reverse=True)

        x_mid, h_ffn, logits = _merge(xs, (oa_c, ob_c, oc_c, od_c), (oa_l, ob_l, oc_l, od_l), z,
                                      w_branch[l].astype(BF16), w_out[l].astype(BF16), mod,
                                      norm_ffn[l].reshape(1, -1), w_router[l], b_router[l].reshape(1, -1))

        rec, dest, src, blk_e, nxt_e, n_used = _route(logits)
        yp = _experts(_dispatch(h_ffn, src), blk_e, nxt_e, n_used, moe_w_gate, moe_w_up, moe_w_down, l)
        if l < DEPTH - 1:
            xs = (_combine(yp, dest, x_mid, rec, mod, None),)
        else:
            y_ctx, y_lat = _combine(yp, dest, x_mid, rec, mod, final_norm.reshape(1, D_MODEL))

        kvh_shape = (BATCH, SEQ, GQA_KVH, HEAD_DIM)
        for lst, val in zip(caches, (ckv.reshape(BATCH, SEQ, MLA_KV_LORA), kr.reshape(BATCH, SEQ, MLA_ROPE),
                                     kg.reshape(kvh_shape), vg.reshape(kvh_shape), ks.reshape(kvh_shape),
                                     vs.reshape(kvh_shape), jnp.stack([s_f, s_b], axis=1))):
            lst.append(val)

    y_prompt = y_ctx.reshape(BATCH, SEQ, D_MODEL)
    y_sample = y_lat.reshape(DEC_BATCH, DEC_SEQ, D_MODEL)
    return (y_prompt, y_sample) + tuple(jnp.stack(lst, axis=1) for lst in caches)
```

```python
import functools

import jax
import jax.numpy as jnp
from jax import lax
from jax.experimental import pallas as pl
from jax.experimental.pallas import tpu as pltpu

F32 = jnp.float32
BF16 = jnp.bfloat16
LANES = 128

D_MODEL = 2048
BATCH = 16
SEQ = 256
DEPTH = 4
DEC_BATCH = 2
DEC_SEQ = 4096
PAST_LEN = 256
GRID_W = 64
ROPE_THETA = 10000.0
EPS = 1e-6
QBLK = 128
NEG_INF = -1e30

MLA_H = 4
MLA_Q_LORA = 512
MLA_KV_LORA = 256
MLA_NOPE = 128
MLA_ROPE = 64
MLA_V = 128
MLA_SCALE = (MLA_NOPE + MLA_ROPE) ** -0.5
MLA_DK = MLA_NOPE + MLA_ROPE

HEAD_DIM = 128
ATTN_SCALE = HEAD_DIM ** -0.5
GQA_H = 4
GQA_KVH = 2
SWA_H = 4
SWA_KVH = 2
WINDOW = 128

RET_H = 4
RET_DK = 64
RET_DV = 128
RET_CHUNK = 128
RET_K_SCALE = RET_DK ** -0.5

N_BRANCH = 4
BRANCH_W = 512

N_GROUPS = 4
EXPERTS_PER_GROUP = 8
N_EXPERTS = N_GROUPS * EXPERTS_PER_GROUP
TOP_K = 2
D_EXPERT = 1024

N_CTX = BATCH * SEQ
N_LAT = DEC_BATCH * DEC_SEQ
N_TOK = N_CTX + N_LAT
N_COND = 8

W_IN_HEAD = MLA_Q_LORA + MLA_KV_LORA + MLA_ROPE
IN_WIDTH = (W_IN_HEAD + (GQA_H + 2 * GQA_KVH) * HEAD_DIM + 2 * RET_H * RET_DK + 3 * RET_H * RET_DV
            + (SWA_H + 2 * SWA_KVH) * HEAD_DIM + N_BRANCH * D_MODEL)
Z_TAIL = 1024
Z_WIDTH = Z_TAIL + IN_WIDTH - W_IN_HEAD
Z_OFF = dict(mq=0, mkv=512, mkr=768, gq=1024, gk=1536, gv=1792, rq=2048, rk=2304, rv=2560, rgf=3072, rgb=3584,
             sq=4096, sk=4608, sv=4864, gate=5120)
Z_TILE = 1024

MOE_BLK = 256
MOE_ASSIGN = N_TOK * TOP_K
MOE_NBLK = (MOE_ASSIGN + N_EXPERTS * (MOE_BLK - 1) + MOE_BLK - 1) // MOE_BLK
MOE_CAP = MOE_NBLK * MOE_BLK
ROUTER_W = 128
R_W0, R_W1, R_E0, R_E1, R_RANK0, R_RANK1 = range(6)

VMEM_LIMIT = 56 * 1024 * 1024


def _params(*sem):
    return pltpu.CompilerParams(dimension_semantics=sem, vmem_limit_bytes=VMEM_LIMIT)


def _cond_row(i, tm):
    start = i * tm
    return jnp.where(start < N_CTX, 0, 1 + (start - N_CTX) // DEC_SEQ)


def _rms(x):
    return x * lax.rsqrt(jnp.mean(x * x, axis=-1, keepdims=True) + EPS)


def _token_specs(xs, tm, index_of):
    if len(xs) == 1:
        return [pl.BlockSpec((tm, D_MODEL), lambda *g: (index_of(*g), 0))]
    n_ct = N_CTX // tm
    return [pl.BlockSpec((tm, D_MODEL), lambda *g: (jnp.minimum(index_of(*g), n_ct - 1), 0),
                         pipeline_mode=pl.Buffered(1)),
            pl.BlockSpec((tm, D_MODEL), lambda *g: (jnp.maximum(index_of(*g) - n_ct, 0), 0),
                         pipeline_mode=pl.Buffered(1))]


def _token_tile(x_refs, i, tm):
    if len(x_refs) == 1:
        return x_refs[0][...]
    return jnp.where(i < N_CTX // tm, x_refs[0][...], x_refs[1][...])


def _ada_kernel(c_ref, w_ref, b_ref, o_ref):
    s = jax.nn.silu(c_ref[...]).astype(BF16)
    o_ref[0] = jnp.dot(s, w_ref[0].astype(BF16), preferred_element_type=F32) + b_ref[0]


def _ada(cond, ada_w, ada_b):
    tn = 1024
    n_out = 6 * D_MODEL
    return pl.pallas_call(
        _ada_kernel,
        grid=(DEPTH, n_out // tn),
        in_specs=[pl.BlockSpec((N_COND, D_MODEL), lambda l, j: (0, 0)),
                  pl.BlockSpec((1, D_MODEL, tn), lambda l, j: (l, 0, j)),
                  pl.BlockSpec((1, 1, tn), lambda l, j: (l, 0, j))],
        out_specs=pl.BlockSpec((1, N_COND, tn), lambda l, j: (l, 0, j)),
        out_shape=jax.ShapeDtypeStruct((DEPTH, N_COND, n_out), F32),
        compiler_params=_params("parallel", "parallel"),
        name="ada_mod",
    )(cond, ada_w, ada_b.reshape(DEPTH, 1, n_out))


def _w_in_kernel(w_ref, o_ref):
    w = w_ref[0]

    @pl.when(pl.program_id(1) == 0)
    def _():
        half = MLA_ROPE // 2
        mkr0 = W_IN_HEAD - MLA_ROPE
        o_ref[0, 0:W_IN_HEAD] = w[0:W_IN_HEAD].astype(BF16)
        o_ref[0, W_IN_HEAD:W_IN_HEAD + half] = w[mkr0 + half:W_IN_HEAD].astype(BF16)
        o_ref[0, W_IN_HEAD + half:W_IN_HEAD + MLA_ROPE] = w[mkr0:mkr0 + half].astype(BF16)
        o_ref[0, W_IN_HEAD + MLA_ROPE:Z_TAIL] = jnp.zeros((Z_TAIL - W_IN_HEAD - MLA_ROPE, w.shape[1]), BF16)

    @pl.when(pl.program_id(1) > 0)
    def _():
        o_ref[0] = w.astype(BF16)


def _w_in_layout(w_in):
    w_t = jnp.swapaxes(w_in, 1, 2)

    def src_row(j):
        return pl.multiple_of(jnp.where(j > 0, (j - 1) * Z_TILE + W_IN_HEAD, 0), 16)

    return pl.pallas_call(
        _w_in_kernel,
        grid=(DEPTH, Z_WIDTH // Z_TILE),
        in_specs=[pl.BlockSpec((pl.Element(1), pl.Element(Z_TILE), pl.Element(D_MODEL)),
                               lambda l, j: (l, src_row(j), 0))],
        out_specs=pl.BlockSpec((1, Z_TILE, D_MODEL), lambda l, j: (l, j, 0)),
        out_shape=jax.ShapeDtypeStruct((DEPTH, Z_WIDTH, D_MODEL), BF16),
        compiler_params=_params("parallel", "parallel"),
        name="w_in_layout",
    )(w_t)


def _in_kernel(*refs, n_x, tm):
    x_refs, (mod_ref, g_ref, w_ref, z_ref, h_ref) = refs[:n_x], refs[n_x:]

    @pl.when(pl.program_id(1) == 0)
    def _():
        m = mod_ref[0]
        y = _rms(_token_tile(x_refs, pl.program_id(0), tm)) * g_ref[...]
        h_ref[...] = (y * (1.0 + m[:, D_MODEL:2 * D_MODEL]) + m[:, 0:D_MODEL]).astype(BF16)

    z_ref[...] = lax.dot_general(h_ref[...], w_ref[...], (((1,), (1,)), ((), ())), preferred_element_type=F32)


def _in_proj(xs, mod, gain, w_all, layer):
    tm, tn = 1024, Z_TILE
    return pl.pallas_call(
        functools.partial(_in_kernel, n_x=len(xs), tm=tm),
        grid=(N_TOK // tm, Z_WIDTH // tn),
        in_specs=_token_specs(xs, tm, lambda i, j: i)
        + [pl.BlockSpec((1, 1, 6 * D_MODEL), lambda i, j: (_cond_row(i, tm), 0, 0)),
           pl.BlockSpec((1, D_MODEL), lambda i, j: (0, 0)),
           pl.BlockSpec((None, tn, D_MODEL), lambda i, j: (layer, j, 0))],
        out_specs=pl.BlockSpec((tm, tn), lambda i, j: (i, j)),
        out_shape=jax.ShapeDtypeStruct((N_TOK, Z_WIDTH), F32),
        scratch_shapes=[pltpu.VMEM((tm, D_MODEL), BF16)],
        compiler_params=_params("parallel", "arbitrary"),
        name="in_proj",
    )(*xs, mod, gain, w_all)


def _rope128(x, cos, sin):
    return x * cos + pltpu.roll(x, HEAD_DIM // 2, 1) * sin


def _mla_kv(ckv, kr, wuk_ref, wuv_ref, ka_ref, va_ref):
    cb = ckv.astype(BF16)
    kn = jnp.dot(cb, wuk_ref[...], preferred_element_type=F32)
    vv = jnp.dot(cb, wuv_ref[...], preferred_element_type=F32)
    for h in range(MLA_H):
        ka_ref[h, :, 0:MLA_NOPE] = kn[:, h * MLA_NOPE:(h + 1) * MLA_NOPE].astype(BF16)
        ka_ref[h, :, MLA_NOPE:MLA_DK] = kr.astype(BF16)
        va_ref[h] = vv[:, h * MLA_V:(h + 1) * MLA_V].astype(BF16)


def _prep_kernel(*refs, latent):
    (mq_ref, gq_ref, sq_ref, mkv_ref, gk_ref, gv_ref, sk_ref, sv_ref, mkr_ref,
     qn_ref, kvn_ref, gqn_ref, gkn_ref, wqn_ref, wqr_ref, wqs_ref, wuk_ref, wuv_ref) = refs[:18]
    refs = refs[18:]
    if latent:
        cos_ref, sin_ref, c64_ref, s64_ref = refs[:4]
        refs = refs[4:]
        cos, sin = cos_ref[...], sin_ref[...]
        c64, s64 = c64_ref[...], s64_ref[...]
    qa_ref, ka_ref, va_ref, qb_ref, kb_ref, vb_ref, qd_ref, kd_ref, vd_ref = refs[:9]
    refs = refs[9:]

    hn = (_rms(mq_ref[...]) * qn_ref[...]).astype(BF16)
    q_nope = jnp.dot(hn, wqn_ref[...], preferred_element_type=F32)
    q_rope = jnp.dot(hn, wqr_ref[...], preferred_element_type=F32)
    if latent:
        q_swap = jnp.dot(hn, wqs_ref[...], preferred_element_type=F32)
    for h in range(MLA_H):
        qa_ref[h, :, 0:MLA_NOPE] = (q_nope[:, h * MLA_NOPE:(h + 1) * MLA_NOPE] * MLA_SCALE).astype(BF16)
        qr = q_rope[:, h * MLA_ROPE:(h + 1) * MLA_ROPE]
        if latent:
            qr = qr * c64 + q_swap[:, h * MLA_ROPE:(h + 1) * MLA_ROPE] * s64
        qa_ref[h, :, MLA_NOPE:MLA_DK] = (qr * MLA_SCALE).astype(BF16)
    ckv = _rms(mkv_ref[...]) * kvn_ref[...]
    kr = mkr_ref[:, 0:MLA_ROPE]
    if latent:
        kr = kr * c64 + mkr_ref[:, MLA_ROPE:2 * MLA_ROPE] * s64
    _mla_kv(ckv, kr, wuk_ref, wuv_ref, ka_ref, va_ref)

    gq = gq_ref[...]
    for h in range(GQA_H):
        q = _rms(gq[:, h * HEAD_DIM:(h + 1) * HEAD_DIM]) * gqn_ref[...]
        if latent:
            q = _rope128(q, cos, sin)
        qb_ref[h] = (q * ATTN_SCALE).astype(BF16)
    gk = gk_ref[...]
    gv = gv_ref[...]
    kg = []
    for h in range(GQA_KVH):
        k = _rms(gk[:, h * HEAD_DIM:(h + 1) * HEAD_DIM]) * gkn_ref[...]
        kg.append(k)
        if latent:
            k = _rope128(k, cos, sin)
        kb_ref[h] = k.astype(BF16)
        vb_ref[h] = gv[:, h * HEAD_DIM:(h + 1) * HEAD_DIM].astype(BF16)

    sq = sq_ref[...]
    for h in range(SWA_H):
        q = sq[:, h * HEAD_DIM:(h + 1) * HEAD_DIM]
        if latent:
            q = _rope128(q, cos, sin)
        qd_ref[h] = (q * ATTN_SCALE).astype(BF16)
    sk = sk_ref[...]
    sv = sv_ref[...]
    for h in range(SWA_KVH):
        k = sk[:, h * HEAD_DIM:(h + 1) * HEAD_DIM]
        if latent:
            k = _rope128(k, cos, sin)
        kd_ref[h] = k.astype(BF16)
        vd_ref[h] = sv[:, h * HEAD_DIM:(h + 1) * HEAD_DIM].astype(BF16)

    if not latent:
        ckv_ref, kr_ref, kg_ref, vg_ref, ks_ref, vs_ref = refs
        ckv_ref[...] = ckv
        kr_ref[...] = kr
        for h in range(GQA_KVH):
            kg_ref[:, h * HEAD_DIM:(h + 1) * HEAD_DIM] = kg[h]
        vg_ref[...] = gv
        ks_ref[...] = sk
        vs_ref[...] = sv


def _prep(z, lw, tabs, latent):
    tm = 512
    rows = N_LAT if latent else N_CTX
    roff = (N_CTX if latent else 0) // tm
    n_t = DEC_SEQ // tm

    def zc(name, width):
        cb = Z_OFF[name] // width
        return pl.BlockSpec((tm, width), lambda i: (i + roff, cb))

    def full(a):
        nd = a.ndim
        return pl.BlockSpec(a.shape, lambda i: (0,) * nd)

    weights = [lw["q_norm"], lw["kv_norm"], lw["gq_norm"], lw["gk_norm"],
               lw["w_uq_nope"], lw["w_uq_rope"], lw["w_uq_swap"], lw["w_uk"], lw["w_uv"]]
    in_specs = [zc("mq", 512), zc("gq", 512), zc("sq", 512), zc("mkv", 256), zc("gk", 256), zc("gv", 256),
                zc("sk", 256), zc("sv", 256), zc("mkr", 128)] + [full(a) for a in weights]
    args = [z] * 9 + weights
    if latent:
        in_specs += [pl.BlockSpec((tm, HEAD_DIM), lambda i: (i % n_t, 0))] * 2
        in_specs += [pl.BlockSpec((tm, MLA_ROPE), lambda i: (i % n_t, 0))] * 2
        args += list(tabs)

    def heads(n, d):
        return (pl.BlockSpec((n, tm, d), lambda i: (0, i, 0)), jax.ShapeDtypeStruct((n, rows, d), BF16))

    outs = [heads(MLA_H, MLA_DK), heads(MLA_H, MLA_DK), heads(MLA_H, MLA_V),
            heads(GQA_H, HEAD_DIM), heads(GQA_KVH, HEAD_DIM), heads(GQA_KVH, HEAD_DIM),
            heads(SWA_H, HEAD_DIM), heads(SWA_KVH, HEAD_DIM), heads(SWA_KVH, HEAD_DIM)]
    if not latent:
        kv_w = GQA_KVH * HEAD_DIM
        for d in (MLA_KV_LORA, MLA_ROPE, kv_w, kv_w, kv_w, kv_w):
            outs.append((pl.BlockSpec((tm, d), lambda i: (i, 0)), jax.ShapeDtypeStruct((rows, d), F32)))
    return pl.pallas_call(
        functools.partial(_prep_kernel, latent=latent),
        grid=(rows // tm,),
        in_specs=in_specs,
        out_specs=[o[0] for o in outs],
        out_shape=[o[1] for o in outs],
        compiler_params=_params("parallel"),
        name="prep_latent" if latent else "prep_context",
    )(*args)


def _mla_ctx_kernel(ckv_ref, kr_ref, wuk_ref, wuv_ref, ka_ref, va_ref):
    _mla_kv(ckv_ref[...], kr_ref[...], wuk_ref, wuv_ref, ka_ref, va_ref)


def _mla_ctx(ckv, kr, lw):
    rows = ckv.shape[0]
    return pl.pallas_call(
        _mla_ctx_kernel,
        out_shape=[jax.ShapeDtypeStruct((MLA_H, rows, MLA_DK), BF16),
                   jax.ShapeDtypeStruct((MLA_H, rows, MLA_V), BF16)],
        name="mla_ctx_kv",
    )(ckv, kr, lw["w_uk"], lw["w_uv"])


def _softmax_pv(q, k, v, sink):
    s = lax.dot_general(q, k, (((1,), (1,)), ((), ())), preferred_element_type=F32)
    m = jnp.max(s, axis=-1, keepdims=True)
    if sink is not None:
        m = jnp.maximum(m, sink)
    p = jnp.exp(s - m)
    l = jnp.sum(p, axis=-1, keepdims=True)
    if sink is not None:
        l = l + jnp.exp(sink - m)
    return jnp.dot(p.astype(BF16), v, preferred_element_type=F32) / l


def _ctx_attn_kernel(sink_ref, qa, ka, va, qb, kb, vb, qd, kd, vd, oa, ob, od):
    for h in range(MLA_H):
        oa[:, h * MLA_V:(h + 1) * MLA_V] = _softmax_pv(qa[h], ka[h], va[h], None).astype(BF16)
    for h in range(GQA_H):
        kv = h // (GQA_H // GQA_KVH)
        ob[:, h * HEAD_DIM:(h + 1) * HEAD_DIM] = _softmax_pv(qb[h], kb[kv], vb[kv], None).astype(BF16)
    for h in range(SWA_H):
        kv = h // (SWA_H // SWA_KVH)
        od[:, h * HEAD_DIM:(h + 1) * HEAD_DIM] = _softmax_pv(qd[h], kd[kv], vd[kv], sink_ref[h]).astype(BF16)


def _ctx_attention(operands, sink):
    def spec(a):
        return pl.BlockSpec((a.shape[0], SEQ, a.shape[2]), lambda b: (0, b, 0))

    out = jax.ShapeDtypeStruct((N_CTX, BRANCH_W), BF16)
    return pl.pallas_call(
        _ctx_attn_kernel,
        grid=(BATCH,),
        in_specs=[pl.BlockSpec(memory_space=pltpu.SMEM)] + [spec(a) for a in operands],
        out_specs=[pl.BlockSpec((SEQ, BRANCH_W), lambda b: (b, 0))] * 3,
        out_shape=[out, out, out],
        compiler_params=_params("parallel"),
        name="context_attention",
    )(sink, *operands)


KEY_CHUNK = 256


def _flash_latent_kernel(q_ref, k_ref, v_ref, k0_ref, v0_ref, o_ref, m_sc, l_sc, acc_sc, *, groups):
    dv = v_ref.shape[-1]
    n_chunks = k_ref.shape[1] // KEY_CHUNK
    dn = (((1,), (1,)), ((), ()))

    def scores(g, k):
        return lax.dot_general(q_ref[g], k, dn, preferred_element_type=F32)

    def lane_max(s):
        m = s[:, 0:LANES]
        for c in range(1, s.shape[1] // LANES):
            m = jnp.maximum(m, s[:, c * LANES:(c + 1) * LANES])
        return m

    for g in range(groups):
        m_sc[g] = lane_max(scores(g, k0_ref[0]))

    def max_body(j, carry):
        off = pl.multiple_of(j * KEY_CHUNK, KEY_CHUNK)
        for g in range(groups):
            m_sc[g] = jnp.maximum(m_sc[g], lane_max(scores(g, k_ref[0, pl.ds(off, KEY_CHUNK), :])))
        return carry

    lax.fori_loop(0, n_chunks, max_body, 0, unroll=True)
    for g in range(groups):
        m_sc[g] = jnp.broadcast_to(jnp.max(m_sc[g], axis=-1, keepdims=True), m_sc.shape[1:])

    def accumulate(g, k, v, first):
        s = scores(g, k)
        m = m_sc[g]
        ps = [jnp.exp(s[:, c * LANES:(c + 1) * LANES] - m) for c in range(s.shape[1] // LANES)]
        lsum = ps[0]
        for p in ps[1:]:
            lsum = lsum + p
        pv = jnp.dot(jnp.concatenate(ps, axis=1).astype(BF16), v, preferred_element_type=F32)
        if first:
            l_sc[g] = lsum
            acc_sc[g] = pv
        else:
            l_sc[g] += lsum
            acc_sc[g] += pv

    for g in range(groups):
        accumulate(g, k0_ref[0], v0_ref[0], True)

    def sum_body(j, carry):
        off = pl.multiple_of(j * KEY_CHUNK, KEY_CHUNK)
        for g in range(groups):
            accumulate(g, k_ref[0, pl.ds(off, KEY_CHUNK), :], v_ref[0, pl.ds(off, KEY_CHUNK), :], False)
        return carry

    lax.fori_loop(0, n_chunks, sum_body, 0, unroll=True)
    for g in range(groups):
        l = jnp.sum(l_sc[g], axis=-1, keepdims=True)
        o_ref[:, g * dv:(g + 1) * dv] = (acc_sc[g] / l).astype(o_ref.dtype)


def _flash_latent(q, k, v, ctx):
    tq = 1024
    n_h, _, dk = q.shape
    kvh, _, dv = v.shape
    groups = n_h // kvh
    nq = DEC_SEQ // tq
    return pl.pallas_call(
        functools.partial(_flash_latent_kernel, groups=groups),
        grid=(DEC_BATCH, kvh, nq),
        in_specs=[pl.BlockSpec((groups, tq, dk), lambda b, h, i: (h, b * nq + i, 0)),
                  pl.BlockSpec((1, DEC_SEQ, dk), lambda b, h, i: (h, b, 0)),
                  pl.BlockSpec((1, DEC_SEQ, dv), lambda b, h, i: (h, b, 0)),
                  pl.BlockSpec((1, PAST_LEN, dk), lambda b, h, i: (h, b, 0)),
                  pl.BlockSpec((1, PAST_LEN, dv), lambda b, h, i: (h, b, 0))],
        out_specs=pl.BlockSpec((tq, groups * dv), lambda b, h, i: (b * nq + i, h)),
        out_shape=jax.ShapeDtypeStruct((N_LAT, n_h * dv), BF16),
        scratch_shapes=[pltpu.VMEM((groups, tq, LANES), F32), pltpu.VMEM((groups, tq, LANES), F32),
                        pltpu.VMEM((groups, tq, dv), F32)],
        compiler_params=_params("parallel", "parallel", "parallel"),
        name="flash_latent",
    )(q, k, v, *ctx)


WIN_TQ = 512
WIN_SPAN = WIN_TQ + 2 * WINDOW


def _lanewise(op, x):
    r = x[:, 0:LANES]
    for c in range(1, x.shape[1] // LANES):
        r = op(r, x[:, c * LANES:(c + 1) * LANES])
    return r


def _exp_tiles(s, m):
    return jnp.concatenate([jnp.exp(s[:, c * LANES:(c + 1) * LANES] - m) for c in range(s.shape[1] // LANES)],
                           axis=1)


def _window_kernel(sink_ref, q_ref, k_ref, v_ref, k0_ref, v0_ref, o_ref, *, groups):
    head0 = pl.program_id(1) * groups
    t = k_ref.shape[1]
    q0 = pl.program_id(2) * WIN_TQ
    k_start = pl.multiple_of(jnp.clip(q0 - WINDOW, 0, t - WIN_SPAN), LANES)
    qpos = q0 + lax.broadcasted_iota(jnp.int32, (WIN_TQ, WIN_SPAN), 0)
    kpos = k_start + lax.broadcasted_iota(jnp.int32, (WIN_TQ, WIN_SPAN), 1)
    band = jnp.abs(kpos - qpos) <= WINDOW
    kw = k_ref[0, pl.ds(k_start, WIN_SPAN), :]
    vw = v_ref[0, pl.ds(k_start, WIN_SPAN), :]
    dn = (((1,), (1,)), ((), ()))
    for g in range(groups):
        q = q_ref[g]
        s_loc = jnp.where(band, lax.dot_general(q, kw, dn, preferred_element_type=F32), NEG_INF)
        s_ctx = lax.dot_general(q, k0_ref[0], dn, preferred_element_type=F32)
        snk = sink_ref[head0 + g]
        m = jnp.max(jnp.maximum(_lanewise(jnp.maximum, s_loc), _lanewise(jnp.maximum, s_ctx)),
                    axis=-1, keepdims=True)
        m = jnp.broadcast_to(jnp.maximum(m, snk), (WIN_TQ, LANES))
        p_loc = _exp_tiles(s_loc, m)
        p_ctx = _exp_tiles(s_ctx, m)
        den = (jnp.sum(_lanewise(jnp.add, p_loc) + _lanewise(jnp.add, p_ctx), axis=-1, keepdims=True)
               + jnp.exp(snk - m[:, 0:1]))
        acc = (jnp.dot(p_loc.astype(BF16), vw, preferred_element_type=F32)
               + jnp.dot(p_ctx.astype(BF16), v0_ref[0], preferred_element_type=F32))
        o_ref[:, g * HEAD_DIM:(g + 1) * HEAD_DIM] = (acc / den).astype(o_ref.dtype)


def _window(q, k, v, k0, v0, sink):
    n_h, kvh = q.shape[0], k.shape[0]
    groups = n_h // kvh
    nq = DEC_SEQ // WIN_TQ
    full = pl.BlockSpec((1, DEC_SEQ, HEAD_DIM), lambda b, h, n: (h, b, 0))
    ctx_spec = pl.BlockSpec((1, PAST_LEN, HEAD_DIM), lambda b, h, n: (h, b, 0))
    return pl.pallas_call(
        functools.partial(_window_kernel, groups=groups),
        grid=(DEC_BATCH, kvh, nq),
        in_specs=[pl.BlockSpec(memory_space=pltpu.SMEM),
                  pl.BlockSpec((groups, WIN_TQ, HEAD_DIM), lambda b, h, n: (h, b * nq + n, 0)),
                  full, full, ctx_spec, ctx_spec],
        out_specs=pl.BlockSpec((WIN_TQ, groups * HEAD_DIM), lambda b, h, n: (b * nq + n, h)),
        out_shape=jax.ShapeDtypeStruct((N_LAT, n_h * HEAD_DIM), BF16),
        compiler_params=_params("parallel", "parallel", "parallel"),
        name="window_attention",
    )(sink, q, k, v, k0, v0)


def _ret_kernel(*refs, reverse, has_prev):
    lg_ref, q_ref, k_ref, v_ref, gate_ref, gn_ref, s0_ref = refs[:7]
    refs = refs[7:]
    if has_prev:
        prev_ref, refs = refs[0], refs[1:]
    o_ref, sfin_ref, s_sc = refs
    c = pl.program_id(1)

    @pl.when(c == 0)
    def _():
        s_sc[...] = s0_ref[0]

    cs = RET_CHUNK
    row = lax.broadcasted_iota(jnp.int32, (cs, cs), 0).astype(F32)
    col = lax.broadcasted_iota(jnp.int32, (cs, cs), 1).astype(F32)
    pos = lax.broadcasted_iota(jnp.int32, (cs, 1), 0).astype(F32)
    diff = (col - row) if reverse else (row - col)
    scan_pos = (cs - 1.0 - pos) if reverse else pos
    q_all, k_all, v_all, gate_all = q_ref[...], k_ref[...], v_ref[...], gate_ref[...]
    for h in range(RET_H):
        lg = lg_ref[h]
        dmask = jnp.where(diff >= 0, jnp.exp(jnp.maximum(diff, 0.0) * lg), 0.0)
        q_dec = jnp.exp((scan_pos + 1.0) * lg)
        k_dec = jnp.exp((cs - 1.0 - scan_pos) * lg)
        c_dec = jnp.exp(cs * lg)
        q = q_all[:, h * RET_DK:(h + 1) * RET_DK]
        k = k_all[:, h * RET_DK:(h + 1) * RET_DK] * RET_K_SCALE
        v = v_all[:, h * RET_DV:(h + 1) * RET_DV].astype(BF16)
        s = s_sc[h]
        a = lax.dot_general(q.astype(BF16), k.astype(BF16), (((1,), (1,)), ((), ())),
                            preferred_element_type=F32) * dmask
        o = (jnp.dot(a.astype(BF16), v, preferred_element_type=F32)
             + jnp.dot((q * q_dec).astype(BF16), s.astype(BF16), preferred_element_type=F32))
        kd_t = jnp.transpose(k * k_dec).astype(BF16)
        s_sc[h] = s * c_dec + jnp.dot(kd_t, v, preferred_element_type=F32)
        mu = jnp.mean(o, axis=-1, keepdims=True)
        var = jnp.mean(jnp.square(o - mu), axis=-1, keepdims=True)
        y = (o - mu) * lax.rsqrt(var + EPS) * gn_ref[:, h * RET_DV:(h + 1) * RET_DV]
        y = jax.nn.silu(gate_all[:, h * RET_DV:(h + 1) * RET_DV]) * y
        if has_prev:
            y = prev_ref[:, h * RET_DV:(h + 1) * RET_DV] + y
        o_ref[:, h * RET_DV:(h + 1) * RET_DV] = y.astype(o_ref.dtype)

    @pl.when(c == pl.num_programs(1) - 1)
    def _():
        sfin_ref[0] = s_sc[...]


def _retention(z, log_gamma, gn, s0, prev, *, latent, reverse):
    batch, t = (DEC_BATCH, DEC_SEQ) if latent else (BATCH, SEQ)
    nc = t // RET_CHUNK
    roff = (N_CTX if latent else 0) // RET_CHUNK
    gate_name = "rgb" if reverse else "rgf"

    def blk(b, c):
        return b * nc + (nc - 1 - c if reverse else c)

    def zc(name, width):
        cb = Z_OFF[name] // width
        return pl.BlockSpec((RET_CHUNK, width), lambda b, c: (roff + blk(b, c), cb))

    w = RET_H * RET_DV
    in_specs = [pl.BlockSpec(memory_space=pltpu.SMEM), zc("rq", 256), zc("rk", 256), zc("rv", 512),
                zc(gate_name, 512), pl.BlockSpec((1, w), lambda b, c: (0, 0)),
                pl.BlockSpec((1, RET_H, RET_DK, RET_DV), lambda b, c: (b, 0, 0, 0))]
    args = [log_gamma, z, z, z, z, gn, s0]
    if prev is not None:
        in_specs.append(pl.BlockSpec((RET_CHUNK, w), lambda b, c: (blk(b, c), 0)))
        args.append(prev)
    return pl.pallas_call(
        functools.partial(_ret_kernel, reverse=reverse, has_prev=prev is not None),
        grid=(batch, nc),
        in_specs=in_specs,
        out_specs=[pl.BlockSpec((RET_CHUNK, w), lambda b, c: (blk(b, c), 0)),
                   pl.BlockSpec((1, RET_H, RET_DK, RET_DV), lambda b, c: (b, 0, 0, 0))],
        out_shape=[jax.ShapeDtypeStruct((batch * t, w), BF16 if reverse else F32),
                   jax.ShapeDtypeStruct((batch, RET_H, RET_DK, RET_DV), F32)],
        scratch_shapes=[pltpu.VMEM((RET_H, RET_DK, RET_DV), F32)],
        compiler_params=_params("parallel", "arbitrary"),
        name="retention_bwd" if reverse else "retention_fwd",
    )(*args)


def _merge_kernel(*refs, n_x, tm):
    x_refs, refs = refs[:n_x], refs[n_x:]
    ctx_refs, lat_refs = refs[:N_BRANCH], refs[N_BRANCH:2 * N_BRANCH]
    (gate_ref, wb_ref, wo_ref, mod_ref, g_ref, wr_ref, br_ref, xo_ref, h_ref, lg_ref) = refs[2 * N_BRANCH:]
    i = pl.program_id(0)
    is_ctx = i < N_CTX // tm

    merged = None
    for n in range(N_BRANCH):
        branch = jnp.where(is_ctx, ctx_refs[n][...], lat_refs[n][...])
        t = jnp.dot(branch, wb_ref[n], preferred_element_type=F32)
        term = jax.nn.sigmoid(gate_ref[:, n * D_MODEL:(n + 1) * D_MODEL]) * t
        merged = term if merged is None else merged + term

    m = mod_ref[0]
    out = jnp.dot(merged.astype(BF16), wo_ref[...], preferred_element_type=F32)
    x = _token_tile(x_refs, i, tm) + m[:, 2 * D_MODEL:3 * D_MODEL] * out
    xo_ref[...] = x
    h = (_rms(x) * g_ref[...]) * (1.0 + m[:, 4 * D_MODEL:5 * D_MODEL]) + m[:, 3 * D_MODEL:4 * D_MODEL]
    h_ref[...] = h
    lg_ref[...] = jnp.dot(h.astype(BF16), wr_ref[...], preferred_element_type=F32) + br_ref[...]


def _merge(xs, ctx_branches, lat_branches, z, w_branch, w_out, mod, gain, w_router, b_router):
    tm = 256
    n_ct = N_CTX // tm
    row = lambda i: (i, 0)
    const2 = lambda i: (0, 0)
    resident = pl.Buffered(1)
    return pl.pallas_call(
        functools.partial(_merge_kernel, n_x=len(xs), tm=tm),
        grid=(N_TOK // tm,),
        in_specs=_token_specs(xs, tm, lambda i: i)
        + [pl.BlockSpec((tm, BRANCH_W), lambda i: (jnp.minimum(i, n_ct - 1), 0))] * N_BRANCH
        + [pl.BlockSpec((tm, BRANCH_W), lambda i: (jnp.maximum(i - n_ct, 0), 0))] * N_BRANCH
        + [pl.BlockSpec((pl.Element(tm), pl.Element(N_BRANCH * D_MODEL)),
                        lambda i: (pl.multiple_of(i * tm, tm), Z_OFF["gate"])),
           pl.BlockSpec((N_BRANCH, BRANCH_W, D_MODEL), lambda i: (0, 0, 0), pipeline_mode=resident),
           pl.BlockSpec((D_MODEL, D_MODEL), const2, pipeline_mode=resident),
           pl.BlockSpec((1, 1, 6 * D_MODEL), lambda i: (_cond_row(i, tm), 0, 0)),
           pl.BlockSpec((1, D_MODEL), const2),
           pl.BlockSpec((D_MODEL, ROUTER_W), const2),
           pl.BlockSpec((1, ROUTER_W), const2)],
        out_specs=[pl.BlockSpec((tm, D_MODEL), row), pl.BlockSpec((tm, D_MODEL), row),
                   pl.BlockSpec((tm, ROUTER_W), row)],
        out_shape=[jax.ShapeDtypeStruct((N_TOK, D_MODEL), F32), jax.ShapeDtypeStruct((N_TOK, D_MODEL), F32),
                   jax.ShapeDtypeStruct((N_TOK, ROUTER_W), F32)],
        compiler_params=_params("parallel"),
        name="merge_out_proj",
    )(*xs, *ctx_branches, *lat_branches, z, w_branch, w_out, mod, gain, w_router, b_router)


def _route_kernel(lg_ref, rec_ref, cnt_ref, tri_sc, carry_sc, *, tm):
    @pl.when(pl.program_id(0) == 0)
    def _():
        r = lax.broadcasted_iota(jnp.int32, (tm, tm), 0)
        c = lax.broadcasted_iota(jnp.int32, (tm, tm), 1)
        tri_sc[...] = (c < r).astype(BF16)
        carry_sc[...] = jnp.zeros(carry_sc.shape, F32)

    lg = lg_ref[...]
    lane = lax.broadcasted_iota(jnp.int32, lg.shape, 1)
    big = jnp.int32(ROUTER_W)

    def first_max(vals):
        top = jnp.max(vals, axis=-1, keepdims=True)
        return top, jnp.min(jnp.where(vals == top, lane, big), axis=-1, keepdims=True)

    g_logits = jnp.where(lane < N_GROUPS, lg, -jnp.inf)
    g_top, grp = first_max(g_logits)
    p_grp = 1.0 / jnp.sum(jnp.exp(g_logits - g_top), axis=-1, keepdims=True)
    lo = N_GROUPS + grp * EXPERTS_PER_GROUP
    e_logits = jnp.where((lane >= lo) & (lane < lo + EXPERTS_PER_GROUP), lg, -jnp.inf)
    v0, i0 = first_max(e_logits)
    v1, i1 = first_max(jnp.where(lane == i0, -jnp.inf, e_logits))
    e1 = jnp.exp(v1 - v0)
    den = 1.0 + e1
    w0 = p_grp * (1.0 / den)
    w1 = p_grp * (e1 / den)

    hot0 = (lane == i0).astype(F32)
    hot1 = (lane == i1).astype(F32)
    before = jnp.dot(tri_sc[...], (hot0 + hot1).astype(BF16), preferred_element_type=F32) + carry_sc[...]
    rank0 = jnp.sum(before * hot0, axis=-1, keepdims=True)
    rank1 = jnp.sum(before * hot1, axis=-1, keepdims=True)
    carry_sc[...] += jnp.sum(hot0 + hot1, axis=0, keepdims=True)

    rec = jnp.zeros(lg.shape, F32)
    for slot, val in ((R_W0, w0), (R_W1, w1), (R_E0, (i0 - N_GROUPS).astype(F32)),
                      (R_E1, (i1 - N_GROUPS).astype(F32)), (R_RANK0, rank0), (R_RANK1, rank1)):
        rec = jnp.where(lane == slot, val, rec)
    rec_ref[...] = rec
    cnt_ref[...] = jnp.broadcast_to(carry_sc[...], cnt_ref.shape)


def _route(logits):
    tm = 512
    rec, cnt = pl.pallas_call(
        functools.partial(_route_kernel, tm=tm),
        grid=(N_TOK // tm,),
        in_specs=[pl.BlockSpec((tm, ROUTER_W), lambda i: (i, 0))],
        out_specs=[pl.BlockSpec((tm, ROUTER_W), lambda i: (i, 0)), pl.BlockSpec((8, ROUTER_W), lambda i: (0, 0))],
        out_shape=[jax.ShapeDtypeStruct((N_TOK, ROUTER_W), F32), jax.ShapeDtypeStruct((8, ROUTER_W), F32)],
        scratch_shapes=[pltpu.VMEM((tm, tm), BF16), pltpu.VMEM((1, ROUTER_W), F32)],
        compiler_params=_params("arbitrary"),
        name="moe_route",
    )(logits)
    expert = rec[:, R_E0:R_E1 + 1].astype(jnp.int32)
    rank = rec[:, R_RANK0:R_RANK1 + 1].astype(jnp.int32)
    counts = cnt[0, N_GROUPS:N_GROUPS + N_EXPERTS].astype(jnp.int32)
    padded = (counts + MOE_BLK - 1) // MOE_BLK * MOE_BLK
    pad_end = jnp.cumsum(padded)
    dest = ((pad_end - padded)[expert] + rank).reshape(-1)
    token = jnp.arange(MOE_ASSIGN, dtype=jnp.int32) // TOP_K
    src = (jnp.arange(MOE_CAP, dtype=jnp.int32) % N_TOK).at[dest].set(token)
    blk_start = jnp.arange(MOE_NBLK, dtype=jnp.int32) * MOE_BLK
    blk_e = jnp.minimum(jnp.sum(pad_end[None, :] <= blk_start[:, None], axis=1), N_EXPERTS - 1).astype(jnp.int32)
    n_used = (pad_end[-1:] // MOE_BLK).astype(jnp.int32)
    ids = jnp.arange(N_EXPERTS, dtype=jnp.int32)
    later = jnp.where((ids[None, :] > ids[:, None]) & (counts[None, :] > 0), ids[None, :], N_EXPERTS)
    nxt = jnp.min(later, axis=1)
    nxt_e = jnp.where(nxt < N_EXPERTS, nxt, -1).astype(jnp.int32)[blk_e]
    return rec, dest, src, blk_e, nxt_e, n_used


def _start_row_gather(src_hbm, dst, sem, index_of, n_rows):
    def issue(r, carry):
        pltpu.make_async_copy(src_hbm.at[pl.ds(index_of(r), 1)], dst.at[pl.ds(r, 1)], sem).start()
        return carry

    lax.fori_loop(0, n_rows, issue, 0, unroll=8)


def _wait_row_gather(src_hbm, dst, sem, n_rows):
    pltpu.make_async_copy(src_hbm.at[pl.ds(0, n_rows)], dst, sem).wait()


def _expert_kernel(be_ref, nx_ref, nu_ref, x_ref, wg_hbm, wu_hbm, wd_hbm, y_ref,
                   stage_g, stage_u, stage_d, res_g, res_u, res_d, wsem, *, layer):
    i = pl.program_id(0)
    n_used = nu_ref[0]

    def weight_copies(e):
        return [pltpu.make_async_copy(w.at[layer, e], st, wsem.at[k])
                for k, (w, st) in enumerate(((wg_hbm, stage_g), (wu_hbm, stage_u), (wd_hbm, stage_d)))]

    def to_bf16(stage, res):
        rows = 256

        def body(c, carry):
            r = pl.multiple_of(c * rows, rows)
            res[pl.ds(r, rows), :] = stage[pl.ds(r, rows), :].astype(BF16)
            return carry

        lax.fori_loop(0, stage.shape[0] // rows, body, 0)

    @pl.when(i < n_used)
    def _():
        e = be_ref[i]
        first = (i == 0) | (e != be_ref[jnp.maximum(i - 1, 0)])

        @pl.when(i == 0)
        def _():
            for cp in weight_copies(e):
                cp.start()

        @pl.when(first)
        def _():
            for cp in weight_copies(e):
                cp.wait()
            to_bf16(stage_g, res_g)
            to_bf16(stage_u, res_u)
            to_bf16(stage_d, res_d)
            nxt = nx_ref[i]

            @pl.when(nxt >= 0)
            def _():
                for cp in weight_copies(nxt):
                    cp.start()

        x = x_ref[...]
        g = jnp.dot(x, res_g[...], preferred_element_type=F32)
        u = jnp.dot(x, res_u[...], preferred_element_type=F32)
        a = (jax.nn.silu(g) * u).astype(BF16)
        y_ref[...] = jnp.dot(a, res_d[...], preferred_element_type=F32)

    @pl.when(i >= n_used)
    def _():
        y_ref[...] = jnp.zeros(y_ref.shape, F32)


def _experts(xp, blk_e, nxt_e, n_used, w_gate, w_up, w_down, layer):
    any_spec = pl.BlockSpec(memory_space=pl.ANY)
    return pl.pallas_call(
        functools.partial(_expert_kernel, layer=layer),
        grid_spec=pltpu.PrefetchScalarGridSpec(
            num_scalar_prefetch=3,
            grid=(MOE_NBLK,),
            in_specs=[pl.BlockSpec((MOE_BLK, D_MODEL), lambda i, be, nx, nu: (jnp.minimum(i, nu[0] - 1), 0)),
                      any_spec, any_spec, any_spec],
            out_specs=pl.BlockSpec((MOE_BLK, D_MODEL), lambda i, *_: (i, 0)),
            scratch_shapes=[pltpu.VMEM((D_MODEL, D_EXPERT), F32), pltpu.VMEM((D_MODEL, D_EXPERT), F32),
                            pltpu.VMEM((D_EXPERT, D_MODEL), F32),
                            pltpu.VMEM((D_MODEL, D_EXPERT), BF16), pltpu.VMEM((D_MODEL, D_EXPERT), BF16),
                            pltpu.VMEM((D_EXPERT, D_MODEL), BF16),
                            pltpu.SemaphoreType.DMA((3,))]),
        out_shape=jax.ShapeDtypeStruct((MOE_CAP, D_MODEL), F32),
        compiler_params=_params("arbitrary"),
        name="moe_experts",
    )(blk_e, nxt_e, n_used, xp, w_gate, w_up, w_down)


DISPATCH_ROWS = 512


def _dispatch_kernel(src_ref, h_hbm, xp_ref, buf, sem):
    i = pl.program_id(0)

    def gather_start(tile, slot):
        _start_row_gather(h_hbm, buf.at[slot], sem.at[slot], lambda r: src_ref[tile * DISPATCH_ROWS + r],
                          DISPATCH_ROWS)

    @pl.when(i == 0)
    def _():
        gather_start(0, 0)

    slot = i % 2
    _wait_row_gather(h_hbm, buf.at[slot], sem.at[slot], DISPATCH_ROWS)

    @pl.when(i + 1 < pl.num_programs(0))
    def _():
        gather_start(i + 1, 1 - slot)

    xp_ref[...] = buf[slot].astype(BF16)


def _dispatch(h, src):
    return pl.pallas_call(
        _dispatch_kernel,
        grid_spec=pltpu.PrefetchScalarGridSpec(
            num_scalar_prefetch=1,
            grid=(MOE_CAP // DISPATCH_ROWS,),
            in_specs=[pl.BlockSpec(memory_space=pl.ANY)],
            out_specs=pl.BlockSpec((DISPATCH_ROWS, D_MODEL), lambda i, s: (i, 0)),
            scratch_shapes=[pltpu.VMEM((2, DISPATCH_ROWS, D_MODEL), F32), pltpu.SemaphoreType.DMA((2,))]),
        out_shape=jax.ShapeDtypeStruct((MOE_CAP, D_MODEL), BF16),
        compiler_params=_params("arbitrary"),
        name="moe_dispatch",
    )(src, h)


def _combine_kernel(*refs, tm, final):
    dest_ref, yp_hbm, x_ref, rec_ref, mod_ref = refs[:5]
    refs = refs[5:]
    if final:
        fn_ref, yc_ref, yl_ref = refs[:3]
    else:
        xo_ref = refs[0]
    ybuf, sem = refs[-2:]
    i = pl.program_id(0)

    def gather_start(tile, slot):
        for k in range(TOP_K):
            _start_row_gather(yp_hbm, ybuf.at[slot, k], sem.at[slot, k],
                              lambda r, k=k: dest_ref[(tile * tm + r) * TOP_K + k], tm)

    @pl.when(i == 0)
    def _():
        gather_start(0, 0)

    slot = i % 2
    for k in range(TOP_K):
        _wait_row_gather(yp_hbm, ybuf.at[slot, k], sem.at[slot, k], tm)

    @pl.when(i + 1 < pl.num_programs(0))
    def _():
        gather_start(i + 1, 1 - slot)

    rec = rec_ref[...]
    y = ybuf[slot, 0] * rec[:, R_W0:R_W0 + 1] + ybuf[slot, 1] * rec[:, R_W1:R_W1 + 1]
    x = x_ref[...] + mod_ref[0][:, 5 * D_MODEL:6 * D_MODEL] * y
    if final:
        y = _rms(x) * fn_ref[...]

        @pl.when(i < N_CTX // tm)
        def _():
            yc_ref[...] = y

        @pl.when(i >= N_CTX // tm)
        def _():
            yl_ref[...] = y
    else:
        xo_ref[...] = x


def _combine(yp, dest, x, rec, mod, final_gain):
    tm = 256
    n_ct = N_CTX // tm
    final = final_gain is not None
    row = lambda i, d: (i, 0)
    in_specs = [pl.BlockSpec(memory_space=pl.ANY),
                pl.BlockSpec((tm, D_MODEL), row),
                pl.BlockSpec((tm, ROUTER_W), row),
                pl.BlockSpec((1, 1, 6 * D_MODEL), lambda i, d: (_cond_row(i, tm), 0, 0))]
    args = [dest, yp, x, rec, mod]
    if final:
        in_specs.append(pl.BlockSpec((1, D_MODEL), lambda i, d: (0, 0)))
        args.append(final_gain)
        out_specs = [pl.BlockSpec((tm, D_MODEL), lambda i, d: (jnp.minimum(i, n_ct - 1), 0)),
                     pl.BlockSpec((tm, D_MODEL), lambda i, d: (jnp.maximum(i - n_ct, 0), 0))]
        out_shape = [jax.ShapeDtypeStruct((N_CTX, D_MODEL), F32), jax.ShapeDtypeStruct((N_LAT, D_MODEL), F32)]
    else:
        out_specs = pl.BlockSpec((tm, D_MODEL), row)
        out_shape = jax.ShapeDtypeStruct((N_TOK, D_MODEL), F32)
    return pl.pallas_call(
        functools.partial(_combine_kernel, tm=tm, final=final),
        grid_spec=pltpu.PrefetchScalarGridSpec(
            num_scalar_prefetch=1,
            grid=(N_TOK // tm,),
            in_specs=in_specs,
            out_specs=out_specs,
            scratch_shapes=[pltpu.VMEM((2, TOP_K, tm, D_MODEL), F32), pltpu.SemaphoreType.DMA((2, TOP_K))]),
        out_shape=out_shape,
        compiler_params=_params("arbitrary"),
        name="moe_combine",
    )(*args)


def _swap_halves(w, chunk):
    k, n = w.shape
    w = w.reshape(k, n // chunk, 2, chunk // 2)
    return w[:, :, ::-1, :].reshape(k, n)


def _rope_tables(rot_dim):
    rows = DEC_SEQ // GRID_W
    row = jnp.repeat(jnp.arange(rows), GRID_W).astype(F32)
    col = (jnp.arange(rows * GRID_W) % GRID_W).astype(F32)
    axis_dim = rot_dim // 2
    inv = ROPE_THETA ** (-jnp.arange(0, axis_dim, 2, dtype=F32) / axis_dim)
    ang = jnp.concatenate([row[:, None] * inv, col[:, None] * inv], axis=-1)
    c, s = jnp.cos(ang), jnp.sin(ang)
    return jnp.concatenate([c, c], axis=-1), jnp.concatenate([-s, s], axis=-1)


def _head_major(cache):
    b, l, kvh, d = cache.shape
    return jnp.transpose(cache, (2, 0, 1, 3)).reshape(kvh, b * l, d).astype(BF16)


def kernel(x_prompt, x_sample, c, cache_mla_ckv, cache_mla_krope, cache_gqa_k, cache_gqa_v, cache_swa_k,
           cache_swa_v, state_ret, c_ctx, ada_w, ada_b, norm_mix, norm_ffn, w_in, mla_q_norm, mla_kv_norm,
           mla_w_uq, mla_w_ukv, gqa_q_norm, gqa_k_norm, ret_decay_logit, ret_gn, swa_sink, w_branch, w_out,
           router_group_w, router_group_b, router_expert_w, router_expert_b, moe_w_gate, moe_w_up, moe_w_down,
           final_norm):
    xs = (x_prompt.reshape(N_CTX, D_MODEL), x_sample.reshape(N_LAT, D_MODEL))
    cond = jnp.concatenate([c_ctx[None, :], c, jnp.zeros((N_COND - 1 - DEC_BATCH, D_MODEL), F32)], axis=0)
    mod_all = _ada(cond, ada_w, ada_b).reshape(DEPTH, N_COND, 1, 6 * D_MODEL)

    w_in_r = _w_in_layout(w_in)
    tabs = _rope_tables(HEAD_DIM) + _rope_tables(MLA_ROPE)
    log_gamma = jax.nn.log_sigmoid(ret_decay_logit.astype(F32))
    uq = mla_w_uq.reshape(DEPTH, MLA_Q_LORA, MLA_H, MLA_DK)
    ukv = mla_w_ukv.reshape(DEPTH, MLA_KV_LORA, MLA_H, MLA_NOPE + MLA_V)
    w_router = jnp.concatenate(
        [router_group_w, router_expert_w,
         jnp.zeros((DEPTH, D_MODEL, ROUTER_W - N_GROUPS - N_EXPERTS), F32)], axis=-1).astype(BF16)
    b_router = jnp.concatenate(
        [router_group_b, router_expert_b, jnp.zeros((DEPTH, ROUTER_W - N_GROUPS - N_EXPERTS), F32)], axis=-1)
    zero_state = jnp.zeros((BATCH, RET_H, RET_DK, RET_DV), F32)

    caches = [[] for _ in range(7)]
    for l in range(DEPTH):
        mod = mod_all[l]
        uq_rope = uq[l, :, :, MLA_NOPE:].reshape(MLA_Q_LORA, MLA_H * MLA_ROPE)
        lw = dict(
            q_norm=mla_q_norm[l].reshape(1, -1), kv_norm=mla_kv_norm[l].reshape(1, -1),
            gq_norm=gqa_q_norm[l].reshape(1, -1), gk_norm=gqa_k_norm[l].reshape(1, -1),
            w_uq_nope=uq[l, :, :, :MLA_NOPE].reshape(MLA_Q_LORA, MLA_H * MLA_NOPE).astype(BF16),
            w_uq_rope=uq_rope.astype(BF16), w_uq_swap=_swap_halves(uq_rope, MLA_ROPE).astype(BF16),
            w_uk=ukv[l, :, :, :MLA_NOPE].reshape(MLA_KV_LORA, MLA_H * MLA_NOPE).astype(BF16),
            w_uv=ukv[l, :, :, MLA_NOPE:].reshape(MLA_KV_LORA, MLA_H * MLA_V).astype(BF16))

        z = _in_proj(xs, mod, norm_mix[l].reshape(1, -1), w_in_r, l)

        ctx_ops = _prep(z, lw, None, latent=False)
        ckv, kr, kg, vg, ks, vs = ctx_ops[9:]
        oa_c, ob_c, od_c = _ctx_attention(ctx_ops[:9], swa_sink[l])
        gn = ret_gn[l]
        of_c, s_f = _retention(z, log_gamma[l, 0], gn[0:1], zero_state, None, latent=False, reverse=False)
        oc_c, s_b = _retention(z, log_gamma[l, 1], gn[1:2], zero_state, of_c, latent=False, reverse=True)

        (qa, ka, va, qb, kb, vb, qd, kd, vd) = _prep(z, lw, tabs, latent=True)
        ka0, va0 = _mla_ctx(cache_mla_ckv[:, l].reshape(DEC_BATCH * PAST_LEN, MLA_KV_LORA),
                            cache_mla_krope[:, l].reshape(DEC_BATCH * PAST_LEN, MLA_ROPE), lw)
        oa_l = _flash_latent(qa, ka, va, (ka0, va0))
        ob_l = _flash_latent(qb, kb, vb, (_head_major(cache_gqa_k[:, l]), _head_major(cache_gqa_v[:, l])))
        od_l = _window(qd, kd, vd, _head_major(cache_swa_k[:, l]), _head_major(cache_swa_v[:, l]), swa_sink[l])
        of_l, _ = _retention(z, log_gamma[l, 0], gn[0:1], state_ret[:, l, 0], None, latent=True, reverse=False)
        oc_l, _ = _retention(z, log_gamma[l, 1], gn[1:2], state_ret[:, l, 1], of_l, latent=True, reverse=True)

        x_mid, h_ffn, logits = _merge(xs, (oa_c, ob_c, oc_c, od_c), (oa_l, ob_l, oc_l, od_l), z,
                                      w_branch[l].astype(BF16), w_out[l].astype(BF16), mod,
                                      norm_ffn[l].reshape(1, -1), w_router[l], b_router[l].reshape(1, -1))

        rec, dest, src, blk_e, nxt_e, n_used = _route(logits)
        yp = _experts(_dispatch(h_ffn, src), blk_e, nxt_e, n_used, moe_w_gate, moe_w_up, moe_w_down, l)
        if l < DEPTH - 1:
            xs = (_combine(yp, dest, x_mid, rec, mod, None),)
        else:
            y_ctx, y_lat = _combine(yp, dest, x_mid, rec, mod, final_norm.reshape(1, D_MODEL))

        kvh_shape = (BATCH, SEQ, GQA_KVH, HEAD_DIM)
        for lst, val in zip(caches, (ckv.reshape(BATCH, SEQ, MLA_KV_LORA), kr.reshape(BATCH, SEQ, MLA_ROPE),
                                     kg.reshape(kvh_shape), vg.reshape(kvh_shape), ks.reshape(kvh_shape),
                                     vs.reshape(kvh_shape), jnp.stack([s_f, s_b], axis=1))):
            lst.append(val)

    y_prompt = y_ctx.reshape(BATCH, SEQ, D_MODEL)
    y_sample = y_lat.reshape(DEC_BATCH, DEC_SEQ, D_MODEL)
    return (y_prompt, y_sample) + tuple(jnp.stack(lst, axis=1) for lst in caches)
```

```python
import functools

import jax
import jax.numpy as jnp
from jax import lax
from jax.experimental import pallas as pl
from jax.experimental.pallas import tpu as pltpu

F32 = jnp.float32
BF16 = jnp.bfloat16
LANES = 128

D_MODEL = 2048
BATCH = 16
SEQ = 256
DEPTH = 4
DEC_BATCH = 2
DEC_SEQ = 4096
PAST_LEN = 256
GRID_W = 64
ROPE_THETA = 10000.0
EPS = 1e-6
QBLK = 128
NEG_INF = -1e30

MLA_H = 4
MLA_Q_LORA = 512
MLA_KV_LORA = 256
MLA_NOPE = 128
MLA_ROPE = 64
MLA_V = 128
MLA_SCALE = (MLA_NOPE + MLA_ROPE) ** -0.5
MLA_DK = MLA_NOPE + MLA_ROPE

HEAD_DIM = 128
ATTN_SCALE = HEAD_DIM ** -0.5
GQA_H = 4
GQA_KVH = 2
SWA_H = 4
SWA_KVH = 2
WINDOW = 128

RET_H = 4
RET_DK = 64
RET_DV = 128
RET_CHUNK = 128
RET_K_SCALE = RET_DK ** -0.5

N_BRANCH = 4
BRANCH_W = 512

N_GROUPS = 4
EXPERTS_PER_GROUP = 8
N_EXPERTS = N_GROUPS * EXPERTS_PER_GROUP
TOP_K = 2
D_EXPERT = 1024

N_CTX = BATCH * SEQ
N_LAT = DEC_BATCH * DEC_SEQ
N_TOK = N_CTX + N_LAT
N_COND = 8

W_IN_HEAD = MLA_Q_LORA + MLA_KV_LORA + MLA_ROPE
IN_WIDTH = (W_IN_HEAD + (GQA_H + 2 * GQA_KVH) * HEAD_DIM + 2 * RET_H * RET_DK + 3 * RET_H * RET_DV
            + (SWA_H + 2 * SWA_KVH) * HEAD_DIM + N_BRANCH * D_MODEL)
Z_TAIL = 1024
Z_WIDTH = Z_TAIL + IN_WIDTH - W_IN_HEAD
Z_OFF = dict(mq=0, mkv=512, mkr=768, gq=1024, gk=1536, gv=1792, rq=2048, rk=2304, rv=2560, rgf=3072, rgb=3584,
             sq=4096, sk=4608, sv=4864, gate=5120)
Z_TILE = 1024

MOE_BLK = 256
MOE_ASSIGN = N_TOK * TOP_K
MOE_NBLK = (MOE_ASSIGN + N_EXPERTS * (MOE_BLK - 1) + MOE_BLK - 1) // MOE_BLK
MOE_CAP = MOE_NBLK * MOE_BLK
ROUTER_W = 128
R_W0, R_W1, R_E0, R_E1, R_RANK0, R_RANK1 = range(6)

VMEM_LIMIT = 56 * 1024 * 1024


def _params(*sem):
    return pltpu.CompilerParams(dimension_semantics=sem, vmem_limit_bytes=VMEM_LIMIT)


def _cond_row(i, tm):
    start = i * tm
    return jnp.where(start < N_CTX, 0, 1 + (start - N_CTX) // DEC_SEQ)


def _rms(x):
    return x * lax.rsqrt(jnp.mean(x * x, axis=-1, keepdims=True) + EPS)


def _token_specs(xs, tm, index_of):
    if len(xs) == 1:
        return [pl.BlockSpec((tm, D_MODEL), lambda *g: (index_of(*g), 0))]
    n_ct = N_CTX // tm
    return [pl.BlockSpec((tm, D_MODEL), lambda *g: (jnp.minimum(index_of(*g), n_ct - 1), 0),
                         pipeline_mode=pl.Buffered(1)),
            pl.BlockSpec((tm, D_MODEL), lambda *g: (jnp.maximum(index_of(*g) - n_ct, 0), 0),
                         pipeline_mode=pl.Buffered(1))]


def _token_tile(x_refs, i, tm):
    if len(x_refs) == 1:
        return x_refs[0][...]
    return jnp.where(i < N_CTX // tm, x_refs[0][...], x_refs[1][...])


def _ada_kernel(c_ref, w_ref, b_ref, o_ref):
    s = jax.nn.silu(c_ref[...]).astype(BF16)
    o_ref[0] = jnp.dot(s, w_ref[0].astype(BF16), preferred_element_type=F32) + b_ref[0]


def _ada(cond, ada_w, ada_b):
    tn = 1024
    n_out = 6 * D_MODEL
    return pl.pallas_call(
        _ada_kernel,
        grid=(DEPTH, n_out // tn),
        in_specs=[pl.BlockSpec((N_COND, D_MODEL), lambda l, j: (0, 0)),
                  pl.BlockSpec((1, D_MODEL, tn), lambda l, j: (l, 0, j)),
                  pl.BlockSpec((1, 1, tn), lambda l, j: (l, 0, j))],
        out_specs=pl.BlockSpec((1, N_COND, tn), lambda l, j: (l, 0, j)),
        out_shape=jax.ShapeDtypeStruct((DEPTH, N_COND, n_out), F32),
        compiler_params=_params("parallel", "parallel"),
        name="ada_mod",
    )(cond, ada_w, ada_b.reshape(DEPTH, 1, n_out))


def _w_in_kernel(w_ref, o_ref):
    w = w_ref[0]

    @pl.when(pl.program_id(1) == 0)
    def _():
        half = MLA_ROPE // 2
        mkr0 = W_IN_HEAD - MLA_ROPE
        o_ref[0, 0:W_IN_HEAD] = w[0:W_IN_HEAD].astype(BF16)
        o_ref[0, W_IN_HEAD:W_IN_HEAD + half] = w[mkr0 + half:W_IN_HEAD].astype(BF16)
        o_ref[0, W_IN_HEAD + half:W_IN_HEAD + MLA_ROPE] = w[mkr0:mkr0 + half].astype(BF16)
        o_ref[0, W_IN_HEAD + MLA_ROPE:Z_TAIL] = jnp.zeros((Z_TAIL - W_IN_HEAD - MLA_ROPE, w.shape[1]), BF16)

    @pl.when(pl.program_id(1) > 0)
    def _():
        o_ref[0] = w.astype(BF16)


def _w_in_layout(w_in):
    w_t = jnp.swapaxes(w_in, 1, 2)

    def src_row(j):
        return pl.multiple_of(jnp.where(j > 0, (j - 1) * Z_TILE + W_IN_HEAD, 0), 16)

    return pl.pallas_call(
        _w_in_kernel,
        grid=(DEPTH, Z_WIDTH // Z_TILE),
        in_specs=[pl.BlockSpec((pl.Element(1), pl.Element(Z_TILE), pl.Element(D_MODEL)),
                               lambda l, j: (l, src_row(j), 0))],
        out_specs=pl.BlockSpec((1, Z_TILE, D_MODEL), lambda l, j: (l, j, 0)),
        out_shape=jax.ShapeDtypeStruct((DEPTH, Z_WIDTH, D_MODEL), BF16),
        compiler_params=_params("parallel", "parallel"),
        name="w_in_layout",
    )(w_t)


def _in_kernel(*refs, n_x, tm):
    x_refs, (mod_ref, g_ref, w_ref, z_ref, h_ref) = refs[:n_x], refs[n_x:]

    @pl.when(pl.program_id(1) == 0)
    def _():
        m = mod_ref[0]
        y = _rms(_token_tile(x_refs, pl.program_id(0), tm)) * g_ref[...]
        h_ref[...] = (y * (1.0 + m[:, D_MODEL:2 * D_MODEL]) + m[:, 0:D_MODEL]).astype(BF16)

    z_ref[...] = lax.dot_general(h_ref[...], w_ref[...], (((1,), (1,)), ((), ())), preferred_element_type=F32)


def _in_proj(xs, mod, gain, w_all, layer):
    tm, tn = 1024, Z_TILE
    return pl.pallas_call(
        functools.partial(_in_kernel, n_x=len(xs), tm=tm),
        grid=(N_TOK // tm, Z_WIDTH // tn),
        in_specs=_token_specs(xs, tm, lambda i, j: i)
        + [pl.BlockSpec((1, 1, 6 * D_MODEL), lambda i, j: (_cond_row(i, tm), 0, 0)),
           pl.BlockSpec((1, D_MODEL), lambda i, j: (0, 0)),
           pl.BlockSpec((None, tn, D_MODEL), lambda i, j: (layer, j, 0))],
        out_specs=pl.BlockSpec((tm, tn), lambda i, j: (i, j)),
        out_shape=jax.ShapeDtypeStruct((N_TOK, Z_WIDTH), F32),
        scratch_shapes=[pltpu.VMEM((tm, D_MODEL), BF16)],
        compiler_params=_params("parallel", "arbitrary"),
        name="in_proj",
    )(*xs, mod, gain, w_all)


def _rope128(x, cos, sin):
    return x * cos + pltpu.roll(x, HEAD_DIM // 2, 1) * sin


def _mla_kv(ckv, kr, wuk_ref, wuv_ref, ka_ref, va_ref):
    cb = ckv.astype(BF16)
    kn = jnp.dot(cb, wuk_ref[...], preferred_element_type=F32)
    vv = jnp.dot(cb, wuv_ref[...], preferred_element_type=F32)
    for h in range(MLA_H):
        ka_ref[h, :, 0:MLA_NOPE] = kn[:, h * MLA_NOPE:(h + 1) * MLA_NOPE].astype(BF16)
        ka_ref[h, :, MLA_NOPE:MLA_DK] = kr.astype(BF16)
        va_ref[h] = vv[:, h * MLA_V:(h + 1) * MLA_V].astype(BF16)


def _prep_kernel(*refs, latent):
    (mq_ref, gq_ref, sq_ref, mkv_ref, gk_ref, gv_ref, sk_ref, sv_ref, mkr_ref,
     qn_ref, kvn_ref, gqn_ref, gkn_ref, wqn_ref, wqr_ref, wqs_ref, wuk_ref, wuv_ref) = refs[:18]
    refs = refs[18:]
    if latent:
        cos_ref, sin_ref, c64_ref, s64_ref = refs[:4]
        refs = refs[4:]
        cos, sin = cos_ref[...], sin_ref[...]
        c64, s64 = c64_ref[...], s64_ref[...]
    qa_ref, ka_ref, va_ref, qb_ref, kb_ref, vb_ref, qd_ref, kd_ref, vd_ref = refs[:9]
    refs = refs[9:]

    hn = (_rms(mq_ref[...]) * qn_ref[...]).astype(BF16)
    q_nope = jnp.dot(hn, wqn_ref[...], preferred_element_type=F32)
    q_rope = jnp.dot(hn, wqr_ref[...], preferred_element_type=F32)
    if latent:
        q_swap = jnp.dot(hn, wqs_ref[...], preferred_element_type=F32)
    for h in range(MLA_H):
        qa_ref[h, :, 0:MLA_NOPE] = (q_nope[:, h * MLA_NOPE:(h + 1) * MLA_NOPE] * MLA_SCALE).astype(BF16)
        qr = q_rope[:, h * MLA_ROPE:(h + 1) * MLA_ROPE]
        if latent:
            qr = qr * c64 + q_swap[:, h * MLA_ROPE:(h + 1) * MLA_ROPE] * s64
        qa_ref[h, :, MLA_NOPE:MLA_DK] = (qr * MLA_SCALE).astype(BF16)
    ckv = _rms(mkv_ref[...]) * kvn_ref[...]
    kr = mkr_ref[:, 0:MLA_ROPE]
    if latent:
        kr = kr * c64 + mkr_ref[:, MLA_ROPE:2 * MLA_ROPE] * s64
    _mla_kv(ckv, kr, wuk_ref, wuv_ref, ka_ref, va_ref)

    gq = gq_ref[...]
    for h in range(GQA_H):
        q = _rms(gq[:, h * HEAD_DIM:(h + 1) * HEAD_DIM]) * gqn_ref[...]
        if latent:
            q = _rope128(q, cos, sin)
        qb_ref[h] = (q * ATTN_SCALE).astype(BF16)
    gk = gk_ref[...]
    gv = gv_ref[...]
    kg = []
    for h in range(GQA_KVH):
        k = _rms(gk[:, h * HEAD_DIM:(h + 1) * HEAD_DIM]) * gkn_ref[...]
        kg.append(k)
        if latent:
            k = _rope128(k, cos, sin)
        kb_ref[h] = k.astype(BF16)
        vb_ref[h] = gv[:, h * HEAD_DIM:(h + 1) * HEAD_DIM].astype(BF16)

    sq = sq_ref[...]
    for h in range(SWA_H):
        q = sq[:, h * HEAD_DIM:(h + 1) * HEAD_DIM]
        if latent:
            q = _rope128(q, cos, sin)
        qd_ref[h] = (q * ATTN_SCALE).astype(BF16)
    sk = sk_ref[...]
    sv = sv_ref[...]
    for h in range(SWA_KVH):
        k = sk[:, h * HEAD_DIM:(h + 1) * HEAD_DIM]
        if latent:
            k = _rope128(k, cos, sin)
        kd_ref[h] = k.astype(BF16)
        vd_ref[h] = sv[:, h * HEAD_DIM:(h + 1) * HEAD_DIM].astype(BF16)

    if not latent:
        ckv_ref, kr_ref, kg_ref, vg_ref, ks_ref, vs_ref = refs
        ckv_ref[...] = ckv
        kr_ref[...] = kr
        for h in range(GQA_KVH):
            kg_ref[:, h * HEAD_DIM:(h + 1) * HEAD_DIM] = kg[h]
        vg_ref[...] = gv
        ks_ref[...] = sk
        vs_ref[...] = sv


def _prep(z, lw, tabs, latent):
    tm = 512
    rows = N_LAT if latent else N_CTX
    roff = (N_CTX if latent else 0) // tm
    n_t = DEC_SEQ // tm

    def zc(name, width):
        cb = Z_OFF[name] // width
        return pl.BlockSpec((tm, width), lambda i: (i + roff, cb))

    def full(a):
        nd = a.ndim
        return pl.BlockSpec(a.shape, lambda i: (0,) * nd)

    weights = [lw["q_norm"], lw["kv_norm"], lw["gq_norm"], lw["gk_norm"],
               lw["w_uq_nope"], lw["w_uq_rope"], lw["w_uq_swap"], lw["w_uk"], lw["w_uv"]]
    in_specs = [zc("mq", 512), zc("gq", 512), zc("sq", 512), zc("mkv", 256), zc("gk", 256), zc("gv", 256),
                zc("sk", 256), zc("sv", 256), zc("mkr", 128)] + [full(a) for a in weights]
    args = [z] * 9 + weights
    if latent:
        in_specs += [pl.BlockSpec((tm, HEAD_DIM), lambda i: (i % n_t, 0))] * 2
        in_specs += [pl.BlockSpec((tm, MLA_ROPE), lambda i: (i % n_t, 0))] * 2
        args += list(tabs)

    def heads(n, d):
        return (pl.BlockSpec((n, tm, d), lambda i: (0, i, 0)), jax.ShapeDtypeStruct((n, rows, d), BF16))

    outs = [heads(MLA_H, MLA_DK), heads(MLA_H, MLA_DK), heads(MLA_H, MLA_V),
            heads(GQA_H, HEAD_DIM), heads(GQA_KVH, HEAD_DIM), heads(GQA_KVH, HEAD_DIM),
            heads(SWA_H, HEAD_DIM), heads(SWA_KVH, HEAD_DIM), heads(SWA_KVH, HEAD_DIM)]
    if not latent:
        kv_w = GQA_KVH * HEAD_DIM
        for d in (MLA_KV_LORA, MLA_ROPE, kv_w, kv_w, kv_w, kv_w):
            outs.append((pl.BlockSpec((tm, d), lambda i: (i, 0)), jax.ShapeDtypeStruct((rows, d), F32)))
    return pl.pallas_call(
        functools.partial(_prep_kernel, latent=latent),
        grid=(rows // tm,),
        in_specs=in_specs,
        out_specs=[o[0] for o in outs],
        out_shape=[o[1] for o in outs],
        compiler_params=_params("parallel"),
        name="prep_latent" if latent else "prep_context",
    )(*args)


def _mla_ctx_kernel(ckv_ref, kr_ref, wuk_ref, wuv_ref, ka_ref, va_ref):
    _mla_kv(ckv_ref[...], kr_ref[...], wuk_ref, wuv_ref, ka_ref, va_ref)


def _mla_ctx(ckv, kr, lw):
    rows = ckv.shape[0]
    return pl.pallas_call(
        _mla_ctx_kernel,
        out_shape=[jax.ShapeDtypeStruct((MLA_H, rows, MLA_DK), BF16),
                   jax.ShapeDtypeStruct((MLA_H, rows, MLA_V), BF16)],
        name="mla_ctx_kv",
    )(ckv, kr, lw["w_uk"], lw["w_uv"])


def _softmax_pv(q, k, v, sink):
    s = lax.dot_general(q, k, (((1,), (1,)), ((), ())), preferred_element_type=F32)
    m = jnp.max(s, axis=-1, keepdims=True)
    if sink is not None:
        m = jnp.maximum(m, sink)
    p = jnp.exp(s - m)
    l = jnp.sum(p, axis=-1, keepdims=True)
    if sink is not None:
        l = l + jnp.exp(sink - m)
    return jnp.dot(p.astype(BF16), v, preferred_element_type=F32) / l


def _ctx_attn_kernel(sink_ref, qa, ka, va, qb, kb, vb, qd, kd, vd, oa, ob, od):
    for h in range(MLA_H):
        oa[:, h * MLA_V:(h + 1) * MLA_V] = _softmax_pv(qa[h], ka[h], va[h], None).astype(BF16)
    for h in range(GQA_H):
        kv = h // (GQA_H // GQA_KVH)
        ob[:, h * HEAD_DIM:(h + 1) * HEAD_DIM] = _softmax_pv(qb[h], kb[kv], vb[kv], None).astype(BF16)
    for h in range(SWA_H):
        kv = h // (SWA_H // SWA_KVH)
        od[:, h * HEAD_DIM:(h + 1) * HEAD_DIM] = _softmax_pv(qd[h], kd[kv], vd[kv], sink_ref[h]).astype(BF16)


def _ctx_attention(operands, sink):
    def spec(a):
        return pl.BlockSpec((a.shape[0], SEQ, a.shape[2]), lambda b: (0, b, 0))

    out = jax.ShapeDtypeStruct((N_CTX, BRANCH_W), BF16)
    return pl.pallas_call(
        _ctx_attn_kernel,
        grid=(BATCH,),
        in_specs=[pl.BlockSpec(memory_space=pltpu.SMEM)] + [spec(a) for a in operands],
        out_specs=[pl.BlockSpec((SEQ, BRANCH_W), lambda b: (b, 0))] * 3,
        out_shape=[out, out, out],
        compiler_params=_params("parallel"),
        name="context_attention",
    )(sink, *operands)


KEY_CHUNK = 256


def _flash_latent_kernel(q_ref, k_ref, v_ref, k0_ref, v0_ref, o_ref, m_sc, l_sc, acc_sc, *, groups):
    dv = v_ref.shape[-1]
    n_chunks = k_ref.shape[1] // KEY_CHUNK
    dn = (((1,), (1,)), ((), ()))

    def scores(g, k):
        return lax.dot_general(q_ref[g], k, dn, preferred_element_type=F32)

    def lane_max(s):
        m = s[:, 0:LANES]
        for c in range(1, s.shape[1] // LANES):
            m = jnp.maximum(m, s[:, c * LANES:(c + 1) * LANES])
        return m

    for g in range(groups):
        m_sc[g] = lane_max(scores(g, k0_ref[0]))

    def max_body(j, carry):
        off = pl.multiple_of(j * KEY_CHUNK, KEY_CHUNK)
        for g in range(groups):
            m_sc[g] = jnp.maximum(m_sc[g], lane_max(scores(g, k_ref[0, pl.ds(off, KEY_CHUNK), :])))
        return carry

    lax.fori_loop(0, n_chunks, max_body, 0, unroll=True)
    for g in range(groups):
        m_sc[g] = jnp.broadcast_to(jnp.max(m_sc[g], axis=-1, keepdims=True), m_sc.shape[1:])

    def accumulate(g, k, v, first):
        s = scores(g, k)
        m = m_sc[g]
        ps = [jnp.exp(s[:, c * LANES:(c + 1) * LANES] - m) for c in range(s.shape[1] // LANES)]
        lsum = ps[0]
        for p in ps[1:]:
            lsum = lsum + p
        pv = jnp.dot(jnp.concatenate(ps, axis=1).astype(BF16), v, preferred_element_type=F32)
        if first:
            l_sc[g] = lsum
            acc_sc[g] = pv
        else:
            l_sc[g] += lsum
            acc_sc[g] += pv

    for g in range(groups):
        accumulate(g, k0_ref[0], v0_ref[0], True)

    def sum_body(j, carry):
        off = pl.multiple_of(j * KEY_CHUNK, KEY_CHUNK)
        for g in range(groups):
            accumulate(g, k_ref[0, pl.ds(off, KEY_CHUNK), :], v_ref[0, pl.ds(off, KEY_CHUNK), :], False)
        return carry

    lax.fori_loop(0, n_chunks, sum_body, 0, unroll=True)
    for g in range(groups):
        l = jnp.sum(l_sc[g], axis=-1, keepdims=True)
        o_ref[:, g * dv:(g + 1) * dv] = (acc_sc[g] / l).astype(o_ref.dtype)


def _flash_latent(q, k, v, ctx):
    tq = 1024
    n_h, _, dk = q.shape
    kvh, _, dv = v.shape
    groups = n_h // kvh
    nq = DEC_SEQ // tq
    return pl.pallas_call(
        functools.partial(_flash_latent_kernel, groups=groups),
        grid=(DEC_BATCH, kvh, nq),
        in_specs=[pl.BlockSpec((groups, tq, dk), lambda b, h, i: (h, b * nq + i, 0)),
                  pl.BlockSpec((1, DEC_SEQ, dk), lambda b, h, i: (h, b, 0)),
                  pl.BlockSpec((1, DEC_SEQ, dv), lambda b, h, i: (h, b, 0)),
                  pl.BlockSpec((1, PAST_LEN, dk), lambda b, h, i: (h, b, 0)),
                  pl.BlockSpec((1, PAST_LEN, dv), lambda b, h, i: (h, b, 0))],
        out_specs=pl.BlockSpec((tq, groups * dv), lambda b, h, i: (b * nq + i, h)),
        out_shape=jax.ShapeDtypeStruct((N_LAT, n_h * dv), BF16),
        scratch_shapes=[pltpu.VMEM((groups, tq, LANES), F32), pltpu.VMEM((groups, tq, LANES), F32),
                        pltpu.VMEM((groups, tq, dv), F32)],
        compiler_params=_params("parallel", "parallel", "parallel"),
        name="flash_latent",
    )(q, k, v, *ctx)


WIN_TQ = 512
WIN_SPAN = WIN_TQ + 2 * WINDOW


def _lanewise(op, x):
    r = x[:, 0:LANES]
    for c in range(1, x.shape[1] // LANES):
        r = op(r, x[:, c * LANES:(c + 1) * LANES])
    return r


def _exp_tiles(s, m):
    return jnp.concatenate([jnp.exp(s[:, c * LANES:(c + 1) * LANES] - m) for c in range(s.shape[1] // LANES)],
                           axis=1)


def _window_kernel(sink_ref, q_ref, k_ref, v_ref, k0_ref, v0_ref, o_ref, *, groups):
    head0 = pl.program_id(1) * groups
    t = k_ref.shape[1]
    q0 = pl.program_id(2) * WIN_TQ
    k_start = pl.multiple_of(jnp.clip(q0 - WINDOW, 0, t - WIN_SPAN), LANES)
    qpos = q0 + lax.broadcasted_iota(jnp.int32, (WIN_TQ, WIN_SPAN), 0)
    kpos = k_start + lax.broadcasted_iota(jnp.int32, (WIN_TQ, WIN_SPAN), 1)
    band = jnp.abs(kpos - qpos) <= WINDOW
    kw = k_ref[0, pl.ds(k_start, WIN_SPAN), :]
    vw = v_ref[0, pl.ds(k_start, WIN_SPAN), :]
    dn = (((1,), (1,)), ((), ()))
    for g in range(groups):
        q = q_ref[g]
        s_loc = jnp.where(band, lax.dot_general(q, kw, dn, preferred_element_type=F32), NEG_INF)
        s_ctx = lax.dot_general(q, k0_ref[0], dn, preferred_element_type=F32)
        snk = sink_ref[head0 + g]
        m = jnp.max(jnp.maximum(_lanewise(jnp.maximum, s_loc), _lanewise(jnp.maximum, s_ctx)),
                    axis=-1, keepdims=True)
        m = jnp.broadcast_to(jnp.maximum(m, snk), (WIN_TQ, LANES))
        p_loc = _exp_tiles(s_loc, m)
        p_ctx = _exp_tiles(s_ctx, m)
        den = (jnp.sum(_lanewise(jnp.add, p_loc) + _lanewise(jnp.add, p_ctx), axis=-1, keepdims=True)
               + jnp.exp(snk - m[:, 0:1]))
        acc = (jnp.dot(p_loc.astype(BF16), vw, preferred_element_type=F32)
               + jnp.dot(p_ctx.astype(BF16), v0_ref[0], preferred_element_type=F32))
        o_ref[:, g * HEAD_DIM:(g + 1) * HEAD_DIM] = (acc / den).astype(o_ref.dtype)


def _window(q, k, v, k0, v0, sink):
    n_h, kvh = q.shape[0], k.shape[0]
    groups = n_h // kvh
    nq = DEC_SEQ // WIN_TQ
    full = pl.BlockSpec((1, DEC_SEQ, HEAD_DIM), lambda b, h, n: (h, b, 0))
    ctx_spec = pl.BlockSpec((1, PAST_LEN, HEAD_DIM), lambda b, h, n: (h, b, 0))
    return pl.pallas_call(
        functools.partial(_window_kernel, groups=groups),
        grid=(DEC_BATCH, kvh, nq),
        in_specs=[pl.BlockSpec(memory_space=pltpu.SMEM),
                  pl.BlockSpec((groups, WIN_TQ, HEAD_DIM), lambda b, h, n: (h, b * nq + n, 0)),
                  full, full, ctx_spec, ctx_spec],
        out_specs=pl.BlockSpec((WIN_TQ, groups * HEAD_DIM), lambda b, h, n: (b * nq + n, h)),
        out_shape=jax.ShapeDtypeStruct((N_LAT, n_h * HEAD_DIM), BF16),
        compiler_params=_params("parallel", "parallel", "parallel"),
        name="window_attention",
    )(sink, q, k, v, k0, v0)


def _ret_kernel(*refs, reverse, has_prev, nseq):
    lg_ref, gn_ref, s0_ref, q_ref, k_ref, v_ref, gate_ref = refs[:7]
    refs = refs[7:]
    if has_prev:
        prev_ref, refs = refs[0], refs[1:]
    o_ref, sfin_ref, s_sc = refs
    c = pl.program_id(1)

    @pl.when(c == 0)
    def _():
        s_sc[...] = s0_ref[...]

    cs = RET_CHUNK
    row = lax.broadcasted_iota(jnp.int32, (cs, cs), 0).astype(F32)
    col = lax.broadcasted_iota(jnp.int32, (cs, cs), 1).astype(F32)
    pos = lax.broadcasted_iota(jnp.int32, (cs, 1), 0).astype(F32)
    diff = (col - row) if reverse else (row - col)
    scan_pos = (cs - 1.0 - pos) if reverse else pos
    for h in range(RET_H):
        lg = lg_ref[h]
        dmask = jnp.where(diff >= 0, jnp.exp(jnp.maximum(diff, 0.0) * lg), 0.0)
        q_dec = jnp.exp((scan_pos + 1.0) * lg)
        k_dec = jnp.exp((cs - 1.0 - scan_pos) * lg)
        c_dec = jnp.exp(cs * lg)
        for e in range(nseq):
            q = q_ref[e, 0, :, h * RET_DK:(h + 1) * RET_DK]
            k = k_ref[e, 0, :, h * RET_DK:(h + 1) * RET_DK] * RET_K_SCALE
            v = v_ref[e, 0, :, h * RET_DV:(h + 1) * RET_DV].astype(BF16)
            s = s_sc[e, h]
            a = lax.dot_general(q.astype(BF16), k.astype(BF16), (((1,), (1,)), ((), ())),
                                preferred_element_type=F32) * dmask
            o = (jnp.dot(a.astype(BF16), v, preferred_element_type=F32)
                 + jnp.dot((q * q_dec).astype(BF16), s.astype(BF16), preferred_element_type=F32))
            kd_t = jnp.transpose(k * k_dec).astype(BF16)
            s_sc[e, h] = s * c_dec + jnp.dot(kd_t, v, preferred_element_type=F32)
            mu = jnp.mean(o, axis=-1, keepdims=True)
            var = jnp.mean(jnp.square(o - mu), axis=-1, keepdims=True)
            y = (o - mu) * lax.rsqrt(var + EPS) * gn_ref[:, h * RET_DV:(h + 1) * RET_DV]
            y = jax.nn.silu(gate_ref[e, 0, :, h * RET_DV:(h + 1) * RET_DV]) * y
            if has_prev:
                y = prev_ref[e, 0, :, h * RET_DV:(h + 1) * RET_DV] + y
            o_ref[e, 0, :, h * RET_DV:(h + 1) * RET_DV] = y.astype(o_ref.dtype)

    @pl.when(c == pl.num_programs(1) - 1)
    def _():
        sfin_ref[...] = s_sc[...]


def _retention(z, log_gamma, gn, s0, prev, *, latent, reverse):
    batch, t, nseq = (DEC_BATCH, DEC_SEQ, DEC_BATCH) if latent else (BATCH, SEQ, 4)
    nc = t // RET_CHUNK
    seq0 = N_CTX // t if latent else 0
    gate_name = "rgb" if reverse else "rgf"
    w = RET_H * RET_DV
    z4 = z.reshape(N_TOK // t, nc, RET_CHUNK, Z_WIDTH)

    def chunk(c):
        return nc - 1 - c if reverse else c

    def zc(name, width):
        return pl.BlockSpec((pl.Element(nseq), pl.Element(1), pl.Element(RET_CHUNK), pl.Element(width)),
                            lambda g, c: (seq0 + g * nseq, chunk(c), 0, Z_OFF[name]))

    rows = pl.BlockSpec((nseq, 1, RET_CHUNK, w), lambda g, c: (g, chunk(c), 0, 0))
    state = pl.BlockSpec((nseq, RET_H, RET_DK, RET_DV), lambda g, c: (g, 0, 0, 0))
    in_specs = [pl.BlockSpec(memory_space=pltpu.SMEM), pl.BlockSpec((1, w), lambda g, c: (0, 0)), state,
                zc("rq", 256), zc("rk", 256), zc("rv", 512), zc(gate_name, 512)]
    args = [log_gamma, gn, s0, z4, z4, z4, z4]
    if prev is not None:
        in_specs.append(rows)
        args.append(prev.reshape(batch, nc, RET_CHUNK, w))
    o, s_fin = pl.pallas_call(
        functools.partial(_ret_kernel, reverse=reverse, has_prev=prev is not None, nseq=nseq),
        grid=(batch // nseq, nc),
        in_specs=in_specs,
        out_specs=[rows, state],
        out_shape=[jax.ShapeDtypeStruct((batch, nc, RET_CHUNK, w), BF16 if reverse else F32),
                   jax.ShapeDtypeStruct((batch, RET_H, RET_DK, RET_DV), F32)],
        scratch_shapes=[pltpu.VMEM((nseq, RET_H, RET_DK, RET_DV), F32)],
        compiler_params=_params("parallel", "arbitrary"),
        name="retention_bwd" if reverse else "retention_fwd",
    )(*args)
    return o.reshape(batch * t, w), s_fin


def _merge_kernel(*refs, n_x, tm):
    x_refs, refs = refs[:n_x], refs[n_x:]
    ctx_refs, lat_refs = refs[:N_BRANCH], refs[N_BRANCH:2 * N_BRANCH]
    (gate_ref, wb_ref, wo_ref, mod_ref, g_ref, wr_ref, br_ref, xo_ref, h_ref, lg_ref) = refs[2 * N_BRANCH:]
    i = pl.program_id(0)
    is_ctx = i < N_CTX // tm

    merged = None
    for n in range(N_BRANCH):
        branch = jnp.where(is_ctx, ctx_refs[n][...], lat_refs[n][...])
        t = jnp.dot(branch, wb_ref[n], preferred_element_type=F32)
        term = jax.nn.sigmoid(gate_ref[:, n * D_MODEL:(n + 1) * D_MODEL]) * t
        merged = term if merged is None else merged + term

    m = mod_ref[0]
    out = jnp.dot(merged.astype(BF16), wo_ref[...], preferred_element_type=F32)
    x = _token_tile(x_refs, i, tm) + m[:, 2 * D_MODEL:3 * D_MODEL] * out
    xo_ref[...] = x
    h = (_rms(x) * g_ref[...]) * (1.0 + m[:, 4 * D_MODEL:5 * D_MODEL]) + m[:, 3 * D_MODEL:4 * D_MODEL]
    h_ref[...] = h
    lg_ref[...] = jnp.dot(h.astype(BF16), wr_ref[...], preferred_element_type=F32) + br_ref[...]


def _merge(xs, ctx_branches, lat_branches, z, w_branch, w_out, mod, gain, w_router, b_router):
    tm = 256
    n_ct = N_CTX // tm
    row = lambda i: (i, 0)
    const2 = lambda i: (0, 0)
    resident = pl.Buffered(1)
    return pl.pallas_call(
        functools.partial(_merge_kernel, n_x=len(xs), tm=tm),
        grid=(N_TOK // tm,),
        in_specs=_token_specs(xs, tm, lambda i: i)
        + [pl.BlockSpec((tm, BRANCH_W), lambda i: (jnp.minimum(i, n_ct - 1), 0))] * N_BRANCH
        + [pl.BlockSpec((tm, BRANCH_W), lambda i: (jnp.maximum(i - n_ct, 0), 0))] * N_BRANCH
        + [pl.BlockSpec((pl.Element(tm), pl.Element(N_BRANCH * D_MODEL)),
                        lambda i: (pl.multiple_of(i * tm, tm), Z_OFF["gate"])),
           pl.BlockSpec((N_BRANCH, BRANCH_W, D_MODEL), lambda i: (0, 0, 0), pipeline_mode=resident),
           pl.BlockSpec((D_MODEL, D_MODEL), const2, pipeline_mode=resident),
           pl.BlockSpec((1, 1, 6 * D_MODEL), lambda i: (_cond_row(i, tm), 0, 0)),
           pl.BlockSpec((1, D_MODEL), const2),
           pl.BlockSpec((D_MODEL, ROUTER_W), const2),
           pl.BlockSpec((1, ROUTER_W), const2)],
        out_specs=[pl.BlockSpec((tm, D_MODEL), row), pl.BlockSpec((tm, D_MODEL), row),
                   pl.BlockSpec((tm, ROUTER_W), row)],
        out_shape=[jax.ShapeDtypeStruct((N_TOK, D_MODEL), F32), jax.ShapeDtypeStruct((N_TOK, D_MODEL), F32),
                   jax.ShapeDtypeStruct((N_TOK, ROUTER_W), F32)],
        compiler_params=_params("parallel"),
        name="merge_out_proj",
    )(*xs, *ctx_branches, *lat_branches, z, w_branch, w_out, mod, gain, w_router, b_router)


def _route_kernel(lg_ref, rec_ref, cnt_ref, tri_sc, carry_sc, *, tm):
    @pl.when(pl.program_id(0) == 0)
    def _():
        r = lax.broadcasted_iota(jnp.int32, (tm, tm), 0)
        c = lax.broadcasted_iota(jnp.int32, (tm, tm), 1)
        tri_sc[...] = (c < r).astype(BF16)
        carry_sc[...] = jnp.zeros(carry_sc.shape, F32)

    lg = lg_ref[...]
    lane = lax.broadcasted_iota(jnp.int32, lg.shape, 1)
    big = jnp.int32(ROUTER_W)

    def first_max(vals):
        top = jnp.max(vals, axis=-1, keepdims=True)
        return top, jnp.min(jnp.where(vals == top, lane, big), axis=-1, keepdims=True)

    g_logits = jnp.where(lane < N_GROUPS, lg, -jnp.inf)
    g_top, grp = first_max(g_logits)
    p_grp = 1.0 / jnp.sum(jnp.exp(g_logits - g_top), axis=-1, keepdims=True)
    lo = N_GROUPS + grp * EXPERTS_PER_GROUP
    e_logits = jnp.where((lane >= lo) & (lane < lo + EXPERTS_PER_GROUP), lg, -jnp.inf)
    v0, i0 = first_max(e_logits)
    v1, i1 = first_max(jnp.where(lane == i0, -jnp.inf, e_logits))
    e1 = jnp.exp(v1 - v0)
    den = 1.0 + e1
    w0 = p_grp * (1.0 / den)
    w1 = p_grp * (e1 / den)

    hot0 = (lane == i0).astype(F32)
    hot1 = (lane == i1).astype(F32)
    before = jnp.dot(tri_sc[...], (hot0 + hot1).astype(BF16), preferred_element_type=F32) + carry_sc[...]
    rank0 = jnp.sum(before * hot0, axis=-1, keepdims=True)
    rank1 = jnp.sum(before * hot1, axis=-1, keepdims=True)
    carry_sc[...] += jnp.sum(hot0 + hot1, axis=0, keepdims=True)

    rec = jnp.zeros(lg.shape, F32)
    for slot, val in ((R_W0, w0), (R_W1, w1), (R_E0, (i0 - N_GROUPS).astype(F32)),
                      (R_E1, (i1 - N_GROUPS).astype(F32)), (R_RANK0, rank0), (R_RANK1, rank1)):
        rec = jnp.where(lane == slot, val, rec)
    rec_ref[...] = rec
    cnt_ref[...] = jnp.broadcast_to(carry_sc[...], cnt_ref.shape)


def _route(logits):
    tm = 512
    rec, cnt = pl.pallas_call(
        functools.partial(_route_kernel, tm=tm),
        grid=(N_TOK // tm,),
        in_specs=[pl.BlockSpec((tm, ROUTER_W), lambda i: (i, 0))],
        out_specs=[pl.BlockSpec((tm, ROUTER_W), lambda i: (i, 0)), pl.BlockSpec((8, ROUTER_W), lambda i: (0, 0))],
        out_shape=[jax.ShapeDtypeStruct((N_TOK, ROUTER_W), F32), jax.ShapeDtypeStruct((8, ROUTER_W), F32)],
        scratch_shapes=[pltpu.VMEM((tm, tm), BF16), pltpu.VMEM((1, ROUTER_W), F32)],
        compiler_params=_params("arbitrary"),
        name="moe_route",
    )(logits)
    expert = rec[:, R_E0:R_E1 + 1].astype(jnp.int32)
    rank = rec[:, R_RANK0:R_RANK1 + 1].astype(jnp.int32)
    counts = cnt[0, N_GROUPS:N_GROUPS + N_EXPERTS].astype(jnp.int32)
    padded = (counts + MOE_BLK - 1) // MOE_BLK * MOE_BLK
    pad_end = jnp.cumsum(padded)
    dest = ((pad_end - padded)[expert] + rank).reshape(-1)
    token = jnp.arange(MOE_ASSIGN, dtype=jnp.int32) // TOP_K
    src = (jnp.arange(MOE_CAP, dtype=jnp.int32) % N_TOK).at[dest].set(token)
    blk_start = jnp.arange(MOE_NBLK, dtype=jnp.int32) * MOE_BLK
    blk_e = jnp.minimum(jnp.sum(pad_end[None, :] <= blk_start[:, None], axis=1), N_EXPERTS - 1).astype(jnp.int32)
    n_used = (pad_end[-1:] // MOE_BLK).astype(jnp.int32)
    ids = jnp.arange(N_EXPERTS, dtype=jnp.int32)
    later = jnp.where((ids[None, :] > ids[:, None]) & (counts[None, :] > 0), ids[None, :], N_EXPERTS)
    nxt = jnp.min(later, axis=1)
    nxt_e = jnp.where(nxt < N_EXPERTS, nxt, -1).astype(jnp.int32)[blk_e]
    return rec, dest, src, blk_e, nxt_e, n_used


def _start_row_gather(src_hbm, dst, sem, index_of, n_rows):
    def issue(r, carry):
        pltpu.make_async_copy(src_hbm.at[pl.ds(index_of(r), 1)], dst.at[pl.ds(r, 1)], sem).start()
        return carry

    lax.fori_loop(0, n_rows, issue, 0, unroll=8)


def _wait_row_gather(src_hbm, dst, sem, n_rows):
    pltpu.make_async_copy(src_hbm.at[pl.ds(0, n_rows)], dst, sem).wait()


def _expert_kernel(be_ref, nx_ref, nu_ref, x_ref, wg_hbm, wu_hbm, wd_hbm, y_ref,
                   stage_g, stage_u, stage_d, res_g, res_u, res_d, wsem, *, layer):
    i = pl.program_id(0)
    n_used = nu_ref[0]

    def weight_copies(e):
        return [pltpu.make_async_copy(w.at[layer, e], st, wsem.at[k])
                for k, (w, st) in enumerate(((wg_hbm, stage_g), (wu_hbm, stage_u), (wd_hbm, stage_d)))]

    def to_bf16(stage, res):
        rows = 256

        def body(c, carry):
            r = pl.multiple_of(c * rows, rows)
            res[pl.ds(r, rows), :] = stage[pl.ds(r, rows), :].astype(BF16)
            return carry

        lax.fori_loop(0, stage.shape[0] // rows, body, 0)

    @pl.when(i < n_used)
    def _():
        e = be_ref[i]
        first = (i == 0) | (e != be_ref[jnp.maximum(i - 1, 0)])

        @pl.when(i == 0)
        def _():
            for cp in weight_copies(e):
                cp.start()

        @pl.when(first)
        def _():
            for cp in weight_copies(e):
                cp.wait()
            to_bf16(stage_g, res_g)
            to_bf16(stage_u, res_u)
            to_bf16(stage_d, res_d)
            nxt = nx_ref[i]

            @pl.when(nxt >= 0)
            def _():
                for cp in weight_copies(nxt):
                    cp.start()

        x = x_ref[...]
        g = jnp.dot(x, res_g[...], preferred_element_type=F32)
        u = jnp.dot(x, res_u[...], preferred_element_type=F32)
        a = (jax.nn.silu(g) * u).astype(BF16)
        y_ref[...] = jnp.dot(a, res_d[...], preferred_element_type=F32)

    @pl.when(i >= n_used)
    def _():
        y_ref[...] = jnp.zeros(y_ref.shape, F32)


def _experts(xp, blk_e, nxt_e, n_used, w_gate, w_up, w_down, layer):
    any_spec = pl.BlockSpec(memory_space=pl.ANY)
    return pl.pallas_call(
        functools.partial(_expert_kernel, layer=layer),
        grid_spec=pltpu.PrefetchScalarGridSpec(
            num_scalar_prefetch=3,
            grid=(MOE_NBLK,),
            in_specs=[pl.BlockSpec((MOE_BLK, D_MODEL), lambda i, be, nx, nu: (jnp.minimum(i, nu[0] - 1), 0)),
                      any_spec, any_spec, any_spec],
            out_specs=pl.BlockSpec((MOE_BLK, D_MODEL), lambda i, *_: (i, 0)),
            scratch_shapes=[pltpu.VMEM((D_MODEL, D_EXPERT), F32), pltpu.VMEM((D_MODEL, D_EXPERT), F32),
                            pltpu.VMEM((D_EXPERT, D_MODEL), F32),
                            pltpu.VMEM((D_MODEL, D_EXPERT), BF16), pltpu.VMEM((D_MODEL, D_EXPERT), BF16),
                            pltpu.VMEM((D_EXPERT, D_MODEL), BF16),
                            pltpu.SemaphoreType.DMA((3,))]),
        out_shape=jax.ShapeDtypeStruct((MOE_CAP, D_MODEL), F32),
        compiler_params=_params("arbitrary"),
        name="moe_experts",
    )(blk_e, nxt_e, n_used, xp, w_gate, w_up, w_down)


DISPATCH_ROWS = 512


def _dispatch_kernel(src_ref, h_hbm, xp_ref, buf, sem):
    i = pl.program_id(0)

    def gather_start(tile, slot):
        _start_row_gather(h_hbm, buf.at[slot], sem.at[slot], lambda r: src_ref[tile * DISPATCH_ROWS + r],
                          DISPATCH_ROWS)

    @pl.when(i == 0)
    def _():
        gather_start(0, 0)

    slot = i % 2
    _wait_row_gather(h_hbm, buf.at[slot], sem.at[slot], DISPATCH_ROWS)

    @pl.when(i + 1 < pl.num_programs(0))
    def _():
        gather_start(i + 1, 1 - slot)

    xp_ref[...] = buf[slot].astype(BF16)


def _dispatch(h, src):
    return pl.pallas_call(
        _dispatch_kernel,
        grid_spec=pltpu.PrefetchScalarGridSpec(
            num_scalar_prefetch=1,
            grid=(MOE_CAP // DISPATCH_ROWS,),
            in_specs=[pl.BlockSpec(memory_space=pl.ANY)],
            out_specs=pl.BlockSpec((DISPATCH_ROWS, D_MODEL), lambda i, s: (i, 0)),
            scratch_shapes=[pltpu.VMEM((2, DISPATCH_ROWS, D_MODEL), F32), pltpu.SemaphoreType.DMA((2,))]),
        out_shape=jax.ShapeDtypeStruct((MOE_CAP, D_MODEL), BF16),
        compiler_params=_params("arbitrary"),
        name="moe_dispatch",
    )(src, h)


def _combine_kernel(*refs, tm, final):
    dest_ref, yp_hbm, x_ref, rec_ref, mod_ref = refs[:5]
    refs = refs[5:]
    if final:
        fn_ref, yc_ref, yl_ref = refs[:3]
    else:
        xo_ref = refs[0]
    ybuf, sem = refs[-2:]
    i = pl.program_id(0)

    def gather_start(tile, slot):
        for k in range(TOP_K):
            _start_row_gather(yp_hbm, ybuf.at[slot, k], sem.at[slot, k],
                              lambda r, k=k: dest_ref[(tile * tm + r) * TOP_K + k], tm)

    @pl.when(i == 0)
    def _():
        gather_start(0, 0)

    slot = i % 2
    for k in range(TOP_K):
        _wait_row_gather(yp_hbm, ybuf.at[slot, k], sem.at[slot, k], tm)

    @pl.when(i + 1 < pl.num_programs(0))
    def _():
        gather_start(i + 1, 1 - slot)

    rec = rec_ref[...]
    y = ybuf[slot, 0] * rec[:, R_W0:R_W0 + 1] + ybuf[slot, 1] * rec[:, R_W1:R_W1 + 1]
    x = x_ref[...] + mod_ref[0][:, 5 * D_MODEL:6 * D_MODEL] * y
    if final:
        y = _rms(x) * fn_ref[...]

        @pl.when(i < N_CTX // tm)
        def _():
            yc_ref[...] = y

        @pl.when(i >= N_CTX // tm)
        def _():
            yl_ref[...] = y
    else:
        xo_ref[...] = x


def _combine(yp, dest, x, rec, mod, final_gain):
    tm = 256
    n_ct = N_CTX // tm
    final = final_gain is not None
    row = lambda i, d: (i, 0)
    in_specs = [pl.BlockSpec(memory_space=pl.ANY),
                pl.BlockSpec((tm, D_MODEL), row),
                pl.BlockSpec((tm, ROUTER_W), row),
                pl.BlockSpec((1, 1, 6 * D_MODEL), lambda i, d: (_cond_row(i, tm), 0, 0))]
    args = [dest, yp, x, rec, mod]
    if final:
        in_specs.append(pl.BlockSpec((1, D_MODEL), lambda i, d: (0, 0)))
        args.append(final_gain)
        out_specs = [pl.BlockSpec((tm, D_MODEL), lambda i, d: (jnp.minimum(i, n_ct - 1), 0)),
                     pl.BlockSpec((tm, D_MODEL), lambda i, d: (jnp.maximum(i - n_ct, 0), 0))]
        out_shape = [jax.ShapeDtypeStruct((N_CTX, D_MODEL), F32), jax.ShapeDtypeStruct((N_LAT, D_MODEL), F32)]
    else:
        out_specs = pl.BlockSpec((tm, D_MODEL), row)
        out_shape = jax.ShapeDtypeStruct((N_TOK, D_MODEL), F32)
    return pl.pallas_call(
        functools.partial(_combine_kernel, tm=tm, final=final),
        grid_spec=pltpu.PrefetchScalarGridSpec(
            num_scalar_prefetch=1,
            grid=(N_TOK // tm,),
            in_specs=in_specs,
            out_specs=out_specs,
            scratch_shapes=[pltpu.VMEM((2, TOP_K, tm, D_MODEL), F32), pltpu.SemaphoreType.DMA((2, TOP_K))]),
        out_shape=out_shape,
        compiler_params=_params("arbitrary"),
        name="moe_combine",
    )(*args)


def _swap_halves(w, chunk):
    k, n = w.shape
    w = w.reshape(k, n // chunk, 2, chunk // 2)
    return w[:, :, ::-1, :].reshape(k, n)


def _rope_tables(rot_dim):
    rows = DEC_SEQ // GRID_W
    row = jnp.repeat(jnp.arange(rows), GRID_W).astype(F32)
    col = (jnp.arange(rows * GRID_W) % GRID_W).astype(F32)
    axis_dim = rot_dim // 2
    inv = ROPE_THETA ** (-jnp.arange(0, axis_dim, 2, dtype=F32) / axis_dim)
    ang = jnp.concatenate([row[:, None] * inv, col[:, None] * inv], axis=-1)
    c, s = jnp.cos(ang), jnp.sin(ang)
    return jnp.concatenate([c, c], axis=-1), jnp.concatenate([-s, s], axis=-1)


def _head_major(cache):
    b, l, kvh, d = cache.shape
    return jnp.transpose(cache, (2, 0, 1, 3)).reshape(kvh, b * l, d).astype(BF16)


def kernel(x_prompt, x_sample, c, cache_mla_ckv, cache_mla_krope, cache_gqa_k, cache_gqa_v, cache_swa_k,
           cache_swa_v, state_ret, c_ctx, ada_w, ada_b, norm_mix, norm_ffn, w_in, mla_q_norm, mla_kv_norm,
           mla_w_uq, mla_w_ukv, gqa_q_norm, gqa_k_norm, ret_decay_logit, ret_gn, swa_sink, w_branch, w_out,
           router_group_w, router_group_b, router_expert_w, router_expert_b, moe_w_gate, moe_w_up, moe_w_down,
           final_norm):
    xs = (x_prompt.reshape(N_CTX, D_MODEL), x_sample.reshape(N_LAT, D_MODEL))
    cond = jnp.concatenate([c_ctx[None, :], c, jnp.zeros((N_COND - 1 - DEC_BATCH, D_MODEL), F32)], axis=0)
    mod_all = _ada(cond, ada_w, ada_b).reshape(DEPTH, N_COND, 1, 6 * D_MODEL)

    w_in_r = _w_in_layout(w_in)
    tabs = _rope_tables(HEAD_DIM) + _rope_tables(MLA_ROPE)
    log_gamma = jax.nn.log_sigmoid(ret_decay_logit.astype(F32))
    uq = mla_w_uq.reshape(DEPTH, MLA_Q_LORA, MLA_H, MLA_DK)
    ukv = mla_w_ukv.reshape(DEPTH, MLA_KV_LORA, MLA_H, MLA_NOPE + MLA_V)
    w_router = jnp.concatenate(
        [router_group_w, router_expert_w,
         jnp.zeros((DEPTH, D_MODEL, ROUTER_W - N_GROUPS - N_EXPERTS), F32)], axis=-1).astype(BF16)
    b_router = jnp.concatenate(
        [router_group_b, router_expert_b, jnp.zeros((DEPTH, ROUTER_W - N_GROUPS - N_EXPERTS), F32)], axis=-1)
    zero_state = jnp.zeros((BATCH, RET_H, RET_DK, RET_DV), F32)

    caches = [[] for _ in range(7)]
    for l in range(DEPTH):
        mod = mod_all[l]
        uq_rope = uq[l, :, :, MLA_NOPE:].reshape(MLA_Q_LORA, MLA_H * MLA_ROPE)
        lw = dict(
            q_norm=mla_q_norm[l].reshape(1, -1), kv_norm=mla_kv_norm[l].reshape(1, -1),
            gq_norm=gqa_q_norm[l].reshape(1, -1), gk_norm=gqa_k_norm[l].reshape(1, -1),
            w_uq_nope=uq[l, :, :, :MLA_NOPE].reshape(MLA_Q_LORA, MLA_H * MLA_NOPE).astype(BF16),
            w_uq_rope=uq_rope.astype(BF16), w_uq_swap=_swap_halves(uq_rope, MLA_ROPE).astype(BF16),
            w_uk=ukv[l, :, :, :MLA_NOPE].reshape(MLA_KV_LORA, MLA_H * MLA_NOPE).astype(BF16),
            w_uv=ukv[l, :, :, MLA_NOPE:].reshape(MLA_KV_LORA, MLA_H * MLA_V).astype(BF16))

        z = _in_proj(xs, mod, norm_mix[l].reshape(1, -1), w_in_r, l)

        ctx_ops = _prep(z, lw, None, latent=False)
        ckv, kr, kg, vg, ks, vs = ctx_ops[9:]
        oa_c, ob_c, od_c = _ctx_attention(ctx_ops[:9], swa_sink[l])
        gn = ret_gn[l]
        of_c, s_f = _retention(z, log_gamma[l, 0], gn[0:1], zero_state, None, latent=False, reverse=False)
        oc_c, s_b = _retention(z, log_gamma[l, 1], gn[1:2], zero_state, of_c, latent=False, reverse=True)

        (qa, ka, va, qb, kb, vb, qd, kd, vd) = _prep(z, lw, tabs, latent=True)
        ka0, va0 = _mla_ctx(cache_mla_ckv[:, l].reshape(DEC_BATCH * PAST_LEN, MLA_KV_LORA),
                            cache_mla_krope[:, l].reshape(DEC_BATCH * PAST_LEN, MLA_ROPE), lw)
        oa_l = _flash_latent(qa, ka, va, (ka0, va0))
        ob_l = _flash_latent(qb, kb, vb, (_head_major(cache_gqa_k[:, l]), _head_major(cache_gqa_v[:, l])))
        od_l = _window(qd, kd, vd, _head_major(cache_swa_k[:, l]), _head_major(cache_swa_v[:, l]), swa_sink[l])
        of_l, _ = _retention(z, log_gamma[l, 0], gn[0:1], state_ret[:, l, 0], None, latent=True, reverse=False)
        oc_l, _ = _retention(z, log_gamma[l, 1], gn[1:2], state_ret[:, l, 1], of_l, latent=True, reverse=True)

        x_mid, h_ffn, logits = _merge(xs, (oa_c, ob_c, oc_c, od_c), (oa_l, ob_l, oc_l, od_l), z,
                                      w_branch[l].astype(BF16), w_out[l].astype(BF16), mod,
                                      norm_ffn[l].reshape(1, -1), w_router[l], b_router[l].reshape(1, -1))

        rec, dest, src, blk_e, nxt_e, n_used = _route(logits)
        yp = _experts(_dispatch(h_ffn, src), blk_e, nxt_e, n_used, moe_w_gate, moe_w_up, moe_w_down, l)
        if l < DEPTH - 1:
            xs = (_combine(yp, dest, x_mid, rec, mod, None),)
        else:
            y_ctx, y_lat = _combine(yp, dest, x_mid, rec, mod, final_norm.reshape(1, D_MODEL))

        kvh_shape = (BATCH, SEQ, GQA_KVH, HEAD_DIM)
        for lst, val in zip(caches, (ckv.reshape(BATCH, SEQ, MLA_KV_LORA), kr.reshape(BATCH, SEQ, MLA_ROPE),
                                     kg.reshape(kvh_shape), vg.reshape(kvh_shape), ks.reshape(kvh_shape),
                                     vs.reshape(kvh_shape), jnp.stack([s_f, s_b], axis=1))):
            lst.append(val)

    y_prompt = y_ctx.reshape(BATCH, SEQ, D_MODEL)
    y_sample = y_lat.reshape(DEC_BATCH, DEC_SEQ, D_MODEL)
    return (y_prompt, y_sample) + tuple(jnp.stack(lst, axis=1) for lst in caches)
```

```python
import functools

import numpy as np
import jax
import jax.numpy as jnp
from jax import lax
from jax.experimental import pallas as pl
from jax.experimental.pallas import tpu as pltpu

F32 = jnp.float32
BF16 = jnp.bfloat16
LANES = 128

D_MODEL = 2048
BATCH = 16
SEQ = 256
DEPTH = 4
DEC_BATCH = 2
DEC_SEQ = 4096
PAST_LEN = 256
GRID_W = 64
ROPE_THETA = 10000.0
EPS = 1e-6
QBLK = 128
NEG_INF = -1e30

MLA_H = 4
MLA_Q_LORA = 512
MLA_KV_LORA = 256
MLA_NOPE = 128
MLA_ROPE = 64
MLA_V = 128
MLA_SCALE = (MLA_NOPE + MLA_ROPE) ** -0.5
MLA_DK = MLA_NOPE + MLA_ROPE

HEAD_DIM = 128
ATTN_SCALE = HEAD_DIM ** -0.5
GQA_H = 4
GQA_KVH = 2
SWA_H = 4
SWA_KVH = 2
WINDOW = 128

RET_H = 4
RET_DK = 64
RET_DV = 128
RET_CHUNK = 128
RET_K_SCALE = RET_DK ** -0.5

N_BRANCH = 4
BRANCH_W = 512

N_GROUPS = 4
EXPERTS_PER_GROUP = 8
N_EXPERTS = N_GROUPS * EXPERTS_PER_GROUP
TOP_K = 2
D_EXPERT = 1024

N_CTX = BATCH * SEQ
N_LAT = DEC_BATCH * DEC_SEQ
N_TOK = N_CTX + N_LAT
N_COND = 8

W_IN_HEAD = MLA_Q_LORA + MLA_KV_LORA + MLA_ROPE
IN_WIDTH = (W_IN_HEAD + (GQA_H + 2 * GQA_KVH) * HEAD_DIM + 2 * RET_H * RET_DK + 3 * RET_H * RET_DV
            + (SWA_H + 2 * SWA_KVH) * HEAD_DIM + N_BRANCH * D_MODEL)
Z_TAIL = 1024
Z_WIDTH = Z_TAIL + IN_WIDTH - W_IN_HEAD
Z_OFF = dict(mq=0, mkv=512, mkr=768, gq=1024, gk=1536, gv=1792, rq=2048, rk=2304, rv=2560, rgf=3072, rgb=3584,
             sq=4096, sk=4608, sv=4864, gate=5120)
Z_TILE = 1024

MOE_BLK = 256
MOE_ASSIGN = N_TOK * TOP_K
MOE_NBLK = (MOE_ASSIGN + N_EXPERTS * (MOE_BLK - 1) + MOE_BLK - 1) // MOE_BLK
MOE_CAP = MOE_NBLK * MOE_BLK
ROUTER_W = 128
R_W0, R_W1, R_E0, R_E1, R_RANK0, R_RANK1 = range(6)

VMEM_LIMIT = 56 * 1024 * 1024


def _params(*sem):
    return pltpu.CompilerParams(dimension_semantics=sem, vmem_limit_bytes=VMEM_LIMIT)


def _cond_row(i, tm):
    start = i * tm
    return jnp.where(start < N_CTX, 0, 1 + (start - N_CTX) // DEC_SEQ)


def _rms(x):
    return x * lax.rsqrt(jnp.mean(x * x, axis=-1, keepdims=True) + EPS)


def _token_specs(xs, tm, index_of):
    if len(xs) == 1:
        return [pl.BlockSpec((tm, D_MODEL), lambda *g: (index_of(*g), 0))]
    n_ct = N_CTX // tm
    return [pl.BlockSpec((tm, D_MODEL), lambda *g: (jnp.minimum(index_of(*g), n_ct - 1), 0),
                         pipeline_mode=pl.Buffered(1)),
            pl.BlockSpec((tm, D_MODEL), lambda *g: (jnp.maximum(index_of(*g) - n_ct, 0), 0),
                         pipeline_mode=pl.Buffered(1))]


def _token_tile(x_refs, i, tm):
    if len(x_refs) == 1:
        return x_refs[0][...]
    return jnp.where(i < N_CTX // tm, x_refs[0][...], x_refs[1][...])


def _ada_kernel(c_ref, w_ref, b_ref, o_ref):
    s = jax.nn.silu(c_ref[...]).astype(BF16)
    o_ref[0] = jnp.dot(s, w_ref[0].astype(BF16), preferred_element_type=F32) + b_ref[0]


def _ada(cond, ada_w, ada_b):
    tn = 1024
    n_out = 6 * D_MODEL
    return pl.pallas_call(
        _ada_kernel,
        grid=(DEPTH, n_out // tn),
        in_specs=[pl.BlockSpec((N_COND, D_MODEL), lambda l, j: (0, 0)),
                  pl.BlockSpec((1, D_MODEL, tn), lambda l, j: (l, 0, j)),
                  pl.BlockSpec((1, 1, tn), lambda l, j: (l, 0, j))],
        out_specs=pl.BlockSpec((1, N_COND, tn), lambda l, j: (l, 0, j)),
        out_shape=jax.ShapeDtypeStruct((DEPTH, N_COND, n_out), F32),
        compiler_params=_params("parallel", "parallel"),
        name="ada_mod",
    )(cond, ada_w, ada_b.reshape(DEPTH, 1, n_out))


def _w_in_kernel(w_ref, o_ref):
    w = w_ref[0]

    @pl.when(pl.program_id(1) == 0)
    def _():
        half = MLA_ROPE // 2
        mkr0 = W_IN_HEAD - MLA_ROPE
        o_ref[0, 0:W_IN_HEAD] = w[0:W_IN_HEAD].astype(BF16)
        o_ref[0, W_IN_HEAD:W_IN_HEAD + half] = w[mkr0 + half:W_IN_HEAD].astype(BF16)
        o_ref[0, W_IN_HEAD + half:W_IN_HEAD + MLA_ROPE] = w[mkr0:mkr0 + half].astype(BF16)
        o_ref[0, W_IN_HEAD + MLA_ROPE:Z_TAIL] = jnp.zeros((Z_TAIL - W_IN_HEAD - MLA_ROPE, w.shape[1]), BF16)

    @pl.when(pl.program_id(1) > 0)
    def _():
        o_ref[0] = w.astype(BF16)


def _w_in_layout(w_in):
    w_t = jnp.swapaxes(w_in, 1, 2)

    def src_row(j):
        return pl.multiple_of(jnp.where(j > 0, (j - 1) * Z_TILE + W_IN_HEAD, 0), 16)

    return pl.pallas_call(
        _w_in_kernel,
        grid=(DEPTH, Z_WIDTH // Z_TILE),
        in_specs=[pl.BlockSpec((pl.Element(1), pl.Element(Z_TILE), pl.Element(D_MODEL)),
                               lambda l, j: (l, src_row(j), 0))],
        out_specs=pl.BlockSpec((1, Z_TILE, D_MODEL), lambda l, j: (l, j, 0)),
        out_shape=jax.ShapeDtypeStruct((DEPTH, Z_WIDTH, D_MODEL), BF16),
        compiler_params=_params("parallel", "parallel"),
        name="w_in_layout",
    )(w_t)


def _in_kernel(*refs, n_x, tm):
    x_refs, (mod_ref, g_ref, w_ref, z_ref, h_ref) = refs[:n_x], refs[n_x:]

    @pl.when(pl.program_id(1) == 0)
    def _():
        m = mod_ref[0]
        y = _rms(_token_tile(x_refs, pl.program_id(0), tm)) * g_ref[...]
        h_ref[...] = (y * (1.0 + m[:, D_MODEL:2 * D_MODEL]) + m[:, 0:D_MODEL]).astype(BF16)

    z_ref[...] = lax.dot_general(h_ref[...], w_ref[...], (((1,), (1,)), ((), ())), preferred_element_type=F32)


def _in_proj(xs, mod, gain, w_all, layer):
    tm, tn = 1024, Z_TILE
    return pl.pallas_call(
        functools.partial(_in_kernel, n_x=len(xs), tm=tm),
        grid=(N_TOK // tm, Z_WIDTH // tn),
        in_specs=_token_specs(xs, tm, lambda i, j: i)
        + [pl.BlockSpec((1, 1, 6 * D_MODEL), lambda i, j: (_cond_row(i, tm), 0, 0)),
           pl.BlockSpec((1, D_MODEL), lambda i, j: (0, 0)),
           pl.BlockSpec((None, tn, D_MODEL), lambda i, j: (layer, j, 0))],
        out_specs=pl.BlockSpec((tm, tn), lambda i, j: (i, j)),
        out_shape=jax.ShapeDtypeStruct((N_TOK, Z_WIDTH), F32),
        scratch_shapes=[pltpu.VMEM((tm, D_MODEL), BF16)],
        compiler_params=_params("parallel", "arbitrary"),
        name="in_proj",
    )(*xs, mod, gain, w_all)


def _rope128(x, cos, sin):
    return x * cos + pltpu.roll(x, HEAD_DIM // 2, 1) * sin


def _mla_kv(ckv, kr, wuk_ref, wuv_ref, ka_ref, va_ref):
    cb = ckv.astype(BF16)
    kn = jnp.dot(cb, wuk_ref[...], preferred_element_type=F32)
    vv = jnp.dot(cb, wuv_ref[...], preferred_element_type=F32)
    for h in range(MLA_H):
        ka_ref[h, :, 0:MLA_NOPE] = kn[:, h * MLA_NOPE:(h + 1) * MLA_NOPE].astype(BF16)
        ka_ref[h, :, MLA_NOPE:MLA_DK] = kr.astype(BF16)
        va_ref[h] = vv[:, h * MLA_V:(h + 1) * MLA_V].astype(BF16)


def _prep_kernel(*refs, latent):
    (mq_ref, gq_ref, sq_ref, mkv_ref, gk_ref, gv_ref, sk_ref, sv_ref, mkr_ref,
     qn_ref, kvn_ref, gqn_ref, gkn_ref, wqn_ref, wqr_ref, wqs_ref, wuk_ref, wuv_ref) = refs[:18]
    refs = refs[18:]
    if latent:
        cos_ref, sin_ref, c64_ref, s64_ref = refs[:4]
        refs = refs[4:]
        cos, sin = cos_ref[...], sin_ref[...]
        c64, s64 = c64_ref[...], s64_ref[...]
    qa_ref, ka_ref, va_ref, qb_ref, kb_ref, vb_ref, qd_ref, kd_ref, vd_ref = refs[:9]
    refs = refs[9:]

    hn = (_rms(mq_ref[...]) * qn_ref[...]).astype(BF16)
    q_nope = jnp.dot(hn, wqn_ref[...], preferred_element_type=F32)
    q_rope = jnp.dot(hn, wqr_ref[...], preferred_element_type=F32)
    if latent:
        q_swap = jnp.dot(hn, wqs_ref[...], preferred_element_type=F32)
    for h in range(MLA_H):
        qa_ref[h, :, 0:MLA_NOPE] = (q_nope[:, h * MLA_NOPE:(h + 1) * MLA_NOPE] * MLA_SCALE).astype(BF16)
        qr = q_rope[:, h * MLA_ROPE:(h + 1) * MLA_ROPE]
        if latent:
            qr = qr * c64 + q_swap[:, h * MLA_ROPE:(h + 1) * MLA_ROPE] * s64
        qa_ref[h, :, MLA_NOPE:MLA_DK] = (qr * MLA_SCALE).astype(BF16)
    ckv = _rms(mkv_ref[...]) * kvn_ref[...]
    kr = mkr_ref[:, 0:MLA_ROPE]
    if latent:
        kr = kr * c64 + mkr_ref[:, MLA_ROPE:2 * MLA_ROPE] * s64
    _mla_kv(ckv, kr, wuk_ref, wuv_ref, ka_ref, va_ref)

    gq = gq_ref[...]
    for h in range(GQA_H):
        q = _rms(gq[:, h * HEAD_DIM:(h + 1) * HEAD_DIM]) * gqn_ref[...]
        if latent:
            q = _rope128(q, cos, sin)
        qb_ref[h] = (q * ATTN_SCALE).astype(BF16)
    gk = gk_ref[...]
    gv = gv_ref[...]
    kg = []
    for h in range(GQA_KVH):
        k = _rms(gk[:, h * HEAD_DIM:(h + 1) * HEAD_DIM]) * gkn_ref[...]
        kg.append(k)
        if latent:
            k = _rope128(k, cos, sin)
        kb_ref[h] = k.astype(BF16)
        vb_ref[h] = gv[:, h * HEAD_DIM:(h + 1) * HEAD_DIM].astype(BF16)

    sq = sq_ref[...]
    for h in range(SWA_H):
        q = sq[:, h * HEAD_DIM:(h + 1) * HEAD_DIM]
        if latent:
            q = _rope128(q, cos, sin)
        qd_ref[h] = (q * ATTN_SCALE).astype(BF16)
    sk = sk_ref[...]
    sv = sv_ref[...]
    for h in range(SWA_KVH):
        k = sk[:, h * HEAD_DIM:(h + 1) * HEAD_DIM]
        if latent:
            k = _rope128(k, cos, sin)
        kd_ref[h] = k.astype(BF16)
        vd_ref[h] = sv[:, h * HEAD_DIM:(h + 1) * HEAD_DIM].astype(BF16)

    if not latent:
        ckv_ref, kr_ref, kg_ref, vg_ref, ks_ref, vs_ref = refs
        ckv_ref[...] = ckv
        kr_ref[...] = kr
        for h in range(GQA_KVH):
            kg_ref[:, h * HEAD_DIM:(h + 1) * HEAD_DIM] = kg[h]
        vg_ref[...] = gv
        ks_ref[...] = sk
        vs_ref[...] = sv


def _prep(z, lw, tabs, latent):
    tm = 512
    rows = N_LAT if latent else N_CTX
    roff = (N_CTX if latent else 0) // tm
    n_t = DEC_SEQ // tm

    def zc(name, width):
        cb = Z_OFF[name] // width
        return pl.BlockSpec((tm, width), lambda i: (i + roff, cb))

    def full(a):
        nd = a.ndim
        return pl.BlockSpec(a.shape, lambda i: (0,) * nd)

    weights = [lw["q_norm"], lw["kv_norm"], lw["gq_norm"], lw["gk_norm"],
               lw["w_uq_nope"], lw["w_uq_rope"], lw["w_uq_swap"], lw["w_uk"], lw["w_uv"]]
    in_specs = [zc("mq", 512), zc("gq", 512), zc("sq", 512), zc("mkv", 256), zc("gk", 256), zc("gv", 256),
                zc("sk", 256), zc("sv", 256), zc("mkr", 128)] + [full(a) for a in weights]
    args = [z] * 9 + weights
    if latent:
        in_specs += [pl.BlockSpec((tm, HEAD_DIM), lambda i: (i % n_t, 0))] * 2
        in_specs += [pl.BlockSpec((tm, MLA_ROPE), lambda i: (i % n_t, 0))] * 2
        args += list(tabs)

    def heads(n, d):
        return (pl.BlockSpec((n, tm, d), lambda i: (0, i, 0)), jax.ShapeDtypeStruct((n, rows, d), BF16))

    outs = [heads(MLA_H, MLA_DK), heads(MLA_H, MLA_DK), heads(MLA_H, MLA_V),
            heads(GQA_H, HEAD_DIM), heads(GQA_KVH, HEAD_DIM), heads(GQA_KVH, HEAD_DIM),
            heads(SWA_H, HEAD_DIM), heads(SWA_KVH, HEAD_DIM), heads(SWA_KVH, HEAD_DIM)]
    if not latent:
        kv_w = GQA_KVH * HEAD_DIM
        for d in (MLA_KV_LORA, MLA_ROPE, kv_w, kv_w, kv_w, kv_w):
            outs.append((pl.BlockSpec((tm, d), lambda i: (i, 0)), jax.ShapeDtypeStruct((rows, d), F32)))
    return pl.pallas_call(
        functools.partial(_prep_kernel, latent=latent),
        grid=(rows // tm,),
        in_specs=in_specs,
        out_specs=[o[0] for o in outs],
        out_shape=[o[1] for o in outs],
        compiler_params=_params("parallel"),
        name="prep_latent" if latent else "prep_context",
    )(*args)


def _mla_ctx_kernel(ckv_ref, kr_ref, wuk_ref, wuv_ref, ka_ref, va_ref):
    _mla_kv(ckv_ref[...], kr_ref[...], wuk_ref, wuv_ref, ka_ref, va_ref)


def _mla_ctx(ckv, kr, lw):
    rows = ckv.shape[0]
    return pl.pallas_call(
        _mla_ctx_kernel,
        out_shape=[jax.ShapeDtypeStruct((MLA_H, rows, MLA_DK), BF16),
                   jax.ShapeDtypeStruct((MLA_H, rows, MLA_V), BF16)],
        name="mla_ctx_kv",
    )(ckv, kr, lw["w_uk"], lw["w_uv"])


def _softmax_pv(q, k, v, sink):
    s = lax.dot_general(q, k, (((1,), (1,)), ((), ())), preferred_element_type=F32)
    m = jnp.max(s, axis=-1, keepdims=True)
    if sink is not None:
        m = jnp.maximum(m, sink)
    p = jnp.exp(s - m)
    l = jnp.sum(p, axis=-1, keepdims=True)
    if sink is not None:
        l = l + jnp.exp(sink - m)
    return jnp.dot(p.astype(BF16), v, preferred_element_type=F32) / l


def _ctx_attn_kernel(sink_ref, qa, ka, va, qb, kb, vb, qd, kd, vd, oa, ob, od):
    for h in range(MLA_H):
        oa[:, h * MLA_V:(h + 1) * MLA_V] = _softmax_pv(qa[h], ka[h], va[h], None).astype(BF16)
    for h in range(GQA_H):
        kv = h // (GQA_H // GQA_KVH)
        ob[:, h * HEAD_DIM:(h + 1) * HEAD_DIM] = _softmax_pv(qb[h], kb[kv], vb[kv], None).astype(BF16)
    for h in range(SWA_H):
        kv = h // (SWA_H // SWA_KVH)
        od[:, h * HEAD_DIM:(h + 1) * HEAD_DIM] = _softmax_pv(qd[h], kd[kv], vd[kv], sink_ref[h]).astype(BF16)


def _ctx_attention(operands, sink):
    def spec(a):
        return pl.BlockSpec((a.shape[0], SEQ, a.shape[2]), lambda b: (0, b, 0))

    out = jax.ShapeDtypeStruct((N_CTX, BRANCH_W), BF16)
    return pl.pallas_call(
        _ctx_attn_kernel,
        grid=(BATCH,),
        in_specs=[pl.BlockSpec(memory_space=pltpu.SMEM)] + [spec(a) for a in operands],
        out_specs=[pl.BlockSpec((SEQ, BRANCH_W), lambda b: (b, 0))] * 3,
        out_shape=[out, out, out],
        compiler_params=_params("parallel"),
        name="context_attention",
    )(sink, *operands)


KEY_CHUNK = 256


def _flash_latent_kernel(q_ref, k_ref, v_ref, k0_ref, v0_ref, o_ref, m_sc, l_sc, acc_sc, *, groups):
    dv = v_ref.shape[-1]
    n_chunks = k_ref.shape[1] // KEY_CHUNK
    dn = (((1,), (1,)), ((), ()))

    def scores(g, k):
        return lax.dot_general(q_ref[g], k, dn, preferred_element_type=F32)

    def lane_max(s):
        m = s[:, 0:LANES]
        for c in range(1, s.shape[1] // LANES):
            m = jnp.maximum(m, s[:, c * LANES:(c + 1) * LANES])
        return m

    for g in range(groups):
        m_sc[g] = lane_max(scores(g, k0_ref[0]))

    def max_body(j, carry):
        off = pl.multiple_of(j * KEY_CHUNK, KEY_CHUNK)
        for g in range(groups):
            m_sc[g] = jnp.maximum(m_sc[g], lane_max(scores(g, k_ref[0, pl.ds(off, KEY_CHUNK), :])))
        return carry

    lax.fori_loop(0, n_chunks, max_body, 0, unroll=True)
    for g in range(groups):
        m_sc[g] = jnp.broadcast_to(jnp.max(m_sc[g], axis=-1, keepdims=True), m_sc.shape[1:])

    def accumulate(g, k, v, first):
        s = scores(g, k)
        m = m_sc[g]
        ps = [jnp.exp(s[:, c * LANES:(c + 1) * LANES] - m) for c in range(s.shape[1] // LANES)]
        lsum = ps[0]
        for p in ps[1:]:
            lsum = lsum + p
        pv = jnp.dot(jnp.concatenate(ps, axis=1).astype(BF16), v, preferred_element_type=F32)
        if first:
            l_sc[g] = lsum
            acc_sc[g] = pv
        else:
            l_sc[g] += lsum
            acc_sc[g] += pv

    for g in range(groups):
        accumulate(g, k0_ref[0], v0_ref[0], True)

    def sum_body(j, carry):
        off = pl.multiple_of(j * KEY_CHUNK, KEY_CHUNK)
        for g in range(groups):
            accumulate(g, k_ref[0, pl.ds(off, KEY_CHUNK), :], v_ref[0, pl.ds(off, KEY_CHUNK), :], False)
        return carry

    lax.fori_loop(0, n_chunks, sum_body, 0, unroll=True)
    for g in range(groups):
        l = jnp.sum(l_sc[g], axis=-1, keepdims=True)
        o_ref[:, g * dv:(g + 1) * dv] = (acc_sc[g] / l).astype(o_ref.dtype)


def _flash_latent(q, k, v, ctx):
    tq = 1024
    n_h, _, dk = q.shape
    kvh, _, dv = v.shape
    groups = n_h // kvh
    nq = DEC_SEQ // tq
    return pl.pallas_call(
        functools.partial(_flash_latent_kernel, groups=groups),
        grid=(DEC_BATCH, kvh, nq),
        in_specs=[pl.BlockSpec((groups, tq, dk), lambda b, h, i: (h, b * nq + i, 0)),
                  pl.BlockSpec((1, DEC_SEQ, dk), lambda b, h, i: (h, b, 0)),
                  pl.BlockSpec((1, DEC_SEQ, dv), lambda b, h, i: (h, b, 0)),
                  pl.BlockSpec((1, PAST_LEN, dk), lambda b, h, i: (h, b, 0)),
                  pl.BlockSpec((1, PAST_LEN, dv), lambda b, h, i: (h, b, 0))],
        out_specs=pl.BlockSpec((tq, groups * dv), lambda b, h, i: (b * nq + i, h)),
        out_shape=jax.ShapeDtypeStruct((N_LAT, n_h * dv), BF16),
        scratch_shapes=[pltpu.VMEM((groups, tq, LANES), F32), pltpu.VMEM((groups, tq, LANES), F32),
                        pltpu.VMEM((groups, tq, dv), F32)],
        compiler_params=_params("parallel", "parallel", "parallel"),
        name="flash_latent",
    )(q, k, v, *ctx)


WIN_TQ = 512
WIN_SPAN = WIN_TQ + 2 * WINDOW


def _lanewise(op, x):
    r = x[:, 0:LANES]
    for c in range(1, x.shape[1] // LANES):
        r = op(r, x[:, c * LANES:(c + 1) * LANES])
    return r


def _exp_tiles(s, m):
    return jnp.concatenate([jnp.exp(s[:, c * LANES:(c + 1) * LANES] - m) for c in range(s.shape[1] // LANES)],
                           axis=1)


def _window_kernel(sink_ref, q_ref, k_ref, v_ref, k0_ref, v0_ref, o_ref, *, groups):
    head0 = pl.program_id(1) * groups
    t = k_ref.shape[1]
    q0 = pl.program_id(2) * WIN_TQ
    k_start = pl.multiple_of(jnp.clip(q0 - WINDOW, 0, t - WIN_SPAN), LANES)
    qpos = q0 + lax.broadcasted_iota(jnp.int32, (WIN_TQ, WIN_SPAN), 0)
    kpos = k_start + lax.broadcasted_iota(jnp.int32, (WIN_TQ, WIN_SPAN), 1)
    band = jnp.abs(kpos - qpos) <= WINDOW
    kw = k_ref[0, pl.ds(k_start, WIN_SPAN), :]
    vw = v_ref[0, pl.ds(k_start, WIN_SPAN), :]
    dn = (((1,), (1,)), ((), ()))
    for g in range(groups):
        q = q_ref[g]
        s_loc = jnp.where(band, lax.dot_general(q, kw, dn, preferred_element_type=F32), NEG_INF)
        s_ctx = lax.dot_general(q, k0_ref[0], dn, preferred_element_type=F32)
        snk = sink_ref[head0 + g]
        m = jnp.max(jnp.maximum(_lanewise(jnp.maximum, s_loc), _lanewise(jnp.maximum, s_ctx)),
                    axis=-1, keepdims=True)
        m = jnp.broadcast_to(jnp.maximum(m, snk), (WIN_TQ, LANES))
        p_loc = _exp_tiles(s_loc, m)
        p_ctx = _exp_tiles(s_ctx, m)
        den = (jnp.sum(_lanewise(jnp.add, p_loc) + _lanewise(jnp.add, p_ctx), axis=-1, keepdims=True)
               + jnp.exp(snk - m[:, 0:1]))
        acc = (jnp.dot(p_loc.astype(BF16), vw, preferred_element_type=F32)
               + jnp.dot(p_ctx.astype(BF16), v0_ref[0], preferred_element_type=F32))
        o_ref[:, g * HEAD_DIM:(g + 1) * HEAD_DIM] = (acc / den).astype(o_ref.dtype)


def _window(q, k, v, k0, v0, sink):
    n_h, kvh = q.shape[0], k.shape[0]
    groups = n_h // kvh
    nq = DEC_SEQ // WIN_TQ
    full = pl.BlockSpec((1, DEC_SEQ, HEAD_DIM), lambda b, h, n: (h, b, 0))
    ctx_spec = pl.BlockSpec((1, PAST_LEN, HEAD_DIM), lambda b, h, n: (h, b, 0))
    return pl.pallas_call(
        functools.partial(_window_kernel, groups=groups),
        grid=(DEC_BATCH, kvh, nq),
        in_specs=[pl.BlockSpec(memory_space=pltpu.SMEM),
                  pl.BlockSpec((groups, WIN_TQ, HEAD_DIM), lambda b, h, n: (h, b * nq + n, 0)),
                  full, full, ctx_spec, ctx_spec],
        out_specs=pl.BlockSpec((WIN_TQ, groups * HEAD_DIM), lambda b, h, n: (b * nq + n, h)),
        out_shape=jax.ShapeDtypeStruct((N_LAT, n_h * HEAD_DIM), BF16),
        compiler_params=_params("parallel", "parallel", "parallel"),
        name="window_attention",
    )(sink, q, k, v, k0, v0)


def _ret_kernel(*refs, reverse, has_prev, nseq):
    lg_ref, gn_ref, s0_ref, q_ref, k_ref, v_ref, gate_ref = refs[:7]
    refs = refs[7:]
    if has_prev:
        prev_ref, refs = refs[0], refs[1:]
    o_ref, sfin_ref, s_sc = refs
    c = pl.program_id(1)

    @pl.when(c == 0)
    def _():
        s_sc[...] = s0_ref[...]

    cs = RET_CHUNK
    row = lax.broadcasted_iota(jnp.int32, (cs, cs), 0).astype(F32)
    col = lax.broadcasted_iota(jnp.int32, (cs, cs), 1).astype(F32)
    pos = lax.broadcasted_iota(jnp.int32, (cs, 1), 0).astype(F32)
    diff = (col - row) if reverse else (row - col)
    scan_pos = (cs - 1.0 - pos) if reverse else pos
    for h in range(RET_H):
        lg = lg_ref[h]
        dmask = jnp.where(diff >= 0, jnp.exp(jnp.maximum(diff, 0.0) * lg), 0.0)
        q_dec = jnp.exp((scan_pos + 1.0) * lg)
        k_dec = jnp.exp((cs - 1.0 - scan_pos) * lg)
        c_dec = jnp.exp(cs * lg)
        for e in range(nseq):
            q = q_ref[e, 0, :, h * RET_DK:(h + 1) * RET_DK]
            k = k_ref[e, 0, :, h * RET_DK:(h + 1) * RET_DK] * RET_K_SCALE
            v = v_ref[e, 0, :, h * RET_DV:(h + 1) * RET_DV].astype(BF16)
            s = s_sc[e, h]
            a = lax.dot_general(q.astype(BF16), k.astype(BF16), (((1,), (1,)), ((), ())),
                                preferred_element_type=F32) * dmask
            o = (jnp.dot(a.astype(BF16), v, preferred_element_type=F32)
                 + jnp.dot((q * q_dec).astype(BF16), s.astype(BF16), preferred_element_type=F32))
            kd_t = jnp.transpose(k * k_dec).astype(BF16)
            s_sc[e, h] = s * c_dec + jnp.dot(kd_t, v, preferred_element_type=F32)
            mu = jnp.mean(o, axis=-1, keepdims=True)
            var = jnp.mean(jnp.square(o - mu), axis=-1, keepdims=True)
            y = (o - mu) * lax.rsqrt(var + EPS) * gn_ref[:, h * RET_DV:(h + 1) * RET_DV]
            y = jax.nn.silu(gate_ref[e, 0, :, h * RET_DV:(h + 1) * RET_DV]) * y
            if has_prev:
                y = prev_ref[e, 0, :, h * RET_DV:(h + 1) * RET_DV] + y
            o_ref[e, 0, :, h * RET_DV:(h + 1) * RET_DV] = y.astype(o_ref.dtype)

    @pl.when(c == pl.num_programs(1) - 1)
    def _():
        sfin_ref[...] = s_sc[...]


def _retention(z, log_gamma, gn, s0, prev, *, latent, reverse):
    batch, t, nseq = (DEC_BATCH, DEC_SEQ, DEC_BATCH) if latent else (BATCH, SEQ, 4)
    nc = t // RET_CHUNK
    seq0 = N_CTX // t if latent else 0
    gate_name = "rgb" if reverse else "rgf"
    w = RET_H * RET_DV
    z4 = z.reshape(N_TOK // t, nc, RET_CHUNK, Z_WIDTH)

    def chunk(c):
        return nc - 1 - c if reverse else c

    def zc(name, width):
        return pl.BlockSpec((pl.Element(nseq), pl.Element(1), pl.Element(RET_CHUNK), pl.Element(width)),
                            lambda g, c: (seq0 + g * nseq, chunk(c), 0, Z_OFF[name]))

    rows = pl.BlockSpec((nseq, 1, RET_CHUNK, w), lambda g, c: (g, chunk(c), 0, 0))
    state = pl.BlockSpec((nseq, RET_H, RET_DK, RET_DV), lambda g, c: (g, 0, 0, 0))
    in_specs = [pl.BlockSpec(memory_space=pltpu.SMEM), pl.BlockSpec((1, w), lambda g, c: (0, 0)), state,
                zc("rq", 256), zc("rk", 256), zc("rv", 512), zc(gate_name, 512)]
    args = [log_gamma, gn, s0, z4, z4, z4, z4]
    if prev is not None:
        in_specs.append(rows)
        args.append(prev.reshape(batch, nc, RET_CHUNK, w))
    o, s_fin = pl.pallas_call(
        functools.partial(_ret_kernel, reverse=reverse, has_prev=prev is not None, nseq=nseq),
        grid=(batch // nseq, nc),
        in_specs=in_specs,
        out_specs=[rows, state],
        out_shape=[jax.ShapeDtypeStruct((batch, nc, RET_CHUNK, w), BF16 if reverse else F32),
                   jax.ShapeDtypeStruct((batch, RET_H, RET_DK, RET_DV), F32)],
        scratch_shapes=[pltpu.VMEM((nseq, RET_H, RET_DK, RET_DV), F32)],
        compiler_params=_params("parallel", "arbitrary"),
        name="retention_bwd" if reverse else "retention_fwd",
    )(*args)
    return o.reshape(batch * t, w), s_fin


def _merge_kernel(*refs, n_x, tm):
    x_refs, refs = refs[:n_x], refs[n_x:]
    ctx_refs, lat_refs = refs[:N_BRANCH], refs[N_BRANCH:2 * N_BRANCH]
    (gate_ref, wb_ref, wo_ref, mod_ref, g_ref, wr_ref, br_ref, xo_ref, h_ref, lg_ref) = refs[2 * N_BRANCH:]
    i = pl.program_id(0)
    is_ctx = i < N_CTX // tm

    merged = None
    for n in range(N_BRANCH):
        branch = jnp.where(is_ctx, ctx_refs[n][...], lat_refs[n][...])
        t = jnp.dot(branch, wb_ref[n], preferred_element_type=F32)
        term = jax.nn.sigmoid(gate_ref[:, n * D_MODEL:(n + 1) * D_MODEL]) * t
        merged = term if merged is None else merged + term

    m = mod_ref[0]
    out = jnp.dot(merged.astype(BF16), wo_ref[...], preferred_element_type=F32)
    x = _token_tile(x_refs, i, tm) + m[:, 2 * D_MODEL:3 * D_MODEL] * out
    xo_ref[...] = x
    h = (_rms(x) * g_ref[...]) * (1.0 + m[:, 4 * D_MODEL:5 * D_MODEL]) + m[:, 3 * D_MODEL:4 * D_MODEL]
    h_ref[...] = h
    lg_ref[...] = jnp.dot(h.astype(BF16), wr_ref[...], preferred_element_type=F32) + br_ref[...]


def _merge(xs, ctx_branches, lat_branches, z, w_branch, w_out, mod, gain, w_router, b_router):
    tm = 256
    n_ct = N_CTX // tm
    row = lambda i: (i, 0)
    const2 = lambda i: (0, 0)
    resident = pl.Buffered(1)
    return pl.pallas_call(
        functools.partial(_merge_kernel, n_x=len(xs), tm=tm),
        grid=(N_TOK // tm,),
        in_specs=_token_specs(xs, tm, lambda i: i)
        + [pl.BlockSpec((tm, BRANCH_W), lambda i: (jnp.minimum(i, n_ct - 1), 0))] * N_BRANCH
        + [pl.BlockSpec((tm, BRANCH_W), lambda i: (jnp.maximum(i - n_ct, 0), 0))] * N_BRANCH
        + [pl.BlockSpec((pl.Element(tm), pl.Element(N_BRANCH * D_MODEL)),
                        lambda i: (pl.multiple_of(i * tm, tm), Z_OFF["gate"])),
           pl.BlockSpec((N_BRANCH, BRANCH_W, D_MODEL), lambda i: (0, 0, 0), pipeline_mode=resident),
           pl.BlockSpec((D_MODEL, D_MODEL), const2, pipeline_mode=resident),
           pl.BlockSpec((1, 1, 6 * D_MODEL), lambda i: (_cond_row(i, tm), 0, 0)),
           pl.BlockSpec((1, D_MODEL), const2),
           pl.BlockSpec((D_MODEL, ROUTER_W), const2),
           pl.BlockSpec((1, ROUTER_W), const2)],
        out_specs=[pl.BlockSpec((tm, D_MODEL), row), pl.BlockSpec((tm, D_MODEL), row),
                   pl.BlockSpec((tm, ROUTER_W), row)],
        out_shape=[jax.ShapeDtypeStruct((N_TOK, D_MODEL), F32), jax.ShapeDtypeStruct((N_TOK, D_MODEL), F32),
                   jax.ShapeDtypeStruct((N_TOK, ROUTER_W), F32)],
        compiler_params=_params("parallel"),
        name="merge_out_proj",
    )(*xs, *ctx_branches, *lat_branches, z, w_branch, w_out, mod, gain, w_router, b_router)


def _route_kernel(lg_ref, rec_ref, cnt_ref, tri_sc, carry_sc, *, tm):
    @pl.when(pl.program_id(0) == 0)
    def _():
        r = lax.broadcasted_iota(jnp.int32, (tm, tm), 0)
        c = lax.broadcasted_iota(jnp.int32, (tm, tm), 1)
        tri_sc[...] = (c < r).astype(BF16)
        carry_sc[...] = jnp.zeros(carry_sc.shape, F32)

    lg = lg_ref[...]
    lane = lax.broadcasted_iota(jnp.int32, lg.shape, 1)
    big = jnp.int32(ROUTER_W)

    def first_max(vals):
        top = jnp.max(vals, axis=-1, keepdims=True)
        return top, jnp.min(jnp.where(vals == top, lane, big), axis=-1, keepdims=True)

    g_logits = jnp.where(lane < N_GROUPS, lg, -jnp.inf)
    g_top, grp = first_max(g_logits)
    p_grp = 1.0 / jnp.sum(jnp.exp(g_logits - g_top), axis=-1, keepdims=True)
    lo = N_GROUPS + grp * EXPERTS_PER_GROUP
    e_logits = jnp.where((lane >= lo) & (lane < lo + EXPERTS_PER_GROUP), lg, -jnp.inf)
    v0, i0 = first_max(e_logits)
    v1, i1 = first_max(jnp.where(lane == i0, -jnp.inf, e_logits))
    e1 = jnp.exp(v1 - v0)
    den = 1.0 + e1
    w0 = p_grp * (1.0 / den)
    w1 = p_grp * (e1 / den)

    hot0 = (lane == i0).astype(F32)
    hot1 = (lane == i1).astype(F32)
    before = jnp.dot(tri_sc[...], (hot0 + hot1).astype(BF16), preferred_element_type=F32) + carry_sc[...]
    rank0 = jnp.sum(before * hot0, axis=-1, keepdims=True)
    rank1 = jnp.sum(before * hot1, axis=-1, keepdims=True)
    carry_sc[...] += jnp.sum(hot0 + hot1, axis=0, keepdims=True)

    rec = jnp.zeros(lg.shape, F32)
    for slot, val in ((R_W0, w0), (R_W1, w1), (R_E0, (i0 - N_GROUPS).astype(F32)),
                      (R_E1, (i1 - N_GROUPS).astype(F32)), (R_RANK0, rank0), (R_RANK1, rank1)):
        rec = jnp.where(lane == slot, val, rec)
    rec_ref[...] = rec
    cnt_ref[...] = jnp.broadcast_to(carry_sc[...], cnt_ref.shape)


def _route(logits):
    tm = 512
    rec, cnt = pl.pallas_call(
        functools.partial(_route_kernel, tm=tm),
        grid=(N_TOK // tm,),
        in_specs=[pl.BlockSpec((tm, ROUTER_W), lambda i: (i, 0))],
        out_specs=[pl.BlockSpec((tm, ROUTER_W), lambda i: (i, 0)), pl.BlockSpec((8, ROUTER_W), lambda i: (0, 0))],
        out_shape=[jax.ShapeDtypeStruct((N_TOK, ROUTER_W), F32), jax.ShapeDtypeStruct((8, ROUTER_W), F32)],
        scratch_shapes=[pltpu.VMEM((tm, tm), BF16), pltpu.VMEM((1, ROUTER_W), F32)],
        compiler_params=_params("arbitrary"),
        name="moe_route",
    )(logits)
    expert = rec[:, R_E0:R_E1 + 1].astype(jnp.int32)
    rank = rec[:, R_RANK0:R_RANK1 + 1].astype(jnp.int32)
    counts = cnt[0, N_GROUPS:N_GROUPS + N_EXPERTS].astype(jnp.int32)
    padded = (counts + MOE_BLK - 1) // MOE_BLK * MOE_BLK
    pad_end = jnp.cumsum(padded)
    dest = ((pad_end - padded)[expert] + rank).reshape(-1)
    token = jnp.arange(MOE_ASSIGN, dtype=jnp.int32) // TOP_K
    src = (jnp.arange(MOE_CAP, dtype=jnp.int32) % N_TOK).at[dest].set(token)
    blk_start = jnp.arange(MOE_NBLK, dtype=jnp.int32) * MOE_BLK
    blk_e = jnp.minimum(jnp.sum(pad_end[None, :] <= blk_start[:, None], axis=1), N_EXPERTS - 1).astype(jnp.int32)
    n_used = (pad_end[-1:] // MOE_BLK).astype(jnp.int32)
    ids = jnp.arange(N_EXPERTS, dtype=jnp.int32)
    later = jnp.where((ids[None, :] > ids[:, None]) & (counts[None, :] > 0), ids[None, :], N_EXPERTS)
    nxt = jnp.min(later, axis=1)
    nxt_e = jnp.where(nxt < N_EXPERTS, nxt, -1).astype(jnp.int32)[blk_e]
    return rec, dest, src, blk_e, nxt_e, n_used


def _start_row_gather(src_hbm, dst, sem, index_of, n_rows):
    group = 8

    def issue_group(gi, carry):
        for j in range(group):
            r = gi * group + j
            pltpu.make_async_copy(src_hbm.at[pl.ds(index_of(r), 1)], dst.at[pl.ds(r, 1)], sem).start(priority=j % 2)
        return carry

    lax.fori_loop(0, n_rows // group, issue_group, 0)


def _wait_row_gather(src_hbm, dst, sem, n_rows):
    pltpu.make_async_copy(src_hbm.at[pl.ds(0, n_rows)], dst, sem).wait()


def _expert_kernel(be_ref, nx_ref, nu_ref, x_ref, wg_hbm, wu_hbm, wd_hbm, y_ref,
                   stage_g, stage_u, stage_d, res_g, res_u, res_d, wsem, *, layer):
    i = pl.program_id(0)
    n_used = nu_ref[0]

    matrices = ((wg_hbm, stage_g, res_g), (wu_hbm, stage_u, res_u), (wd_hbm, stage_d, res_d))

    def weight_copy(e, k):
        return pltpu.make_async_copy(matrices[k][0].at[layer, e], matrices[k][1], wsem.at[k])

    def to_bf16(stage, res):
        rows = 256

        def body(c, carry):
            r = pl.multiple_of(c * rows, rows)
            res[pl.ds(r, rows), :] = stage[pl.ds(r, rows), :].astype(BF16)
            return carry

        lax.fori_loop(0, stage.shape[0] // rows, body, 0)

    @pl.when(i < n_used)
    def _():
        e = be_ref[i]
        first = (i == 0) | (e != be_ref[jnp.maximum(i - 1, 0)])

        @pl.when(i == 0)
        def _():
            for k in range(len(matrices)):
                weight_copy(e, k).start()

        @pl.when(first)
        def _():
            nxt = nx_ref[i]
            for k, (_, stage, res) in enumerate(matrices):
                weight_copy(e, k).wait()
                to_bf16(stage, res)

                @pl.when(nxt >= 0)
                def _(k=k):
                    weight_copy(nxt, k).start()

        x = x_ref[...]
        g = jnp.dot(x, res_g[...], preferred_element_type=F32)
        u = jnp.dot(x, res_u[...], preferred_element_type=F32)
        a = (jax.nn.silu(g) * u).astype(BF16)
        y_ref[...] = jnp.dot(a, res_d[...], preferred_element_type=F32)

    @pl.when(i >= n_used)
    def _():
        y_ref[...] = jnp.zeros(y_ref.shape, F32)


def _experts(xp, blk_e, nxt_e, n_used, w_gate, w_up, w_down, layer):
    any_spec = pl.BlockSpec(memory_space=pl.ANY)
    return pl.pallas_call(
        functools.partial(_expert_kernel, layer=layer),
        grid_spec=pltpu.PrefetchScalarGridSpec(
            num_scalar_prefetch=3,
            grid=(MOE_NBLK,),
            in_specs=[pl.BlockSpec((MOE_BLK, D_MODEL), lambda i, be, nx, nu: (jnp.minimum(i, nu[0] - 1), 0)),
                      any_spec, any_spec, any_spec],
            out_specs=pl.BlockSpec((MOE_BLK, D_MODEL), lambda i, *_: (i, 0)),
            scratch_shapes=[pltpu.VMEM((D_MODEL, D_EXPERT), F32), pltpu.VMEM((D_MODEL, D_EXPERT), F32),
                            pltpu.VMEM((D_EXPERT, D_MODEL), F32),
                            pltpu.VMEM((D_MODEL, D_EXPERT), BF16), pltpu.VMEM((D_MODEL, D_EXPERT), BF16),
                            pltpu.VMEM((D_EXPERT, D_MODEL), BF16),
                            pltpu.SemaphoreType.DMA((3,))]),
        out_shape=jax.ShapeDtypeStruct((MOE_CAP, D_MODEL), F32),
        compiler_params=_params("arbitrary"),
        name="moe_experts",
    )(blk_e, nxt_e, n_used, xp, w_gate, w_up, w_down)


DISPATCH_ROWS = 512


def _dispatch_kernel(nu_ref, src_ref, h_hbm, xp_ref, buf, sem):
    i = pl.program_id(0)
    used_tiles = pl.cdiv(nu_ref[0] * MOE_BLK, DISPATCH_ROWS)

    def gather_start(tile, slot):
        _start_row_gather(h_hbm, buf.at[slot], sem.at[slot], lambda r: src_ref[tile * DISPATCH_ROWS + r],
                          DISPATCH_ROWS)

    @pl.when(i == 0)
    def _():
        gather_start(0, 0)

    @pl.when(i < used_tiles)
    def _():
        slot = i % 2
        _wait_row_gather(h_hbm, buf.at[slot], sem.at[slot], DISPATCH_ROWS)

        @pl.when(i + 1 < used_tiles)
        def _():
            gather_start(i + 1, 1 - slot)

        xp_ref[...] = buf[slot].astype(BF16)

    @pl.when(i >= used_tiles)
    def _():
        xp_ref[...] = jnp.zeros(xp_ref.shape, BF16)


def _dispatch(h, src, n_used):
    return pl.pallas_call(
        _dispatch_kernel,
        grid_spec=pltpu.PrefetchScalarGridSpec(
            num_scalar_prefetch=2,
            grid=(MOE_CAP // DISPATCH_ROWS,),
            in_specs=[pl.BlockSpec(memory_space=pl.ANY)],
            out_specs=pl.BlockSpec((DISPATCH_ROWS, D_MODEL), lambda i, *_: (i, 0)),
            scratch_shapes=[pltpu.VMEM((2, DISPATCH_ROWS, D_MODEL), F32), pltpu.SemaphoreType.DMA((2,))]),
        out_shape=jax.ShapeDtypeStruct((MOE_CAP, D_MODEL), BF16),
        compiler_params=_params("arbitrary"),
        name="moe_dispatch",
    )(n_used, src, h)


def _combine_kernel(*refs, tm, final):
    dest_ref, yp_hbm, x_ref, rec_ref, mod_ref = refs[:5]
    refs = refs[5:]
    if final:
        fn_ref, yc_ref, yl_ref = refs[:3]
    else:
        xo_ref = refs[0]
    ybuf, sem = refs[-2:]
    i = pl.program_id(0)

    def gather_start(tile, slot):
        for k in range(TOP_K):
            _start_row_gather(yp_hbm, ybuf.at[slot, k], sem.at[slot, k],
                              lambda r, k=k: dest_ref[(tile * tm + r) * TOP_K + k], tm)

    @pl.when(i == 0)
    def _():
        gather_start(0, 0)

    slot = i % 2
    for k in range(TOP_K):
        _wait_row_gather(yp_hbm, ybuf.at[slot, k], sem.at[slot, k], tm)

    @pl.when(i + 1 < pl.num_programs(0))
    def _():
        gather_start(i + 1, 1 - slot)

    rec = rec_ref[...]
    y = ybuf[slot, 0] * rec[:, R_W0:R_W0 + 1] + ybuf[slot, 1] * rec[:, R_W1:R_W1 + 1]
    x = x_ref[...] + mod_ref[0][:, 5 * D_MODEL:6 * D_MODEL] * y
    if final:
        y = _rms(x) * fn_ref[...]

        @pl.when(i < N_CTX // tm)
        def _():
            yc_ref[...] = y

        @pl.when(i >= N_CTX // tm)
        def _():
            yl_ref[...] = y
    else:
        xo_ref[...] = x


def _combine(yp, dest, x, rec, mod, final_gain):
    tm = 256
    n_ct = N_CTX // tm
    final = final_gain is not None
    row = lambda i, d: (i, 0)
    in_specs = [pl.BlockSpec(memory_space=pl.ANY),
                pl.BlockSpec((tm, D_MODEL), row),
                pl.BlockSpec((tm, ROUTER_W), row),
                pl.BlockSpec((1, 1, 6 * D_MODEL), lambda i, d: (_cond_row(i, tm), 0, 0))]
    args = [dest, yp, x, rec, mod]
    if final:
        in_specs.append(pl.BlockSpec((1, D_MODEL), lambda i, d: (0, 0)))
        args.append(final_gain)
        out_specs = [pl.BlockSpec((tm, D_MODEL), lambda i, d: (jnp.minimum(i, n_ct - 1), 0)),
                     pl.BlockSpec((tm, D_MODEL), lambda i, d: (jnp.maximum(i - n_ct, 0), 0))]
        out_shape = [jax.ShapeDtypeStruct((N_CTX, D_MODEL), F32), jax.ShapeDtypeStruct((N_LAT, D_MODEL), F32)]
    else:
        out_specs = pl.BlockSpec((tm, D_MODEL), row)
        out_shape = jax.ShapeDtypeStruct((N_TOK, D_MODEL), F32)
    return pl.pallas_call(
        functools.partial(_combine_kernel, tm=tm, final=final),
        grid_spec=pltpu.PrefetchScalarGridSpec(
            num_scalar_prefetch=1,
            grid=(N_TOK // tm,),
            in_specs=in_specs,
            out_specs=out_specs,
            scratch_shapes=[pltpu.VMEM((2, TOP_K, tm, D_MODEL), F32), pltpu.SemaphoreType.DMA((2, TOP_K))]),
        out_shape=out_shape,
        compiler_params=_params("arbitrary"),
        name="moe_combine",
    )(*args)


def _swap_halves(w, chunk):
    k, n = w.shape
    w = w.reshape(k, n // chunk, 2, chunk // 2)
    return w[:, :, ::-1, :].reshape(k, n)


def _rope_tables(rot_dim):
    rows = DEC_SEQ // GRID_W
    row = np.repeat(np.arange(rows), GRID_W).astype(np.float64)
    col = (np.arange(rows * GRID_W) % GRID_W).astype(np.float64)
    axis_dim = rot_dim // 2
    inv = ROPE_THETA ** (-np.arange(0, axis_dim, 2, dtype=np.float64) / axis_dim)
    ang = np.concatenate([row[:, None] * inv, col[:, None] * inv], axis=-1)
    c, s = np.cos(ang), np.sin(ang)
    return (jnp.asarray(np.concatenate([c, c], axis=-1), F32), jnp.asarray(np.concatenate([-s, s], axis=-1), F32))


def _head_major(cache):
    b, l, kvh, d = cache.shape
    return jnp.transpose(cache, (2, 0, 1, 3)).reshape(kvh, b * l, d).astype(BF16)


def kernel(x_prompt, x_sample, c, cache_mla_ckv, cache_mla_krope, cache_gqa_k, cache_gqa_v, cache_swa_k,
           cache_swa_v, state_ret, c_ctx, ada_w, ada_b, norm_mix, norm_ffn, w_in, mla_q_norm, mla_kv_norm,
           mla_w_uq, mla_w_ukv, gqa_q_norm, gqa_k_norm, ret_decay_logit, ret_gn, swa_sink, w_branch, w_out,
           router_group_w, router_group_b, router_expert_w, router_expert_b, moe_w_gate, moe_w_up, moe_w_down,
           final_norm):
    xs = (x_prompt.reshape(N_CTX, D_MODEL), x_sample.reshape(N_LAT, D_MODEL))
    cond = jnp.concatenate([c_ctx[None, :], c, jnp.zeros((N_COND - 1 - DEC_BATCH, D_MODEL), F32)], axis=0)
    mod_all = _ada(cond, ada_w, ada_b).reshape(DEPTH, N_COND, 1, 6 * D_MODEL)

    w_in_r = _w_in_layout(w_in)
    tabs = _rope_tables(HEAD_DIM) + _rope_tables(MLA_ROPE)
    log_gamma = jax.nn.log_sigmoid(ret_decay_logit.astype(F32))
    uq = mla_w_uq.reshape(DEPTH, MLA_Q_LORA, MLA_H, MLA_DK)
    ukv = mla_w_ukv.reshape(DEPTH, MLA_KV_LORA, MLA_H, MLA_NOPE + MLA_V)
    w_router = jnp.concatenate(
        [router_group_w, router_expert_w,
         jnp.zeros((DEPTH, D_MODEL, ROUTER_W - N_GROUPS - N_EXPERTS), F32)], axis=-1).astype(BF16)
    b_router = jnp.concatenate(
        [router_group_b, router_expert_b, jnp.zeros((DEPTH, ROUTER_W - N_GROUPS - N_EXPERTS), F32)], axis=-1)
    zero_state = jnp.zeros((BATCH, RET_H, RET_DK, RET_DV), F32)

    caches = [[] for _ in range(7)]
    for l in range(DEPTH):
        mod = mod_all[l]
        uq_rope = uq[l, :, :, MLA_NOPE:].reshape(MLA_Q_LORA, MLA_H * MLA_ROPE)
        lw = dict(
            q_norm=mla_q_norm[l].reshape(1, -1), kv_norm=mla_kv_norm[l].reshape(1, -1),
            gq_norm=gqa_q_norm[l].reshape(1, -1), gk_norm=gqa_k_norm[l].reshape(1, -1),
            w_uq_nope=uq[l, :, :, :MLA_NOPE].reshape(MLA_Q_LORA, MLA_H * MLA_NOPE).astype(BF16),
            w_uq_rope=uq_rope.astype(BF16), w_uq_swap=_swap_halves(uq_rope, MLA_ROPE).astype(BF16),
            w_uk=ukv[l, :, :, :MLA_NOPE].reshape(MLA_KV_LORA, MLA_H * MLA_NOPE).astype(BF16),
            w_uv=ukv[l, :, :, MLA_NOPE:].reshape(MLA_KV_LORA, MLA_H * MLA_V).astype(BF16))

        z = _in_proj(xs, mod, norm_mix[l].reshape(1, -1), w_in_r, l)

        ctx_ops = _prep(z, lw, None, latent=False)
        ckv, kr, kg, vg, ks, vs = ctx_ops[9:]
        oa_c, ob_c, od_c = _ctx_attention(ctx_ops[:9], swa_sink[l])
        gn = ret_gn[l]
        of_c, s_f = _retention(z, log_gamma[l, 0], gn[0:1], zero_state, None, latent=False, reverse=False)
        oc_c, s_b = _retention(z, log_gamma[l, 1], gn[1:2], zero_state, of_c, latent=False, reverse=True)

        (qa, ka, va, qb, kb, vb, qd, kd, vd) = _prep(z, lw, tabs, latent=True)
        ka0, va0 = _mla_ctx(cache_mla_ckv[:, l].reshape(DEC_BATCH * PAST_LEN, MLA_KV_LORA),
                            cache_mla_krope[:, l].reshape(DEC_BATCH * PAST_LEN, MLA_ROPE), lw)
        oa_l = _flash_latent(qa, ka, va, (ka0, va0))
        ob_l = _flash_latent(qb, kb, vb, (_head_major(cache_gqa_k[:, l]), _head_major(cache_gqa_v[:, l])))
        od_l = _window(qd, kd, vd, _head_major(cache_swa_k[:, l]), _head_major(cache_swa_v[:, l]), swa_sink[l])
        of_l, _ = _retention(z, log_gamma[l, 0], gn[0:1], state_ret[:, l, 0], None, latent=True, reverse=False)
        oc_l, _ = _retention(z, log_gamma[l, 1], gn[1:2], state_ret[:, l, 1], of_l, latent=True, reverse=True)

        x_mid, h_ffn, logits = _merge(xs, (oa_c, ob_c, oc_c, od_c), (oa_l, ob_l, oc_l, od_l), z,
                                      w_branch[l].astype(BF16), w_out[l].astype(BF16), mod,
                                      norm_ffn[l].reshape(1, -1), w_router[l], b_router[l].reshape(1, -1))

        rec, dest, src, blk_e, nxt_e, n_used = _route(logits)
        yp = _experts(_dispatch(h_ffn, src, n_used), blk_e, nxt_e, n_used, moe_w_gate, moe_w_up, moe_w_down, l)
        if l < DEPTH - 1:
            xs = (_combine(yp, dest, x_mid, rec, mod, None),)
        else:
            y_ctx, y_lat = _combine(yp, dest, x_mid, rec, mod, final_norm.reshape(1, D_MODEL))

        kvh_shape = (BATCH, SEQ, GQA_KVH, HEAD_DIM)
        for lst, val in zip(caches, (ckv.reshape(BATCH, SEQ, MLA_KV_LORA), kr.reshape(BATCH, SEQ, MLA_ROPE),
                                     kg.reshape(kvh_shape), vg.reshape(kvh_shape), ks.reshape(kvh_shape),
                                     vs.reshape(kvh_shape), jnp.stack([s_f, s_b], axis=1))):
            lst.append(val)

    y_prompt = y_ctx.reshape(BATCH, SEQ, D_MODEL)
    y_sample = y_lat.reshape(DEC_BATCH, DEC_SEQ, D_MODEL)
    return (y_prompt, y_sample) + tuple(jnp.stack(lst, axis=1) for lst in caches)
```

```python
import functools

import numpy as np
import jax
import jax.numpy as jnp
from jax import lax
from jax.experimental import pallas as pl
from jax.experimental.pallas import tpu as pltpu

F32 = jnp.float32
BF16 = jnp.bfloat16
LANES = 128
LOG2_E = 1.4426950408889634

D_MODEL = 2048
BATCH = 16
SEQ = 256
DEPTH = 4
DEC_BATCH = 2
DEC_SEQ = 4096
PAST_LEN = 256
GRID_W = 64
ROPE_THETA = 10000.0
EPS = 1e-6
QBLK = 128
NEG_INF = -1e30

MLA_H = 4
MLA_Q_LORA = 512
MLA_KV_LORA = 256
MLA_NOPE = 128
MLA_ROPE = 64
MLA_V = 128
MLA_SCALE = (MLA_NOPE + MLA_ROPE) ** -0.5
MLA_DK = MLA_NOPE + MLA_ROPE

HEAD_DIM = 128
ATTN_SCALE = HEAD_DIM ** -0.5
GQA_H = 4
GQA_KVH = 2
SWA_H = 4
SWA_KVH = 2
WINDOW = 128

RET_H = 4
RET_DK = 64
RET_DV = 128
RET_CHUNK = 128
RET_K_SCALE = RET_DK ** -0.5

N_BRANCH = 4
BRANCH_W = 512

N_GROUPS = 4
EXPERTS_PER_GROUP = 8
N_EXPERTS = N_GROUPS * EXPERTS_PER_GROUP
TOP_K = 2
D_EXPERT = 1024

N_CTX = BATCH * SEQ
N_LAT = DEC_BATCH * DEC_SEQ
N_TOK = N_CTX + N_LAT
N_COND = 8

W_IN_HEAD = MLA_Q_LORA + MLA_KV_LORA + MLA_ROPE
IN_WIDTH = (W_IN_HEAD + (GQA_H + 2 * GQA_KVH) * HEAD_DIM + 2 * RET_H * RET_DK + 3 * RET_H * RET_DV
            + (SWA_H + 2 * SWA_KVH) * HEAD_DIM + N_BRANCH * D_MODEL)
Z_TAIL = 1024
Z_WIDTH = Z_TAIL + IN_WIDTH - W_IN_HEAD
Z_OFF = dict(mq=0, mkv=512, mkr=768, gq=1024, gk=1536, gv=1792, rq=2048, rk=2304, rv=2560, rgf=3072, rgb=3584,
             sq=4096, sk=4608, sv=4864, gate=5120)
Z_TILE = 1024

MOE_BLK = 256
MOE_ASSIGN = N_TOK * TOP_K
MOE_NBLK = (MOE_ASSIGN + N_EXPERTS * (MOE_BLK - 1) + MOE_BLK - 1) // MOE_BLK
MOE_CAP = MOE_NBLK * MOE_BLK
ROUTER_W = 128
R_W0, R_W1, R_E0, R_E1, R_RANK0, R_RANK1 = range(6)

VMEM_LIMIT = 56 * 1024 * 1024


def _params(*sem):
    return pltpu.CompilerParams(dimension_semantics=sem, vmem_limit_bytes=VMEM_LIMIT)


def _cond_row(i, tm):
    start = i * tm
    return jnp.where(start < N_CTX, 0, 1 + (start - N_CTX) // DEC_SEQ)


def _rms(x):
    return x * lax.rsqrt(jnp.mean(x * x, axis=-1, keepdims=True) + EPS)


def _token_specs(xs, tm, index_of):
    if len(xs) == 1:
        return [pl.BlockSpec((tm, D_MODEL), lambda *g: (index_of(*g), 0))]
    n_ct = N_CTX // tm
    return [pl.BlockSpec((tm, D_MODEL), lambda *g: (jnp.minimum(index_of(*g), n_ct - 1), 0),
                         pipeline_mode=pl.Buffered(1)),
            pl.BlockSpec((tm, D_MODEL), lambda *g: (jnp.maximum(index_of(*g) - n_ct, 0), 0),
                         pipeline_mode=pl.Buffered(1))]


def _token_tile(x_refs, i, tm):
    if len(x_refs) == 1:
        return x_refs[0][...]
    return jnp.where(i < N_CTX // tm, x_refs[0][...], x_refs[1][...])


def _ada_kernel(c_ref, w_ref, b_ref, o_ref):
    s = jax.nn.silu(c_ref[...]).astype(BF16)
    o_ref[0] = jnp.dot(s, w_ref[0].astype(BF16), preferred_element_type=F32) + b_ref[0]


def _ada(cond, ada_w, ada_b):
    tn = 1024
    n_out = 6 * D_MODEL
    return pl.pallas_call(
        _ada_kernel,
        grid=(DEPTH, n_out // tn),
        in_specs=[pl.BlockSpec((N_COND, D_MODEL), lambda l, j: (0, 0)),
                  pl.BlockSpec((1, D_MODEL, tn), lambda l, j: (l, 0, j)),
                  pl.BlockSpec((1, 1, tn), lambda l, j: (l, 0, j))],
        out_specs=pl.BlockSpec((1, N_COND, tn), lambda l, j: (l, 0, j)),
        out_shape=jax.ShapeDtypeStruct((DEPTH, N_COND, n_out), F32),
        compiler_params=_params("parallel", "parallel"),
        name="ada_mod",
    )(cond, ada_w, ada_b.reshape(DEPTH, 1, n_out))


def _w_in_kernel(w_ref, o_ref):
    w = w_ref[0]

    @pl.when(pl.program_id(1) == 0)
    def _():
        half = MLA_ROPE // 2
        mkr0 = W_IN_HEAD - MLA_ROPE
        o_ref[0, 0:W_IN_HEAD] = w[0:W_IN_HEAD].astype(BF16)
        o_ref[0, W_IN_HEAD:W_IN_HEAD + half] = w[mkr0 + half:W_IN_HEAD].astype(BF16)
        o_ref[0, W_IN_HEAD + half:W_IN_HEAD + MLA_ROPE] = w[mkr0:mkr0 + half].astype(BF16)
        o_ref[0, W_IN_HEAD + MLA_ROPE:Z_TAIL] = jnp.zeros((Z_TAIL - W_IN_HEAD - MLA_ROPE, w.shape[1]), BF16)

    @pl.when(pl.program_id(1) > 0)
    def _():
        o_ref[0] = w.astype(BF16)


def _w_in_layout(w_in):
    w_t = jnp.swapaxes(w_in, 1, 2)

    def src_row(j):
        return pl.multiple_of(jnp.where(j > 0, (j - 1) * Z_TILE + W_IN_HEAD, 0), 16)

    return pl.pallas_call(
        _w_in_kernel,
        grid=(DEPTH, Z_WIDTH // Z_TILE),
        in_specs=[pl.BlockSpec((pl.Element(1), pl.Element(Z_TILE), pl.Element(D_MODEL)),
                               lambda l, j: (l, src_row(j), 0))],
        out_specs=pl.BlockSpec((1, Z_TILE, D_MODEL), lambda l, j: (l, j, 0)),
        out_shape=jax.ShapeDtypeStruct((DEPTH, Z_WIDTH, D_MODEL), BF16),
        compiler_params=_params("parallel", "parallel"),
        name="w_in_layout",
    )(w_t)


def _in_kernel(*refs, n_x, tm):
    x_refs, (mod_ref, g_ref, w_ref, z_ref, h_ref) = refs[:n_x], refs[n_x:]

    @pl.when(pl.program_id(1) == 0)
    def _():
        m = mod_ref[0]
        y = _rms(_token_tile(x_refs, pl.program_id(0), tm)) * g_ref[...]
        h_ref[...] = (y * (1.0 + m[:, D_MODEL:2 * D_MODEL]) + m[:, 0:D_MODEL]).astype(BF16)

    z_ref[...] = lax.dot_general(h_ref[...], w_ref[...], (((1,), (1,)), ((), ())), preferred_element_type=F32)


def _in_proj(xs, mod, gain, w_all, layer):
    tm, tn = 1024, Z_TILE
    return pl.pallas_call(
        functools.partial(_in_kernel, n_x=len(xs), tm=tm),
        grid=(N_TOK // tm, Z_WIDTH // tn),
        in_specs=_token_specs(xs, tm, lambda i, j: i)
        + [pl.BlockSpec((1, 1, 6 * D_MODEL), lambda i, j: (_cond_row(i, tm), 0, 0)),
           pl.BlockSpec((1, D_MODEL), lambda i, j: (0, 0)),
           pl.BlockSpec((None, tn, D_MODEL), lambda i, j: (layer, j, 0))],
        out_specs=pl.BlockSpec((tm, tn), lambda i, j: (i, j)),
        out_shape=jax.ShapeDtypeStruct((N_TOK, Z_WIDTH), F32),
        scratch_shapes=[pltpu.VMEM((tm, D_MODEL), BF16)],
        compiler_params=_params("parallel", "arbitrary"),
        name="in_proj",
    )(*xs, mod, gain, w_all)


def _rope128(x, cos, sin):
    return x * cos + pltpu.roll(x, HEAD_DIM // 2, 1) * sin


def _mla_kv(ckv, kr, wuk_ref, wuv_ref, ka_ref, va_ref):
    cb = ckv.astype(BF16)
    kn = jnp.dot(cb, wuk_ref[...], preferred_element_type=F32)
    vv = jnp.dot(cb, wuv_ref[...], preferred_element_type=F32)
    for h in range(MLA_H):
        ka_ref[h, :, 0:MLA_NOPE] = kn[:, h * MLA_NOPE:(h + 1) * MLA_NOPE].astype(BF16)
        ka_ref[h, :, MLA_NOPE:MLA_DK] = kr.astype(BF16)
        va_ref[h] = vv[:, h * MLA_V:(h + 1) * MLA_V].astype(BF16)


def _prep_kernel(*refs, latent):
    (mq_ref, gq_ref, sq_ref, mkv_ref, gk_ref, gv_ref, sk_ref, sv_ref, mkr_ref,
     qn_ref, kvn_ref, gqn_ref, gkn_ref, wqn_ref, wqr_ref, wqs_ref, wuk_ref, wuv_ref) = refs[:18]
    refs = refs[18:]
    if latent:
        cos_ref, sin_ref, c64_ref, s64_ref = refs[:4]
        refs = refs[4:]
        cos, sin = cos_ref[...], sin_ref[...]
        c64, s64 = c64_ref[...], s64_ref[...]
    qa_ref, ka_ref, va_ref, qb_ref, kb_ref, vb_ref, qd_ref, kd_ref, vd_ref = refs[:9]
    refs = refs[9:]
    base2 = LOG2_E if latent else 1.0

    hn = (_rms(mq_ref[...]) * qn_ref[...]).astype(BF16)
    q_nope = jnp.dot(hn, wqn_ref[...], preferred_element_type=F32)
    q_rope = jnp.dot(hn, wqr_ref[...], preferred_element_type=F32)
    if latent:
        q_swap = jnp.dot(hn, wqs_ref[...], preferred_element_type=F32)
    for h in range(MLA_H):
        qa_ref[h, :, 0:MLA_NOPE] = (q_nope[:, h * MLA_NOPE:(h + 1) * MLA_NOPE] * (MLA_SCALE * base2)).astype(BF16)
        qr = q_rope[:, h * MLA_ROPE:(h + 1) * MLA_ROPE]
        if latent:
            qr = qr * c64 + q_swap[:, h * MLA_ROPE:(h + 1) * MLA_ROPE] * s64
        qa_ref[h, :, MLA_NOPE:MLA_DK] = (qr * (MLA_SCALE * base2)).astype(BF16)
    ckv = _rms(mkv_ref[...]) * kvn_ref[...]
    kr = mkr_ref[:, 0:MLA_ROPE]
    if latent:
        kr = kr * c64 + mkr_ref[:, MLA_ROPE:2 * MLA_ROPE] * s64
    _mla_kv(ckv, kr, wuk_ref, wuv_ref, ka_ref, va_ref)

    gq = gq_ref[...]
    for h in range(GQA_H):
        q = _rms(gq[:, h * HEAD_DIM:(h + 1) * HEAD_DIM]) * gqn_ref[...]
        if latent:
            q = _rope128(q, cos, sin)
        qb_ref[h] = (q * (ATTN_SCALE * base2)).astype(BF16)
    gk = gk_ref[...]
    gv = gv_ref[...]
    kg = []
    for h in range(GQA_KVH):
        k = _rms(gk[:, h * HEAD_DIM:(h + 1) * HEAD_DIM]) * gkn_ref[...]
        kg.append(k)
        if latent:
            k = _rope128(k, cos, sin)
        kb_ref[h] = k.astype(BF16)
        vb_ref[h] = gv[:, h * HEAD_DIM:(h + 1) * HEAD_DIM].astype(BF16)

    sq = sq_ref[...]
    for h in range(SWA_H):
        q = sq[:, h * HEAD_DIM:(h + 1) * HEAD_DIM]
        if latent:
            q = _rope128(q, cos, sin)
        qd_ref[h] = (q * ATTN_SCALE).astype(BF16)
    sk = sk_ref[...]
    sv = sv_ref[...]
    for h in range(SWA_KVH):
        k = sk[:, h * HEAD_DIM:(h + 1) * HEAD_DIM]
        if latent:
            k = _rope128(k, cos, sin)
        kd_ref[h] = k.astype(BF16)
        vd_ref[h] = sv[:, h * HEAD_DIM:(h + 1) * HEAD_DIM].astype(BF16)

    if not latent:
        ckv_ref, kr_ref, kg_ref, vg_ref, ks_ref, vs_ref = refs
        ckv_ref[...] = ckv
        kr_ref[...] = kr
        for h in range(GQA_KVH):
            kg_ref[:, h * HEAD_DIM:(h + 1) * HEAD_DIM] = kg[h]
        vg_ref[...] = gv
        ks_ref[...] = sk
        vs_ref[...] = sv


def _prep(z, lw, tabs, latent):
    tm = 512
    rows = N_LAT if latent else N_CTX
    roff = (N_CTX if latent else 0) // tm
    n_t = DEC_SEQ // tm

    def zc(name, width):
        cb = Z_OFF[name] // width
        return pl.BlockSpec((tm, width), lambda i: (i + roff, cb))

    def full(a):
        nd = a.ndim
        return pl.BlockSpec(a.shape, lambda i: (0,) * nd)

    weights = [lw["q_norm"], lw["kv_norm"], lw["gq_norm"], lw["gk_norm"],
               lw["w_uq_nope"], lw["w_uq_rope"], lw["w_uq_swap"], lw["w_uk"], lw["w_uv"]]
    in_specs = [zc("mq", 512), zc("gq", 512), zc("sq", 512), zc("mkv", 256), zc("gk", 256), zc("gv", 256),
                zc("sk", 256), zc("sv", 256), zc("mkr", 128)] + [full(a) for a in weights]
    args = [z] * 9 + weights
    if latent:
        in_specs += [pl.BlockSpec((tm, HEAD_DIM), lambda i: (i % n_t, 0))] * 2
        in_specs += [pl.BlockSpec((tm, MLA_ROPE), lambda i: (i % n_t, 0))] * 2
        args += list(tabs)

    def heads(n, d):
        return (pl.BlockSpec((n, tm, d), lambda i: (0, i, 0)), jax.ShapeDtypeStruct((n, rows, d), BF16))

    outs = [heads(MLA_H, MLA_DK), heads(MLA_H, MLA_DK), heads(MLA_H, MLA_V),
            heads(GQA_H, HEAD_DIM), heads(GQA_KVH, HEAD_DIM), heads(GQA_KVH, HEAD_DIM),
            heads(SWA_H, HEAD_DIM), heads(SWA_KVH, HEAD_DIM), heads(SWA_KVH, HEAD_DIM)]
    if not latent:
        kv_w = GQA_KVH * HEAD_DIM
        for d in (MLA_KV_LORA, MLA_ROPE, kv_w, kv_w, kv_w, kv_w):
            outs.append((pl.BlockSpec((tm, d), lambda i: (i, 0)), jax.ShapeDtypeStruct((rows, d), F32)))
    return pl.pallas_call(
        functools.partial(_prep_kernel, latent=latent),
        grid=(rows // tm,),
        in_specs=in_specs,
        out_specs=[o[0] for o in outs],
        out_shape=[o[1] for o in outs],
        compiler_params=_params("parallel"),
        name="prep_latent" if latent else "prep_context",
    )(*args)


def _mla_ctx_kernel(ckv_ref, kr_ref, wuk_ref, wuv_ref, ka_ref, va_ref):
    _mla_kv(ckv_ref[...], kr_ref[...], wuk_ref, wuv_ref, ka_ref, va_ref)


def _mla_ctx(ckv, kr, lw):
    rows = ckv.shape[0]
    return pl.pallas_call(
        _mla_ctx_kernel,
        out_shape=[jax.ShapeDtypeStruct((MLA_H, rows, MLA_DK), BF16),
                   jax.ShapeDtypeStruct((MLA_H, rows, MLA_V), BF16)],
        name="mla_ctx_kv",
    )(ckv, kr, lw["w_uk"], lw["w_uv"])


def _softmax_pv(q, k, v, sink):
    s = lax.dot_general(q, k, (((1,), (1,)), ((), ())), preferred_element_type=F32)
    m = jnp.max(s, axis=-1, keepdims=True)
    if sink is not None:
        m = jnp.maximum(m, sink)
    p = jnp.exp(s - m)
    l = jnp.sum(p, axis=-1, keepdims=True)
    if sink is not None:
        l = l + jnp.exp(sink - m)
    return jnp.dot(p.astype(BF16), v, preferred_element_type=F32) / l


def _ctx_attn_kernel(sink_ref, qa, ka, va, qb, kb, vb, qd, kd, vd, oa, ob, od):
    for h in range(MLA_H):
        oa[:, h * MLA_V:(h + 1) * MLA_V] = _softmax_pv(qa[h], ka[h], va[h], None).astype(BF16)
    for h in range(GQA_H):
        kv = h // (GQA_H // GQA_KVH)
        ob[:, h * HEAD_DIM:(h + 1) * HEAD_DIM] = _softmax_pv(qb[h], kb[kv], vb[kv], None).astype(BF16)
    for h in range(SWA_H):
        kv = h // (SWA_H // SWA_KVH)
        od[:, h * HEAD_DIM:(h + 1) * HEAD_DIM] = _softmax_pv(qd[h], kd[kv], vd[kv], sink_ref[h]).astype(BF16)


def _ctx_attention(operands, sink):
    def spec(a):
        return pl.BlockSpec((a.shape[0], SEQ, a.shape[2]), lambda b: (0, b, 0))

    out = jax.ShapeDtypeStruct((N_CTX, BRANCH_W), BF16)
    return pl.pallas_call(
        _ctx_attn_kernel,
        grid=(BATCH,),
        in_specs=[pl.BlockSpec(memory_space=pltpu.SMEM)] + [spec(a) for a in operands],
        out_specs=[pl.BlockSpec((SEQ, BRANCH_W), lambda b: (b, 0))] * 3,
        out_shape=[out, out, out],
        compiler_params=_params("parallel"),
        name="context_attention",
    )(sink, *operands)


KEY_CHUNK = 256


def _flash_latent_kernel(q_ref, k_ref, v_ref, k0_ref, v0_ref, o_ref, m_sc, l_sc, acc_sc, *, groups):
    dv = v_ref.shape[-1]
    n_chunks = k_ref.shape[1] // KEY_CHUNK
    dn = (((1,), (1,)), ((), ()))

    def update(g, k, v, first):
        s = lax.dot_general(q_ref[g], k, dn, preferred_element_type=F32)
        m_chunk = jnp.max(_lanewise(jnp.maximum, s), axis=-1, keepdims=True)
        if first:
            m_new = jnp.broadcast_to(m_chunk, (s.shape[0], LANES))
        else:
            m_prev = m_sc[g]
            m_new = jnp.maximum(m_prev, m_chunk)
            alpha = jnp.exp2(m_prev - m_new)
        p = jnp.concatenate([jnp.exp2(s[:, c * LANES:(c + 1) * LANES] - m_new) for c in range(s.shape[1] // LANES)],
                            axis=1)
        lsum = _lanewise(jnp.add, p)
        pv = jnp.dot(p.astype(BF16), v, preferred_element_type=F32)
        if first:
            l_sc[g] = lsum
            acc_sc[g] = pv
        else:
            l_sc[g] = alpha * l_sc[g] + lsum
            acc_sc[g] = alpha * acc_sc[g] + pv
        m_sc[g] = m_new

    for g in range(groups):
        update(g, k0_ref[0], v0_ref[0], True)
    for j in range(n_chunks):
        for g in range(groups):
            update(g, k_ref[0, j * KEY_CHUNK:(j + 1) * KEY_CHUNK, :], v_ref[0, j * KEY_CHUNK:(j + 1) * KEY_CHUNK, :],
                   False)
    for g in range(groups):
        l = jnp.sum(l_sc[g], axis=-1, keepdims=True)
        o_ref[:, g * dv:(g + 1) * dv] = (acc_sc[g] / l).astype(o_ref.dtype)


def _flash_latent(q, k, v, ctx):
    tq = 1024
    n_h, _, dk = q.shape
    kvh, _, dv = v.shape
    groups = n_h // kvh
    nq = DEC_SEQ // tq
    return pl.pallas_call(
        functools.partial(_flash_latent_kernel, groups=groups),
        grid=(DEC_BATCH, kvh, nq),
        in_specs=[pl.BlockSpec((groups, tq, dk), lambda b, h, i: (h, b * nq + i, 0)),
                  pl.BlockSpec((1, DEC_SEQ, dk), lambda b, h, i: (h, b, 0)),
                  pl.BlockSpec((1, DEC_SEQ, dv), lambda b, h, i: (h, b, 0)),
                  pl.BlockSpec((1, PAST_LEN, dk), lambda b, h, i: (h, b, 0)),
                  pl.BlockSpec((1, PAST_LEN, dv), lambda b, h, i: (h, b, 0))],
        out_specs=pl.BlockSpec((tq, groups * dv), lambda b, h, i: (b * nq + i, h)),
        out_shape=jax.ShapeDtypeStruct((N_LAT, n_h * dv), BF16),
        scratch_shapes=[pltpu.VMEM((groups, tq, LANES), F32), pltpu.VMEM((groups, tq, LANES), F32),
                        pltpu.VMEM((groups, tq, dv), F32)],
        compiler_params=_params("parallel", "parallel", "parallel"),
        name="flash_latent",
    )(q, k, v, *ctx)


WIN_TQ = 512
WIN_SPAN = WIN_TQ + 2 * WINDOW


def _lanewise(op, x):
    r = x[:, 0:LANES]
    for c in range(1, x.shape[1] // LANES):
        r = op(r, x[:, c * LANES:(c + 1) * LANES])
    return r


def _exp_tiles(s, m):
    return jnp.concatenate([jnp.exp(s[:, c * LANES:(c + 1) * LANES] - m) for c in range(s.shape[1] // LANES)],
                           axis=1)


def _window_kernel(sink_ref, q_ref, k_ref, v_ref, k0_ref, v0_ref, o_ref, *, groups):
    head0 = pl.program_id(1) * groups
    t = k_ref.shape[1]
    q0 = pl.program_id(2) * WIN_TQ
    k_start = pl.multiple_of(jnp.clip(q0 - WINDOW, 0, t - WIN_SPAN), LANES)
    qpos = q0 + lax.broadcasted_iota(jnp.int32, (WIN_TQ, WIN_SPAN), 0)
    kpos = k_start + lax.broadcasted_iota(jnp.int32, (WIN_TQ, WIN_SPAN), 1)
    band = jnp.abs(kpos - qpos) <= WINDOW
    kw = k_ref[0, pl.ds(k_start, WIN_SPAN), :]
    vw = v_ref[0, pl.ds(k_start, WIN_SPAN), :]
    dn = (((1,), (1,)), ((), ()))
    for g in range(groups):
        q = q_ref[g]
        s_loc = jnp.where(band, lax.dot_general(q, kw, dn, preferred_element_type=F32), NEG_INF)
        s_ctx = lax.dot_general(q, k0_ref[0], dn, preferred_element_type=F32)
        snk = sink_ref[head0 + g]
        m = jnp.max(jnp.maximum(_lanewise(jnp.maximum, s_loc), _lanewise(jnp.maximum, s_ctx)),
                    axis=-1, keepdims=True)
        m = jnp.broadcast_to(jnp.maximum(m, snk), (WIN_TQ, LANES))
        p_loc = _exp_tiles(s_loc, m)
        p_ctx = _exp_tiles(s_ctx, m)
        den = (jnp.sum(_lanewise(jnp.add, p_loc) + _lanewise(jnp.add, p_ctx), axis=-1, keepdims=True)
               + jnp.exp(snk - m[:, 0:1]))
        acc = (jnp.dot(p_loc.astype(BF16), vw, preferred_element_type=F32)
               + jnp.dot(p_ctx.astype(BF16), v0_ref[0], preferred_element_type=F32))
        o_ref[:, g * HEAD_DIM:(g + 1) * HEAD_DIM] = (acc / den).astype(o_ref.dtype)


def _window(q, k, v, k0, v0, sink):
    n_h, kvh = q.shape[0], k.shape[0]
    groups = n_h // kvh
    nq = DEC_SEQ // WIN_TQ
    full = pl.BlockSpec((1, DEC_SEQ, HEAD_DIM), lambda b, h, n: (h, b, 0))
    ctx_spec = pl.BlockSpec((1, PAST_LEN, HEAD_DIM), lambda b, h, n: (h, b, 0))
    return pl.pallas_call(
        functools.partial(_window_kernel, groups=groups),
        grid=(DEC_BATCH, kvh, nq),
        in_specs=[pl.BlockSpec(memory_space=pltpu.SMEM),
                  pl.BlockSpec((groups, WIN_TQ, HEAD_DIM), lambda b, h, n: (h, b * nq + n, 0)),
                  full, full, ctx_spec, ctx_spec],
        out_specs=pl.BlockSpec((WIN_TQ, groups * HEAD_DIM), lambda b, h, n: (b * nq + n, h)),
        out_shape=jax.ShapeDtypeStruct((N_LAT, n_h * HEAD_DIM), BF16),
        compiler_params=_params("parallel", "parallel", "parallel"),
        name="window_attention",
    )(sink, q, k, v, k0, v0)


def _ret_kernel(*refs, reverse, has_prev, nseq):
    lg_ref, gn_ref, s0_ref, q_ref, k_ref, v_ref, gate_ref = refs[:7]
    refs = refs[7:]
    if has_prev:
        prev_ref, refs = refs[0], refs[1:]
    o_ref, sfin_ref, s_sc = refs
    c = pl.program_id(1)

    @pl.when(c == 0)
    def _():
        s_sc[...] = s0_ref[...]

    cs = RET_CHUNK
    row = lax.broadcasted_iota(jnp.int32, (cs, cs), 0).astype(F32)
    col = lax.broadcasted_iota(jnp.int32, (cs, cs), 1).astype(F32)
    pos = lax.broadcasted_iota(jnp.int32, (cs, 1), 0).astype(F32)
    diff = (col - row) if reverse else (row - col)
    scan_pos = (cs - 1.0 - pos) if reverse else pos
    for h in range(RET_H):
        lg = lg_ref[h]
        dmask = jnp.where(diff >= 0, jnp.exp(jnp.maximum(diff, 0.0) * lg), 0.0)
        q_dec = jnp.exp((scan_pos + 1.0) * lg)
        k_dec = jnp.exp((cs - 1.0 - scan_pos) * lg)
        c_dec = jnp.exp(cs * lg)
        for e in range(nseq):
            q = q_ref[e, 0, :, h * RET_DK:(h + 1) * RET_DK]
            k = k_ref[e, 0, :, h * RET_DK:(h + 1) * RET_DK] * RET_K_SCALE
            v = v_ref[e, 0, :, h * RET_DV:(h + 1) * RET_DV].astype(BF16)
            s = s_sc[e, h]
            a = lax.dot_general(q.astype(BF16), k.astype(BF16), (((1,), (1,)), ((), ())),
                                preferred_element_type=F32) * dmask
            o = (jnp.dot(a.astype(BF16), v, preferred_element_type=F32)
                 + jnp.dot((q * q_dec).astype(BF16), s.astype(BF16), preferred_element_type=F32))
            kd_t = jnp.transpose(k * k_dec).astype(BF16)
            s_sc[e, h] = s * c_dec + jnp.dot(kd_t, v, preferred_element_type=F32)
            mu = jnp.mean(o, axis=-1, keepdims=True)
            var = jnp.mean(jnp.square(o - mu), axis=-1, keepdims=True)
            y = (o - mu) * lax.rsqrt(var + EPS) * gn_ref[:, h * RET_DV:(h + 1) * RET_DV]
            y = jax.nn.silu(gate_ref[e, 0, :, h * RET_DV:(h + 1) * RET_DV]) * y
            if has_prev:
                y = prev_ref[e, 0, :, h * RET_DV:(h + 1) * RET_DV] + y
            o_ref[e, 0, :, h * RET_DV:(h + 1) * RET_DV] = y.astype(o_ref.dtype)

    @pl.when(c == pl.num_programs(1) - 1)
    def _():
        sfin_ref[...] = s_sc[...]


def _retention(z, log_gamma, gn, s0, prev, *, latent, reverse):
    batch, t, nseq = (DEC_BATCH, DEC_SEQ, DEC_BATCH) if latent else (BATCH, SEQ, 4)
    nc = t // RET_CHUNK
    seq0 = N_CTX // t if latent else 0
    gate_name = "rgb" if reverse else "rgf"
    w = RET_H * RET_DV
    z4 = z.reshape(N_TOK // t, nc, RET_CHUNK, Z_WIDTH)

    def chunk(c):
        return nc - 1 - c if reverse else c

    def zc(name, width):
        return pl.BlockSpec((pl.Element(nseq), pl.Element(1), pl.Element(RET_CHUNK), pl.Element(width)),
                            lambda g, c: (seq0 + g * nseq, chunk(c), 0, Z_OFF[name]))

    rows = pl.BlockSpec((nseq, 1, RET_CHUNK, w), lambda g, c: (g, chunk(c), 0, 0))
    state = pl.BlockSpec((nseq, RET_H, RET_DK, RET_DV), lambda g, c: (g, 0, 0, 0))
    in_specs = [pl.BlockSpec(memory_space=pltpu.SMEM), pl.BlockSpec((1, w), lambda g, c: (0, 0)), state,
                zc("rq", 256), zc("rk", 256), zc("rv", 512), zc(gate_name, 512)]
    args = [log_gamma, gn, s0, z4, z4, z4, z4]
    if prev is not None:
        in_specs.append(rows)
        args.append(prev.reshape(batch, nc, RET_CHUNK, w))
    o, s_fin = pl.pallas_call(
        functools.partial(_ret_kernel, reverse=reverse, has_prev=prev is not None, nseq=nseq),
        grid=(batch // nseq, nc),
        in_specs=in_specs,
        out_specs=[rows, state],
        out_shape=[jax.ShapeDtypeStruct((batch, nc, RET_CHUNK, w), BF16 if reverse else F32),
                   jax.ShapeDtypeStruct((batch, RET_H, RET_DK, RET_DV), F32)],
        scratch_shapes=[pltpu.VMEM((nseq, RET_H, RET_DK, RET_DV), F32)],
        compiler_params=_params("parallel", "arbitrary"),
        name="retention_bwd" if reverse else "retention_fwd",
    )(*args)
    return o.reshape(batch * t, w), s_fin


def _merge_kernel(*refs, n_x, tm):
    x_refs, refs = refs[:n_x], refs[n_x:]
    ctx_refs, lat_refs = refs[:N_BRANCH], refs[N_BRANCH:2 * N_BRANCH]
    (gate_ref, wb_ref, wo_ref, mod_ref, g_ref, wr_ref, br_ref, xo_ref, h_ref, lg_ref) = refs[2 * N_BRANCH:]
    i = pl.program_id(0)
    is_ctx = i < N_CTX // tm

    merged = None
    for n in range(N_BRANCH):
        branch = jnp.where(is_ctx, ctx_refs[n][...], lat_refs[n][...])
        t = jnp.dot(branch, wb_ref[n], preferred_element_type=F32)
        term = jax.nn.sigmoid(gate_ref[:, n * D_MODEL:(n + 1) * D_MODEL]) * t
        merged = term if merged is None else merged + term

    m = mod_ref[0]
    out = jnp.dot(merged.astype(BF16), wo_ref[...], preferred_element_type=F32)
    x = _token_tile(x_refs, i, tm) + m[:, 2 * D_MODEL:3 * D_MODEL] * out
    xo_ref[...] = x
    h = (_rms(x) * g_ref[...]) * (1.0 + m[:, 4 * D_MODEL:5 * D_MODEL]) + m[:, 3 * D_MODEL:4 * D_MODEL]
    h_ref[...] = h
    lg_ref[...] = jnp.dot(h.astype(BF16), wr_ref[...], preferred_element_type=F32) + br_ref[...]


def _merge(xs, ctx_branches, lat_branches, z, w_branch, w_out, mod, gain, w_router, b_router):
    tm = 256
    n_ct = N_CTX // tm
    row = lambda i: (i, 0)
    const2 = lambda i: (0, 0)
    resident = pl.Buffered(1)
    return pl.pallas_call(
        functools.partial(_merge_kernel, n_x=len(xs), tm=tm),
        grid=(N_TOK // tm,),
        in_specs=_token_specs(xs, tm, lambda i: i)
        + [pl.BlockSpec((tm, BRANCH_W), lambda i: (jnp.minimum(i, n_ct - 1), 0))] * N_BRANCH
        + [pl.BlockSpec((tm, BRANCH_W), lambda i: (jnp.maximum(i - n_ct, 0), 0))] * N_BRANCH
        + [pl.BlockSpec((pl.Element(tm), pl.Element(N_BRANCH * D_MODEL)),
                        lambda i: (pl.multiple_of(i * tm, tm), Z_OFF["gate"])),
           pl.BlockSpec((N_BRANCH, BRANCH_W, D_MODEL), lambda i: (0, 0, 0), pipeline_mode=resident),
           pl.BlockSpec((D_MODEL, D_MODEL), const2, pipeline_mode=resident),
           pl.BlockSpec((1, 1, 6 * D_MODEL), lambda i: (_cond_row(i, tm), 0, 0)),
           pl.BlockSpec((1, D_MODEL), const2),
           pl.BlockSpec((D_MODEL, ROUTER_W), const2),
           pl.BlockSpec((1, ROUTER_W), const2)],
        out_specs=[pl.BlockSpec((tm, D_MODEL), row), pl.BlockSpec((tm, D_MODEL), row),
                   pl.BlockSpec((tm, ROUTER_W), row)],
        out_shape=[jax.ShapeDtypeStruct((N_TOK, D_MODEL), F32), jax.ShapeDtypeStruct((N_TOK, D_MODEL), F32),
                   jax.ShapeDtypeStruct((N_TOK, ROUTER_W), F32)],
        compiler_params=_params("parallel"),
        name="merge_out_proj",
    )(*xs, *ctx_branches, *lat_branches, z, w_branch, w_out, mod, gain, w_router, b_router)


def _route_kernel(lg_ref, rec_ref, cnt_ref, tri_sc, carry_sc, *, tm):
    @pl.when(pl.program_id(0) == 0)
    def _():
        r = lax.broadcasted_iota(jnp.int32, (tm, tm), 0)
        c = lax.broadcasted_iota(jnp.int32, (tm, tm), 1)
        tri_sc[...] = (c < r).astype(BF16)
        carry_sc[...] = jnp.zeros(carry_sc.shape, F32)

    lg = lg_ref[...]
    lane = lax.broadcasted_iota(jnp.int32, lg.shape, 1)
    big = jnp.int32(ROUTER_W)

    def first_max(vals):
        top = jnp.max(vals, axis=-1, keepdims=True)
        return top, jnp.min(jnp.where(vals == top, lane, big), axis=-1, keepdims=True)

    g_logits = jnp.where(lane < N_GROUPS, lg, -jnp.inf)
    g_top, grp = first_max(g_logits)
    p_grp = 1.0 / jnp.sum(jnp.exp(g_logits - g_top), axis=-1, keepdims=True)
    lo = N_GROUPS + grp * EXPERTS_PER_GROUP
    e_logits = jnp.where((lane >= lo) & (lane < lo + EXPERTS_PER_GROUP), lg, -jnp.inf)
    v0, i0 = first_max(e_logits)
    v1, i1 = first_max(jnp.where(lane == i0, -jnp.inf, e_logits))
    e1 = jnp.exp(v1 - v0)
    den = 1.0 + e1
    w0 = p_grp * (1.0 / den)
    w1 = p_grp * (e1 / den)

    hot0 = (lane == i0).astype(F32)
    hot1 = (lane == i1).astype(F32)
    before = jnp.dot(tri_sc[...], (hot0 + hot1).astype(BF16), preferred_element_type=F32) + carry_sc[...]
    rank0 = jnp.sum(before * hot0, axis=-1, keepdims=True)
    rank1 = jnp.sum(before * hot1, axis=-1, keepdims=True)
    carry_sc[...] += jnp.sum(hot0 + hot1, axis=0, keepdims=True)

    rec = jnp.zeros(lg.shape, F32)
    for slot, val in ((R_W0, w0), (R_W1, w1), (R_E0, (i0 - N_GROUPS).astype(F32)),
                      (R_E1, (i1 - N_GROUPS).astype(F32)), (R_RANK0, rank0), (R_RANK1, rank1)):
        rec = jnp.where(lane == slot, val, rec)
    rec_ref[...] = rec
    cnt_ref[...] = jnp.broadcast_to(carry_sc[...], cnt_ref.shape)


def _route(logits):
    tm = 512
    rec, cnt = pl.pallas_call(
        functools.partial(_route_kernel, tm=tm),
        grid=(N_TOK // tm,),
        in_specs=[pl.BlockSpec((tm, ROUTER_W), lambda i: (i, 0))],
        out_specs=[pl.BlockSpec((tm, ROUTER_W), lambda i: (i, 0)), pl.BlockSpec((8, ROUTER_W), lambda i: (0, 0))],
        out_shape=[jax.ShapeDtypeStruct((N_TOK, ROUTER_W), F32), jax.ShapeDtypeStruct((8, ROUTER_W), F32)],
        scratch_shapes=[pltpu.VMEM((tm, tm), BF16), pltpu.VMEM((1, ROUTER_W), F32)],
        compiler_params=_params("arbitrary"),
        name="moe_route",
    )(logits)
    expert = rec[:, R_E0:R_E1 + 1].astype(jnp.int32)
    rank = rec[:, R_RANK0:R_RANK1 + 1].astype(jnp.int32)
    counts = cnt[0, N_GROUPS:N_GROUPS + N_EXPERTS].astype(jnp.int32)
    padded = (counts + MOE_BLK - 1) // MOE_BLK * MOE_BLK
    pad_end = jnp.cumsum(padded)
    dest = ((pad_end - padded)[expert] + rank).reshape(-1)
    token = jnp.arange(MOE_ASSIGN, dtype=jnp.int32) // TOP_K
    src = (jnp.arange(MOE_CAP, dtype=jnp.int32) % N_TOK).at[dest].set(token)
    blk_start = jnp.arange(MOE_NBLK, dtype=jnp.int32) * MOE_BLK
    blk_e = jnp.minimum(jnp.sum(pad_end[None, :] <= blk_start[:, None], axis=1), N_EXPERTS - 1).astype(jnp.int32)
    n_used = (pad_end[-1:] // MOE_BLK).astype(jnp.int32)
    ids = jnp.arange(N_EXPERTS, dtype=jnp.int32)
    later = jnp.where((ids[None, :] > ids[:, None]) & (counts[None, :] > 0), ids[None, :], N_EXPERTS)
    nxt = jnp.min(later, axis=1)
    nxt_e = jnp.where(nxt < N_EXPERTS, nxt, -1).astype(jnp.int32)[blk_e]
    return rec, dest, src, blk_e, nxt_e, n_used


def _start_row_gather(src_hbm, dst, sem, index_of, n_rows):
    group = 8

    def issue_group(gi, carry):
        for j in range(group):
            r = gi * group + j
            pltpu.make_async_copy(src_hbm.at[pl.ds(index_of(r), 1)], dst.at[pl.ds(r, 1)], sem).start(priority=j % 2)
        return carry

    lax.fori_loop(0, n_rows // group, issue_group, 0)


def _wait_row_gather(src_hbm, dst, sem, n_rows):
    pltpu.make_async_copy(src_hbm.at[pl.ds(0, n_rows)], dst, sem).wait()


def _expert_kernel(be_ref, nx_ref, nu_ref, x_ref, wg_hbm, wu_hbm, wd_hbm, y_ref,
                   stage_g, stage_u, stage_d, res_g, res_u, res_d, wsem, *, layer):
    i = pl.program_id(0)
    n_used = nu_ref[0]

    matrices = ((wg_hbm, stage_g, res_g), (wu_hbm, stage_u, res_u), (wd_hbm, stage_d, res_d))

    def weight_copy(e, k):
        return pltpu.make_async_copy(matrices[k][0].at[layer, e], matrices[k][1], wsem.at[k])

    def to_bf16(stage, res):
        rows = 256

        def body(c, carry):
            r = pl.multiple_of(c * rows, rows)
            res[pl.ds(r, rows), :] = stage[pl.ds(r, rows), :].astype(BF16)
            return carry

        lax.fori_loop(0, stage.shape[0] // rows, body, 0)

    @pl.when(i < n_used)
    def _():
        e = be_ref[i]
        first = (i == 0) | (e != be_ref[jnp.maximum(i - 1, 0)])

        @pl.when(i == 0)
        def _():
            for k in range(len(matrices)):
                weight_copy(e, k).start()

        @pl.when(first)
        def _():
            nxt = nx_ref[i]
            for k, (_, stage, res) in enumerate(matrices):
                weight_copy(e, k).wait()
                to_bf16(stage, res)

                @pl.when(nxt >= 0)
                def _(k=k):
                    weight_copy(nxt, k).start()

        x = x_ref[...]
        g = jnp.dot(x, res_g[...], preferred_element_type=F32)
        u = jnp.dot(x, res_u[...], preferred_element_type=F32)
        a = (jax.nn.silu(g) * u).astype(BF16)
        y_ref[...] = jnp.dot(a, res_d[...], preferred_element_type=F32)

    @pl.when(i >= n_used)
    def _():
        y_ref[...] = jnp.zeros(y_ref.shape, F32)


def _experts(xp, blk_e, nxt_e, n_used, w_gate, w_up, w_down, layer):
    any_spec = pl.BlockSpec(memory_space=pl.ANY)
    return pl.pallas_call(
        functools.partial(_expert_kernel, layer=layer),
        grid_spec=pltpu.PrefetchScalarGridSpec(
            num_scalar_prefetch=3,
            grid=(MOE_NBLK,),
            in_specs=[pl.BlockSpec((MOE_BLK, D_MODEL), lambda i, be, nx, nu: (jnp.minimum(i, nu[0] - 1), 0)),
                      any_spec, any_spec, any_spec],
            out_specs=pl.BlockSpec((MOE_BLK, D_MODEL), lambda i, *_: (i, 0)),
            scratch_shapes=[pltpu.VMEM((D_MODEL, D_EXPERT), F32), pltpu.VMEM((D_MODEL, D_EXPERT), F32),
                            pltpu.VMEM((D_EXPERT, D_MODEL), F32),
                            pltpu.VMEM((D_MODEL, D_EXPERT), BF16), pltpu.VMEM((D_MODEL, D_EXPERT), BF16),
                            pltpu.VMEM((D_EXPERT, D_MODEL), BF16),
                            pltpu.SemaphoreType.DMA((3,))]),
        out_shape=jax.ShapeDtypeStruct((MOE_CAP, D_MODEL), F32),
        compiler_params=_params("arbitrary"),
        name="moe_experts",
    )(blk_e, nxt_e, n_used, xp, w_gate, w_up, w_down)


DISPATCH_ROWS = 512


def _dispatch_kernel(nu_ref, src_ref, h_hbm, xp_ref, buf, sem):
    i = pl.program_id(0)
    used_tiles = pl.cdiv(nu_ref[0] * MOE_BLK, DISPATCH_ROWS)

    def gather_start(tile, slot):
        _start_row_gather(h_hbm, buf.at[slot], sem.at[slot], lambda r: src_ref[tile * DISPATCH_ROWS + r],
                          DISPATCH_ROWS)

    @pl.when(i == 0)
    def _():
        gather_start(0, 0)

    @pl.when(i < used_tiles)
    def _():
        slot = i % 2
        _wait_row_gather(h_hbm, buf.at[slot], sem.at[slot], DISPATCH_ROWS)

        @pl.when(i + 1 < used_tiles)
        def _():
            gather_start(i + 1, 1 - slot)

        xp_ref[...] = buf[slot].astype(BF16)

    @pl.when(i >= used_tiles)
    def _():
        xp_ref[...] = jnp.zeros(xp_ref.shape, BF16)


def _dispatch(h, src, n_used):
    return pl.pallas_call(
        _dispatch_kernel,
        grid_spec=pltpu.PrefetchScalarGridSpec(
            num_scalar_prefetch=2,
            grid=(MOE_CAP // DISPATCH_ROWS,),
            in_specs=[pl.BlockSpec(memory_space=pl.ANY)],
            out_specs=pl.BlockSpec((DISPATCH_ROWS, D_MODEL), lambda i, *_: (i, 0)),
            scratch_shapes=[pltpu.VMEM((2, DISPATCH_ROWS, D_MODEL), F32), pltpu.SemaphoreType.DMA((2,))]),
        out_shape=jax.ShapeDtypeStruct((MOE_CAP, D_MODEL), BF16),
        compiler_params=_params("arbitrary"),
        name="moe_dispatch",
    )(n_used, src, h)


def _combine_kernel(*refs, tm, final):
    dest_ref, yp_hbm, x_ref, rec_ref, mod_ref = refs[:5]
    refs = refs[5:]
    if final:
        fn_ref, yc_ref, yl_ref = refs[:3]
    else:
        xo_ref = refs[0]
    ybuf, sem = refs[-2:]
    i = pl.program_id(0)

    def gather_start(tile, slot):
        for k in range(TOP_K):
            _start_row_gather(yp_hbm, ybuf.at[slot, k], sem.at[slot, k],
                              lambda r, k=k: dest_ref[(tile * tm + r) * TOP_K + k], tm)

    @pl.when(i == 0)
    def _():
        gather_start(0, 0)

    slot = i % 2
    for k in range(TOP_K):
        _wait_row_gather(yp_hbm, ybuf.at[slot, k], sem.at[slot, k], tm)

    @pl.when(i + 1 < pl.num_programs(0))
    def _():
        gather_start(i + 1, 1 - slot)

    rec = rec_ref[...]
    y = ybuf[slot, 0] * rec[:, R_W0:R_W0 + 1] + ybuf[slot, 1] * rec[:, R_W1:R_W1 + 1]
    x = x_ref[...] + mod_ref[0][:, 5 * D_MODEL:6 * D_MODEL] * y
    if final:
        y = _rms(x) * fn_ref[...]

        @pl.when(i < N_CTX // tm)
        def _():
            yc_ref[...] = y

        @pl.when(i >= N_CTX // tm)
        def _():
            yl_ref[...] = y
    else:
        xo_ref[...] = x


def _combine(yp, dest, x, rec, mod, final_gain):
    tm = 256
    n_ct = N_CTX // tm
    final = final_gain is not None
    row = lambda i, d: (i, 0)
    in_specs = [pl.BlockSpec(memory_space=pl.ANY),
                pl.BlockSpec((tm, D_MODEL), row),
                pl.BlockSpec((tm, ROUTER_W), row),
                pl.BlockSpec((1, 1, 6 * D_MODEL), lambda i, d: (_cond_row(i, tm), 0, 0))]
    args = [dest, yp, x, rec, mod]
    if final:
        in_specs.append(pl.BlockSpec((1, D_MODEL), lambda i, d: (0, 0)))
        args.append(final_gain)
        out_specs = [pl.BlockSpec((tm, D_MODEL), lambda i, d: (jnp.minimum(i, n_ct - 1), 0)),
                     pl.BlockSpec((tm, D_MODEL), lambda i, d: (jnp.maximum(i - n_ct, 0), 0))]
        out_shape = [jax.ShapeDtypeStruct((N_CTX, D_MODEL), F32), jax.ShapeDtypeStruct((N_LAT, D_MODEL), F32)]
    else:
        out_specs = pl.BlockSpec((tm, D_MODEL), row)
        out_shape = jax.ShapeDtypeStruct((N_TOK, D_MODEL), F32)
    return pl.pallas_call(
        functools.partial(_combine_kernel, tm=tm, final=final),
        grid_spec=pltpu.PrefetchScalarGridSpec(
            num_scalar_prefetch=1,
            grid=(N_TOK // tm,),
            in_specs=in_specs,
            out_specs=out_specs,
            scratch_shapes=[pltpu.VMEM((2, TOP_K, tm, D_MODEL), F32), pltpu.SemaphoreType.DMA((2, TOP_K))]),
        out_shape=out_shape,
        compiler_params=_params("arbitrary"),
        name="moe_combine",
    )(*args)


def _swap_halves(w, chunk):
    k, n = w.shape
    w = w.reshape(k, n // chunk, 2, chunk // 2)
    return w[:, :, ::-1, :].reshape(k, n)


def _rope_tables(rot_dim):
    rows = DEC_SEQ // GRID_W
    row = np.repeat(np.arange(rows), GRID_W).astype(np.float64)
    col = (np.arange(rows * GRID_W) % GRID_W).astype(np.float64)
    axis_dim = rot_dim // 2
    inv = ROPE_THETA ** (-np.arange(0, axis_dim, 2, dtype=np.float64) / axis_dim)
    ang = np.concatenate([row[:, None] * inv, col[:, None] * inv], axis=-1)
    c, s = np.cos(ang), np.sin(ang)
    return (jnp.asarray(np.concatenate([c, c], axis=-1), F32), jnp.asarray(np.concatenate([-s, s], axis=-1), F32))


def _head_major(cache):
    b, l, kvh, d = cache.shape
    return jnp.transpose(cache, (2, 0, 1, 3)).reshape(kvh, b * l, d).astype(BF16)


def kernel(x_prompt, x_sample, c, cache_mla_ckv, cache_mla_krope, cache_gqa_k, cache_gqa_v, cache_swa_k,
           cache_swa_v, state_ret, c_ctx, ada_w, ada_b, norm_mix, norm_ffn, w_in, mla_q_norm, mla_kv_norm,
           mla_w_uq, mla_w_ukv, gqa_q_norm, gqa_k_norm, ret_decay_logit, ret_gn, swa_sink, w_branch, w_out,
           router_group_w, router_group_b, router_expert_w, router_expert_b, moe_w_gate, moe_w_up, moe_w_down,
           final_norm):
    xs = (x_prompt.reshape(N_CTX, D_MODEL), x_sample.reshape(N_LAT, D_MODEL))
    cond = jnp.concatenate([c_ctx[None, :], c, jnp.zeros((N_COND - 1 - DEC_BATCH, D_MODEL), F32)], axis=0)
    mod_all = _ada(cond, ada_w, ada_b).reshape(DEPTH, N_COND, 1, 6 * D_MODEL)

    w_in_r = _w_in_layout(w_in)
    tabs = _rope_tables(HEAD_DIM) + _rope_tables(MLA_ROPE)
    log_gamma = jax.nn.log_sigmoid(ret_decay_logit.astype(F32))
    uq = mla_w_uq.reshape(DEPTH, MLA_Q_LORA, MLA_H, MLA_DK)
    ukv = mla_w_ukv.reshape(DEPTH, MLA_KV_LORA, MLA_H, MLA_NOPE + MLA_V)
    w_router = jnp.concatenate(
        [router_group_w, router_expert_w,
         jnp.zeros((DEPTH, D_MODEL, ROUTER_W - N_GROUPS - N_EXPERTS), F32)], axis=-1).astype(BF16)
    b_router = jnp.concatenate(
        [router_group_b, router_expert_b, jnp.zeros((DEPTH, ROUTER_W - N_GROUPS - N_EXPERTS), F32)], axis=-1)
    zero_state = jnp.zeros((BATCH, RET_H, RET_DK, RET_DV), F32)

    caches = [[] for _ in range(7)]
    for l in range(DEPTH):
        mod = mod_all[l]
        uq_rope = uq[l, :, :, MLA_NOPE:].reshape(MLA_Q_LORA, MLA_H * MLA_ROPE)
        lw = dict(
            q_norm=mla_q_norm[l].reshape(1, -1), kv_norm=mla_kv_norm[l].reshape(1, -1),
            gq_norm=gqa_q_norm[l].reshape(1, -1), gk_norm=gqa_k_norm[l].reshape(1, -1),
            w_uq_nope=uq[l, :, :, :MLA_NOPE].reshape(MLA_Q_LORA, MLA_H * MLA_NOPE).astype(BF16),
            w_uq_rope=uq_rope.astype(BF16), w_uq_swap=_swap_halves(uq_rope, MLA_ROPE).astype(BF16),
            w_uk=ukv[l, :, :, :MLA_NOPE].reshape(MLA_KV_LORA, MLA_H * MLA_NOPE).astype(BF16),
            w_uv=ukv[l, :, :, MLA_NOPE:].reshape(MLA_KV_LORA, MLA_H * MLA_V).astype(BF16))

        z = _in_proj(xs, mod, norm_mix[l].reshape(1, -1), w_in_r, l)

        ctx_ops = _prep(z, lw, None, latent=False)
        ckv, kr, kg, vg, ks, vs = ctx_ops[9:]
        oa_c, ob_c, od_c = _ctx_attention(ctx_ops[:9], swa_sink[l])
        gn = ret_gn[l]
        of_c, s_f = _retention(z, log_gamma[l, 0], gn[0:1], zero_state, None, latent=False, reverse=False)
        oc_c, s_b = _retention(z, log_gamma[l, 1], gn[1:2], zero_state, of_c, latent=False, reverse=True)

        (qa, ka, va, qb, kb, vb, qd, kd, vd) = _prep(z, lw, tabs, latent=True)
        ka0, va0 = _mla_ctx(cache_mla_ckv[:, l].reshape(DEC_BATCH * PAST_LEN, MLA_KV_LORA),
                            cache_mla_krope[:, l].reshape(DEC_BATCH * PAST_LEN, MLA_ROPE), lw)
        oa_l = _flash_latent(qa, ka, va, (ka0, va0))
        ob_l = _flash_latent(qb, kb, vb, (_head_major(cache_gqa_k[:, l]), _head_major(cache_gqa_v[:, l])))
        od_l = _window(qd, kd, vd, _head_major(cache_swa_k[:, l]), _head_major(cache_swa_v[:, l]), swa_sink[l])
        of_l, _ = _retention(z, log_gamma[l, 0], gn[0:1], state_ret[:, l, 0], None, latent=True, reverse=False)
        oc_l, _ = _retention(z, log_gamma[l, 1], gn[1:2], state_ret[:, l, 1], of_l, latent=True, reverse=True)

        x_mid, h_ffn, logits = _merge(xs, (oa_c, ob_c, oc_c, od_c), (oa_l, ob_l, oc_l, od_l), z,
                                      w_branch[l].astype(BF16), w_out[l].astype(BF16), mod,
                                      norm_ffn[l].reshape(1, -1), w_router[l], b_router[l].reshape(1, -1))

        rec, dest, src, blk_e, nxt_e, n_used = _route(logits)
        yp = _experts(_dispatch(h_ffn, src, n_used), blk_e, nxt_e, n_used, moe_w_gate, moe_w_up, moe_w_down, l)
        if l < DEPTH - 1:
            xs = (_combine(yp, dest, x_mid, rec, mod, None),)
        else:
            y_ctx, y_lat = _combine(yp, dest, x_mid, rec, mod, final_norm.reshape(1, D_MODEL))

        kvh_shape = (BATCH, SEQ, GQA_KVH, HEAD_DIM)
        for lst, val in zip(caches, (ckv.reshape(BATCH, SEQ, MLA_KV_LORA), kr.reshape(BATCH, SEQ, MLA_ROPE),
                                     kg.reshape(kvh_shape), vg.reshape(kvh_shape), ks.reshape(kvh_shape),
                                     vs.reshape(kvh_shape), jnp.stack([s_f, s_b], axis=1))):
            lst.append(val)

    y_prompt = y_ctx.reshape(BATCH, SEQ, D_MODEL)
    y_sample = y_lat.reshape(DEC_BATCH, DEC_SEQ, D_MODEL)
    return (y_prompt, y_sample) + tuple(jnp.stack(lst, axis=1) for lst in caches)
```

```python
import functools

import numpy as np
import jax
import jax.numpy as jnp
from jax import lax
from jax.experimental import pallas as pl
from jax.experimental.pallas import tpu as pltpu

F32 = jnp.float32
BF16 = jnp.bfloat16
LANES = 128
LOG2_E = 1.4426950408889634

D_MODEL = 2048
BATCH = 16
SEQ = 256
DEPTH = 4
DEC_BATCH = 2
DEC_SEQ = 4096
PAST_LEN = 256
GRID_W = 64
ROPE_THETA = 10000.0
EPS = 1e-6
NEG_INF = -1e30

MLA_H = 4
MLA_Q_LORA = 512
MLA_KV_LORA = 256
MLA_NOPE = 128
MLA_ROPE = 64
MLA_V = 128
MLA_SCALE = (MLA_NOPE + MLA_ROPE) ** -0.5
MLA_DK = MLA_NOPE + MLA_ROPE

HEAD_DIM = 128
ATTN_SCALE = HEAD_DIM ** -0.5
GQA_H = 4
GQA_KVH = 2
SWA_H = 4
SWA_KVH = 2
WINDOW = 128

RET_H = 4
RET_DK = 64
RET_DV = 128
RET_CHUNK = 128
RET_K_SCALE = RET_DK ** -0.5

N_BRANCH = 4
BRANCH_W = 512

N_GROUPS = 4
EXPERTS_PER_GROUP = 8
N_EXPERTS = N_GROUPS * EXPERTS_PER_GROUP
TOP_K = 2
D_EXPERT = 1024

N_CTX = BATCH * SEQ
N_LAT = DEC_BATCH * DEC_SEQ
N_TOK = N_CTX + N_LAT
N_COND = 8

W_IN_HEAD = MLA_Q_LORA + MLA_KV_LORA + MLA_ROPE
IN_WIDTH = (W_IN_HEAD + (GQA_H + 2 * GQA_KVH) * HEAD_DIM + 2 * RET_H * RET_DK + 3 * RET_H * RET_DV
            + (SWA_H + 2 * SWA_KVH) * HEAD_DIM + N_BRANCH * D_MODEL)
Z_TAIL = 1024
Z_WIDTH = Z_TAIL + IN_WIDTH - W_IN_HEAD
Z_OFF = dict(mq=0, mkv=512, mkr=768, gq=1024, gk=1536, gv=1792, rq=2048, rk=2304, rv=2560, rgf=3072, rgb=3584,
             sq=4096, sk=4608, sv=4864, gate=5120)
Z_TILE = 1024

MOE_BLK = 256
MOE_ASSIGN = N_TOK * TOP_K
MOE_NBLK = (MOE_ASSIGN + N_EXPERTS * (MOE_BLK - 1) + MOE_BLK - 1) // MOE_BLK
MOE_CAP = MOE_NBLK * MOE_BLK
ROUTER_W = 128
R_W0, R_W1, R_E0, R_E1, R_RANK0, R_RANK1 = range(6)

VMEM_LIMIT = 56 * 1024 * 1024


def _params(*sem):
    return pltpu.CompilerParams(dimension_semantics=sem, vmem_limit_bytes=VMEM_LIMIT)


def _cond_row(i, tm):
    start = i * tm
    return jnp.where(start < N_CTX, 0, 1 + (start - N_CTX) // DEC_SEQ)


def _rms(x):
    return x * lax.rsqrt(jnp.mean(x * x, axis=-1, keepdims=True) + EPS)


def _token_specs(xs, tm, index_of):
    if len(xs) == 1:
        return [pl.BlockSpec((tm, D_MODEL), lambda *g: (index_of(*g), 0))]
    n_ct = N_CTX // tm
    return [pl.BlockSpec((tm, D_MODEL), lambda *g: (jnp.minimum(index_of(*g), n_ct - 1), 0),
                         pipeline_mode=pl.Buffered(1)),
            pl.BlockSpec((tm, D_MODEL), lambda *g: (jnp.maximum(index_of(*g) - n_ct, 0), 0),
                         pipeline_mode=pl.Buffered(1))]


def _token_tile(x_refs, i, tm):
    if len(x_refs) == 1:
        return x_refs[0][...]
    return jnp.where(i < N_CTX // tm, x_refs[0][...], x_refs[1][...])


def _ada_kernel(c_ref, w_ref, b_ref, o_ref):
    s = jax.nn.silu(c_ref[...]).astype(BF16)
    o_ref[0] = jnp.dot(s, w_ref[0].astype(BF16), preferred_element_type=F32) + b_ref[0]


def _ada(cond, ada_w, ada_b):
    tn = 1024
    n_out = 6 * D_MODEL
    return pl.pallas_call(
        _ada_kernel,
        grid=(DEPTH, n_out // tn),
        in_specs=[pl.BlockSpec((N_COND, D_MODEL), lambda l, j: (0, 0)),
                  pl.BlockSpec((1, D_MODEL, tn), lambda l, j: (l, 0, j)),
                  pl.BlockSpec((1, 1, tn), lambda l, j: (l, 0, j))],
        out_specs=pl.BlockSpec((1, N_COND, tn), lambda l, j: (l, 0, j)),
        out_shape=jax.ShapeDtypeStruct((DEPTH, N_COND, n_out), F32),
        compiler_params=_params("parallel", "parallel"),
        name="ada_mod",
    )(cond, ada_w, ada_b.reshape(DEPTH, 1, n_out))


def _w_in_kernel(w_ref, o_ref):
    w = w_ref[0]

    @pl.when(pl.program_id(1) == 0)
    def _():
        half = MLA_ROPE // 2
        mkr0 = W_IN_HEAD - MLA_ROPE
        o_ref[0, 0:W_IN_HEAD] = w[0:W_IN_HEAD].astype(BF16)
        o_ref[0, W_IN_HEAD:W_IN_HEAD + half] = w[mkr0 + half:W_IN_HEAD].astype(BF16)
        o_ref[0, W_IN_HEAD + half:W_IN_HEAD + MLA_ROPE] = w[mkr0:mkr0 + half].astype(BF16)
        o_ref[0, W_IN_HEAD + MLA_ROPE:Z_TAIL] = jnp.zeros((Z_TAIL - W_IN_HEAD - MLA_ROPE, w.shape[1]), BF16)

    @pl.when(pl.program_id(1) > 0)
    def _():
        o_ref[0] = w.astype(BF16)


def _w_in_layout(w_in):
    w_t = jnp.swapaxes(w_in, 1, 2)

    def src_row(j):
        return pl.multiple_of(jnp.where(j > 0, (j - 1) * Z_TILE + W_IN_HEAD, 0), 16)

    return pl.pallas_call(
        _w_in_kernel,
        grid=(DEPTH, Z_WIDTH // Z_TILE),
        in_specs=[pl.BlockSpec((pl.Element(1), pl.Element(Z_TILE), pl.Element(D_MODEL)),
                               lambda l, j: (l, src_row(j), 0))],
        out_specs=pl.BlockSpec((1, Z_TILE, D_MODEL), lambda l, j: (l, j, 0)),
        out_shape=jax.ShapeDtypeStruct((DEPTH, Z_WIDTH, D_MODEL), BF16),
        compiler_params=_params("parallel", "parallel"),
        name="w_in_layout",
    )(w_t)


def _in_kernel(*refs, n_x, tm):
    x_refs, (mod_ref, g_ref, w_ref, z_ref, h_ref) = refs[:n_x], refs[n_x:]

    @pl.when(pl.program_id(1) == 0)
    def _():
        m = mod_ref[0]
        y = _rms(_token_tile(x_refs, pl.program_id(0), tm)) * g_ref[...]
        h_ref[...] = (y * (1.0 + m[:, D_MODEL:2 * D_MODEL]) + m[:, 0:D_MODEL]).astype(BF16)

    z_ref[...] = lax.dot_general(h_ref[...], w_ref[...], (((1,), (1,)), ((), ())), preferred_element_type=F32)


def _in_proj(xs, mod, gain, w_all, layer):
    tm, tn = 1024, Z_TILE
    return pl.pallas_call(
        functools.partial(_in_kernel, n_x=len(xs), tm=tm),
        grid=(N_TOK // tm, Z_WIDTH // tn),
        in_specs=_token_specs(xs, tm, lambda i, j: i)
        + [pl.BlockSpec((1, 1, 6 * D_MODEL), lambda i, j: (_cond_row(i, tm), 0, 0)),
           pl.BlockSpec((1, D_MODEL), lambda i, j: (0, 0)),
           pl.BlockSpec((None, tn, D_MODEL), lambda i, j: (layer, j, 0))],
        out_specs=pl.BlockSpec((tm, tn), lambda i, j: (i, j)),
        out_shape=jax.ShapeDtypeStruct((N_TOK, Z_WIDTH), F32),
        scratch_shapes=[pltpu.VMEM((tm, D_MODEL), BF16)],
        compiler_params=_params("parallel", "arbitrary"),
        name="in_proj",
    )(*xs, mod, gain, w_all)


def _rope128(x, cos, sin):
    return x * cos + pltpu.roll(x, HEAD_DIM // 2, 1) * sin


def _mla_kv(ckv, kr, wuk_ref, wuv_ref, ka_ref, va_ref):
    cb = ckv.astype(BF16)
    kn = jnp.dot(cb, wuk_ref[...], preferred_element_type=F32)
    vv = jnp.dot(cb, wuv_ref[...], preferred_element_type=F32)
    for h in range(MLA_H):
        ka_ref[h, :, 0:MLA_NOPE] = kn[:, h * MLA_NOPE:(h + 1) * MLA_NOPE].astype(BF16)
        ka_ref[h, :, MLA_NOPE:MLA_DK] = kr.astype(BF16)
        va_ref[h] = vv[:, h * MLA_V:(h + 1) * MLA_V].astype(BF16)


def _prep_kernel(*refs, latent):
    (mq_ref, gq_ref, sq_ref, mkv_ref, gk_ref, gv_ref, sk_ref, sv_ref, mkr_ref,
     qn_ref, kvn_ref, gqn_ref, gkn_ref, wqn_ref, wqr_ref, wqs_ref, wuk_ref, wuv_ref) = refs[:18]
    refs = refs[18:]
    if latent:
        cos_ref, sin_ref, c64_ref, s64_ref = refs[:4]
        refs = refs[4:]
        cos, sin = cos_ref[...], sin_ref[...]
        c64, s64 = c64_ref[...], s64_ref[...]
    qa_ref, ka_ref, va_ref, qb_ref, kb_ref, vb_ref, qd_ref, kd_ref, vd_ref = refs[:9]
    refs = refs[9:]
    base2 = LOG2_E if latent else 1.0

    hn = (_rms(mq_ref[...]) * qn_ref[...]).astype(BF16)
    q_nope = jnp.dot(hn, wqn_ref[...], preferred_element_type=F32)
    q_rope = jnp.dot(hn, wqr_ref[...], preferred_element_type=F32)
    if latent:
        q_swap = jnp.dot(hn, wqs_ref[...], preferred_element_type=F32)
    for h in range(MLA_H):
        qa_ref[h, :, 0:MLA_NOPE] = (q_nope[:, h * MLA_NOPE:(h + 1) * MLA_NOPE] * (MLA_SCALE * base2)).astype(BF16)
        qr = q_rope[:, h * MLA_ROPE:(h + 1) * MLA_ROPE]
        if latent:
            qr = qr * c64 + q_swap[:, h * MLA_ROPE:(h + 1) * MLA_ROPE] * s64
        qa_ref[h, :, MLA_NOPE:MLA_DK] = (qr * (MLA_SCALE * base2)).astype(BF16)
    ckv = _rms(mkv_ref[...]) * kvn_ref[...]
    kr = mkr_ref[:, 0:MLA_ROPE]
    if latent:
        kr = kr * c64 + mkr_ref[:, MLA_ROPE:2 * MLA_ROPE] * s64
    _mla_kv(ckv, kr, wuk_ref, wuv_ref, ka_ref, va_ref)

    gq = gq_ref[...]
    for h in range(GQA_H):
        q = _rms(gq[:, h * HEAD_DIM:(h + 1) * HEAD_DIM]) * gqn_ref[...]
        if latent:
            q = _rope128(q, cos, sin)
        qb_ref[h] = (q * (ATTN_SCALE * base2)).astype(BF16)
    gk = gk_ref[...]
    gv = gv_ref[...]
    kg = []
    for h in range(GQA_KVH):
        k = _rms(gk[:, h * HEAD_DIM:(h + 1) * HEAD_DIM]) * gkn_ref[...]
        kg.append(k)
        if latent:
            k = _rope128(k, cos, sin)
        kb_ref[h] = k.astype(BF16)
        vb_ref[h] = gv[:, h * HEAD_DIM:(h + 1) * HEAD_DIM].astype(BF16)

    sq = sq_ref[...]
    for h in range(SWA_H):
        q = sq[:, h * HEAD_DIM:(h + 1) * HEAD_DIM]
        if latent:
            q = _rope128(q, cos, sin)
        qd_ref[h] = (q * ATTN_SCALE).astype(BF16)
    sk = sk_ref[...]
    sv = sv_ref[...]
    for h in range(SWA_KVH):
        k = sk[:, h * HEAD_DIM:(h + 1) * HEAD_DIM]
        if latent:
            k = _rope128(k, cos, sin)
        kd_ref[h] = k.astype(BF16)
        vd_ref[h] = sv[:, h * HEAD_DIM:(h + 1) * HEAD_DIM].astype(BF16)

    if not latent:
        ckv_ref, kr_ref, kg_ref, vg_ref, ks_ref, vs_ref = refs
        ckv_ref[...] = ckv
        kr_ref[...] = kr
        for h in range(GQA_KVH):
            kg_ref[:, h * HEAD_DIM:(h + 1) * HEAD_DIM] = kg[h]
        vg_ref[...] = gv
        ks_ref[...] = sk
        vs_ref[...] = sv


def _prep(z, lw, tabs, latent):
    tm = 512
    rows = N_LAT if latent else N_CTX
    roff = (N_CTX if latent else 0) // tm
    n_t = DEC_SEQ // tm

    def zc(name, width):
        cb = Z_OFF[name] // width
        return pl.BlockSpec((tm, width), lambda i: (i + roff, cb))

    def full(a):
        nd = a.ndim
        return pl.BlockSpec(a.shape, lambda i: (0,) * nd)

    weights = [lw["q_norm"], lw["kv_norm"], lw["gq_norm"], lw["gk_norm"],
               lw["w_uq_nope"], lw["w_uq_rope"], lw["w_uq_swap"], lw["w_uk"], lw["w_uv"]]
    in_specs = [zc("mq", 512), zc("gq", 512), zc("sq", 512), zc("mkv", 256), zc("gk", 256), zc("gv", 256),
                zc("sk", 256), zc("sv", 256), zc("mkr", 128)] + [full(a) for a in weights]
    args = [z] * 9 + weights
    if latent:
        in_specs += [pl.BlockSpec((tm, HEAD_DIM), lambda i: (i % n_t, 0))] * 2
        in_specs += [pl.BlockSpec((tm, MLA_ROPE), lambda i: (i % n_t, 0))] * 2
        args += list(tabs)

    def heads(n, d):
        return (pl.BlockSpec((n, tm, d), lambda i: (0, i, 0)), jax.ShapeDtypeStruct((n, rows, d), BF16))

    outs = [heads(MLA_H, MLA_DK), heads(MLA_H, MLA_DK), heads(MLA_H, MLA_V),
            heads(GQA_H, HEAD_DIM), heads(GQA_KVH, HEAD_DIM), heads(GQA_KVH, HEAD_DIM),
            heads(SWA_H, HEAD_DIM), heads(SWA_KVH, HEAD_DIM), heads(SWA_KVH, HEAD_DIM)]
    if not latent:
        kv_w = GQA_KVH * HEAD_DIM
        for d in (MLA_KV_LORA, MLA_ROPE, kv_w, kv_w, kv_w, kv_w):
            outs.append((pl.BlockSpec((tm, d), lambda i: (i, 0)), jax.ShapeDtypeStruct((rows, d), F32)))
    return pl.pallas_call(
        functools.partial(_prep_kernel, latent=latent),
        grid=(rows // tm,),
        in_specs=in_specs,
        out_specs=[o[0] for o in outs],
        out_shape=[o[1] for o in outs],
        compiler_params=_params("parallel"),
        name="prep_latent" if latent else "prep_context",
    )(*args)


def _mla_ctx_kernel(ckv_ref, kr_ref, wuk_ref, wuv_ref, ka_ref, va_ref):
    _mla_kv(ckv_ref[...], kr_ref[...], wuk_ref, wuv_ref, ka_ref, va_ref)


def _mla_ctx(ckv, kr, lw):
    rows = ckv.shape[0]
    return pl.pallas_call(
        _mla_ctx_kernel,
        out_shape=[jax.ShapeDtypeStruct((MLA_H, rows, MLA_DK), BF16),
                   jax.ShapeDtypeStruct((MLA_H, rows, MLA_V), BF16)],
        name="mla_ctx_kv",
    )(ckv, kr, lw["w_uk"], lw["w_uv"])


def _softmax_pv(q, k, v, sink):
    s = lax.dot_general(q, k, (((1,), (1,)), ((), ())), preferred_element_type=F32)
    m = jnp.max(s, axis=-1, keepdims=True)
    if sink is not None:
        m = jnp.maximum(m, sink)
    p = jnp.exp(s - m)
    l = jnp.sum(p, axis=-1, keepdims=True)
    if sink is not None:
        l = l + jnp.exp(sink - m)
    return jnp.dot(p.astype(BF16), v, preferred_element_type=F32) / l


def _ctx_attn_kernel(sink_ref, qa, ka, va, qb, kb, vb, qd, kd, vd, oa, ob, od):
    for h in range(MLA_H):
        oa[:, h * MLA_V:(h + 1) * MLA_V] = _softmax_pv(qa[h], ka[h], va[h], None).astype(BF16)
    for h in range(GQA_H):
        kv = h // (GQA_H // GQA_KVH)
        ob[:, h * HEAD_DIM:(h + 1) * HEAD_DIM] = _softmax_pv(qb[h], kb[kv], vb[kv], None).astype(BF16)
    for h in range(SWA_H):
        kv = h // (SWA_H // SWA_KVH)
        od[:, h * HEAD_DIM:(h + 1) * HEAD_DIM] = _softmax_pv(qd[h], kd[kv], vd[kv], sink_ref[h]).astype(BF16)


def _ctx_attention(operands, sink):
    def spec(a):
        return pl.BlockSpec((a.shape[0], SEQ, a.shape[2]), lambda b: (0, b, 0))

    out = jax.ShapeDtypeStruct((N_CTX, BRANCH_W), BF16)
    return pl.pallas_call(
        _ctx_attn_kernel,
        grid=(BATCH,),
        in_specs=[pl.BlockSpec(memory_space=pltpu.SMEM)] + [spec(a) for a in operands],
        out_specs=[pl.BlockSpec((SEQ, BRANCH_W), lambda b: (b, 0))] * 3,
        out_shape=[out, out, out],
        compiler_params=_params("parallel"),
        name="context_attention",
    )(sink, *operands)


KEY_CHUNK = 256


def _flash_latent_kernel(q_ref, k_ref, v_ref, k0_ref, v0_ref, o_ref, m_sc, l_sc, acc_sc, *, groups):
    dv = v_ref.shape[-1]
    n_chunks = k_ref.shape[1] // KEY_CHUNK
    dn = (((1,), (1,)), ((), ()))

    def update(g, k, v, first):
        s = lax.dot_general(q_ref[g], k, dn, preferred_element_type=F32)
        m_chunk = jnp.max(_lanewise(jnp.maximum, s), axis=-1, keepdims=True)
        if first:
            m_new = jnp.broadcast_to(m_chunk, (s.shape[0], LANES))
        else:
            m_prev = m_sc[g]
            m_new = jnp.maximum(m_prev, m_chunk)
            alpha = jnp.exp2(m_prev - m_new)
        p = jnp.concatenate([jnp.exp2(s[:, c * LANES:(c + 1) * LANES] - m_new) for c in range(s.shape[1] // LANES)],
                            axis=1)
        lsum = _lanewise(jnp.add, p)
        pv = jnp.dot(p.astype(BF16), v, preferred_element_type=F32)
        if first:
            l_sc[g] = lsum
            acc_sc[g] = pv
        else:
            l_sc[g] = alpha * l_sc[g] + lsum
            acc_sc[g] = alpha * acc_sc[g] + pv
        m_sc[g] = m_new

    for g in range(groups):
        update(g, k0_ref[0], v0_ref[0], True)
    for j in range(n_chunks):
        for g in range(groups):
            update(g, k_ref[0, j * KEY_CHUNK:(j + 1) * KEY_CHUNK, :], v_ref[0, j * KEY_CHUNK:(j + 1) * KEY_CHUNK, :],
                   False)
    for g in range(groups):
        l = jnp.sum(l_sc[g], axis=-1, keepdims=True)
        o_ref[:, g * dv:(g + 1) * dv] = (acc_sc[g] / l).astype(o_ref.dtype)


def _flash_latent(q, k, v, ctx):
    tq = 1024
    n_h, _, dk = q.shape
    kvh, _, dv = v.shape
    groups = n_h // kvh
    nq = DEC_SEQ // tq
    return pl.pallas_call(
        functools.partial(_flash_latent_kernel, groups=groups),
        grid=(DEC_BATCH, kvh, nq),
        in_specs=[pl.BlockSpec((groups, tq, dk), lambda b, h, i: (h, b * nq + i, 0)),
                  pl.BlockSpec((1, DEC_SEQ, dk), lambda b, h, i: (h, b, 0)),
                  pl.BlockSpec((1, DEC_SEQ, dv), lambda b, h, i: (h, b, 0)),
                  pl.BlockSpec((1, PAST_LEN, dk), lambda b, h, i: (h, b, 0)),
                  pl.BlockSpec((1, PAST_LEN, dv), lambda b, h, i: (h, b, 0))],
        out_specs=pl.BlockSpec((tq, groups * dv), lambda b, h, i: (b * nq + i, h)),
        out_shape=jax.ShapeDtypeStruct((N_LAT, n_h * dv), BF16),
        scratch_shapes=[pltpu.VMEM((groups, tq, LANES), F32), pltpu.VMEM((groups, tq, LANES), F32),
                        pltpu.VMEM((groups, tq, dv), F32)],
        compiler_params=_params("parallel", "parallel", "parallel"),
        name="flash_latent",
    )(q, k, v, *ctx)


WIN_TQ = 512
WIN_SPAN = WIN_TQ + 2 * WINDOW


def _lanewise(op, x):
    r = x[:, 0:LANES]
    for c in range(1, x.shape[1] // LANES):
        r = op(r, x[:, c * LANES:(c + 1) * LANES])
    return r


def _exp_tiles(s, m):
    return jnp.concatenate([jnp.exp(s[:, c * LANES:(c + 1) * LANES] - m) for c in range(s.shape[1] // LANES)],
                           axis=1)


def _window_kernel(sink_ref, q_ref, k_ref, v_ref, k0_ref, v0_ref, o_ref, *, groups):
    head0 = pl.program_id(1) * groups
    t = k_ref.shape[1]
    q0 = pl.program_id(2) * WIN_TQ
    k_start = pl.multiple_of(jnp.clip(q0 - WINDOW, 0, t - WIN_SPAN), LANES)
    qpos = q0 + lax.broadcasted_iota(jnp.int32, (WIN_TQ, WIN_SPAN), 0)
    kpos = k_start + lax.broadcasted_iota(jnp.int32, (WIN_TQ, WIN_SPAN), 1)
    band = jnp.abs(kpos - qpos) <= WINDOW
    kw = k_ref[0, pl.ds(k_start, WIN_SPAN), :]
    vw = v_ref[0, pl.ds(k_start, WIN_SPAN), :]
    dn = (((1,), (1,)), ((), ()))
    for g in range(groups):
        q = q_ref[g]
        s_loc = jnp.where(band, lax.dot_general(q, kw, dn, preferred_element_type=F32), NEG_INF)
        s_ctx = lax.dot_general(q, k0_ref[0], dn, preferred_element_type=F32)
        snk = sink_ref[head0 + g]
        m = jnp.max(jnp.maximum(_lanewise(jnp.maximum, s_loc), _lanewise(jnp.maximum, s_ctx)),
                    axis=-1, keepdims=True)
        m = jnp.broadcast_to(jnp.maximum(m, snk), (WIN_TQ, LANES))
        p_loc = _exp_tiles(s_loc, m)
        p_ctx = _exp_tiles(s_ctx, m)
        den = (jnp.sum(_lanewise(jnp.add, p_loc) + _lanewise(jnp.add, p_ctx), axis=-1, keepdims=True)
               + jnp.exp(snk - m[:, 0:1]))
        acc = (jnp.dot(p_loc.astype(BF16), vw, preferred_element_type=F32)
               + jnp.dot(p_ctx.astype(BF16), v0_ref[0], preferred_element_type=F32))
        o_ref[:, g * HEAD_DIM:(g + 1) * HEAD_DIM] = (acc / den).astype(o_ref.dtype)


def _window(q, k, v, k0, v0, sink):
    n_h, kvh = q.shape[0], k.shape[0]
    groups = n_h // kvh
    nq = DEC_SEQ // WIN_TQ
    full = pl.BlockSpec((1, DEC_SEQ, HEAD_DIM), lambda b, h, n: (h, b, 0))
    ctx_spec = pl.BlockSpec((1, PAST_LEN, HEAD_DIM), lambda b, h, n: (h, b, 0))
    return pl.pallas_call(
        functools.partial(_window_kernel, groups=groups),
        grid=(DEC_BATCH, kvh, nq),
        in_specs=[pl.BlockSpec(memory_space=pltpu.SMEM),
                  pl.BlockSpec((groups, WIN_TQ, HEAD_DIM), lambda b, h, n: (h, b * nq + n, 0)),
                  full, full, ctx_spec, ctx_spec],
        out_specs=pl.BlockSpec((WIN_TQ, groups * HEAD_DIM), lambda b, h, n: (b * nq + n, h)),
        out_shape=jax.ShapeDtypeStruct((N_LAT, n_h * HEAD_DIM), BF16),
        compiler_params=_params("parallel", "parallel", "parallel"),
        name="window_attention",
    )(sink, q, k, v, k0, v0)


def _ret_kernel(*refs, reverse, has_prev, nseq):
    lg_ref, gn_ref, s0_ref, q_ref, k_ref, v_ref, gate_ref = refs[:7]
    refs = refs[7:]
    if has_prev:
        prev_ref, refs = refs[0], refs[1:]
    o_ref, sfin_ref, s_sc = refs
    c = pl.program_id(1)

    @pl.when(c == 0)
    def _():
        s_sc[...] = s0_ref[...]

    cs = RET_CHUNK
    row = lax.broadcasted_iota(jnp.int32, (cs, cs), 0).astype(F32)
    col = lax.broadcasted_iota(jnp.int32, (cs, cs), 1).astype(F32)
    pos = lax.broadcasted_iota(jnp.int32, (cs, 1), 0).astype(F32)
    diff = (col - row) if reverse else (row - col)
    scan_pos = (cs - 1.0 - pos) if reverse else pos
    for h in range(RET_H):
        lg = lg_ref[h]
        dmask = jnp.where(diff >= 0, jnp.exp(jnp.maximum(diff, 0.0) * lg), 0.0)
        q_dec = jnp.exp((scan_pos + 1.0) * lg)
        k_dec = jnp.exp((cs - 1.0 - scan_pos) * lg)
        c_dec = jnp.exp(cs * lg)
        for e in range(nseq):
            q = q_ref[e, 0, :, h * RET_DK:(h + 1) * RET_DK]
            k = k_ref[e, 0, :, h * RET_DK:(h + 1) * RET_DK] * RET_K_SCALE
            v = v_ref[e, 0, :, h * RET_DV:(h + 1) * RET_DV].astype(BF16)
            s = s_sc[e, h]
            a = lax.dot_general(q.astype(BF16), k.astype(BF16), (((1,), (1,)), ((), ())),
                                preferred_element_type=F32) * dmask
            o = (jnp.dot(a.astype(BF16), v, preferred_element_type=F32)
                 + jnp.dot((q * q_dec).astype(BF16), s.astype(BF16), preferred_element_type=F32))
            kd_t = jnp.transpose(k * k_dec).astype(BF16)
            s_sc[e, h] = s * c_dec + jnp.dot(kd_t, v, preferred_element_type=F32)
            mu = jnp.mean(o, axis=-1, keepdims=True)
            var = jnp.mean(jnp.square(o - mu), axis=-1, keepdims=True)
            y = (o - mu) * lax.rsqrt(var + EPS) * gn_ref[:, h * RET_DV:(h + 1) * RET_DV]
            y = jax.nn.silu(gate_ref[e, 0, :, h * RET_DV:(h + 1) * RET_DV]) * y
            if has_prev:
                y = prev_ref[e, 0, :, h * RET_DV:(h + 1) * RET_DV] + y
            o_ref[e, 0, :, h * RET_DV:(h + 1) * RET_DV] = y.astype(o_ref.dtype)

    @pl.when(c == pl.num_programs(1) - 1)
    def _():
        sfin_ref[...] = s_sc[...]


def _retention(z, log_gamma, gn, s0, prev, *, latent, reverse):
    batch, t, nseq = (DEC_BATCH, DEC_SEQ, DEC_BATCH) if latent else (BATCH, SEQ, 4)
    nc = t // RET_CHUNK
    seq0 = N_CTX // t if latent else 0
    gate_name = "rgb" if reverse else "rgf"
    w = RET_H * RET_DV
    z4 = z.reshape(N_TOK // t, nc, RET_CHUNK, Z_WIDTH)

    def chunk(c):
        return nc - 1 - c if reverse else c

    def zc(name, width):
        return pl.BlockSpec((pl.Element(nseq), pl.Element(1), pl.Element(RET_CHUNK), pl.Element(width)),
                            lambda g, c: (seq0 + g * nseq, chunk(c), 0, Z_OFF[name]))

    rows = pl.BlockSpec((nseq, 1, RET_CHUNK, w), lambda g, c: (g, chunk(c), 0, 0))
    state = pl.BlockSpec((nseq, RET_H, RET_DK, RET_DV), lambda g, c: (g, 0, 0, 0))
    in_specs = [pl.BlockSpec(memory_space=pltpu.SMEM), pl.BlockSpec((1, w), lambda g, c: (0, 0)), state,
                zc("rq", 256), zc("rk", 256), zc("rv", 512), zc(gate_name, 512)]
    args = [log_gamma, gn, s0, z4, z4, z4, z4]
    if prev is not None:
        in_specs.append(rows)
        args.append(prev.reshape(batch, nc, RET_CHUNK, w))
    o, s_fin = pl.pallas_call(
        functools.partial(_ret_kernel, reverse=reverse, has_prev=prev is not None, nseq=nseq),
        grid=(batch // nseq, nc),
        in_specs=in_specs,
        out_specs=[rows, state],
        out_shape=[jax.ShapeDtypeStruct((batch, nc, RET_CHUNK, w), BF16 if reverse else F32),
                   jax.ShapeDtypeStruct((batch, RET_H, RET_DK, RET_DV), F32)],
        scratch_shapes=[pltpu.VMEM((nseq, RET_H, RET_DK, RET_DV), F32)],
        compiler_params=_params("parallel", "arbitrary"),
        name="retention_bwd" if reverse else "retention_fwd",
    )(*args)
    return o.reshape(batch * t, w), s_fin


def _merge_kernel(*refs, n_x, tm):
    x_refs, refs = refs[:n_x], refs[n_x:]
    ctx_refs, lat_refs = refs[:N_BRANCH], refs[N_BRANCH:2 * N_BRANCH]
    (gate_ref, wb_ref, wo_ref, mod_ref, g_ref, wr_ref, br_ref, xo_ref, h_ref, lg_ref) = refs[2 * N_BRANCH:]
    i = pl.program_id(0)
    is_ctx = i < N_CTX // tm

    merged = None
    for n in range(N_BRANCH):
        branch = jnp.where(is_ctx, ctx_refs[n][...], lat_refs[n][...])
        t = jnp.dot(branch, wb_ref[n], preferred_element_type=F32)
        term = jax.nn.sigmoid(gate_ref[:, n * D_MODEL:(n + 1) * D_MODEL]) * t
        merged = term if merged is None else merged + term

    m = mod_ref[0]
    out = jnp.dot(merged.astype(BF16), wo_ref[...], preferred_element_type=F32)
    x = _token_tile(x_refs, i, tm) + m[:, 2 * D_MODEL:3 * D_MODEL] * out
    xo_ref[...] = x
    h = (_rms(x) * g_ref[...]) * (1.0 + m[:, 4 * D_MODEL:5 * D_MODEL]) + m[:, 3 * D_MODEL:4 * D_MODEL]
    h_ref[...] = h
    lg_ref[...] = jnp.dot(h.astype(BF16), wr_ref[...], preferred_element_type=F32) + br_ref[...]


def _merge(xs, ctx_branches, lat_branches, z, w_branch, w_out, layer, mod, gain, w_router, b_router):
    tm = 256
    n_ct = N_CTX // tm
    row = lambda i: (i, 0)
    const2 = lambda i: (0, 0)
    resident = pl.Buffered(1)
    return pl.pallas_call(
        functools.partial(_merge_kernel, n_x=len(xs), tm=tm),
        grid=(N_TOK // tm,),
        in_specs=_token_specs(xs, tm, lambda i: i)
        + [pl.BlockSpec((tm, BRANCH_W), lambda i: (jnp.minimum(i, n_ct - 1), 0))] * N_BRANCH
        + [pl.BlockSpec((tm, BRANCH_W), lambda i: (jnp.maximum(i - n_ct, 0), 0))] * N_BRANCH
        + [pl.BlockSpec((pl.Element(tm), pl.Element(N_BRANCH * D_MODEL)),
                        lambda i: (pl.multiple_of(i * tm, tm), Z_OFF["gate"])),
           pl.BlockSpec((None, N_BRANCH, BRANCH_W, D_MODEL), lambda i: (layer, 0, 0, 0), pipeline_mode=resident),
           pl.BlockSpec((None, D_MODEL, D_MODEL), lambda i: (layer, 0, 0), pipeline_mode=resident),
           pl.BlockSpec((1, 1, 6 * D_MODEL), lambda i: (_cond_row(i, tm), 0, 0)),
           pl.BlockSpec((1, D_MODEL), const2),
           pl.BlockSpec((D_MODEL, ROUTER_W), const2),
           pl.BlockSpec((1, ROUTER_W), const2)],
        out_specs=[pl.BlockSpec((tm, D_MODEL), row), pl.BlockSpec((tm, D_MODEL), row),
                   pl.BlockSpec((tm, ROUTER_W), row)],
        out_shape=[jax.ShapeDtypeStruct((N_TOK, D_MODEL), F32), jax.ShapeDtypeStruct((N_TOK, D_MODEL), F32),
                   jax.ShapeDtypeStruct((N_TOK, ROUTER_W), F32)],
        compiler_params=_params("parallel"),
        name="merge_out_proj",
    )(*xs, *ctx_branches, *lat_branches, z, w_branch, w_out, mod, gain, w_router, b_router)


def _route_kernel(lg_ref, rec_ref, cnt_ref, tri_sc, carry_sc, *, tm):
    @pl.when(pl.program_id(0) == 0)
    def _():
        r = lax.broadcasted_iota(jnp.int32, (tm, tm), 0)
        c = lax.broadcasted_iota(jnp.int32, (tm, tm), 1)
        tri_sc[...] = (c < r).astype(BF16)
        carry_sc[...] = jnp.zeros(carry_sc.shape, F32)

    lg = lg_ref[...]
    lane = lax.broadcasted_iota(jnp.int32, lg.shape, 1)
    big = jnp.int32(ROUTER_W)

    def first_max(vals):
        top = jnp.max(vals, axis=-1, keepdims=True)
        return top, jnp.min(jnp.where(vals == top, lane, big), axis=-1, keepdims=True)

    g_logits = jnp.where(lane < N_GROUPS, lg, -jnp.inf)
    g_top, grp = first_max(g_logits)
    p_grp = 1.0 / jnp.sum(jnp.exp(g_logits - g_top), axis=-1, keepdims=True)
    lo = N_GROUPS + grp * EXPERTS_PER_GROUP
    e_logits = jnp.where((lane >= lo) & (lane < lo + EXPERTS_PER_GROUP), lg, -jnp.inf)
    v0, i0 = first_max(e_logits)
    v1, i1 = first_max(jnp.where(lane == i0, -jnp.inf, e_logits))
    e1 = jnp.exp(v1 - v0)
    den = 1.0 + e1
    w0 = p_grp * (1.0 / den)
    w1 = p_grp * (e1 / den)

    hot0 = (lane == i0).astype(F32)
    hot1 = (lane == i1).astype(F32)
    before = jnp.dot(tri_sc[...], (hot0 + hot1).astype(BF16), preferred_element_type=F32) + carry_sc[...]
    rank0 = jnp.sum(before * hot0, axis=-1, keepdims=True)
    rank1 = jnp.sum(before * hot1, axis=-1, keepdims=True)
    carry_sc[...] += jnp.sum(hot0 + hot1, axis=0, keepdims=True)

    rec = jnp.zeros(lg.shape, F32)
    for slot, val in ((R_W0, w0), (R_W1, w1), (R_E0, (i0 - N_GROUPS).astype(F32)),
                      (R_E1, (i1 - N_GROUPS).astype(F32)), (R_RANK0, rank0), (R_RANK1, rank1)):
        rec = jnp.where(lane == slot, val, rec)
    rec_ref[...] = rec
    cnt_ref[...] = jnp.broadcast_to(carry_sc[...], cnt_ref.shape)


def _route(logits):
    tm = 512
    rec, cnt = pl.pallas_call(
        functools.partial(_route_kernel, tm=tm),
        grid=(N_TOK // tm,),
        in_specs=[pl.BlockSpec((tm, ROUTER_W), lambda i: (i, 0))],
        out_specs=[pl.BlockSpec((tm, ROUTER_W), lambda i: (i, 0)), pl.BlockSpec((8, ROUTER_W), lambda i: (0, 0))],
        out_shape=[jax.ShapeDtypeStruct((N_TOK, ROUTER_W), F32), jax.ShapeDtypeStruct((8, ROUTER_W), F32)],
        scratch_shapes=[pltpu.VMEM((tm, tm), BF16), pltpu.VMEM((1, ROUTER_W), F32)],
        compiler_params=_params("arbitrary"),
        name="moe_route",
    )(logits)
    expert = rec[:, R_E0:R_E1 + 1].astype(jnp.int32)
    rank = rec[:, R_RANK0:R_RANK1 + 1].astype(jnp.int32)
    counts = cnt[0, N_GROUPS:N_GROUPS + N_EXPERTS].astype(jnp.int32)
    padded = (counts + MOE_BLK - 1) // MOE_BLK * MOE_BLK
    pad_end = jnp.cumsum(padded)
    dest = ((pad_end - padded)[expert] + rank).reshape(-1)
    token = jnp.arange(MOE_ASSIGN, dtype=jnp.int32) // TOP_K
    src = (jnp.arange(MOE_CAP, dtype=jnp.int32) % N_TOK).at[dest].set(token)
    blk_start = jnp.arange(MOE_NBLK, dtype=jnp.int32) * MOE_BLK
    blk_e = jnp.minimum(jnp.sum(pad_end[None, :] <= blk_start[:, None], axis=1), N_EXPERTS - 1).astype(jnp.int32)
    n_used = (pad_end[-1:] // MOE_BLK).astype(jnp.int32)
    ids = jnp.arange(N_EXPERTS, dtype=jnp.int32)
    later = jnp.where((ids[None, :] > ids[:, None]) & (counts[None, :] > 0), ids[None, :], N_EXPERTS)
    nxt = jnp.min(later, axis=1)
    nxt_e = jnp.where(nxt < N_EXPERTS, nxt, -1).astype(jnp.int32)[blk_e]
    return rec, dest, src, blk_e, nxt_e, n_used


def _start_row_gather(src_hbm, dst, sem, index_of, n_rows):
    group = 8

    def issue_group(gi, carry):
        for j in range(group):
            r = gi * group + j
            pltpu.make_async_copy(src_hbm.at[pl.ds(index_of(r), 1)], dst.at[pl.ds(r, 1)], sem).start(priority=j % 2)
        return carry

    lax.fori_loop(0, n_rows // group, issue_group, 0)


def _wait_row_gather(src_hbm, dst, sem, n_rows):
    pltpu.make_async_copy(src_hbm.at[pl.ds(0, n_rows)], dst, sem).wait()


def _expert_kernel(be_ref, nx_ref, nu_ref, x_ref, wg_hbm, wu_hbm, wd_hbm, y_ref,
                   stage_g, stage_u, stage_d, res_g, res_u, res_d, wsem, *, layer):
    i = pl.program_id(0)
    n_used = nu_ref[0]

    matrices = ((wg_hbm, stage_g, res_g), (wu_hbm, stage_u, res_u), (wd_hbm, stage_d, res_d))

    def weight_copy(e, k):
        return pltpu.make_async_copy(matrices[k][0].at[layer, e], matrices[k][1], wsem.at[k])

    def to_bf16(stage, res):
        rows = 256

        def body(c, carry):
            r = pl.multiple_of(c * rows, rows)
            res[pl.ds(r, rows), :] = stage[pl.ds(r, rows), :].astype(BF16)
            return carry

        lax.fori_loop(0, stage.shape[0] // rows, body, 0)

    @pl.when(i < n_used)
    def _():
        e = be_ref[i]
        first = (i == 0) | (e != be_ref[jnp.maximum(i - 1, 0)])

        @pl.when(i == 0)
        def _():
            for k in range(len(matrices)):
                weight_copy(e, k).start()

        @pl.when(first)
        def _():
            nxt = nx_ref[i]
            for k, (_, stage, res) in enumerate(matrices):
                weight_copy(e, k).wait()
                to_bf16(stage, res)

                @pl.when(nxt >= 0)
                def _(k=k):
                    weight_copy(nxt, k).start()

        x = x_ref[...]
        g = jnp.dot(x, res_g[...], preferred_element_type=F32)
        u = jnp.dot(x, res_u[...], preferred_element_type=F32)
        a = (jax.nn.silu(g) * u).astype(BF16)
        y_ref[...] = jnp.dot(a, res_d[...], preferred_element_type=F32)

    @pl.when(i >= n_used)
    def _():
        y_ref[...] = jnp.zeros(y_ref.shape, F32)


def _experts(xp, blk_e, nxt_e, n_used, w_gate, w_up, w_down, layer):
    any_spec = pl.BlockSpec(memory_space=pl.ANY)
    return pl.pallas_call(
        functools.partial(_expert_kernel, layer=layer),
        grid_spec=pltpu.PrefetchScalarGridSpec(
            num_scalar_prefetch=3,
            grid=(MOE_NBLK,),
            in_specs=[pl.BlockSpec((MOE_BLK, D_MODEL), lambda i, be, nx, nu: (jnp.minimum(i, nu[0] - 1), 0)),
                      any_spec, any_spec, any_spec],
            out_specs=pl.BlockSpec((MOE_BLK, D_MODEL), lambda i, *_: (i, 0)),
            scratch_shapes=[pltpu.VMEM((D_MODEL, D_EXPERT), F32), pltpu.VMEM((D_MODEL, D_EXPERT), F32),
                            pltpu.VMEM((D_EXPERT, D_MODEL), F32),
                            pltpu.VMEM((D_MODEL, D_EXPERT), BF16), pltpu.VMEM((D_MODEL, D_EXPERT), BF16),
                            pltpu.VMEM((D_EXPERT, D_MODEL), BF16),
                            pltpu.SemaphoreType.DMA((3,))]),
        out_shape=jax.ShapeDtypeStruct((MOE_CAP, D_MODEL), F32),
        compiler_params=_params("arbitrary"),
        name="moe_experts",
    )(blk_e, nxt_e, n_used, xp, w_gate, w_up, w_down)


DISPATCH_ROWS = 512


def _dispatch_kernel(nu_ref, src_ref, h_hbm, xp_ref, buf, sem):
    i = pl.program_id(0)
    used_tiles = pl.cdiv(nu_ref[0] * MOE_BLK, DISPATCH_ROWS)

    def gather_start(tile, slot):
        _start_row_gather(h_hbm, buf.at[slot], sem.at[slot], lambda r: src_ref[tile * DISPATCH_ROWS + r],
                          DISPATCH_ROWS)

    @pl.when(i == 0)
    def _():
        gather_start(0, 0)

    @pl.when(i < used_tiles)
    def _():
        slot = i % 2
        _wait_row_gather(h_hbm, buf.at[slot], sem.at[slot], DISPATCH_ROWS)

        @pl.when(i + 1 < used_tiles)
        def _():
            gather_start(i + 1, 1 - slot)

        xp_ref[...] = buf[slot].astype(BF16)

    @pl.when(i >= used_tiles)
    def _():
        xp_ref[...] = jnp.zeros(xp_ref.shape, BF16)


def _dispatch(h, src, n_used):
    return pl.pallas_call(
        _dispatch_kernel,
        grid_spec=pltpu.PrefetchScalarGridSpec(
            num_scalar_prefetch=2,
            grid=(MOE_CAP // DISPATCH_ROWS,),
            in_specs=[pl.BlockSpec(memory_space=pl.ANY)],
            out_specs=pl.BlockSpec((DISPATCH_ROWS, D_MODEL), lambda i, *_: (i, 0)),
            scratch_shapes=[pltpu.VMEM((2, DISPATCH_ROWS, D_MODEL), F32), pltpu.SemaphoreType.DMA((2,))]),
        out_shape=jax.ShapeDtypeStruct((MOE_CAP, D_MODEL), BF16),
        compiler_params=_params("arbitrary"),
        name="moe_dispatch",
    )(n_used, src, h)


def _combine_kernel(*refs, tm, final):
    dest_ref, yp_hbm, x_ref, rec_ref, mod_ref = refs[:5]
    refs = refs[5:]
    if final:
        fn_ref, yc_ref, yl_ref = refs[:3]
    else:
        xo_ref = refs[0]
    ybuf, sem = refs[-2:]
    i = pl.program_id(0)

    def gather_start(tile, slot):
        for k in range(TOP_K):
            _start_row_gather(yp_hbm, ybuf.at[slot, k], sem.at[slot, k],
                              lambda r, k=k: dest_ref[(tile * tm + r) * TOP_K + k], tm)

    @pl.when(i == 0)
    def _():
        gather_start(0, 0)

    slot = i % 2
    for k in range(TOP_K):
        _wait_row_gather(yp_hbm, ybuf.at[slot, k], sem.at[slot, k], tm)

    @pl.when(i + 1 < pl.num_programs(0))
    def _():
        gather_start(i + 1, 1 - slot)

    rec = rec_ref[...]
    y = ybuf[slot, 0] * rec[:, R_W0:R_W0 + 1] + ybuf[slot, 1] * rec[:, R_W1:R_W1 + 1]
    x = x_ref[...] + mod_ref[0][:, 5 * D_MODEL:6 * D_MODEL] * y
    if final:
        y = _rms(x) * fn_ref[...]

        @pl.when(i < N_CTX // tm)
        def _():
            yc_ref[...] = y

        @pl.when(i >= N_CTX // tm)
        def _():
            yl_ref[...] = y
    else:
        xo_ref[...] = x


def _combine(yp, dest, x, rec, mod, final_gain):
    tm = 256
    n_ct = N_CTX // tm
    final = final_gain is not None
    row = lambda i, d: (i, 0)
    in_specs = [pl.BlockSpec(memory_space=pl.ANY),
                pl.BlockSpec((tm, D_MODEL), row),
                pl.BlockSpec((tm, ROUTER_W), row),
                pl.BlockSpec((1, 1, 6 * D_MODEL), lambda i, d: (_cond_row(i, tm), 0, 0))]
    args = [dest, yp, x, rec, mod]
    if final:
        in_specs.append(pl.BlockSpec((1, D_MODEL), lambda i, d: (0, 0)))
        args.append(final_gain)
        out_specs = [pl.BlockSpec((tm, D_MODEL), lambda i, d: (jnp.minimum(i, n_ct - 1), 0)),
                     pl.BlockSpec((tm, D_MODEL), lambda i, d: (jnp.maximum(i - n_ct, 0), 0))]
        out_shape = [jax.ShapeDtypeStruct((N_CTX, D_MODEL), F32), jax.ShapeDtypeStruct((N_LAT, D_MODEL), F32)]
    else:
        out_specs = pl.BlockSpec((tm, D_MODEL), row)
        out_shape = jax.ShapeDtypeStruct((N_TOK, D_MODEL), F32)
    return pl.pallas_call(
        functools.partial(_combine_kernel, tm=tm, final=final),
        grid_spec=pltpu.PrefetchScalarGridSpec(
            num_scalar_prefetch=1,
            grid=(N_TOK // tm,),
            in_specs=in_specs,
            out_specs=out_specs,
            scratch_shapes=[pltpu.VMEM((2, TOP_K, tm, D_MODEL), F32), pltpu.SemaphoreType.DMA((2, TOP_K))]),
        out_shape=out_shape,
        compiler_params=_params("arbitrary"),
        name="moe_combine",
    )(*args)


def _swap_halves(w, chunk):
    k, n = w.shape
    w = w.reshape(k, n // chunk, 2, chunk // 2)
    return w[:, :, ::-1, :].reshape(k, n)


def _rope_tables(rot_dim):
    rows = DEC_SEQ // GRID_W
    row = np.repeat(np.arange(rows), GRID_W).astype(np.float64)
    col = (np.arange(rows * GRID_W) % GRID_W).astype(np.float64)
    axis_dim = rot_dim // 2
    inv = ROPE_THETA ** (-np.arange(0, axis_dim, 2, dtype=np.float64) / axis_dim)
    ang = np.concatenate([row[:, None] * inv, col[:, None] * inv], axis=-1)
    c, s = np.cos(ang), np.sin(ang)
    return (jnp.asarray(np.concatenate([c, c], axis=-1), F32), jnp.asarray(np.concatenate([-s, s], axis=-1), F32))


def _head_major(cache):
    b, l, kvh, d = cache.shape
    return jnp.transpose(cache, (2, 0, 1, 3)).reshape(kvh, b * l, d).astype(BF16)


def kernel(x_prompt, x_sample, c, cache_mla_ckv, cache_mla_krope, cache_gqa_k, cache_gqa_v, cache_swa_k,
           cache_swa_v, state_ret, c_ctx, ada_w, ada_b, norm_mix, norm_ffn, w_in, mla_q_norm, mla_kv_norm,
           mla_w_uq, mla_w_ukv, gqa_q_norm, gqa_k_norm, ret_decay_logit, ret_gn, swa_sink, w_branch, w_out,
           router_group_w, router_group_b, router_expert_w, router_expert_b, moe_w_gate, moe_w_up, moe_w_down,
           final_norm):
    xs = (x_prompt.reshape(N_CTX, D_MODEL), x_sample.reshape(N_LAT, D_MODEL))
    cond = jnp.concatenate([c_ctx[None, :], c, jnp.zeros((N_COND - 1 - DEC_BATCH, D_MODEL), F32)], axis=0)
    mod_all = _ada(cond, ada_w, ada_b).reshape(DEPTH, N_COND, 1, 6 * D_MODEL)

    w_in_r = _w_in_layout(w_in)
    tabs = _rope_tables(HEAD_DIM) + _rope_tables(MLA_ROPE)
    log_gamma = jax.nn.log_sigmoid(ret_decay_logit.astype(F32))
    uq = mla_w_uq.reshape(DEPTH, MLA_Q_LORA, MLA_H, MLA_DK)
    ukv = mla_w_ukv.reshape(DEPTH, MLA_KV_LORA, MLA_H, MLA_NOPE + MLA_V)
    w_router = jnp.concatenate(
        [router_group_w, router_expert_w,
         jnp.zeros((DEPTH, D_MODEL, ROUTER_W - N_GROUPS - N_EXPERTS), F32)], axis=-1).astype(BF16)
    b_router = jnp.concatenate(
        [router_group_b, router_expert_b, jnp.zeros((DEPTH, ROUTER_W - N_GROUPS - N_EXPERTS), F32)], axis=-1)
    zero_state = jnp.zeros((BATCH, RET_H, RET_DK, RET_DV), F32)
    w_branch_bf16, w_out_bf16 = w_branch.astype(BF16), w_out.astype(BF16)

    caches = [[] for _ in range(7)]
    for l in range(DEPTH):
        mod = mod_all[l]
        uq_rope = uq[l, :, :, MLA_NOPE:].reshape(MLA_Q_LORA, MLA_H * MLA_ROPE)
        lw = dict(
            q_norm=mla_q_norm[l].reshape(1, -1), kv_norm=mla_kv_norm[l].reshape(1, -1),
            gq_norm=gqa_q_norm[l].reshape(1, -1), gk_norm=gqa_k_norm[l].reshape(1, -1),
            w_uq_nope=uq[l, :, :, :MLA_NOPE].reshape(MLA_Q_LORA, MLA_H * MLA_NOPE).astype(BF16),
            w_uq_rope=uq_rope.astype(BF16), w_uq_swap=_swap_halves(uq_rope, MLA_ROPE).astype(BF16),
            w_uk=ukv[l, :, :, :MLA_NOPE].reshape(MLA_KV_LORA, MLA_H * MLA_NOPE).astype(BF16),
            w_uv=ukv[l, :, :, MLA_NOPE:].reshape(MLA_KV_LORA, MLA_H * MLA_V).astype(BF16))

        z = _in_proj(xs, mod, norm_mix[l].reshape(1, -1), w_in_r, l)

        ctx_ops = _prep(z, lw, None, latent=False)
        ckv, kr, kg, vg, ks, vs = ctx_ops[9:]
        oa_c, ob_c, od_c = _ctx_attention(ctx_ops[:9], swa_sink[l])
        gn = ret_gn[l]
        of_c, s_f = _retention(z, log_gamma[l, 0], gn[0:1], zero_state, None, latent=False, reverse=False)
        oc_c, s_b = _retention(z, log_gamma[l, 1], gn[1:2], zero_state, of_c, latent=False, reverse=True)

        (qa, ka, va, qb, kb, vb, qd, kd, vd) = _prep(z, lw, tabs, latent=True)
        ka0, va0 = _mla_ctx(cache_mla_ckv[:, l].reshape(DEC_BATCH * PAST_LEN, MLA_KV_LORA),
                            cache_mla_krope[:, l].reshape(DEC_BATCH * PAST_LEN, MLA_ROPE), lw)
        oa_l = _flash_latent(qa, ka, va, (ka0, va0))
        ob_l = _flash_latent(qb, kb, vb, (_head_major(cache_gqa_k[:, l]), _head_major(cache_gqa_v[:, l])))
        od_l = _window(qd, kd, vd, _head_major(cache_swa_k[:, l]), _head_major(cache_swa_v[:, l]), swa_sink[l])
        of_l, _ = _retention(z, log_gamma[l, 0], gn[0:1], state_ret[:, l, 0], None, latent=True, reverse=False)
        oc_l, _ = _retention(z, log_gamma[l, 1], gn[1:2], state_ret[:, l, 1], of_l, latent=True, reverse=True)

        x_mid, h_ffn, logits = _merge(xs, (oa_c, ob_c, oc_c, od_c), (oa_l, ob_l, oc_l, od_l), z,
                                      w_branch_bf16, w_out_bf16, l, mod,
                                      norm_ffn[l].reshape(1, -1), w_router[l], b_router[l].reshape(1, -1))

        rec, dest, src, blk_e, nxt_e, n_used = _route(logits)
        yp = _experts(_dispatch(h_ffn, src, n_used), blk_e, nxt_e, n_used, moe_w_gate, moe_w_up, moe_w_down, l)
        if l < DEPTH - 1:
            xs = (_combine(yp, dest, x_mid, rec, mod, None),)
        else:
            y_ctx, y_lat = _combine(yp, dest, x_mid, rec, mod, final_norm.reshape(1, D_MODEL))

        kvh_shape = (BATCH, SEQ, GQA_KVH, HEAD_DIM)
        for lst, val in zip(caches, (ckv.reshape(BATCH, SEQ, MLA_KV_LORA), kr.reshape(BATCH, SEQ, MLA_ROPE),
                                     kg.reshape(kvh_shape), vg.reshape(kvh_shape), ks.reshape(kvh_shape),
                                     vs.reshape(kvh_shape), jnp.stack([s_f, s_b], axis=1))):
            lst.append(val)

    y_prompt = y_ctx.reshape(BATCH, SEQ, D_MODEL)
    y_sample = y_lat.reshape(DEC_BATCH, DEC_SEQ, D_MODEL)
    return (y_prompt, y_sample) + tuple(jnp.stack(lst, axis=1) for lst in caches)
```
